```python
import jax
import jax.numpy as jnp
from jax import lax
import numpy as np

D_MODEL = 2048
BATCH = 1
SEQ = 8192
DEPTH = 1
DEC_BATCH = 32
DEC_SEQ = 8
PAST_LEN = 16384
PAGE_SIZE = 128

HEAD_DIM = 128
D_MIX = D_MODEL
C_CONV = D_MIX // 2
N_HEADS = (D_MIX - C_CONV) // HEAD_DIM
N_KV = 2
GROUP = N_HEADS // N_KV
CONV_WIDTH = 31
CMP_STRIDE = 16
CMP_BLOCK = 2 * CMP_STRIDE
CMP_HIDDEN = HEAD_DIM
SEL_BLOCK = 64
N_SELECT = 16
WINDOW = 512
Q_BLOCK = 128
EPS = 1e-6
SCALE = HEAD_DIM ** -0.5
IN_SIZES = (C_CONV, C_CONV, C_CONV,
            N_HEADS * HEAD_DIM,
            N_KV * HEAD_DIM, N_KV * HEAD_DIM,
            N_KV * HEAD_DIM, N_KV * HEAD_DIM,
            N_KV * HEAD_DIM, N_KV * HEAD_DIM,
            3 * N_HEADS,
            N_HEADS * HEAD_DIM)
D_IN = sum(IN_SIZES)

kernel_name = 'hymba_conformer_nsa_alibi_step'


def _rmsnorm(x, g):
    xf = x.astype(jnp.float32)
    y = xf * lax.rsqrt(jnp.mean(xf * xf, axis=-1, keepdims=True) + EPS)
    return (y * g.astype(jnp.float32)).astype(x.dtype)


def _layernorm(x, g, b):
    xf = x.astype(jnp.float32)
    xc = xf - jnp.mean(xf, axis=-1, keepdims=True)
    y = xc * lax.rsqrt(jnp.mean(xc * xc, axis=-1, keepdims=True) + EPS)
    return (y * g.astype(jnp.float32) + b.astype(jnp.float32)).astype(x.dtype)


def _alibi_slopes():
    h = jnp.arange(1, N_HEADS + 1, dtype=jnp.float32)
    return jnp.exp2(-8.0 * h / N_HEADS).reshape(N_KV, GROUP)


def _in_proj(h, w_in, g_q, g_k_slc, g_k_win):
    b, t, _ = h.shape
    z = jnp.einsum('btd,de->bte', h, w_in)
    offs = np.cumsum(IN_SIZES)[:-1].tolist()
    ua, ub, zc, q, kc, vc, ks, vs, kw, vw, gt, zn = jnp.split(z, offs, axis=-1)
    kv = lambda a: a.reshape(b, t, N_KV, HEAD_DIM)
    q = _rmsnorm(q.reshape(b, t, N_KV, GROUP, HEAD_DIM), g_q)
    gates = jax.nn.sigmoid(gt.astype(jnp.float32)).reshape(b, t, N_KV, GROUP, 3).astype(h.dtype)
    return (ua, ub, zc, q, kv(kc), kv(vc), _rmsnorm(kv(ks), g_k_slc), kv(vs),
            _rmsnorm(kv(kw), g_k_win), kv(vw), gates, zn)


def _conv_module(ua, ub, zc, prefix, w_dw, b_dw, ln_g, ln_b, w_pw2, b_pw2):
    u = ua * jax.nn.sigmoid(ub)
    full = jnp.concatenate([prefix.astype(u.dtype), u], axis=1)
    y = lax.conv_general_dilated(full, w_dw[:, None, :].astype(u.dtype), window_strides=(1,), padding='VALID',
                                 dimension_numbers=('NWC', 'WIO', 'NWC'), feature_group_count=C_CONV)
    y = jax.nn.silu(_layernorm(y + b_dw, ln_g, ln_b))
    y = jnp.einsum('btc,ce->bte', y, w_pw2) + b_pw2
    return y * jax.nn.silu(zc), full[:, full.shape[1] - (CONV_WIDTH - 1):]


def _chunk_feats(rows, pe, w1):
    b, l, g, d = rows.shape
    c = rows.reshape(b, l // CMP_STRIDE, CMP_STRIDE, g, d)
    fa = jnp.einsum('bncgd,cde->bnge', c + pe[:CMP_STRIDE, None, :], w1[:CMP_STRIDE])
    fb = jnp.einsum('bncgd,cde->bnge', c + pe[CMP_STRIDE:, None, :], w1[CMP_STRIDE:])
    return fa, fb


def _compress(rows_list, pe, w1, w2):
    feats = [_chunk_feats(r, pe, w1) for r in rows_list]
    fa = jnp.concatenate([f[0] for f in feats], axis=1)
    fb = jnp.concatenate([f[1] for f in feats], axis=1)
    hid = jax.nn.silu(fa[:, :-1] + fb[:, 1:])
    return jnp.einsum('bnge,ed->bngd', hid, w2)


def _masked_softmax(s, mask):
    s = jnp.where(mask, s, -jnp.inf)
    m = jnp.max(s, axis=-1, keepdims=True)
    m = jnp.where(jnp.isfinite(m), m, 0.0)
    e = jnp.where(mask, jnp.exp(s - m), 0.0)
    return e / jnp.maximum(jnp.sum(e, axis=-1, keepdims=True), 1e-30)


def _cmp_branch(q, kc, vc, q_pos, slopes):
    end = jnp.arange(kc.shape[1]) * CMP_STRIDE + (CMP_BLOCK - 1)
    dist = q_pos[:, None] - end[None, :]
    s = jnp.einsum('bqgrd,bngd->bqgrn', q, kc).astype(jnp.float32) * SCALE
    s = s - slopes[:, :, None] * dist[:, None, None, :].astype(jnp.float32)
    p = _masked_softmax(s, (dist >= 0)[:, None, None, :])
    return jnp.einsum('bqgrn,bngd->bqgrd', p.astype(vc.dtype), vc), p


def _cmp_to_sel(imp, n_sel):
    r = SEL_BLOCK // CMP_STRIDE
    lead = CMP_BLOCK // CMP_STRIDE - 1
    total = lead + r * n_sel + r
    pp = jnp.pad(imp, ((0, 0), (0, 0), (0, 0), (lead, total - lead - imp.shape[-1])))
    out = jnp.zeros(imp.shape[:-1] + (n_sel,), imp.dtype)
    for o in range(-lead, r):
        start = o * CMP_STRIDE
        w = (min(start + CMP_BLOCK, SEL_BLOCK) - max(start, 0)) / CMP_BLOCK
        out = out + w * pp[..., lead + o: lead + o + r * n_sel: r]
    return out


def _select_blocks(p, q_pos, n_sel):
    imp = _cmp_to_sel(jnp.sum(p, axis=3), n_sel)
    j = jnp.arange(n_sel)[None, :]
    cur = (q_pos // SEL_BLOCK)[:, None]
    valid = j <= cur
    forced = (j == 0) | (j == cur) | (j == cur - 1)
    score = jnp.where(forced[:, None, :], jnp.inf, jnp.where(valid[:, None, :], imp, -jnp.inf))
    _, idx = lax.top_k(score, min(N_SELECT, n_sel))
    return idx


def _gathered_attend(q, k, v, k_pos, q_pos, slopes):
    dist = q_pos[None, :, None, None] - k_pos
    s = jnp.einsum('bqgrd,bqgsd->bqgrs', q, k).astype(jnp.float32) * SCALE
    s = s - slopes[:, :, None] * dist[:, :, :, None, :].astype(jnp.float32)
    p = _masked_softmax(s, (dist >= 0)[:, :, :, None, :])
    return jnp.einsum('bqgrs,bqgsd->bqgrd', p.astype(v.dtype), v)


def _band_attend(q, k, v, q_pos, k_pos, slopes):
    dist = q_pos[:, :, None] - k_pos[:, None, :]
    mask = (k_pos[:, None, :] >= 0) & (dist >= 0) & (dist < WINDOW)
    s = jnp.einsum('bnqgrd,bnsgd->bnqgrs', q, k).astype(jnp.float32) * SCALE
    s = s - slopes[:, :, None] * dist[:, :, None, None, :].astype(jnp.float32)
    p = _masked_softmax(s, mask[:, :, None, None, :])
    return jnp.einsum('bnqgrs,bnsgd->bnqgrd', p.astype(v.dtype), v)


def _prompt_selected(q, ks, vs, idx, slopes):
    b, t = q.shape[:2]
    n_sel = t // SEL_BLOCK
    nb = t // Q_BLOCK
    n_k = idx.shape[-1]
    kb = ks.reshape(b, n_sel, SEL_BLOCK, N_KV, HEAD_DIM)
    vb = vs.reshape(b, n_sel, SEL_BLOCK, N_KV, HEAD_DIM)
    bi = jnp.arange(b)[:, None, None, None]
    gi = jnp.arange(N_KV)[None, None, :, None]
    off = jnp.arange(SEL_BLOCK)
    shape = (b, Q_BLOCK, N_KV, n_k * SEL_BLOCK)

    def block(args):
        qb, ib, pb = args
        kg = kb[bi, ib, :, gi, :].reshape(shape + (HEAD_DIM,))
        vg = vb[bi, ib, :, gi, :].reshape(shape + (HEAD_DIM,))
        kpos = (ib[..., None] * SEL_BLOCK + off).reshape(shape)
        return _gathered_attend(qb, kg, vg, kpos, pb, slopes)

    xs = (q.reshape(b, nb, Q_BLOCK, N_KV, GROUP, HEAD_DIM).swapaxes(0, 1),
          idx.reshape(b, nb, Q_BLOCK, N_KV, n_k).swapaxes(0, 1),
          jnp.arange(t).reshape(nb, Q_BLOCK))
    out = lax.map(block, xs)
    return out.swapaxes(0, 1).reshape(b, t, N_KV, GROUP, HEAD_DIM)


def _sample_selected(q, ks, vs, cache_k, cache_v, page_table, idx, q_pos, slopes):
    db, ds = q.shape[:2]
    bpp = PAGE_SIZE // SEL_BLOCK
    nb_past = page_table.shape[1] * bpp
    n_tail = -(-ds // SEL_BLOCK)
    n_k = idx.shape[-1]
    bi = jnp.arange(db)[:, None, None, None]
    gi = jnp.arange(N_KV)[None, None, :, None]
    jp = jnp.minimum(idx, nb_past - 1)
    phys = page_table[bi, jp // bpp]
    jt = jnp.clip(idx - nb_past, 0, n_tail - 1)
    in_past = (idx < nb_past)[..., None, None]
    shape = (db, ds, N_KV, n_k * SEL_BLOCK)

    def gather(pool, new):
        pb = pool.reshape(pool.shape[0], bpp, SEL_BLOCK, N_KV, HEAD_DIM)
        tb = jnp.pad(new, ((0, 0), (0, n_tail * SEL_BLOCK - ds), (0, 0), (0, 0)))
        tb = tb.reshape(db, n_tail, SEL_BLOCK, N_KV, HEAD_DIM)
        g = jnp.where(in_past, pb[phys, jp % bpp, :, gi, :], tb[bi, jt, :, gi, :])
        return g.reshape(shape + (HEAD_DIM,))

    kpos = (idx[..., None] * SEL_BLOCK + jnp.arange(SEL_BLOCK)).reshape(shape)
    return _gathered_attend(q, gather(cache_k, ks), gather(cache_v, vs), kpos, q_pos, slopes)


def _prompt_window(q, kw, vw, slopes):
    b, t = q.shape[:2]
    nb = t // Q_BLOCK
    idx = jnp.arange(nb)[:, None] * Q_BLOCK + jnp.arange(WINDOW + Q_BLOCK)[None, :]
    padw = ((0, 0), (WINDOW, 0), (0, 0), (0, 0))
    kb = jnp.pad(kw, padw)[:, idx]
    vb = jnp.pad(vw, padw)[:, idx]
    qb = q.reshape(b, nb, Q_BLOCK, N_KV, GROUP, HEAD_DIM)
    o = _band_attend(qb, kb, vb, jnp.arange(t).reshape(nb, Q_BLOCK), idx - WINDOW, slopes)
    return o.reshape(b, t, N_KV, GROUP, HEAD_DIM)


def _merge(x, conv_o, o_c, o_s, o_w, gates, zn, w_out):
    b, t = x.shape[:2]
    o = gates[..., 0:1] * o_c + gates[..., 1:2] * o_s + gates[..., 2:3] * o_w
    o = o.reshape(b, t, N_HEADS * HEAD_DIM) * jax.nn.silu(zn)
    mixed = jnp.concatenate([conv_o, o], axis=-1)
    return x + jnp.einsum('bte,ed->btd', mixed, w_out)


def setup_inputs(seed: int = 0) -> dict:
    key = jax.random.key(seed)
    k = jax.random.split(key, 29)
    f32 = jnp.float32
    n_pages = PAST_LEN // PAGE_SIZE
    n_used = DEC_BATCH * n_pages
    n_phys = n_used + (n_used + 3) // 4
    w_buf = min(WINDOW, PAST_LEN)
    pool = (n_phys, PAGE_SIZE, N_KV, HEAD_DIM)

    def nrm(i, shape, scale):
        return jax.random.normal(k[i], shape, f32) * scale

    def gain(i, n):
        return 1.0 + nrm(i, (n,), 0.02)

    page_table = jax.random.permutation(k[9], n_phys)[:n_used].reshape(DEC_BATCH, n_pages).astype(jnp.int32)
    return {
        'x_prompt': nrm(0, (BATCH, SEQ, D_MODEL), 1.0),
        'x_sample': nrm(1, (DEC_BATCH, DEC_SEQ, D_MODEL), 1.0),
        'cache_k_cmp': nrm(2, pool, 1.0),
        'cache_v_cmp': nrm(3, pool, 1.0),
        'cache_k_slc': nrm(4, pool, 1.0),
        'cache_v_slc': nrm(5, pool, 1.0),
        'state_k_win': nrm(6, (DEC_BATCH, w_buf, N_KV, HEAD_DIM), 1.0),
        'state_v_win': nrm(7, (DEC_BATCH, w_buf, N_KV, HEAD_DIM), 1.0),
        'state_conv': nrm(8, (DEC_BATCH, CONV_WIDTH - 1, C_CONV), 0.5),
        'page_table': page_table,
        'g_norm': gain(10, D_MODEL),
        'w_in': nrm(11, (D_MODEL, D_IN), D_MODEL ** -0.5),
        'pe_cmp_k': nrm(12, (CMP_BLOCK, HEAD_DIM), 0.1),
        'w_cmp_k1': nrm(13, (CMP_BLOCK, HEAD_DIM, CMP_HIDDEN), (CMP_BLOCK * HEAD_DIM) ** -0.5),
        'w_cmp_k2': nrm(14, (CMP_HIDDEN, HEAD_DIM), CMP_HIDDEN ** -0.5),
        'pe_cmp_v': nrm(15, (CMP_BLOCK, HEAD_DIM), 0.1),
        'w_cmp_v1': nrm(16, (CMP_BLOCK, HEAD_DIM, CMP_HIDDEN), (CMP_BLOCK * HEAD_DIM) ** -0.5),
        'w_cmp_v2': nrm(17, (CMP_HIDDEN, HEAD_DIM), CMP_HIDDEN ** -0.5),
        'g_q': gain(18, HEAD_DIM),
        'g_k_cmp': gain(19, HEAD_DIM),
        'g_k_slc': gain(20, HEAD_DIM),
        'g_k_win': gain(21, HEAD_DIM),
        'w_dw': nrm(22, (CONV_WIDTH, C_CONV), CONV_WIDTH ** -0.5),
        'b_dw': nrm(23, (C_CONV,), 0.01),
        'ln_g': gain(24, C_CONV),
        'ln_b': nrm(25, (C_CONV,), 0.01),
        'w_pw2': nrm(26, (C_CONV, C_CONV), C_CONV ** -0.5),
        'b_pw2': nrm(27, (C_CONV,), 0.01),
        'w_out': nrm(28, (D_MIX, D_MODEL), D_MIX ** -0.5),
    }


def reference(x_prompt, x_sample, cache_k_cmp, cache_v_cmp, cache_k_slc, cache_v_slc, state_k_win, state_v_win,
              state_conv, page_table, g_norm, w_in, pe_cmp_k, w_cmp_k1, w_cmp_k2, pe_cmp_v, w_cmp_v1, w_cmp_v2,
              g_q, g_k_cmp, g_k_slc, g_k_win, w_dw, b_dw, ln_g, ln_b, w_pw2, b_pw2, w_out):
    slopes = _alibi_slopes()

    b, t, _ = x_prompt.shape
    ua, ub, zc, q, kc, vc, ks, vs, kw, vw, gates, zn = _in_proj(_rmsnorm(x_prompt, g_norm), w_in, g_q, g_k_slc, g_k_win)
    conv_o, conv_p = _conv_module(ua, ub, zc, jnp.zeros((b, CONV_WIDTH - 1, C_CONV), x_prompt.dtype),
                                  w_dw, b_dw, ln_g, ln_b, w_pw2, b_pw2)
    pos = jnp.arange(t)
    k_c = _rmsnorm(_compress([kc], pe_cmp_k, w_cmp_k1, w_cmp_k2), g_k_cmp)
    v_c = _compress([vc], pe_cmp_v, w_cmp_v1, w_cmp_v2)
    o_c, p_c = _cmp_branch(q, k_c, v_c, pos, slopes)
    idx = _select_blocks(p_c, pos, -(-t // SEL_BLOCK))
    o_s = _prompt_selected(q, ks, vs, idx, slopes)
    o_w = _prompt_window(q, kw, vw, slopes)
    y_prompt = _merge(x_prompt, conv_o, o_c, o_s, o_w, gates, zn, w_out)
    w_p = min(WINDOW, t)

    db, ds, _ = x_sample.shape
    past = page_table.shape[1] * PAGE_SIZE
    ua2, ub2, zc2, q2, kc2, vc2, ks2, vs2, kw2, vw2, gates2, zn2 = _in_proj(
        _rmsnorm(x_sample, g_norm), w_in, g_q, g_k_slc, g_k_win)
    conv_o2, conv_s = _conv_module(ua2, ub2, zc2, state_conv, w_dw, b_dw, ln_g, ln_b, w_pw2, b_pw2)
    spos = past + jnp.arange(ds)
    n_new = (ds // CMP_STRIDE) * CMP_STRIDE
    past_kc = cache_k_cmp[page_table].reshape(db, past, N_KV, HEAD_DIM)
    past_vc = cache_v_cmp[page_table].reshape(db, past, N_KV, HEAD_DIM)
    k_c2 = _rmsnorm(_compress([past_kc, kc2[:, :n_new]], pe_cmp_k, w_cmp_k1, w_cmp_k2), g_k_cmp)
    v_c2 = _compress([past_vc, vc2[:, :n_new]], pe_cmp_v, w_cmp_v1, w_cmp_v2)
    o_c2, p_c2 = _cmp_branch(q2, k_c2, v_c2, spos, slopes)
    idx2 = _select_blocks(p_c2, spos, -(-(past + ds) // SEL_BLOCK))
    o_s2 = _sample_selected(q2, ks2, vs2, cache_k_slc, cache_v_slc, page_table, idx2, spos, slopes)
    wb = state_k_win.shape[1]
    kw_all = jnp.concatenate([state_k_win.astype(kw2.dtype), kw2], axis=1)
    vw_all = jnp.concatenate([state_v_win.astype(vw2.dtype), vw2], axis=1)
    kpos_w = past - wb + jnp.arange(wb + ds)
    o_w2 = _band_attend(q2[:, None], kw_all[:, None], vw_all[:, None], spos[None], kpos_w[None], slopes)[:, 0]
    y_sample = _merge(x_sample, conv_o2, o_c2, o_s2, o_w2, gates2, zn2, w_out)
    n_all = wb + ds

    return (y_prompt, y_sample,
            kc, vc, ks, vs, kw[:, t - w_p:], vw[:, t - w_p:], conv_p,
            kc2, vc2, ks2, vs2, kw_all[:, n_all - wb:], vw_all[:, n_all - wb:], conv_s)
```

```python
import functools

import numpy as np
import jax
import jax.numpy as jnp
from jax import lax
from jax.experimental import pallas as pl
from jax.experimental.pallas import tpu as pltpu

F32 = jnp.float32
BF16 = jnp.bfloat16

HEAD_DIM = 128
N_HEADS = 8
N_KV = 2
GROUP = 4
C_CONV = 1024
CONV_WIDTH = 31
CMP_STRIDE = 16
CMP_BLOCK = 32
SEL_BLOCK = 64
N_SELECT = 16
WINDOW = 512
PAGE_SIZE = 128
EPS = 1e-6
SCALE = HEAD_DIM ** -0.5
SLOPES = tuple(2.0 ** -(h + 1) for h in range(N_HEADS))

TN = 512
N_TILES = 14
ZW = TN * N_TILES
COL_UA, COL_UB, COL_ZC, COL_Q = 0, 1024, 2048, 3072
COL_KC, COL_VC, COL_KS, COL_VS, COL_KW, COL_VW = 4096, 4352, 4608, 4864, 5120, 5376
COL_GT, COL_ZN = 5632, 6144
TILE_Q0, TILE_Q1, TILE_SLC, TILE_WIN, TILE_GT = 6, 7, 9, 10, 11

NEG = -1e30
M_INIT = -1e29
VMEM_LIMIT = 48 * 1024 * 1024


def _sigmoid(x):
    return 1.0 / (1.0 + jnp.exp(-x))


def _silu(x):
    return x * _sigmoid(x)


def _dot(a, b):
    return jnp.dot(a, b, preferred_element_type=F32)


def _dot_nt(a, b):
    return lax.dot_general(a, b, (((1,), (1,)), ((), ())), preferred_element_type=F32)


def _rms(a, g):
    return a * lax.rsqrt(jnp.mean(a * a, axis=-1, keepdims=True) + EPS) * g


def _iota(shape, dim):
    return lax.broadcasted_iota(jnp.int32, shape, dim)


def _inproj_kernel(x_ref, gn_ref, w_ref, gq_ref, gks_ref, gkw_ref, z_ref, xn_ref):
    j = pl.program_id(1)

    @pl.when(j == 0)
    def _():
        x = x_ref[...]
        ms = jnp.mean(x * x, axis=-1, keepdims=True)
        xn_ref[...] = (x * lax.rsqrt(ms + EPS) * gn_ref[...]).astype(BF16)

    acc = _dot(xn_ref[...], w_ref[...])

    def normed(g_ref, n):
        parts = []
        for c in range(TN // HEAD_DIM):
            a = acc[:, c * HEAD_DIM:(c + 1) * HEAD_DIM]
            parts.append(_rms(a, g_ref[...]) if c < n else a)
        return jnp.concatenate(parts, axis=1)

    is_q = (j == TILE_Q0) | (j == TILE_Q1)
    is_slc = j == TILE_SLC
    is_win = j == TILE_WIN
    is_gt = j == TILE_GT

    @pl.when(is_q)
    def _():
        z_ref[...] = normed(gq_ref, 4)

    @pl.when(is_slc)
    def _():
        z_ref[...] = normed(gks_ref, 2)

    @pl.when(is_win)
    def _():
        z_ref[...] = normed(gkw_ref, 2)

    @pl.when(is_gt)
    def _():
        z_ref[...] = _sigmoid(acc)

    @pl.when(jnp.logical_not(is_q | is_slc | is_win | is_gt))
    def _():
        z_ref[...] = acc


def _in_proj(x, g_norm, w_p, g_q, g_ks, g_kw, tm):
    t, d = x.shape
    row = lambda a: a.reshape(1, -1)
    return pl.pallas_call(
        _inproj_kernel,
        out_shape=jax.ShapeDtypeStruct((t, ZW), F32),
        grid=(t // tm, N_TILES),
        in_specs=[
            pl.BlockSpec((tm, d), lambda i, j: (i, 0)),
            pl.BlockSpec((1, d), lambda i, j: (0, 0)),
            pl.BlockSpec((d, TN), lambda i, j: (0, j)),
            pl.BlockSpec((1, HEAD_DIM), lambda i, j: (0, 0)),
            pl.BlockSpec((1, HEAD_DIM), lambda i, j: (0, 0)),
            pl.BlockSpec((1, HEAD_DIM), lambda i, j: (0, 0)),
        ],
        out_specs=pl.BlockSpec((tm, TN), lambda i, j: (i, j)),
        scratch_shapes=[pltpu.VMEM((tm, d), BF16)],
        compiler_params=pltpu.CompilerParams(
            dimension_semantics=("arbitrary", "arbitrary"), vmem_limit_bytes=VMEM_LIMIT),
        name="in_proj",
    )(x, row(g_norm), w_p, row(g_q), row(g_ks), row(g_kw))


HALO = 32
CONV_RB = 32
CONV_CB = 256


def _conv_tail(y, zc, lng_ref, lnb_ref, wpw_ref, bpw_ref):
    mu = jnp.mean(y, axis=-1, keepdims=True)
    yc = y - mu
    var = jnp.mean(yc * yc, axis=-1, keepdims=True)
    yn = yc * lax.rsqrt(var + EPS) * lng_ref[...] + lnb_ref[...]
    act = _silu(yn).astype(BF16)
    return (_dot(act, wpw_ref[...]) + bpw_ref[...]) * _silu(zc)


def _conv_prompt_kernel(ua_ref, ub_ref, zc_ref, uah_ref, ubh_ref, wdw_ref, bdw_ref, lng_ref, lnb_ref,
                        wpw_ref, bpw_ref, o_ref, st_ref, buf, ybuf, *, tt):
    i = pl.program_id(0)
    uh = uah_ref[...] * _sigmoid(ubh_ref[...])
    buf[0:HALO, :] = jnp.where(i > 0, uh, 0.0)
    buf[HALO:HALO + tt, :] = ua_ref[...] * _sigmoid(ub_ref[...])
    off = HALO - (CONV_WIDTH - 1)
    for c0 in range(0, C_CONV, CONV_CB):
        for r0 in range(0, tt, CONV_RB):
            acc = jnp.broadcast_to(bdw_ref[:, c0:c0 + CONV_CB], (CONV_RB, CONV_CB))
            for k in range(CONV_WIDTH):
                acc = acc + wdw_ref[k:k + 1, c0:c0 + CONV_CB] * buf[r0 + off + k:r0 + off + k + CONV_RB,
                                                                     c0:c0 + CONV_CB]
            ybuf[r0:r0 + CONV_RB, c0:c0 + CONV_CB] = acc
    o_ref[...] = _conv_tail(ybuf[...], zc_ref[...], lng_ref, lnb_ref, wpw_ref, bpw_ref)

    @pl.when(i == pl.num_programs(0) - 1)
    def _():
        st_ref[...] = buf[tt:tt + HALO, :]


def _conv_prompt(z, w_dw, b_dw, ln_g, ln_b, w_pw_bf, b_pw, tt=256):
    t = z.shape[0]
    row = lambda a: a.reshape(1, -1)
    hb = tt // HALO
    cur = lambda c: pl.BlockSpec((tt, C_CONV), lambda i: (i, c))
    halo = lambda c: pl.BlockSpec((HALO, C_CONV), lambda i: (jnp.maximum(i * hb - 1, 0), c))
    full = lambda shape: pl.BlockSpec(shape, lambda i: (0, 0))
    return pl.pallas_call(
        functools.partial(_conv_prompt_kernel, tt=tt),
        out_shape=(jax.ShapeDtypeStruct((t, C_CONV), F32), jax.ShapeDtypeStruct((HALO, C_CONV), F32)),
        grid=(t // tt,),
        in_specs=[cur(0), cur(1), cur(2), halo(0), halo(1),
                  full((CONV_WIDTH, C_CONV)), full((1, C_CONV)), full((1, C_CONV)), full((1, C_CONV)),
                  full((C_CONV, C_CONV)), full((1, C_CONV))],
        out_specs=(pl.BlockSpec((tt, C_CONV), lambda i: (i, 0)), pl.BlockSpec((HALO, C_CONV), lambda i: (0, 0))),
        scratch_shapes=[pltpu.VMEM((HALO + tt, C_CONV), F32), pltpu.VMEM((tt, C_CONV), F32)],
        compiler_params=pltpu.CompilerParams(dimension_semantics=("arbitrary",), vmem_limit_bytes=VMEM_LIMIT),
        name="conv_prompt",
    )(z, z, z, z, z, w_dw, row(b_dw), row(ln_g), row(ln_b), w_pw_bf, row(b_pw))


ST_ROWS = 40


def _conv_sample_kernel(ua_ref, ub_ref, zc_ref, st_ref, wdw_ref, bdw_ref, lng_ref, lnb_ref, wpw_ref, bpw_ref,
                        o_ref, u_ref, fbuf, ybuf, *, nb, ds):
    u = ua_ref[...] * _sigmoid(ub_ref[...])
    u_ref[...] = u
    rows = _iota((nb * ds, 1), 0) & (ds - 1)
    acc_u = jnp.broadcast_to(bdw_ref[...], (nb * ds, C_CONV))
    for d in range(ds):
        sh = u if d == 0 else pltpu.roll(u, d, 0)
        acc_u = acc_u + jnp.where(rows >= d, sh, 0.0) * wdw_ref[CONV_WIDTH - 1 - d:CONV_WIDTH - d, :]
    ybuf[...] = acc_u
    fbuf[:, 24:ST_ROWS, :] = jnp.zeros((nb, ST_ROWS - 24, C_CONV), F32)
    fbuf[:, 0:CONV_WIDTH - 1, :] = st_ref[...]

    def body(b, carry):
        acc = jnp.zeros((ds, C_CONV), F32)
        for k in range(CONV_WIDTH - 1):
            acc = acc + wdw_ref[k:k + 1, :] * fbuf[b, k:k + ds, :]
        r = pl.multiple_of(b * ds, ds)
        ybuf[pl.ds(r, ds), :] = ybuf[pl.ds(r, ds), :] + acc
        return carry

    lax.fori_loop(0, nb, body, 0)
    o_ref[...] = _conv_tail(ybuf[...], zc_ref[...], lng_ref, lnb_ref, wpw_ref, bpw_ref)


def _conv_sample(z2, state_conv, w_dw, b_dw, ln_g, ln_b, w_pw_bf, b_pw):
    nb, sw, _ = state_conv.shape
    t = z2.shape[0]
    ds = t // nb
    assert sw == CONV_WIDTH - 1 and ds == 8
    row = lambda a: a.reshape(1, -1)
    col = lambda c: pl.BlockSpec((t, C_CONV), lambda i: (0, c))
    full = lambda shape: pl.BlockSpec(shape, lambda i: (0,) * len(shape))
    return pl.pallas_call(
        functools.partial(_conv_sample_kernel, nb=nb, ds=ds),
        out_shape=(jax.ShapeDtypeStruct((t, C_CONV), F32), jax.ShapeDtypeStruct((t, C_CONV), F32)),
        grid=(1,),
        in_specs=[col(0), col(1), col(2), full((nb, sw, C_CONV)),
                  full((CONV_WIDTH, C_CONV)), full((1, C_CONV)), full((1, C_CONV)), full((1, C_CONV)),
                  full((C_CONV, C_CONV)), full((1, C_CONV))],
        out_specs=(full((t, C_CONV)), full((t, C_CONV))),
        scratch_shapes=[pltpu.VMEM((nb, ST_ROWS, C_CONV), F32), pltpu.VMEM((t, C_CONV), F32)],
        compiler_params=pltpu.CompilerParams(dimension_semantics=("arbitrary",), vmem_limit_bytes=VMEM_LIMIT),
        name="conv_sample",
    )(z2, z2, z2, state_conv, w_dw, row(b_dw), row(ln_g), row(ln_b), w_pw_bf, row(b_pw))


CHUNK_W = CMP_STRIDE * N_KV * HEAD_DIM
FEAT_W = 2 * N_KV * HEAD_DIM


def _chunk_feats(x_ref, w):
    outs = []
    for g in range(N_KV):
        xg = jnp.concatenate(
            [x_ref[:, (c * N_KV + g) * HEAD_DIM:(c * N_KV + g + 1) * HEAD_DIM].astype(BF16)
             for c in range(CMP_STRIDE)], axis=1)
        outs.append(_dot(xg, w))
    return jnp.concatenate(outs, axis=1)


def _feats_kernel(xk_ref, xv_ref, wk_ref, wv_ref, fk_ref, fv_ref):
    fk_ref[...] = _chunk_feats(xk_ref, wk_ref[...])
    fv_ref[...] = _chunk_feats(xv_ref, wv_ref[...])


def _feats_prompt(xk, xv, wk, wv, tm=128):
    nc = xk.shape[0]
    full = lambda shape: pl.BlockSpec(shape, lambda i: (0, 0))
    rows = lambda w: pl.BlockSpec((tm, w), lambda i: (i, 0))
    return pl.pallas_call(
        _feats_kernel,
        out_shape=(jax.ShapeDtypeStruct((nc, FEAT_W), F32),) * 2,
        grid=(nc // tm,),
        in_specs=[rows(CHUNK_W), rows(CHUNK_W), full(wk.shape), full(wv.shape)],
        out_specs=(rows(FEAT_W),) * 2,
        compiler_params=pltpu.CompilerParams(dimension_semantics=("arbitrary",), vmem_limit_bytes=VMEM_LIMIT),
        name="feats_prompt",
    )(xk, xv, wk, wv)


FEAT_PAGES = 32
CHUNKS_PER_PAGE = PAGE_SIZE // CMP_STRIDE


def _page_copies(pt_ref, srcs, bufs, sem, step, slot, pages, rows):
    out = []
    for p in range(pages):
        page = pt_ref[step * pages + p]
        for n, (src, buf) in enumerate(zip(srcs, bufs)):
            out.append(pltpu.make_async_copy(src.at[page], buf.at[slot, pl.ds(p * rows, rows)], sem.at[n, slot]))
    return out


def _gather_pipeline(pt_ref, srcs, bufs, sem, pages, rows):
    s = pl.program_id(0)
    slot = s % 2

    @pl.when(s == 0)
    def _():
        for c in _page_copies(pt_ref, srcs, bufs, sem, 0, 0, pages, rows):
            c.start()

    @pl.when(s + 1 < pl.num_programs(0))
    def _():
        for c in _page_copies(pt_ref, srcs, bufs, sem, s + 1, 1 - slot, pages, rows):
            c.start()

    for c in _page_copies(pt_ref, srcs, bufs, sem, s, slot, pages, rows):
        c.wait()
    return slot


def _feats_gather_kernel(pt_ref, ck_hbm, cv_hbm, wk_ref, wv_ref, fk_ref, fv_ref, kbuf, vbuf, sem):
    slot = _gather_pipeline(pt_ref, (ck_hbm, cv_hbm), (kbuf, vbuf), sem, FEAT_PAGES, CHUNKS_PER_PAGE)
    fk_ref[...] = _chunk_feats(kbuf.at[slot], wk_ref[...])
    fv_ref[...] = _chunk_feats(vbuf.at[slot], wv_ref[...])


def _feats_sample(page_table, cache_k, cache_v, wk, wv):
    n_pages = page_table.size
    n_phys = cache_k.shape[0]
    ck = cache_k.reshape(n_phys, CHUNKS_PER_PAGE, CHUNK_W)
    cv = cache_v.reshape(n_phys, CHUNKS_PER_PAGE, CHUNK_W)
    rows = FEAT_PAGES * CHUNKS_PER_PAGE
    wspec = pl.BlockSpec(wk.shape, lambda s, pt: (0, 0))
    ospec = pl.BlockSpec((rows, FEAT_W), lambda s, pt: (s, 0))
    return pl.pallas_call(
        _feats_gather_kernel,
        out_shape=(jax.ShapeDtypeStruct((n_pages * CHUNKS_PER_PAGE, FEAT_W), F32),) * 2,
        grid_spec=pltpu.PrefetchScalarGridSpec(
            num_scalar_prefetch=1,
            grid=(n_pages // FEAT_PAGES,),
            in_specs=[pl.BlockSpec(memory_space=pl.ANY), pl.BlockSpec(memory_space=pl.ANY), wspec, wspec],
            out_specs=(ospec, ospec),
            scratch_shapes=[pltpu.VMEM((2, rows, CHUNK_W), F32), pltpu.VMEM((2, rows, CHUNK_W), F32),
                            pltpu.SemaphoreType.DMA((2, 2))],
        ),
        compiler_params=pltpu.CompilerParams(dimension_semantics=("arbitrary",), vmem_limit_bytes=VMEM_LIMIT),
        name="feats_sample",
    )(page_table.reshape(-1), ck, cv, wk, wv)


def _cmp_finish_kernel(f_ref, w1_ref, pe_ref, w2_ref, g_ref, o_ref, *, nc, normalize):
    bias = _dot(pe_ref[...].astype(BF16), w1_ref[...])
    ba = bias[0:1, 0:HEAD_DIM]
    bb = bias[1:2, HEAD_DIM:2 * HEAD_DIM]
    row = _iota((nc, 1), 0)
    for g in range(N_KV):
        c0 = g * 2 * HEAD_DIM
        fa = f_ref[:, c0:c0 + HEAD_DIM] + ba
        fb = f_ref[:, c0 + HEAD_DIM:c0 + 2 * HEAD_DIM] + bb
        hid = _silu(fa + pltpu.roll(fb, nc - 1, 0))
        o = _dot(hid.astype(BF16), w2_ref[...])
        if normalize:
            o = _rms(o, g_ref[...])
        o_ref[:, g * HEAD_DIM:(g + 1) * HEAD_DIM] = jnp.where(row < nc - 1, o, 0.0)


def _cmp_finish(feats, w1cat, pe2, w2_bf, gain, nc, normalize):
    nb = feats.shape[0] // nc
    full = lambda shape: pl.BlockSpec(shape, lambda b: (0, 0))
    return pl.pallas_call(
        functools.partial(_cmp_finish_kernel, nc=nc, normalize=normalize),
        out_shape=jax.ShapeDtypeStruct((nb * nc, N_KV * HEAD_DIM), F32),
        grid=(nb,),
        in_specs=[pl.BlockSpec((nc, FEAT_W), lambda b: (b, 0)), full(w1cat.shape), full(pe2.shape),
                  full(w2_bf.shape), full((1, HEAD_DIM))],
        out_specs=pl.BlockSpec((nc, N_KV * HEAD_DIM), lambda b: (b, 0)),
        compiler_params=pltpu.CompilerParams(dimension_semantics=("arbitrary",), vmem_limit_bytes=VMEM_LIMIT),
        name="cmp_finish",
    )(feats, w1cat, pe2, w2_bf, gain.reshape(1, -1))


def _select_mask(imp, cur):
    shape = imp.shape
    j = _iota(shape, 1)
    jf = j.astype(F32)
    forced = (j == 0) | (j == cur) | (j == cur - 1)
    score0 = jnp.where(forced, 1e30, jnp.where(j <= cur, imp, -1.0))

    def body(_, carry):
        score, sel = carry
        m = jnp.max(score, axis=-1, keepdims=True)
        first = jnp.min(jnp.where(score == m, jf, 1e9), axis=-1, keepdims=True)
        hit = jf == first
        return jnp.where(hit, -2.0, score), jnp.where(hit, 1.0, sel)

    _, sel = lax.fori_loop(0, N_SELECT, body, (score0, jnp.zeros(shape, F32)))
    return sel


def _imp_matrix(n_cmp_pad, n_sel_pad, n_cmp, n_sel):
    r = SEL_BLOCK // CMP_STRIDE
    lead = CMP_BLOCK // CMP_STRIDE - 1
    m = np.zeros((n_cmp_pad, n_sel_pad), np.float32)
    for jb in range(n_sel):
        for o in range(-lead, r):
            start = o * CMP_STRIDE
            w = (min(start + CMP_BLOCK, SEL_BLOCK) - max(start, 0)) / CMP_BLOCK
            i = r * jb + o
            if 0 <= i < n_cmp:
                m[i, jb] += w
    return jnp.asarray(m)


def _cmp_select_kernel(q_ref, kc_ref, vc_ref, m_ref, oc_ref, sel_ref, *, tq, nc):
    i = pl.program_id(0)
    qpos = i * tq + _iota((tq, 1), 0)
    end = _iota((1, nc), 1) * CMP_STRIDE + (CMP_BLOCK - 1)
    dist = qpos - end
    mask = dist >= 0
    distf = dist.astype(F32)
    cur = qpos >> 6
    for g in range(N_KV):
        kg = kc_ref[:, g * HEAD_DIM:(g + 1) * HEAD_DIM].astype(BF16)
        vg = vc_ref[:, g * HEAD_DIM:(g + 1) * HEAD_DIM].astype(BF16)
        psum = jnp.zeros((tq, nc), F32)
        for r in range(GROUP):
            h = g * GROUP + r
            qh = q_ref[:, h * HEAD_DIM:(h + 1) * HEAD_DIM].astype(BF16)
            s = _dot_nt(qh, kg) * SCALE - SLOPES[h] * distf
            s = jnp.where(mask, s, NEG)
            m = jnp.max(s, axis=-1, keepdims=True)
            e = jnp.where(mask, jnp.exp(s - m), 0.0)
            p = e * (1.0 / jnp.maximum(jnp.sum(e, axis=-1, keepdims=True), 1e-30))
            psum = psum + p
            oc_ref[:, h * HEAD_DIM:(h + 1) * HEAD_DIM] = _dot(p.astype(BF16), vg)
        imp = jnp.dot(psum, m_ref[...], precision=lax.Precision.HIGHEST, preferred_element_type=F32)
        sel_ref[:, g * 128:(g + 1) * 128] = _select_mask(imp, cur)


def _cmp_select(z, kcmp, vcmp, imp_m, tq=256):
    t = z.shape[0]
    nc = kcmp.shape[0]
    n_sel = imp_m.shape[1]
    assert n_sel == 128
    full = lambda shape: pl.BlockSpec(shape, lambda i: (0, 0))
    return pl.pallas_call(
        functools.partial(_cmp_select_kernel, tq=tq, nc=nc),
        out_shape=(jax.ShapeDtypeStruct((t, N_HEADS * HEAD_DIM), F32), jax.ShapeDtypeStruct((t, N_KV * 128), F32)),
        grid=(t // tq,),
        in_specs=[pl.BlockSpec((tq, N_HEADS * HEAD_DIM), lambda i: (i, COL_Q // (N_HEADS * HEAD_DIM))),
                  full(kcmp.shape), full(vcmp.shape), full(imp_m.shape)],
        out_specs=(pl.BlockSpec((tq, N_HEADS * HEAD_DIM), lambda i: (i, 0)),
                   pl.BlockSpec((tq, N_KV * 128), lambda i: (i, 0))),
        compiler_params=pltpu.CompilerParams(dimension_semantics=("arbitrary",), vmem_limit_bytes=VMEM_LIMIT),
        name="cmp_select",
    )(z, kcmp, vcmp, imp_m)


def _expand_blocks(sel_bf, first_block, n_keys):
    n_blk = sel_bf.shape[1]
    e = (_iota((n_blk, n_keys), 0) == first_block + (_iota((n_blk, n_keys), 1) >> 6)).astype(BF16)
    return _dot(sel_bf, e)


def _flash_update(s, v_bf, m_ref, l_ref, acc_ref, rows):
    m_old = m_ref[rows, :]
    m_new = jnp.maximum(m_old, jnp.max(s, axis=-1, keepdims=True))
    alpha = jnp.exp(m_old - m_new)
    p = jnp.exp(s - m_new)
    l_ref[rows, :] = alpha * l_ref[rows, :] + jnp.sum(p, axis=-1, keepdims=True)
    acc_ref[rows, :] = alpha * acc_ref[rows, :] + _dot(p.astype(BF16), v_bf)
    m_ref[rows, :] = m_new


def _selected_prompt_kernel(q_ref, k_ref, v_ref, sel_ref, o_ref, qs, m_sc, l_sc, acc_sc, *, tq, tk):
    g = pl.program_id(0)
    qi = pl.program_id(1)
    ki = pl.program_id(2)

    @pl.when(ki == 0)
    def _():
        for r in range(GROUP):
            qs[r * tq:(r + 1) * tq, :] = q_ref[:, r * HEAD_DIM:(r + 1) * HEAD_DIM].astype(BF16)
        m_sc[...] = jnp.full(m_sc.shape, M_INIT, F32)
        l_sc[...] = jnp.zeros(l_sc.shape, F32)
        acc_sc[...] = jnp.zeros(acc_sc.shape, F32)

    @pl.when(ki * tk <= qi * tq + tq - 1)
    def _():
        k = k_ref[...].astype(BF16)
        v = v_ref[...].astype(BF16)
        selexp = _expand_blocks(sel_ref[...].astype(BF16), ki * (tk // SEL_BLOCK), tk)
        qpos = qi * tq + _iota((tq, 1), 0)
        kpos = ki * tk + _iota((1, tk), 1)
        mb = jnp.where((selexp > 0.5) & (kpos <= qpos), 0.0, NEG)
        krel = (kpos - qi * tq).astype(F32)
        for r in range(GROUP):
            slope = jnp.where(g == 0, SLOPES[r], SLOPES[GROUP + r])
            rows = slice(r * tq, (r + 1) * tq)
            s = _dot_nt(qs[rows, :], k) * SCALE + (mb + slope * krel)
            _flash_update(s, v, m_sc, l_sc, acc_sc, rows)

    @pl.when(ki == pl.num_programs(2) - 1)
    def _():
        for r in range(GROUP):
            rows = slice(r * tq, (r + 1) * tq)
            o_ref[:, r * HEAD_DIM:(r + 1) * HEAD_DIM] = acc_sc[rows, :] * (1.0 / l_sc[rows, :])


def _selected_prompt(z, sel, tq=256, tk=512):
    t = z.shape[0]
    gw = GROUP * HEAD_DIM
    kmax = lambda qi: (qi * tq + tq - 1) // tk
    return pl.pallas_call(
        functools.partial(_selected_prompt_kernel, tq=tq, tk=tk),
        out_shape=jax.ShapeDtypeStruct((t, N_HEADS * HEAD_DIM), F32),
        grid=(N_KV, t // tq, t // tk),
        in_specs=[
            pl.BlockSpec((tq, gw), lambda g, qi, ki: (qi, COL_Q // gw + g)),
            pl.BlockSpec((tk, HEAD_DIM), lambda g, qi, ki: (jnp.minimum(ki, kmax(qi)), COL_KS // HEAD_DIM + g)),
            pl.BlockSpec((tk, HEAD_DIM), lambda g, qi, ki: (jnp.minimum(ki, kmax(qi)), COL_VS // HEAD_DIM + g)),
            pl.BlockSpec((tq, 128), lambda g, qi, ki: (qi, g)),
        ],
        out_specs=pl.BlockSpec((tq, gw), lambda g, qi, ki: (qi, g)),
        scratch_shapes=[pltpu.VMEM((GROUP * tq, HEAD_DIM), BF16), pltpu.VMEM((GROUP * tq, 1), F32),
                        pltpu.VMEM((GROUP * tq, 1), F32), pltpu.VMEM((GROUP * tq, HEAD_DIM), F32)],
        compiler_params=pltpu.CompilerParams(
            dimension_semantics=("arbitrary", "arbitrary", "arbitrary"), vmem_limit_bytes=VMEM_LIMIT),
        name="selected_prompt",
    )(z, z, z, sel)


def _window_prompt_kernel(q_ref, kp_ref, kc_ref, vp_ref, vc_ref, o_ref, *, tq):
    g = pl.program_id(0)
    qi = pl.program_id(1)
    k = jnp.concatenate([kp_ref[...], kc_ref[...]], axis=0).astype(BF16)
    v = jnp.concatenate([vp_ref[...], vc_ref[...]], axis=0).astype(BF16)
    krel = _iota((1, 2 * tq), 1) - tq
    dist = _iota((tq, 1), 0) - krel
    mask = (dist >= 0) & (dist < WINDOW) & ((krel >= 0) | (qi > 0))
    distf = dist.astype(F32)
    for r in range(GROUP):
        slope = jnp.where(g == 0, SLOPES[r], SLOPES[GROUP + r])
        qh = q_ref[:, r * HEAD_DIM:(r + 1) * HEAD_DIM].astype(BF16)
        s = jnp.where(mask, _dot_nt(qh, k) * SCALE - slope * distf, NEG)
        m = jnp.max(s, axis=-1, keepdims=True)
        e = jnp.where(mask, jnp.exp(s - m), 0.0)
        p = e * (1.0 / jnp.maximum(jnp.sum(e, axis=-1, keepdims=True), 1e-30))
        o_ref[:, r * HEAD_DIM:(r + 1) * HEAD_DIM] = _dot(p.astype(BF16), v)


def _window_prompt(z, tq=WINDOW):
    t = z.shape[0]
    gw = GROUP * HEAD_DIM
    prev = lambda c: pl.BlockSpec((tq, HEAD_DIM), lambda g, qi: (jnp.maximum(qi - 1, 0), c // HEAD_DIM + g))
    cur = lambda c: pl.BlockSpec((tq, HEAD_DIM), lambda g, qi: (qi, c // HEAD_DIM + g))
    return pl.pallas_call(
        functools.partial(_window_prompt_kernel, tq=tq),
        out_shape=jax.ShapeDtypeStruct((t, N_HEADS * HEAD_DIM), F32),
        grid=(N_KV, t // tq),
        in_specs=[pl.BlockSpec((tq, gw), lambda g, qi: (qi, COL_Q // gw + g)),
                  prev(COL_KW), cur(COL_KW), prev(COL_VW), cur(COL_VW)],
        out_specs=pl.BlockSpec((tq, gw), lambda g, qi: (qi, g)),
        compiler_params=pltpu.CompilerParams(
            dimension_semantics=("arbitrary", "arbitrary"), vmem_limit_bytes=VMEM_LIMIT),
        name="window_prompt",
    )(z, z, z, z, z)


SEL_PAD = 384


def _slope_col(g, ds):
    return jnp.concatenate([jnp.full((ds, 1), SLOPES[g * GROUP + r], F32) for r in range(GROUP)], axis=0)


def _stack_heads(q_ref, g):
    return jnp.concatenate([q_ref[:, (g * GROUP + r) * HEAD_DIM:(g * GROUP + r + 1) * HEAD_DIM]
                            for r in range(GROUP)], axis=0).astype(BF16)


def _pad_rows(a, n):
    return jnp.concatenate([a, jnp.zeros((n - a.shape[0], a.shape[1]), a.dtype)], axis=0)


def _sample_small_kernel(q_ref, kc_ref, vc_ref, m_ref, sk_ref, sv_ref, new_ref, oc_ref, sel_ref, ow_ref,
                         *, ds, nc, past, wb):
    rows = GROUP * ds
    qidx = _iota((rows, 1), 0) & (ds - 1)
    spos = past + qidx
    end = _iota((1, nc), 1) * CMP_STRIDE + (CMP_BLOCK - 1)
    dist_c = spos - end
    mask_c = dist_c >= 0
    cur = (past + _iota((ds, 1), 0)) >> 6
    ist = _iota((1, wb), 1)
    dist_s = wb + qidx - ist
    mask_s = dist_s < WINDOW
    jn = _iota((1, 128), 1)
    dist_n = qidx - jn
    mask_n = (dist_n >= 0) & (jn < ds)
    for g in range(N_KV):
        lanes = slice(g * HEAD_DIM, (g + 1) * HEAD_DIM)
        qs = _stack_heads(q_ref, g)
        slope = _slope_col(g, ds)
        s = _dot_nt(qs, kc_ref[:, lanes].astype(BF16)) * SCALE - slope * dist_c.astype(F32)
        s = jnp.where(mask_c, s, NEG)
        m = jnp.max(s, axis=-1, keepdims=True)
        e = jnp.where(mask_c, jnp.exp(s - m), 0.0)
        p = e * (1.0 / jnp.maximum(jnp.sum(e, axis=-1, keepdims=True), 1e-30))
        o = _dot(p.astype(BF16), vc_ref[:, lanes].astype(BF16))
        psum = p[0:ds]
        for r in range(1, GROUP):
            psum = psum + p[r * ds:(r + 1) * ds]
        imp = jnp.dot(psum, m_ref[...], precision=lax.Precision.HIGHEST, preferred_element_type=F32)
        sel_ref[:, g * SEL_PAD:(g + 1) * SEL_PAD] = _select_mask(imp, cur)
        kn = _pad_rows(new_ref[:, lanes], 128).astype(BF16)
        vn = _pad_rows(new_ref[:, N_KV * HEAD_DIM + g * HEAD_DIM:N_KV * HEAD_DIM + (g + 1) * HEAD_DIM], 128)
        s1 = jnp.where(mask_s, _dot_nt(qs, sk_ref[:, lanes].astype(BF16)) * SCALE - slope * dist_s.astype(F32), NEG)
        s2 = jnp.where(mask_n, _dot_nt(qs, kn) * SCALE - slope * dist_n.astype(F32), NEG)
        mw = jnp.maximum(jnp.max(s1, axis=-1, keepdims=True), jnp.max(s2, axis=-1, keepdims=True))
        e1 = jnp.where(mask_s, jnp.exp(s1 - mw), 0.0)
        e2 = jnp.where(mask_n, jnp.exp(s2 - mw), 0.0)
        inv = 1.0 / jnp.maximum(jnp.sum(e1, axis=-1, keepdims=True) + jnp.sum(e2, axis=-1, keepdims=True), 1e-30)
        w = _dot((e1 * inv).astype(BF16), sv_ref[:, lanes].astype(BF16)) + _dot((e2 * inv).astype(BF16),
                                                                                vn.astype(BF16))
        for r in range(GROUP):
            h = g * GROUP + r
            oc_ref[:, h * HEAD_DIM:(h + 1) * HEAD_DIM] = o[r * ds:(r + 1) * ds]
            ow_ref[:, h * HEAD_DIM:(h + 1) * HEAD_DIM] = w[r * ds:(r + 1) * ds]


def _sample_small(z2, kcmp2, vcmp2, imp_m2, state_k, state_v, nb, past):
    t = z2.shape[0]
    ds = t // nb
    nc = kcmp2.shape[0] // nb
    wb = state_k.shape[0] // nb
    qw = N_HEADS * HEAD_DIM
    kvw = N_KV * HEAD_DIM
    return pl.pallas_call(
        functools.partial(_sample_small_kernel, ds=ds, nc=nc, past=past, wb=wb),
        out_shape=(jax.ShapeDtypeStruct((t, qw), F32), jax.ShapeDtypeStruct((t, N_KV * SEL_PAD), F32),
                   jax.ShapeDtypeStruct((t, qw), F32)),
        grid=(nb,),
        in_specs=[pl.BlockSpec((ds, qw), lambda b: (b, COL_Q // qw)),
                  pl.BlockSpec((nc, kvw), lambda b: (b, 0)), pl.BlockSpec((nc, kvw), lambda b: (b, 0)),
                  pl.BlockSpec(imp_m2.shape, lambda b: (0, 0)),
                  pl.BlockSpec((wb, kvw), lambda b: (b, 0)), pl.BlockSpec((wb, kvw), lambda b: (b, 0)),
                  pl.BlockSpec((ds, TN), lambda b: (b, TILE_WIN))],
        out_specs=(pl.BlockSpec((ds, qw), lambda b: (b, 0)), pl.BlockSpec((ds, N_KV * SEL_PAD), lambda b: (b, 0)),
                   pl.BlockSpec((ds, qw), lambda b: (b, 0))),
        compiler_params=pltpu.CompilerParams(dimension_semantics=("arbitrary",), vmem_limit_bytes=VMEM_LIMIT),
        name="sample_small",
    )(z2, kcmp2, vcmp2, imp_m2, state_k, state_v, z2)


SLC_PAGES = 16


def _selected_sample_kernel(pt_ref, ck_hbm, cv_hbm, q_ref, sel_ref, new_ref, o_ref,
                            kbuf, vbuf, sem, m_sc, l_sc, acc_sc, *, ds, steps, past):
    slot = _gather_pipeline(pt_ref, (ck_hbm, cv_hbm), (kbuf, vbuf), sem, SLC_PAGES, PAGE_SIZE)
    kt = pl.program_id(0) % steps
    rows = GROUP * ds
    nk = SLC_PAGES * PAGE_SIZE
    qidx = _iota((rows, 1), 0) & (ds - 1)

    @pl.when(kt == 0)
    def _():
        m_sc[...] = jnp.full(m_sc.shape, M_INIT, F32)
        l_sc[...] = jnp.zeros(l_sc.shape, F32)
        acc_sc[...] = jnp.zeros(acc_sc.shape, F32)

    krel = (kt * nk - past + _iota((1, nk), 1)).astype(F32)
    for g in range(N_KV):
        lanes = slice(g * HEAD_DIM, (g + 1) * HEAD_DIM)
        grows = slice(g * rows, (g + 1) * rows)
        qs = _stack_heads(q_ref, g)
        slope = _slope_col(g, ds)
        selexp = _expand_blocks(sel_ref[:, g * SEL_PAD:(g + 1) * SEL_PAD].astype(BF16), kt * (nk // SEL_BLOCK), nk)
        mb = jnp.where(jnp.concatenate([selexp] * GROUP, axis=0) > 0.5, 0.0, NEG)
        s = _dot_nt(qs, kbuf[slot, :, lanes].astype(BF16)) * SCALE + (mb + slope * krel)
        _flash_update(s, vbuf[slot, :, lanes].astype(BF16), m_sc, l_sc, acc_sc, grows)

    @pl.when(kt == steps - 1)
    def _():
        jn = _iota((1, 128), 1)
        mb_n = jnp.where((jn <= qidx) & (jn < ds), 0.0, NEG)
        for g in range(N_KV):
            lanes = slice(g * HEAD_DIM, (g + 1) * HEAD_DIM)
            grows = slice(g * rows, (g + 1) * rows)
            qs = _stack_heads(q_ref, g)
            kn = _pad_rows(new_ref[:, lanes], 128).astype(BF16)
            vn = _pad_rows(new_ref[:, N_KV * HEAD_DIM + g * HEAD_DIM:N_KV * HEAD_DIM + (g + 1) * HEAD_DIM], 128)
            s = _dot_nt(qs, kn) * SCALE + (mb_n + _slope_col(g, ds) * jn.astype(F32))
            _flash_update(s, vn.astype(BF16), m_sc, l_sc, acc_sc, grows)
            o = acc_sc[grows, :] * (1.0 / l_sc[grows, :])
            for r in range(GROUP):
                h = g * GROUP + r
                o_ref[:, h * HEAD_DIM:(h + 1) * HEAD_DIM] = o[r * ds:(r + 1) * ds]


def _selected_sample(page_table, cache_k, cache_v, z2, sel2, nb, past):
    t = z2.shape[0]
    ds = t // nb
    n_phys = cache_k.shape[0]
    kvw = N_KV * HEAD_DIM
    ck = cache_k.reshape(n_phys, PAGE_SIZE, kvw)
    cv = cache_v.reshape(n_phys, PAGE_SIZE, kvw)
    steps = page_table.shape[1] // SLC_PAGES
    qw = N_HEADS * HEAD_DIM
    nk = SLC_PAGES * PAGE_SIZE
    rows = N_KV * GROUP * ds
    return pl.pallas_call(
        functools.partial(_selected_sample_kernel, ds=ds, steps=steps, past=past),
        out_shape=jax.ShapeDtypeStruct((t, qw), F32),
        grid_spec=pltpu.PrefetchScalarGridSpec(
            num_scalar_prefetch=1,
            grid=(nb * steps,),
            in_specs=[pl.BlockSpec(memory_space=pl.ANY), pl.BlockSpec(memory_space=pl.ANY),
                      pl.BlockSpec((ds, qw), lambda s, pt: (s // steps, COL_Q // qw)),
                      pl.BlockSpec((ds, N_KV * SEL_PAD), lambda s, pt: (s // steps, 0)),
                      pl.BlockSpec((ds, TN), lambda s, pt: (s // steps, TILE_SLC))],
            out_specs=pl.BlockSpec((ds, qw), lambda s, pt: (s // steps, 0)),
            scratch_shapes=[pltpu.VMEM((2, nk, kvw), F32), pltpu.VMEM((2, nk, kvw), F32),
                            pltpu.SemaphoreType.DMA((2, 2)),
                            pltpu.VMEM((rows, 1), F32), pltpu.VMEM((rows, 1), F32),
                            pltpu.VMEM((rows, HEAD_DIM), F32)],
        ),
        compiler_params=pltpu.CompilerParams(dimension_semantics=("arbitrary",), vmem_limit_bytes=VMEM_LIMIT),
        name="selected_sample",
    )(page_table.reshape(-1), ck, cv, z2, sel2, z2)


def _merge_kernel(x_ref, conv_ref, oc_ref, os_ref, ow_ref, gt_ref, zn_ref, w_ref, y_ref):
    gt = gt_ref[...]
    parts = [conv_ref[...].astype(BF16)]
    for h in range(N_HEADS):
        lanes = slice(h * HEAD_DIM, (h + 1) * HEAD_DIM)
        o = (gt[:, 3 * h:3 * h + 1] * oc_ref[:, lanes] + gt[:, 3 * h + 1:3 * h + 2] * os_ref[:, lanes]
             + gt[:, 3 * h + 2:3 * h + 3] * ow_ref[:, lanes])
        parts.append((o * _silu(zn_ref[:, lanes])).astype(BF16))
    y_ref[...] = x_ref[...] + _dot(jnp.concatenate(parts, axis=1), w_ref[...])


def _merge(x, conv_o, o_c, o_s, o_w, z, w_out_bf, tm=256):
    t, d = x.shape
    qw = N_HEADS * HEAD_DIM
    rowblk = lambda w: pl.BlockSpec((tm, w), lambda i: (i, 0))
    return pl.pallas_call(
        _merge_kernel,
        out_shape=jax.ShapeDtypeStruct((t, d), F32),
        grid=(t // tm,),
        in_specs=[rowblk(d), rowblk(C_CONV), rowblk(qw), rowblk(qw), rowblk(qw),
                  pl.BlockSpec((tm, 128), lambda i: (i, COL_GT // 128)),
                  pl.BlockSpec((tm, qw), lambda i: (i, COL_ZN // qw)),
                  pl.BlockSpec(w_out_bf.shape, lambda i: (0, 0))],
        out_specs=rowblk(d),
        compiler_params=pltpu.CompilerParams(dimension_semantics=("arbitrary",), vmem_limit_bytes=VMEM_LIMIT),
        name="merge",
    )(x, conv_o, o_c, o_s, o_w, z, z, w_out_bf)


def _pad_w_in(w_in):
    d = w_in.shape[0]
    gt0 = 3 * C_CONV + N_HEADS * HEAD_DIM + 6 * N_KV * HEAD_DIM
    n_gt = 3 * N_HEADS
    pieces = [w_in[:, :gt0], w_in[:, gt0:gt0 + n_gt], jnp.zeros((d, COL_ZN - COL_GT - n_gt), w_in.dtype),
              w_in[:, gt0 + n_gt:]]
    return jnp.concatenate(pieces, axis=1).astype(BF16)


def _cmp_weights(pe, w1, w2):
    half = CMP_STRIDE
    k = half * HEAD_DIM
    w1cat = jnp.concatenate([w1[:half].reshape(k, -1), w1[half:].reshape(k, -1)], axis=1).astype(BF16)
    pe2 = jnp.concatenate([pe[:half].reshape(1, k), pe[half:].reshape(1, k), jnp.zeros((6, k), pe.dtype)], axis=0)
    return w1cat, pe2, w2.astype(BF16)


def kernel(x_prompt, x_sample, cache_k_cmp, cache_v_cmp, cache_k_slc, cache_v_slc, state_k_win, state_v_win,
           state_conv, page_table, g_norm, w_in, pe_cmp_k, w_cmp_k1, w_cmp_k2, pe_cmp_v, w_cmp_v1, w_cmp_v2,
           g_q, g_k_cmp, g_k_slc, g_k_win, w_dw, b_dw, ln_g, ln_b, w_pw2, b_pw2, w_out):
    _, t, d = x_prompt.shape
    db, ds, _ = x_sample.shape
    past = page_table.shape[1] * PAGE_SIZE
    wb = state_k_win.shape[1]
    kvw = N_KV * HEAD_DIM
    assert ds < CMP_STRIDE and wb == WINDOW and t % SEL_BLOCK == 0 and t // SEL_BLOCK == 128

    w_p = _pad_w_in(w_in)
    w_pw_bf = w_pw2.astype(BF16)
    w_out_bf = w_out.astype(BF16)
    wk1, pek, wk2 = _cmp_weights(pe_cmp_k, w_cmp_k1, w_cmp_k2)
    wv1, pev, wv2 = _cmp_weights(pe_cmp_v, w_cmp_v1, w_cmp_v2)
    ones = jnp.ones((HEAD_DIM,), F32)

    xp = x_prompt.reshape(t, d)
    z = _in_proj(xp, g_norm, w_p, g_q, g_k_slc, g_k_win, tm=512)
    conv_o, conv_st = _conv_prompt(z, w_dw, b_dw, ln_g, ln_b, w_pw_bf, b_pw2)
    nc = t // CMP_STRIDE
    kc = z[:, COL_KC:COL_KC + kvw]
    vc = z[:, COL_VC:COL_VC + kvw]
    fk, fv = _feats_prompt(kc.reshape(nc, CHUNK_W), vc.reshape(nc, CHUNK_W), wk1, wv1)
    kcmp = _cmp_finish(fk, wk1, pek, wk2, g_k_cmp, nc, True)
    vcmp = _cmp_finish(fv, wv1, pev, wv2, ones, nc, False)
    imp_m = _imp_matrix(nc, t // SEL_BLOCK, nc - 1, t // SEL_BLOCK)
    o_c, sel = _cmp_select(z, kcmp, vcmp, imp_m)
    o_s = _selected_prompt(z, sel)
    o_w = _window_prompt(z)
    y_prompt = _merge(xp, conv_o, o_c, o_s, o_w, z, w_out_bf)

    xs = x_sample.reshape(db * ds, d)
    z2 = _in_proj(xs, g_norm, w_p, g_q, g_k_slc, g_k_win, tm=db * ds)
    conv_o2, u2 = _conv_sample(z2, state_conv, w_dw, b_dw, ln_g, ln_b, w_pw_bf, b_pw2)
    nc2 = past // CMP_STRIDE
    fk2, fv2 = _feats_sample(page_table, cache_k_cmp, cache_v_cmp, wk1, wv1)
    kcmp2 = _cmp_finish(fk2, wk1, pek, wk2, g_k_cmp, nc2, True)
    vcmp2 = _cmp_finish(fv2, wv1, pev, wv2, ones, nc2, False)
    n_sel2 = -(-(past + ds) // SEL_BLOCK)
    assert n_sel2 <= SEL_PAD
    imp_m2 = _imp_matrix(nc2, SEL_PAD, nc2 - 1, n_sel2)
    o_c2, sel2, o_w2 = _sample_small(z2, kcmp2, vcmp2, imp_m2, state_k_win.reshape(db * wb, kvw),
                                     state_v_win.reshape(db * wb, kvw), db, past)
    o_s2 = _selected_sample(page_table, cache_k_slc, cache_v_slc, z2, sel2, db, past)
    y_sample = _merge(xs, conv_o2, o_c2, o_s2, o_w2, z2, w_out_bf)

    kv4 = lambda a, b: a.reshape(b, -1, N_KV, HEAD_DIM)
    cols = lambda a, c: a[:, c:c + kvw]
    kw2 = kv4(cols(z2, COL_KW), db)
    vw2 = kv4(cols(z2, COL_VW), db)
    return (y_prompt.reshape(1, t, d), y_sample.reshape(db, ds, d),
            kv4(kc, 1), kv4(vc, 1), kv4(cols(z, COL_KS), 1), kv4(cols(z, COL_VS), 1),
            kv4(cols(z, COL_KW), 1)[:, t - wb:], kv4(cols(z, COL_VW), 1)[:, t - wb:],
            conv_st[HALO - (CONV_WIDTH - 1):][None],
            kv4(cols(z2, COL_KC), db), kv4(cols(z2, COL_VC), db), kv4(cols(z2, COL_KS), db), kv4(cols(z2, COL_VS), db),
            jnp.concatenate([state_k_win[:, ds:], kw2], axis=1), jnp.concatenate([state_v_win[:, ds:], vw2], axis=1),
            jnp.concatenate([state_conv[:, ds:], u2.reshape(db, ds, C_CONV)], axis=1))
```

```python
import functools

import numpy as np
import jax
import jax.numpy as jnp
from jax import lax
from jax.experimental import pallas as pl
from jax.experimental.pallas import tpu as pltpu

F32 = jnp.float32
BF16 = jnp.bfloat16

HEAD_DIM = 128
N_HEADS = 8
N_KV = 2
GROUP = 4
C_CONV = 1024
CONV_WIDTH = 31
CMP_STRIDE = 16
CMP_BLOCK = 32
SEL_BLOCK = 64
N_SELECT = 16
WINDOW = 512
PAGE_SIZE = 128
EPS = 1e-6
SCALE = HEAD_DIM ** -0.5
SLOPES = tuple(2.0 ** -(h + 1) for h in range(N_HEADS))

TN = 512
N_TILES = 14
ZW = TN * N_TILES
COL_UA, COL_UB, COL_ZC, COL_Q = 0, 1024, 2048, 3072
COL_KC, COL_VC, COL_KS, COL_VS, COL_KW, COL_VW = 4096, 4352, 4608, 4864, 5120, 5376
COL_GT, COL_ZN = 5632, 6144
TILE_Q0, TILE_Q1, TILE_SLC, TILE_WIN, TILE_GT = 6, 7, 9, 10, 11

NEG = -1e30
M_INIT = -1e29
VMEM_LIMIT = 48 * 1024 * 1024


def _sigmoid(x):
    return 1.0 / (1.0 + jnp.exp(-x))


def _silu(x):
    return x * _sigmoid(x)


def _dot(a, b):
    return jnp.dot(a, b, preferred_element_type=F32)


def _dot_nt(a, b):
    return lax.dot_general(a, b, (((1,), (1,)), ((), ())), preferred_element_type=F32)


def _rms(a, g):
    return a * lax.rsqrt(jnp.mean(a * a, axis=-1, keepdims=True) + EPS) * g


def _iota(shape, dim):
    return lax.broadcasted_iota(jnp.int32, shape, dim)


def _inproj_kernel(x_ref, gn_ref, w_ref, gq_ref, gks_ref, gkw_ref, z_ref, xn_ref):
    j = pl.program_id(1)

    @pl.when(j == 0)
    def _():
        x = x_ref[...]
        ms = jnp.mean(x * x, axis=-1, keepdims=True)
        xn_ref[...] = (x * lax.rsqrt(ms + EPS) * gn_ref[...]).astype(BF16)

    acc = _dot(xn_ref[...], w_ref[...])

    def normed(g_ref, n):
        parts = []
        for c in range(TN // HEAD_DIM):
            a = acc[:, c * HEAD_DIM:(c + 1) * HEAD_DIM]
            parts.append(_rms(a, g_ref[...]) if c < n else a)
        return jnp.concatenate(parts, axis=1)

    is_q = (j == TILE_Q0) | (j == TILE_Q1)
    is_slc = j == TILE_SLC
    is_win = j == TILE_WIN
    is_gt = j == TILE_GT

    @pl.when(is_q)
    def _():
        z_ref[...] = normed(gq_ref, 4)

    @pl.when(is_slc)
    def _():
        z_ref[...] = normed(gks_ref, 2)

    @pl.when(is_win)
    def _():
        z_ref[...] = normed(gkw_ref, 2)

    @pl.when(is_gt)
    def _():
        z_ref[...] = _sigmoid(acc)

    @pl.when(jnp.logical_not(is_q | is_slc | is_win | is_gt))
    def _():
        z_ref[...] = acc


def _in_proj(x, g_norm, w_p, g_q, g_ks, g_kw, tm):
    t, d = x.shape
    row = lambda a: a.reshape(1, -1)
    return pl.pallas_call(
        _inproj_kernel,
        out_shape=jax.ShapeDtypeStruct((t, ZW), F32),
        grid=(t // tm, N_TILES),
        in_specs=[
            pl.BlockSpec((tm, d), lambda i, j: (i, 0)),
            pl.BlockSpec((1, d), lambda i, j: (0, 0)),
            pl.BlockSpec((d, TN), lambda i, j: (0, j)),
            pl.BlockSpec((1, HEAD_DIM), lambda i, j: (0, 0)),
            pl.BlockSpec((1, HEAD_DIM), lambda i, j: (0, 0)),
            pl.BlockSpec((1, HEAD_DIM), lambda i, j: (0, 0)),
        ],
        out_specs=pl.BlockSpec((tm, TN), lambda i, j: (i, j)),
        scratch_shapes=[pltpu.VMEM((tm, d), BF16)],
        compiler_params=pltpu.CompilerParams(
            dimension_semantics=("arbitrary", "arbitrary"), vmem_limit_bytes=VMEM_LIMIT),
        name="in_proj",
    )(x, row(g_norm), w_p, row(g_q), row(g_ks), row(g_kw))


HALO = 32
CONV_RB = 32
CONV_CB = 256


def _conv_tail(y, zc, lng_ref, lnb_ref, wpw_ref, bpw_ref):
    mu = jnp.mean(y, axis=-1, keepdims=True)
    yc = y - mu
    var = jnp.mean(yc * yc, axis=-1, keepdims=True)
    yn = yc * lax.rsqrt(var + EPS) * lng_ref[...] + lnb_ref[...]
    act = _silu(yn).astype(BF16)
    return (_dot(act, wpw_ref[...]) + bpw_ref[...]) * _silu(zc)


def _conv_prompt_kernel(ua_ref, ub_ref, zc_ref, uah_ref, ubh_ref, wdw_ref, bdw_ref, lng_ref, lnb_ref,
                        wpw_ref, bpw_ref, o_ref, st_ref, buf, ybuf, *, tt):
    i = pl.program_id(0)
    uh = uah_ref[...] * _sigmoid(ubh_ref[...])
    buf[0:HALO, :] = jnp.where(i > 0, uh, 0.0)
    buf[HALO:HALO + tt, :] = ua_ref[...] * _sigmoid(ub_ref[...])
    off = HALO - (CONV_WIDTH - 1)
    for c0 in range(0, C_CONV, CONV_CB):
        for r0 in range(0, tt, CONV_RB):
            acc = jnp.broadcast_to(bdw_ref[:, c0:c0 + CONV_CB], (CONV_RB, CONV_CB))
            for k in range(CONV_WIDTH):
                acc = acc + wdw_ref[k:k + 1, c0:c0 + CONV_CB] * buf[r0 + off + k:r0 + off + k + CONV_RB,
                                                                     c0:c0 + CONV_CB]
            ybuf[r0:r0 + CONV_RB, c0:c0 + CONV_CB] = acc
    o_ref[...] = _conv_tail(ybuf[...], zc_ref[...], lng_ref, lnb_ref, wpw_ref, bpw_ref)

    @pl.when(i == pl.num_programs(0) - 1)
    def _():
        st_ref[...] = buf[tt:tt + HALO, :]


def _conv_prompt(z, w_dw, b_dw, ln_g, ln_b, w_pw_bf, b_pw, tt=256):
    t = z.shape[0]
    row = lambda a: a.reshape(1, -1)
    hb = tt // HALO
    cur = lambda c: pl.BlockSpec((tt, C_CONV), lambda i: (i, c))
    halo = lambda c: pl.BlockSpec((HALO, C_CONV), lambda i: (jnp.maximum(i * hb - 1, 0), c))
    full = lambda shape: pl.BlockSpec(shape, lambda i: (0, 0))
    return pl.pallas_call(
        functools.partial(_conv_prompt_kernel, tt=tt),
        out_shape=(jax.ShapeDtypeStruct((t, C_CONV), F32), jax.ShapeDtypeStruct((HALO, C_CONV), F32)),
        grid=(t // tt,),
        in_specs=[cur(0), cur(1), cur(2), halo(0), halo(1),
                  full((CONV_WIDTH, C_CONV)), full((1, C_CONV)), full((1, C_CONV)), full((1, C_CONV)),
                  full((C_CONV, C_CONV)), full((1, C_CONV))],
        out_specs=(pl.BlockSpec((tt, C_CONV), lambda i: (i, 0)), pl.BlockSpec((HALO, C_CONV), lambda i: (0, 0))),
        scratch_shapes=[pltpu.VMEM((HALO + tt, C_CONV), F32), pltpu.VMEM((tt, C_CONV), F32)],
        compiler_params=pltpu.CompilerParams(dimension_semantics=("arbitrary",), vmem_limit_bytes=VMEM_LIMIT),
        name="conv_prompt",
    )(z, z, z, z, z, w_dw, row(b_dw), row(ln_g), row(ln_b), w_pw_bf, row(b_pw))


ST_ROWS = 40


def _conv_sample_kernel(ua_ref, ub_ref, zc_ref, st_ref, wdw_ref, bdw_ref, lng_ref, lnb_ref, wpw_ref, bpw_ref,
                        o_ref, u_ref, fbuf, ybuf, *, nb, ds):
    u = ua_ref[...] * _sigmoid(ub_ref[...])
    u_ref[...] = u
    rows = _iota((nb * ds, 1), 0) & (ds - 1)
    acc_u = jnp.broadcast_to(bdw_ref[...], (nb * ds, C_CONV))
    for d in range(ds):
        sh = u if d == 0 else pltpu.roll(u, d, 0)
        acc_u = acc_u + jnp.where(rows >= d, sh, 0.0) * wdw_ref[CONV_WIDTH - 1 - d:CONV_WIDTH - d, :]
    ybuf[...] = acc_u
    fbuf[:, 24:ST_ROWS, :] = jnp.zeros((nb, ST_ROWS - 24, C_CONV), F32)
    fbuf[:, 0:CONV_WIDTH - 1, :] = st_ref[...]

    def body(b, carry):
        acc = jnp.zeros((ds, C_CONV), F32)
        for k in range(CONV_WIDTH - 1):
            acc = acc + wdw_ref[k:k + 1, :] * fbuf[b, k:k + ds, :]
        r = pl.multiple_of(b * ds, ds)
        ybuf[pl.ds(r, ds), :] = ybuf[pl.ds(r, ds), :] + acc
        return carry

    lax.fori_loop(0, nb, body, 0)
    o_ref[...] = _conv_tail(ybuf[...], zc_ref[...], lng_ref, lnb_ref, wpw_ref, bpw_ref)


def _conv_sample(z2, state_conv, w_dw, b_dw, ln_g, ln_b, w_pw_bf, b_pw):
    nb, sw, _ = state_conv.shape
    t = z2.shape[0]
    ds = t // nb
    assert sw == CONV_WIDTH - 1 and ds == 8
    row = lambda a: a.reshape(1, -1)
    col = lambda c: pl.BlockSpec((t, C_CONV), lambda i: (0, c))
    full = lambda shape: pl.BlockSpec(shape, lambda i: (0,) * len(shape))
    return pl.pallas_call(
        functools.partial(_conv_sample_kernel, nb=nb, ds=ds),
        out_shape=(jax.ShapeDtypeStruct((t, C_CONV), F32), jax.ShapeDtypeStruct((t, C_CONV), F32)),
        grid=(1,),
        in_specs=[col(0), col(1), col(2), full((nb, sw, C_CONV)),
                  full((CONV_WIDTH, C_CONV)), full((1, C_CONV)), full((1, C_CONV)), full((1, C_CONV)),
                  full((C_CONV, C_CONV)), full((1, C_CONV))],
        out_specs=(full((t, C_CONV)), full((t, C_CONV))),
        scratch_shapes=[pltpu.VMEM((nb, ST_ROWS, C_CONV), F32), pltpu.VMEM((t, C_CONV), F32)],
        compiler_params=pltpu.CompilerParams(dimension_semantics=("arbitrary",), vmem_limit_bytes=VMEM_LIMIT),
        name="conv_sample",
    )(z2, z2, z2, state_conv, w_dw, row(b_dw), row(ln_g), row(ln_b), w_pw_bf, row(b_pw))


CHUNK_W = CMP_STRIDE * N_KV * HEAD_DIM
FEAT_W = 2 * N_KV * HEAD_DIM


N_FEAT = FEAT_W // HEAD_DIM


def _chunk_feats(piece, w):
    outs = []
    for g in range(N_KV):
        xg = jnp.concatenate([piece(c, g).astype(BF16) for c in range(CMP_STRIDE)], axis=1)
        outs.append(_dot(xg, w))
    return jnp.concatenate(outs, axis=1)


def _flat_piece(x_ref):
    return lambda c, g: x_ref[:, (c * N_KV + g) * HEAD_DIM:(c * N_KV + g + 1) * HEAD_DIM]


def _feats_kernel(xk_ref, xv_ref, wk_ref, wv_ref, fk_ref, fv_ref):
    for x_ref, w_ref, f_ref in ((xk_ref, wk_ref, fk_ref), (xv_ref, wv_ref, fv_ref)):
        f = _chunk_feats(_flat_piece(x_ref), w_ref[...])
        for cb in range(N_FEAT):
            f_ref[cb] = f[:, cb * HEAD_DIM:(cb + 1) * HEAD_DIM]


def _feats_prompt(xk, xv, wk, wv, tm=128):
    nc = xk.shape[0]
    full = lambda shape: pl.BlockSpec(shape, lambda i: (0, 0))
    rows = lambda w: pl.BlockSpec((tm, w), lambda i: (i, 0))
    ospec = pl.BlockSpec((N_FEAT, tm, HEAD_DIM), lambda i: (0, i, 0))
    return pl.pallas_call(
        _feats_kernel,
        out_shape=(jax.ShapeDtypeStruct((N_FEAT, nc, HEAD_DIM), F32),) * 2,
        grid=(nc // tm,),
        in_specs=[rows(CHUNK_W), rows(CHUNK_W), full(wk.shape), full(wv.shape)],
        out_specs=(ospec, ospec),
        compiler_params=pltpu.CompilerParams(dimension_semantics=("arbitrary",), vmem_limit_bytes=VMEM_LIMIT),
        name="feats_prompt",
    )(xk, xv, wk, wv)


FEAT_PAGES = 32
CHUNKS_PER_PAGE = PAGE_SIZE // CMP_STRIDE
PAGE_ROWS = PAGE_SIZE * N_KV
FEAT_PITCH = PAGE_ROWS + 8


def _page_copies(pt_ref, srcs, bufs, sem, step, slot, pages, pitch):
    out = []
    for p in range(pages):
        page = pt_ref[step * pages + p]
        for n, (src, buf) in enumerate(zip(srcs, bufs)):
            out.append(pltpu.make_async_copy(src.at[page], buf.at[slot, pl.ds(p * pitch, PAGE_ROWS)], sem.at[n, slot]))
    return out


def _gather_pipeline(pt_ref, srcs, bufs, sem, pages, pitch):
    s = pl.program_id(0)
    slot = s % 2

    @pl.when(s == 0)
    def _():
        for c in _page_copies(pt_ref, srcs, bufs, sem, 0, 0, pages, pitch):
            c.start()

    @pl.when(s + 1 < pl.num_programs(0))
    def _():
        for c in _page_copies(pt_ref, srcs, bufs, sem, s + 1, 1 - slot, pages, pitch):
            c.start()

    for c in _page_copies(pt_ref, srcs, bufs, sem, s, slot, pages, pitch):
        c.wait()
    return slot


def _feats_gather_kernel(pt_ref, ck_hbm, cv_hbm, wk_ref, wv_ref, fk_ref, fv_ref, kbuf, vbuf, sem):
    slot = _gather_pipeline(pt_ref, (ck_hbm, cv_hbm), (kbuf, vbuf), sem, FEAT_PAGES, FEAT_PITCH)
    crows = CMP_STRIDE * N_KV

    def piece(buf):
        return lambda c, g: jnp.concatenate(
            [buf[slot, pl.ds(n * crows + c * N_KV + g, FEAT_PAGES, stride=FEAT_PITCH), :]
             for n in range(CHUNKS_PER_PAGE)], axis=0)

    for buf, w_ref, f_ref in ((kbuf, wk_ref, fk_ref), (vbuf, wv_ref, fv_ref)):
        f = _chunk_feats(piece(buf), w_ref[...])
        for n in range(CHUNKS_PER_PAGE):
            for cb in range(N_FEAT):
                f_ref[cb, pl.ds(n, FEAT_PAGES, stride=CHUNKS_PER_PAGE), :] = (
                    f[n * FEAT_PAGES:(n + 1) * FEAT_PAGES, cb * HEAD_DIM:(cb + 1) * HEAD_DIM])


def _feats_sample(page_table, cache_k, cache_v, wk, wv):
    n_pages = page_table.size
    n_phys = cache_k.shape[0]
    ck = cache_k.reshape(n_phys, PAGE_ROWS, HEAD_DIM)
    cv = cache_v.reshape(n_phys, PAGE_ROWS, HEAD_DIM)
    rows = FEAT_PAGES * CHUNKS_PER_PAGE
    brows = FEAT_PAGES * FEAT_PITCH
    wspec = pl.BlockSpec(wk.shape, lambda s, pt: (0, 0))
    ospec = pl.BlockSpec((N_FEAT, rows, HEAD_DIM), lambda s, pt: (0, s, 0))
    return pl.pallas_call(
        _feats_gather_kernel,
        out_shape=(jax.ShapeDtypeStruct((N_FEAT, n_pages * CHUNKS_PER_PAGE, HEAD_DIM), F32),) * 2,
        grid_spec=pltpu.PrefetchScalarGridSpec(
            num_scalar_prefetch=1,
            grid=(n_pages // FEAT_PAGES,),
            in_specs=[pl.BlockSpec(memory_space=pl.ANY), pl.BlockSpec(memory_space=pl.ANY), wspec, wspec],
            out_specs=(ospec, ospec),
            scratch_shapes=[pltpu.VMEM((2, brows, HEAD_DIM), F32), pltpu.VMEM((2, brows, HEAD_DIM), F32),
                            pltpu.SemaphoreType.DMA((2, 2))],
        ),
        compiler_params=pltpu.CompilerParams(dimension_semantics=("arbitrary",), vmem_limit_bytes=VMEM_LIMIT),
        name="feats_sample",
    )(page_table.reshape(-1), ck, cv, wk, wv)


def _cmp_finish_kernel(f_ref, w1_ref, pe_ref, w2_ref, g_ref, o_ref, *, nc, normalize):
    bias = _dot(pe_ref[...].astype(BF16), w1_ref[...])
    ba = bias[0:1, 0:HEAD_DIM]
    bb = bias[1:2, HEAD_DIM:2 * HEAD_DIM]
    row = _iota((nc, 1), 0)
    for g in range(N_KV):
        fa = f_ref[2 * g] + ba
        fb = f_ref[2 * g + 1] + bb
        hid = _silu(fa + pltpu.roll(fb, nc - 1, 0))
        o = _dot(hid.astype(BF16), w2_ref[...])
        if normalize:
            o = _rms(o, g_ref[...])
        o_ref[:, g * HEAD_DIM:(g + 1) * HEAD_DIM] = jnp.where(row < nc - 1, o, 0.0)


def _cmp_finish(feats, w1cat, pe2, w2_bf, gain, nc, normalize):
    nb = feats.shape[1] // nc
    full = lambda shape: pl.BlockSpec(shape, lambda b: (0, 0))
    return pl.pallas_call(
        functools.partial(_cmp_finish_kernel, nc=nc, normalize=normalize),
        out_shape=jax.ShapeDtypeStruct((nb * nc, N_KV * HEAD_DIM), F32),
        grid=(nb,),
        in_specs=[pl.BlockSpec((N_FEAT, nc, HEAD_DIM), lambda b: (0, b, 0)), full(w1cat.shape), full(pe2.shape),
                  full(w2_bf.shape), full((1, HEAD_DIM))],
        out_specs=pl.BlockSpec((nc, N_KV * HEAD_DIM), lambda b: (b, 0)),
        compiler_params=pltpu.CompilerParams(dimension_semantics=("arbitrary",), vmem_limit_bytes=VMEM_LIMIT),
        name="cmp_finish",
    )(feats, w1cat, pe2, w2_bf, gain.reshape(1, -1))


def _select_mask(imp, cur):
    shape = imp.shape
    j = _iota(shape, 1)
    jf = j.astype(F32)
    forced = (j == 0) | (j == cur) | (j == cur - 1)
    score0 = jnp.where(forced, 1e30, jnp.where(j <= cur, imp, -1.0))

    def body(_, carry):
        score, sel = carry
        m = jnp.max(score, axis=-1, keepdims=True)
        first = jnp.min(jnp.where(score == m, jf, 1e9), axis=-1, keepdims=True)
        hit = jf == first
        return jnp.where(hit, -2.0, score), jnp.where(hit, 1.0, sel)

    _, sel = lax.fori_loop(0, N_SELECT, body, (score0, jnp.zeros(shape, F32)))
    return sel


def _imp_matrix(n_cmp_pad, n_sel_pad, n_cmp, n_sel):
    r = SEL_BLOCK // CMP_STRIDE
    lead = CMP_BLOCK // CMP_STRIDE - 1
    m = np.zeros((n_cmp_pad, n_sel_pad), np.float32)
    for jb in range(n_sel):
        for o in range(-lead, r):
            start = o * CMP_STRIDE
            w = (min(start + CMP_BLOCK, SEL_BLOCK) - max(start, 0)) / CMP_BLOCK
            i = r * jb + o
            if 0 <= i < n_cmp:
                m[i, jb] += w
    return jnp.asarray(m)


def _cmp_select_kernel(q_ref, kc_ref, vc_ref, m_ref, oc_ref, sel_ref, any_ref, *, tq, nc):
    i = pl.program_id(0)
    qpos = i * tq + _iota((tq, 1), 0)
    end = _iota((1, nc), 1) * CMP_STRIDE + (CMP_BLOCK - 1)
    dist = qpos - end
    mask = dist >= 0
    distf = dist.astype(F32)
    cur = qpos >> 6
    imps = []
    for g in range(N_KV):
        kg = kc_ref[:, g * HEAD_DIM:(g + 1) * HEAD_DIM].astype(BF16)
        vg = vc_ref[:, g * HEAD_DIM:(g + 1) * HEAD_DIM].astype(BF16)
        psum = jnp.zeros((tq, nc), F32)
        for r in range(GROUP):
            h = g * GROUP + r
            qh = q_ref[:, h * HEAD_DIM:(h + 1) * HEAD_DIM].astype(BF16)
            s = _dot_nt(qh, kg) * SCALE - SLOPES[h] * distf
            s = jnp.where(mask, s, NEG)
            m = jnp.max(s, axis=-1, keepdims=True)
            e = jnp.where(mask, jnp.exp(s - m), 0.0)
            p = e * (1.0 / jnp.maximum(jnp.sum(e, axis=-1, keepdims=True), 1e-30))
            psum = psum + p
            oc_ref[:, h * HEAD_DIM:(h + 1) * HEAD_DIM] = _dot(p.astype(BF16), vg)
        imps.append(jnp.dot(psum, m_ref[...], precision=lax.Precision.HIGHEST, preferred_element_type=F32))
    sel_all = _select_mask(jnp.concatenate(imps, axis=0), jnp.concatenate([cur] * N_KV, axis=0))
    for g in range(N_KV):
        sel = sel_all[g * tq:(g + 1) * tq]
        sel_ref[:, g * 128:(g + 1) * 128] = sel
        any_ref[:, g * 128:(g + 1) * 128] = jnp.broadcast_to(jnp.max(sel, axis=0, keepdims=True), (8, 128))


def _cmp_select(z, kcmp, vcmp, imp_m, tq):
    t = z.shape[0]
    nc = kcmp.shape[0]
    n_sel = imp_m.shape[1]
    assert n_sel == 128
    full = lambda shape: pl.BlockSpec(shape, lambda i: (0, 0))
    return pl.pallas_call(
        functools.partial(_cmp_select_kernel, tq=tq, nc=nc),
        out_shape=(jax.ShapeDtypeStruct((t, N_HEADS * HEAD_DIM), F32), jax.ShapeDtypeStruct((t, N_KV * 128), F32),
                   jax.ShapeDtypeStruct((t // tq * 8, N_KV * 128), F32)),
        grid=(t // tq,),
        in_specs=[pl.BlockSpec((tq, N_HEADS * HEAD_DIM), lambda i: (i, COL_Q // (N_HEADS * HEAD_DIM))),
                  full(kcmp.shape), full(vcmp.shape), full(imp_m.shape)],
        out_specs=(pl.BlockSpec((tq, N_HEADS * HEAD_DIM), lambda i: (i, 0)),
                   pl.BlockSpec((tq, N_KV * 128), lambda i: (i, 0)),
                   pl.BlockSpec((8, N_KV * 128), lambda i: (i, 0))),
        compiler_params=pltpu.CompilerParams(dimension_semantics=("arbitrary",), vmem_limit_bytes=VMEM_LIMIT),
        name="cmp_select",
    )(z, kcmp, vcmp, imp_m)


def _expand_blocks(sel_bf, first_block, n_keys):
    n_blk = sel_bf.shape[1]
    e = (_iota((n_blk, n_keys), 0) == first_block + (_iota((n_blk, n_keys), 1) >> 6)).astype(BF16)
    return _dot(sel_bf, e)


def _flash_update(s, v_bf, m_ref, l_ref, acc_ref, rows):
    m_old = m_ref[rows, :]
    m_new = jnp.maximum(m_old, jnp.max(s, axis=-1, keepdims=True))
    alpha = jnp.exp(m_old - m_new)
    p = jnp.exp(s - m_new)
    l_ref[rows, :] = alpha * l_ref[rows, :] + jnp.sum(p, axis=-1, keepdims=True)
    acc_ref[rows, :] = alpha * acc_ref[rows, :] + _dot(p.astype(BF16), v_bf)
    m_ref[rows, :] = m_new


def _selected_prompt_kernel(order_ref, cnt_ref, q_ref, k_ref, v_ref, sel_ref, o_ref,
                            kbf, vbf, qs, m_sc, l_sc, acc_sc, *, tq, tk):
    g = pl.program_id(0)
    qi = pl.program_id(1)
    nkt = kbf.shape[0] // tk
    step = g * pl.num_programs(1) + qi

    @pl.when(qi == 0)
    def _():
        kbf[...] = k_ref[...].astype(BF16)
        vbf[...] = v_ref[...].astype(BF16)

    for r in range(GROUP):
        qs[r * tq:(r + 1) * tq, :] = q_ref[:, r * HEAD_DIM:(r + 1) * HEAD_DIM].astype(BF16)
    m_sc[...] = jnp.full(m_sc.shape, M_INIT, F32)
    l_sc[...] = jnp.zeros(l_sc.shape, F32)
    acc_sc[...] = jnp.zeros(acc_sc.shape, F32)
    sel_bf = sel_ref[...].astype(BF16)
    qpos = qi * tq + _iota((tq, 1), 0)

    def body(j, carry):
        ki = order_ref[step * nkt + j]
        k0 = pl.multiple_of(ki * tk, tk)
        k = kbf[pl.ds(k0, tk), :]
        v = vbf[pl.ds(k0, tk), :]
        selexp = _expand_blocks(sel_bf, ki * (tk // SEL_BLOCK), tk)
        kpos = ki * tk + _iota((1, tk), 1)
        mb = jnp.where((selexp > 0.5) & (kpos <= qpos), 0.0, NEG)
        krel = (kpos - qi * tq).astype(F32)
        for r in range(GROUP):
            slope = jnp.where(g == 0, SLOPES[r], SLOPES[GROUP + r])
            rows = slice(r * tq, (r + 1) * tq)
            s = _dot_nt(qs[rows, :], k) * SCALE + (mb + slope * krel)
            _flash_update(s, v, m_sc, l_sc, acc_sc, rows)
        return carry

    lax.fori_loop(0, cnt_ref[step], body, 0)
    for r in range(GROUP):
        rows = slice(r * tq, (r + 1) * tq)
        o_ref[:, r * HEAD_DIM:(r + 1) * HEAD_DIM] = acc_sc[rows, :] * (1.0 / l_sc[rows, :])


def _active_tiles(anyb, t, tq, tk):
    nq, nkt = t // tq, t // tk
    a = anyb.reshape(nq, 8, N_KV, nkt, tk // SEL_BLOCK)[:, 0]
    causal = (jnp.arange(nkt)[None, :] * tk) <= (jnp.arange(nq)[:, None] * tq + tq - 1)
    flags = jnp.transpose((jnp.max(a, axis=-1) > 0.5) & causal[:, None, :], (1, 0, 2))
    order = jnp.argsort(jnp.logical_not(flags), axis=-1, stable=True).astype(jnp.int32)
    return order.reshape(-1), jnp.sum(flags, axis=-1).astype(jnp.int32).reshape(-1)


def _selected_prompt(z, sel, anyb, tq, tk=512):
    t = z.shape[0]
    gw = GROUP * HEAD_DIM
    order, cnt = _active_tiles(anyb, t, tq, tk)
    return pl.pallas_call(
        functools.partial(_selected_prompt_kernel, tq=tq, tk=tk),
        out_shape=jax.ShapeDtypeStruct((t, N_HEADS * HEAD_DIM), F32),
        grid_spec=pltpu.PrefetchScalarGridSpec(
            num_scalar_prefetch=2,
            grid=(N_KV, t // tq),
            in_specs=[
                pl.BlockSpec((tq, gw), lambda g, qi, o, c: (qi, COL_Q // gw + g)),
                pl.BlockSpec((t, HEAD_DIM), lambda g, qi, o, c: (0, COL_KS // HEAD_DIM + g)),
                pl.BlockSpec((t, HEAD_DIM), lambda g, qi, o, c: (0, COL_VS // HEAD_DIM + g)),
                pl.BlockSpec((tq, 128), lambda g, qi, o, c: (qi, g)),
            ],
            out_specs=pl.BlockSpec((tq, gw), lambda g, qi, o, c: (qi, g)),
            scratch_shapes=[pltpu.VMEM((t, HEAD_DIM), BF16), pltpu.VMEM((t, HEAD_DIM), BF16),
                            pltpu.VMEM((GROUP * tq, HEAD_DIM), BF16), pltpu.VMEM((GROUP * tq, 1), F32),
                            pltpu.VMEM((GROUP * tq, 1), F32), pltpu.VMEM((GROUP * tq, HEAD_DIM), F32)],
        ),
        compiler_params=pltpu.CompilerParams(
            dimension_semantics=("arbitrary", "arbitrary"), vmem_limit_bytes=VMEM_LIMIT),
        name="selected_prompt",
    )(order, cnt, z, z, z, sel)


def _window_prompt_kernel(q_ref, kp_ref, kc_ref, vp_ref, vc_ref, o_ref, *, tq):
    g = pl.program_id(0)
    qi = pl.program_id(1)
    k = jnp.concatenate([kp_ref[...], kc_ref[...]], axis=0).astype(BF16)
    v = jnp.concatenate([vp_ref[...], vc_ref[...]], axis=0).astype(BF16)
    krel = _iota((1, 2 * tq), 1) - tq
    dist = _iota((tq, 1), 0) - krel
    mask = (dist >= 0) & (dist < WINDOW) & ((krel >= 0) | (qi > 0))
    distf = dist.astype(F32)
    for r in range(GROUP):
        slope = jnp.where(g == 0, SLOPES[r], SLOPES[GROUP + r])
        qh = q_ref[:, r * HEAD_DIM:(r + 1) * HEAD_DIM].astype(BF16)
        s = jnp.where(mask, _dot_nt(qh, k) * SCALE - slope * distf, NEG)
        m = jnp.max(s, axis=-1, keepdims=True)
        e = jnp.where(mask, jnp.exp(s - m), 0.0)
        p = e * (1.0 / jnp.maximum(jnp.sum(e, axis=-1, keepdims=True), 1e-30))
        o_ref[:, r * HEAD_DIM:(r + 1) * HEAD_DIM] = _dot(p.astype(BF16), v)


def _window_prompt(z, tq=WINDOW):
    t = z.shape[0]
    gw = GROUP * HEAD_DIM
    prev = lambda c: pl.BlockSpec((tq, HEAD_DIM), lambda g, qi: (jnp.maximum(qi - 1, 0), c // HEAD_DIM + g))
    cur = lambda c: pl.BlockSpec((tq, HEAD_DIM), lambda g, qi: (qi, c // HEAD_DIM + g))
    return pl.pallas_call(
        functools.partial(_window_prompt_kernel, tq=tq),
        out_shape=jax.ShapeDtypeStruct((t, N_HEADS * HEAD_DIM), F32),
        grid=(N_KV, t // tq),
        in_specs=[pl.BlockSpec((tq, gw), lambda g, qi: (qi, COL_Q // gw + g)),
                  prev(COL_KW), cur(COL_KW), prev(COL_VW), cur(COL_VW)],
        out_specs=pl.BlockSpec((tq, gw), lambda g, qi: (qi, g)),
        compiler_params=pltpu.CompilerParams(
            dimension_semantics=("arbitrary", "arbitrary"), vmem_limit_bytes=VMEM_LIMIT),
        name="window_prompt",
    )(z, z, z, z, z)


SEL_PAD = 384


def _slope_col(g, ds):
    return jnp.concatenate([jnp.full((ds, 1), SLOPES[g * GROUP + r], F32) for r in range(GROUP)], axis=0)


def _stack_heads(q_ref, g):
    return jnp.concatenate([q_ref[:, (g * GROUP + r) * HEAD_DIM:(g * GROUP + r + 1) * HEAD_DIM]
                            for r in range(GROUP)], axis=0).astype(BF16)


def _pad_rows(a, n):
    return jnp.concatenate([a, jnp.zeros((n - a.shape[0], a.shape[1]), a.dtype)], axis=0)


def _sample_small_kernel(q_ref, kc_ref, vc_ref, m_ref, sk_ref, sv_ref, new_ref, oc_ref, sel_ref, ow_ref,
                         *, ds, nc, past, wb):
    rows = GROUP * ds
    qidx = _iota((rows, 1), 0) & (ds - 1)
    spos = past + qidx
    end = _iota((1, nc), 1) * CMP_STRIDE + (CMP_BLOCK - 1)
    dist_c = spos - end
    mask_c = dist_c >= 0
    cur = (past + _iota((ds, 1), 0)) >> 6
    ist = _iota((1, wb), 1)
    dist_s = wb + qidx - ist
    mask_s = dist_s < WINDOW
    jn = _iota((1, 128), 1)
    dist_n = qidx - jn
    mask_n = (dist_n >= 0) & (jn < ds)
    imps = []
    for g in range(N_KV):
        lanes = slice(g * HEAD_DIM, (g + 1) * HEAD_DIM)
        qs = _stack_heads(q_ref, g)
        slope = _slope_col(g, ds)
        s = _dot_nt(qs, kc_ref[:, lanes].astype(BF16)) * SCALE - slope * dist_c.astype(F32)
        s = jnp.where(mask_c, s, NEG)
        m = jnp.max(s, axis=-1, keepdims=True)
        e = jnp.where(mask_c, jnp.exp(s - m), 0.0)
        p = e * (1.0 / jnp.maximum(jnp.sum(e, axis=-1, keepdims=True), 1e-30))
        o = _dot(p.astype(BF16), vc_ref[:, lanes].astype(BF16))
        psum = p[0:ds]
        for r in range(1, GROUP):
            psum = psum + p[r * ds:(r + 1) * ds]
        imps.append(jnp.dot(psum, m_ref[...], precision=lax.Precision.HIGHEST, preferred_element_type=F32))
        kn = _pad_rows(new_ref[:, lanes], 128).astype(BF16)
        vn = _pad_rows(new_ref[:, N_KV * HEAD_DIM + g * HEAD_DIM:N_KV * HEAD_DIM + (g + 1) * HEAD_DIM], 128)
        s1 = jnp.where(mask_s, _dot_nt(qs, sk_ref[:, lanes].astype(BF16)) * SCALE - slope * dist_s.astype(F32), NEG)
        s2 = jnp.where(mask_n, _dot_nt(qs, kn) * SCALE - slope * dist_n.astype(F32), NEG)
        mw = jnp.maximum(jnp.max(s1, axis=-1, keepdims=True), jnp.max(s2, axis=-1, keepdims=True))
        e1 = jnp.where(mask_s, jnp.exp(s1 - mw), 0.0)
        e2 = jnp.where(mask_n, jnp.exp(s2 - mw), 0.0)
        inv = 1.0 / jnp.maximum(jnp.sum(e1, axis=-1, keepdims=True) + jnp.sum(e2, axis=-1, keepdims=True), 1e-30)
        w = _dot((e1 * inv).astype(BF16), sv_ref[:, lanes].astype(BF16)) + _dot((e2 * inv).astype(BF16),
                                                                                vn.astype(BF16))
        for r in range(GROUP):
            h = g * GROUP + r
            oc_ref[:, h * HEAD_DIM:(h + 1) * HEAD_DIM] = o[r * ds:(r + 1) * ds]
            ow_ref[:, h * HEAD_DIM:(h + 1) * HEAD_DIM] = w[r * ds:(r + 1) * ds]
    sel_all = _select_mask(jnp.concatenate(imps, axis=0), jnp.concatenate([cur] * N_KV, axis=0))
    for g in range(N_KV):
        sel_ref[:, g * SEL_PAD:(g + 1) * SEL_PAD] = sel_all[g * ds:(g + 1) * ds]


def _sample_small(z2, kcmp2, vcmp2, imp_m2, state_k, state_v, nb, past):
    t = z2.shape[0]
    ds = t // nb
    nc = kcmp2.shape[0] // nb
    wb = state_k.shape[0] // nb
    qw = N_HEADS * HEAD_DIM
    kvw = N_KV * HEAD_DIM
    return pl.pallas_call(
        functools.partial(_sample_small_kernel, ds=ds, nc=nc, past=past, wb=wb),
        out_shape=(jax.ShapeDtypeStruct((t, qw), F32), jax.ShapeDtypeStruct((t, N_KV * SEL_PAD), F32),
                   jax.ShapeDtypeStruct((t, qw), F32)),
        grid=(nb,),
        in_specs=[pl.BlockSpec((ds, qw), lambda b: (b, COL_Q // qw)),
                  pl.BlockSpec((nc, kvw), lambda b: (b, 0)), pl.BlockSpec((nc, kvw), lambda b: (b, 0)),
                  pl.BlockSpec(imp_m2.shape, lambda b: (0, 0)),
                  pl.BlockSpec((wb, kvw), lambda b: (b, 0)), pl.BlockSpec((wb, kvw), lambda b: (b, 0)),
                  pl.BlockSpec((ds, TN), lambda b: (b, TILE_WIN))],
        out_specs=(pl.BlockSpec((ds, qw), lambda b: (b, 0)), pl.BlockSpec((ds, N_KV * SEL_PAD), lambda b: (b, 0)),
                   pl.BlockSpec((ds, qw), lambda b: (b, 0))),
        compiler_params=pltpu.CompilerParams(dimension_semantics=("arbitrary",), vmem_limit_bytes=VMEM_LIMIT),
        name="sample_small",
    )(z2, kcmp2, vcmp2, imp_m2, state_k, state_v, z2)


SLC_PAGES = 16


def _selected_sample_kernel(pt_ref, ck_hbm, cv_hbm, q_ref, sel_ref, new_ref, o_ref,
                            kbuf, vbuf, sem, m_sc, l_sc, acc_sc, *, ds, steps, past):
    slot = _gather_pipeline(pt_ref, (ck_hbm, cv_hbm), (kbuf, vbuf), sem, SLC_PAGES, PAGE_ROWS)
    kt = pl.program_id(0) % steps
    rows = GROUP * ds
    nk = SLC_PAGES * PAGE_SIZE
    qidx = _iota((rows, 1), 0) & (ds - 1)

    @pl.when(kt == 0)
    def _():
        m_sc[...] = jnp.full(m_sc.shape, M_INIT, F32)
        l_sc[...] = jnp.zeros(l_sc.shape, F32)
        acc_sc[...] = jnp.zeros(acc_sc.shape, F32)

    krel = (kt * nk - past + _iota((1, nk), 1)).astype(F32)
    for g in range(N_KV):
        grows = slice(g * rows, (g + 1) * rows)
        qs = _stack_heads(q_ref, g)
        slope = _slope_col(g, ds)
        selexp = _expand_blocks(sel_ref[:, g * SEL_PAD:(g + 1) * SEL_PAD].astype(BF16), kt * (nk // SEL_BLOCK), nk)
        mb = jnp.where(jnp.concatenate([selexp] * GROUP, axis=0) > 0.5, 0.0, NEG)
        kg = kbuf[slot, pl.ds(g, nk, stride=N_KV), :].astype(BF16)
        vg = vbuf[slot, pl.ds(g, nk, stride=N_KV), :].astype(BF16)
        s = _dot_nt(qs, kg) * SCALE + (mb + slope * krel)
        _flash_update(s, vg, m_sc, l_sc, acc_sc, grows)

    @pl.when(kt == steps - 1)
    def _():
        jn = _iota((1, 128), 1)
        mb_n = jnp.where((jn <= qidx) & (jn < ds), 0.0, NEG)
        for g in range(N_KV):
            lanes = slice(g * HEAD_DIM, (g + 1) * HEAD_DIM)
            grows = slice(g * rows, (g + 1) * rows)
            qs = _stack_heads(q_ref, g)
            kn = _pad_rows(new_ref[:, lanes], 128).astype(BF16)
            vn = _pad_rows(new_ref[:, N_KV * HEAD_DIM + g * HEAD_DIM:N_KV * HEAD_DIM + (g + 1) * HEAD_DIM], 128)
            s = _dot_nt(qs, kn) * SCALE + (mb_n + _slope_col(g, ds) * jn.astype(F32))
            _flash_update(s, vn.astype(BF16), m_sc, l_sc, acc_sc, grows)
            o = acc_sc[grows, :] * (1.0 / l_sc[grows, :])
            for r in range(GROUP):
                h = g * GROUP + r
                o_ref[:, h * HEAD_DIM:(h + 1) * HEAD_DIM] = o[r * ds:(r + 1) * ds]


def _selected_sample(page_table, cache_k, cache_v, z2, sel2, nb, past):
    t = z2.shape[0]
    ds = t // nb
    n_phys = cache_k.shape[0]
    ck = cache_k.reshape(n_phys, PAGE_ROWS, HEAD_DIM)
    cv = cache_v.reshape(n_phys, PAGE_ROWS, HEAD_DIM)
    steps = page_table.shape[1] // SLC_PAGES
    qw = N_HEADS * HEAD_DIM
    rows = N_KV * GROUP * ds
    return pl.pallas_call(
        functools.partial(_selected_sample_kernel, ds=ds, steps=steps, past=past),
        out_shape=jax.ShapeDtypeStruct((t, qw), F32),
        grid_spec=pltpu.PrefetchScalarGridSpec(
            num_scalar_prefetch=1,
            grid=(nb * steps,),
            in_specs=[pl.BlockSpec(memory_space=pl.ANY), pl.BlockSpec(memory_space=pl.ANY),
                      pl.BlockSpec((ds, qw), lambda s, pt: (s // steps, COL_Q // qw)),
                      pl.BlockSpec((ds, N_KV * SEL_PAD), lambda s, pt: (s // steps, 0)),
                      pl.BlockSpec((ds, TN), lambda s, pt: (s // steps, TILE_SLC))],
            out_specs=pl.BlockSpec((ds, qw), lambda s, pt: (s // steps, 0)),
            scratch_shapes=[pltpu.VMEM((2, SLC_PAGES * PAGE_ROWS, HEAD_DIM), F32),
                            pltpu.VMEM((2, SLC_PAGES * PAGE_ROWS, HEAD_DIM), F32),
                            pltpu.SemaphoreType.DMA((2, 2)),
                            pltpu.VMEM((rows, 1), F32), pltpu.VMEM((rows, 1), F32),
                            pltpu.VMEM((rows, HEAD_DIM), F32)],
        ),
        compiler_params=pltpu.CompilerParams(dimension_semantics=("arbitrary",), vmem_limit_bytes=VMEM_LIMIT),
        name="selected_sample",
    )(page_table.reshape(-1), ck, cv, z2, sel2, z2)


def _merge_kernel(x_ref, conv_ref, oc_ref, os_ref, ow_ref, gt_ref, zn_ref, w_ref, y_ref):
    gt = gt_ref[...]
    parts = [conv_ref[...].astype(BF16)]
    for h in range(N_HEADS):
        lanes = slice(h * HEAD_DIM, (h + 1) * HEAD_DIM)
        o = (gt[:, 3 * h:3 * h + 1] * oc_ref[:, lanes] + gt[:, 3 * h + 1:3 * h + 2] * os_ref[:, lanes]
             + gt[:, 3 * h + 2:3 * h + 3] * ow_ref[:, lanes])
        parts.append((o * _silu(zn_ref[:, lanes])).astype(BF16))
    y_ref[...] = x_ref[...] + _dot(jnp.concatenate(parts, axis=1), w_ref[...])


def _merge(x, conv_o, o_c, o_s, o_w, z, w_out_bf, tm=256):
    t, d = x.shape
    qw = N_HEADS * HEAD_DIM
    rowblk = lambda w: pl.BlockSpec((tm, w), lambda i: (i, 0))
    return pl.pallas_call(
        _merge_kernel,
        out_shape=jax.ShapeDtypeStruct((t, d), F32),
        grid=(t // tm,),
        in_specs=[rowblk(d), rowblk(C_CONV), rowblk(qw), rowblk(qw), rowblk(qw),
                  pl.BlockSpec((tm, 128), lambda i: (i, COL_GT // 128)),
                  pl.BlockSpec((tm, qw), lambda i: (i, COL_ZN // qw)),
                  pl.BlockSpec(w_out_bf.shape, lambda i: (0, 0))],
        out_specs=rowblk(d),
        compiler_params=pltpu.CompilerParams(dimension_semantics=("arbitrary",), vmem_limit_bytes=VMEM_LIMIT),
        name="merge",
    )(x, conv_o, o_c, o_s, o_w, z, z, w_out_bf)


def _pad_w_in(w_in):
    d = w_in.shape[0]
    gt0 = 3 * C_CONV + N_HEADS * HEAD_DIM + 6 * N_KV * HEAD_DIM
    n_gt = 3 * N_HEADS
    pieces = [w_in[:, :gt0], w_in[:, gt0:gt0 + n_gt], jnp.zeros((d, COL_ZN - COL_GT - n_gt), w_in.dtype),
              w_in[:, gt0 + n_gt:]]
    return jnp.concatenate(pieces, axis=1).astype(BF16)


def _cmp_weights(pe, w1, w2):
    half = CMP_STRIDE
    k = half * HEAD_DIM
    w1cat = jnp.concatenate([w1[:half].reshape(k, -1), w1[half:].reshape(k, -1)], axis=1).astype(BF16)
    pe2 = jnp.concatenate([pe[:half].reshape(1, k), pe[half:].reshape(1, k), jnp.zeros((6, k), pe.dtype)], axis=0)
    return w1cat, pe2, w2.astype(BF16)


def kernel(x_prompt, x_sample, cache_k_cmp, cache_v_cmp, cache_k_slc, cache_v_slc, state_k_win, state_v_win,
           state_conv, page_table, g_norm, w_in, pe_cmp_k, w_cmp_k1, w_cmp_k2, pe_cmp_v, w_cmp_v1, w_cmp_v2,
           g_q, g_k_cmp, g_k_slc, g_k_win, w_dw, b_dw, ln_g, ln_b, w_pw2, b_pw2, w_out):
    _, t, d = x_prompt.shape
    db, ds, _ = x_sample.shape
    past = page_table.shape[1] * PAGE_SIZE
    wb = state_k_win.shape[1]
    kvw = N_KV * HEAD_DIM
    assert ds < CMP_STRIDE and wb == WINDOW and t % SEL_BLOCK == 0 and t // SEL_BLOCK == 128

    w_p = _pad_w_in(w_in)
    w_pw_bf = w_pw2.astype(BF16)
    w_out_bf = w_out.astype(BF16)
    wk1, pek, wk2 = _cmp_weights(pe_cmp_k, w_cmp_k1, w_cmp_k2)
    wv1, pev, wv2 = _cmp_weights(pe_cmp_v, w_cmp_v1, w_cmp_v2)
    ones = jnp.ones((HEAD_DIM,), F32)

    xp = x_prompt.reshape(t, d)
    z = _in_proj(xp, g_norm, w_p, g_q, g_k_slc, g_k_win, tm=512)
    conv_o, conv_st = _conv_prompt(z, w_dw, b_dw, ln_g, ln_b, w_pw_bf, b_pw2)
    nc = t // CMP_STRIDE
    kc = z[:, COL_KC:COL_KC + kvw]
    vc = z[:, COL_VC:COL_VC + kvw]
    fk, fv = _feats_prompt(kc.reshape(nc, CHUNK_W), vc.reshape(nc, CHUNK_W), wk1, wv1)
    kcmp = _cmp_finish(fk, wk1, pek, wk2, g_k_cmp, nc, True)
    vcmp = _cmp_finish(fv, wv1, pev, wv2, ones, nc, False)
    imp_m = _imp_matrix(nc, t // SEL_BLOCK, nc - 1, t // SEL_BLOCK)
    o_c, sel, anyb = _cmp_select(z, kcmp, vcmp, imp_m, tq=256)
    o_s = _selected_prompt(z, sel, anyb, tq=256)
    o_w = _window_prompt(z)
    y_prompt = _merge(xp, conv_o, o_c, o_s, o_w, z, w_out_bf)

    xs = x_sample.reshape(db * ds, d)
    z2 = _in_proj(xs, g_norm, w_p, g_q, g_k_slc, g_k_win, tm=db * ds)
    conv_o2, u2 = _conv_sample(z2, state_conv, w_dw, b_dw, ln_g, ln_b, w_pw_bf, b_pw2)
    nc2 = past // CMP_STRIDE
    fk2, fv2 = _feats_sample(page_table, cache_k_cmp, cache_v_cmp, wk1, wv1)
    kcmp2 = _cmp_finish(fk2, wk1, pek, wk2, g_k_cmp, nc2, True)
    vcmp2 = _cmp_finish(fv2, wv1, pev, wv2, ones, nc2, False)
    n_sel2 = -(-(past + ds) // SEL_BLOCK)
    assert n_sel2 <= SEL_PAD
    imp_m2 = _imp_matrix(nc2, SEL_PAD, nc2 - 1, n_sel2)
    o_c2, sel2, o_w2 = _sample_small(z2, kcmp2, vcmp2, imp_m2, state_k_win.reshape(db * wb, kvw),
                                     state_v_win.reshape(db * wb, kvw), db, past)
    o_s2 = _selected_sample(page_table, cache_k_slc, cache_v_slc, z2, sel2, db, past)
    y_sample = _merge(xs, conv_o2, o_c2, o_s2, o_w2, z2, w_out_bf)

    kv4 = lambda a, b: a.reshape(b, -1, N_KV, HEAD_DIM)
    cols = lambda a, c: a[:, c:c + kvw]
    kw2 = kv4(cols(z2, COL_KW), db)
    vw2 = kv4(cols(z2, COL_VW), db)
    return (y_prompt.reshape(1, t, d), y_sample.reshape(db, ds, d),
            kv4(kc, 1), kv4(vc, 1), kv4(cols(z, COL_KS), 1), kv4(cols(z, COL_VS), 1),
            kv4(cols(z, COL_KW), 1)[:, t - wb:], kv4(cols(z, COL_VW), 1)[:, t - wb:],
            conv_st[HALO - (CONV_WIDTH - 1):][None],
            kv4(cols(z2, COL_KC), db), kv4(cols(z2, COL_VC), db), kv4(cols(z2, COL_KS), db), kv4(cols(z2, COL_VS), db),
            jnp.concatenate([state_k_win[:, ds:], kw2], axis=1), jnp.concatenate([state_v_win[:, ds:], vw2], axis=1),
            jnp.concatenate([state_conv[:, ds:], u2.reshape(db, ds, C_CONV)], axis=1))
```

```python
import functools

import numpy as np
import jax
import jax.numpy as jnp
from jax import lax
from jax.experimental import pallas as pl
from jax.experimental.pallas import tpu as pltpu

F32 = jnp.float32
BF16 = jnp.bfloat16

HEAD_DIM = 128
N_HEADS = 8
N_KV = 2
GROUP = 4
C_CONV = 1024
CONV_WIDTH = 31
CMP_STRIDE = 16
CMP_BLOCK = 32
SEL_BLOCK = 64
N_SELECT = 16
WINDOW = 512
PAGE_SIZE = 128
EPS = 1e-6
SCALE = HEAD_DIM ** -0.5
SLOPES = tuple(2.0 ** -(h + 1) for h in range(N_HEADS))

TN = 512
N_TILES = 14
ZW = TN * N_TILES
COL_UA, COL_UB, COL_ZC, COL_Q = 0, 1024, 2048, 3072
COL_KC, COL_VC, COL_KS, COL_VS, COL_KW, COL_VW = 4096, 4352, 4608, 4864, 5120, 5376
COL_GT, COL_ZN = 5632, 6144
TILE_Q0, TILE_Q1, TILE_CMP, TILE_SLC, TILE_WIN, TILE_GT = 6, 7, 8, 9, 10, 11

NEG = -1e30
M_INIT = -1e29
VMEM_LIMIT = 48 * 1024 * 1024
INPROJ_VMEM_LIMIT = 56 * 1024 * 1024


def _sigmoid(x):
    return 1.0 / (1.0 + jnp.exp(-x))


def _silu(x):
    return x * _sigmoid(x)


def _dot(a, b):
    return jnp.dot(a, b, preferred_element_type=F32)


def _dot_nt(a, b):
    return lax.dot_general(a, b, (((1,), (1,)), ((), ())), preferred_element_type=F32)


def _rms(a, g):
    return a * lax.rsqrt(jnp.mean(a * a, axis=-1, keepdims=True) + EPS) * g


def _iota(shape, dim):
    return lax.broadcasted_iota(jnp.int32, shape, dim)


def _inproj_kernel(x_ref, gn_ref, w_ref, gq_ref, gks_ref, gkw_ref,
                   z_ref, kc_o, vc_o, ks_o, vs_o, kw_o, vw_o, xn_ref, *, tm):
    j = pl.program_id(1)

    @pl.when(j == 0)
    def _():
        x = x_ref[...]
        ms = jnp.mean(x * x, axis=-1, keepdims=True)
        xn_ref[...] = (x * lax.rsqrt(ms + EPS) * gn_ref[...]).astype(BF16)

    z_ref[...] = _dot(xn_ref[...], w_ref[...])

    def kv_tile(g_ref, k_o, v_o):
        for c in range(TN // HEAD_DIM):
            lanes = slice(c * HEAD_DIM, (c + 1) * HEAD_DIM)
            a = z_ref[:, lanes]
            if c < N_KV and g_ref is not None:
                a = _rms(a, g_ref[...])
                z_ref[:, lanes] = a
            (k_o if c < N_KV else v_o)[pl.ds(c % N_KV, tm, stride=N_KV), :] = a

    @pl.when((j == TILE_Q0) | (j == TILE_Q1))
    def _():
        for c in range(TN // HEAD_DIM):
            lanes = slice(c * HEAD_DIM, (c + 1) * HEAD_DIM)
            z_ref[:, lanes] = _rms(z_ref[:, lanes], gq_ref[...])

    @pl.when(j == TILE_CMP)
    def _():
        kv_tile(None, kc_o, vc_o)

    @pl.when(j == TILE_SLC)
    def _():
        kv_tile(gks_ref, ks_o, vs_o)

    @pl.when(j == TILE_WIN)
    def _():
        kv_tile(gkw_ref, kw_o, vw_o)

    @pl.when(j == TILE_GT)
    def _():
        z_ref[...] = _sigmoid(z_ref[...])


def _in_proj(x, g_norm, w_p, g_q, g_ks, g_kw, tm):
    t, d = x.shape
    row = lambda a: a.reshape(1, -1)
    kv_shape = jax.ShapeDtypeStruct((t * N_KV, HEAD_DIM), F32)
    kv_spec = pl.BlockSpec((tm * N_KV, HEAD_DIM), lambda i, j: (i, 0))
    return pl.pallas_call(
        functools.partial(_inproj_kernel, tm=tm),
        out_shape=(jax.ShapeDtypeStruct((t, ZW), F32),) + (kv_shape,) * 6,
        grid=(t // tm, N_TILES),
        in_specs=[
            pl.BlockSpec((tm, d), lambda i, j: (i, 0)),
            pl.BlockSpec((1, d), lambda i, j: (0, 0)),
            pl.BlockSpec((d, TN), lambda i, j: (0, j)),
            pl.BlockSpec((1, HEAD_DIM), lambda i, j: (0, 0)),
            pl.BlockSpec((1, HEAD_DIM), lambda i, j: (0, 0)),
            pl.BlockSpec((1, HEAD_DIM), lambda i, j: (0, 0)),
        ],
        out_specs=(pl.BlockSpec((tm, TN), lambda i, j: (i, j)),) + (kv_spec,) * 6,
        scratch_shapes=[pltpu.VMEM((tm, d), BF16)],
        compiler_params=pltpu.CompilerParams(
            dimension_semantics=("arbitrary", "arbitrary"), vmem_limit_bytes=INPROJ_VMEM_LIMIT),
        name="in_proj",
    )(x, row(g_norm), w_p, row(g_q), row(g_ks), row(g_kw))


HALO = 32
CONV_RB = 32
CONV_CB = 256


def _conv_tail(y, zc, lng_ref, lnb_ref, wpw_ref, bpw_ref):
    mu = jnp.mean(y, axis=-1, keepdims=True)
    yc = y - mu
    var = jnp.mean(yc * yc, axis=-1, keepdims=True)
    yn = yc * lax.rsqrt(var + EPS) * lng_ref[...] + lnb_ref[...]
    act = _silu(yn).astype(BF16)
    return (_dot(act, wpw_ref[...]) + bpw_ref[...]) * _silu(zc)


def _conv_prompt_kernel(ua_ref, ub_ref, zc_ref, uah_ref, ubh_ref, wdw_ref, bdw_ref, lng_ref, lnb_ref,
                        wpw_ref, bpw_ref, o_ref, st_ref, buf, ybuf, *, tt):
    i = pl.program_id(0)
    uh = uah_ref[...] * _sigmoid(ubh_ref[...])
    buf[0:HALO, :] = jnp.where(i > 0, uh, 0.0)
    buf[HALO:HALO + tt, :] = ua_ref[...] * _sigmoid(ub_ref[...])
    off = HALO - (CONV_WIDTH - 1)
    for c0 in range(0, C_CONV, CONV_CB):
        for r0 in range(0, tt, CONV_RB):
            acc = jnp.broadcast_to(bdw_ref[:, c0:c0 + CONV_CB], (CONV_RB, CONV_CB))
            for k in range(CONV_WIDTH):
                acc = acc + wdw_ref[k:k + 1, c0:c0 + CONV_CB] * buf[r0 + off + k:r0 + off + k + CONV_RB,
                                                                     c0:c0 + CONV_CB]
            ybuf[r0:r0 + CONV_RB, c0:c0 + CONV_CB] = acc
    o_ref[...] = _conv_tail(ybuf[...], zc_ref[...], lng_ref, lnb_ref, wpw_ref, bpw_ref)

    @pl.when(i == pl.num_programs(0) - 1)
    def _():
        st_ref[...] = buf[tt:tt + HALO, :]


def _conv_prompt(z, w_dw, b_dw, ln_g, ln_b, w_pw_bf, b_pw, tt=256):
    t = z.shape[0]
    row = lambda a: a.reshape(1, -1)
    hb = tt // HALO
    cur = lambda c: pl.BlockSpec((tt, C_CONV), lambda i: (i, c))
    halo = lambda c: pl.BlockSpec((HALO, C_CONV), lambda i: (jnp.maximum(i * hb - 1, 0), c))
    full = lambda shape: pl.BlockSpec(shape, lambda i: (0, 0))
    return pl.pallas_call(
        functools.partial(_conv_prompt_kernel, tt=tt),
        out_shape=(jax.ShapeDtypeStruct((t, C_CONV), F32), jax.ShapeDtypeStruct((HALO, C_CONV), F32)),
        grid=(t // tt,),
        in_specs=[cur(0), cur(1), cur(2), halo(0), halo(1),
                  full((CONV_WIDTH, C_CONV)), full((1, C_CONV)), full((1, C_CONV)), full((1, C_CONV)),
                  full((C_CONV, C_CONV)), full((1, C_CONV))],
        out_specs=(pl.BlockSpec((tt, C_CONV), lambda i: (i, 0)), pl.BlockSpec((HALO, C_CONV), lambda i: (0, 0))),
        scratch_shapes=[pltpu.VMEM((HALO + tt, C_CONV), F32), pltpu.VMEM((tt, C_CONV), F32)],
        compiler_params=pltpu.CompilerParams(dimension_semantics=("arbitrary",), vmem_limit_bytes=VMEM_LIMIT),
        name="conv_prompt",
    )(z, z, z, z, z, w_dw, row(b_dw), row(ln_g), row(ln_b), w_pw_bf, row(b_pw))


ST_ROWS = 40


def _conv_sample_kernel(ua_ref, ub_ref, zc_ref, st_ref, wdw_ref, bdw_ref, lng_ref, lnb_ref, wpw_ref, bpw_ref,
                        o_ref, u_ref, fbuf, ybuf, *, nb, ds):
    u = ua_ref[...] * _sigmoid(ub_ref[...])
    u_ref[...] = u
    rows = _iota((nb * ds, 1), 0) & (ds - 1)
    acc_u = jnp.broadcast_to(bdw_ref[...], (nb * ds, C_CONV))
    for d in range(ds):
        sh = u if d == 0 else pltpu.roll(u, d, 0)
        acc_u = acc_u + jnp.where(rows >= d, sh, 0.0) * wdw_ref[CONV_WIDTH - 1 - d:CONV_WIDTH - d, :]
    ybuf[...] = acc_u
    fbuf[:, 24:ST_ROWS, :] = jnp.zeros((nb, ST_ROWS - 24, C_CONV), F32)
    fbuf[:, 0:CONV_WIDTH - 1, :] = st_ref[...]

    def body(b, carry):
        acc = jnp.zeros((ds, C_CONV), F32)
        for k in range(CONV_WIDTH - 1):
            acc = acc + wdw_ref[k:k + 1, :] * fbuf[b, k:k + ds, :]
        r = pl.multiple_of(b * ds, ds)
        ybuf[pl.ds(r, ds), :] = ybuf[pl.ds(r, ds), :] + acc
        return carry

    lax.fori_loop(0, nb, body, 0)
    o_ref[...] = _conv_tail(ybuf[...], zc_ref[...], lng_ref, lnb_ref, wpw_ref, bpw_ref)


def _conv_sample(z2, state_conv, w_dw, b_dw, ln_g, ln_b, w_pw_bf, b_pw):
    nb, sw, _ = state_conv.shape
    t = z2.shape[0]
    ds = t // nb
    assert sw == CONV_WIDTH - 1 and ds == 8
    row = lambda a: a.reshape(1, -1)
    col = lambda c: pl.BlockSpec((t, C_CONV), lambda i: (0, c))
    full = lambda shape: pl.BlockSpec(shape, lambda i: (0,) * len(shape))
    return pl.pallas_call(
        functools.partial(_conv_sample_kernel, nb=nb, ds=ds),
        out_shape=(jax.ShapeDtypeStruct((t, C_CONV), F32), jax.ShapeDtypeStruct((t, C_CONV), F32)),
        grid=(1,),
        in_specs=[col(0), col(1), col(2), full((nb, sw, C_CONV)),
                  full((CONV_WIDTH, C_CONV)), full((1, C_CONV)), full((1, C_CONV)), full((1, C_CONV)),
                  full((C_CONV, C_CONV)), full((1, C_CONV))],
        out_specs=(full((t, C_CONV)), full((t, C_CONV))),
        scratch_shapes=[pltpu.VMEM((nb, ST_ROWS, C_CONV), F32), pltpu.VMEM((t, C_CONV), F32)],
        compiler_params=pltpu.CompilerParams(dimension_semantics=("arbitrary",), vmem_limit_bytes=VMEM_LIMIT),
        name="conv_sample",
    )(z2, z2, z2, state_conv, w_dw, row(b_dw), row(ln_g), row(ln_b), w_pw_bf, row(b_pw))


FEAT_W = 2 * N_KV * HEAD_DIM


N_FEAT = FEAT_W // HEAD_DIM


def _chunk_feats(piece, w):
    outs = []
    for g in range(N_KV):
        xg = jnp.concatenate([piece(c, g).astype(BF16) for c in range(CMP_STRIDE)], axis=1)
        outs.append(_dot(xg, w))
    return jnp.concatenate(outs, axis=1)


CHUNK_ROWS = CMP_STRIDE * N_KV


def _feats_kernel(xk_ref, xv_ref, wk_ref, wv_ref, fk_ref, fv_ref, *, tm):
    for x_ref, w_ref, f_ref in ((xk_ref, wk_ref, fk_ref), (xv_ref, wv_ref, fv_ref)):
        f = _chunk_feats(lambda c, g: x_ref[pl.ds(c * N_KV + g, tm, stride=CHUNK_ROWS), :], w_ref[...])
        for cb in range(N_FEAT):
            f_ref[cb] = f[:, cb * HEAD_DIM:(cb + 1) * HEAD_DIM]


def _feats_prompt(xk, xv, wk, wv, tm=128):
    nc = xk.shape[0] // CHUNK_ROWS
    full = lambda shape: pl.BlockSpec(shape, lambda i: (0, 0))
    rows = pl.BlockSpec((tm * CHUNK_ROWS, HEAD_DIM), lambda i: (i, 0))
    ospec = pl.BlockSpec((N_FEAT, tm, HEAD_DIM), lambda i: (0, i, 0))
    return pl.pallas_call(
        functools.partial(_feats_kernel, tm=tm),
        out_shape=(jax.ShapeDtypeStruct((N_FEAT, nc, HEAD_DIM), F32),) * 2,
        grid=(nc // tm,),
        in_specs=[rows, rows, full(wk.shape), full(wv.shape)],
        out_specs=(ospec, ospec),
        compiler_params=pltpu.CompilerParams(dimension_semantics=("arbitrary",), vmem_limit_bytes=VMEM_LIMIT),
        name="feats_prompt",
    )(xk, xv, wk, wv)


FEAT_PAGES = 32
CHUNKS_PER_PAGE = PAGE_SIZE // CMP_STRIDE
PAGE_ROWS = PAGE_SIZE * N_KV
FEAT_PITCH = PAGE_ROWS + 8


def _page_copies(pt_ref, srcs, bufs, sem, step, slot, pages, pitch):
    out = []
    for p in range(pages):
        page = pt_ref[step * pages + p]
        for n, (src, buf) in enumerate(zip(srcs, bufs)):
            out.append(pltpu.make_async_copy(src.at[page], buf.at[slot, pl.ds(p * pitch, PAGE_ROWS)], sem.at[n, slot]))
    return out


def _gather_pipeline(pt_ref, srcs, bufs, sem, pages, pitch):
    s = pl.program_id(0)
    slot = s % 2

    @pl.when(s == 0)
    def _():
        for c in _page_copies(pt_ref, srcs, bufs, sem, 0, 0, pages, pitch):
            c.start()

    @pl.when(s + 1 < pl.num_programs(0))
    def _():
        for c in _page_copies(pt_ref, srcs, bufs, sem, s + 1, 1 - slot, pages, pitch):
            c.start()

    for c in _page_copies(pt_ref, srcs, bufs, sem, s, slot, pages, pitch):
        c.wait()
    return slot


def _feats_gather_kernel(pt_ref, ck_hbm, cv_hbm, wk_ref, wv_ref, fk_ref, fv_ref, kbuf, vbuf, sem):
    slot = _gather_pipeline(pt_ref, (ck_hbm, cv_hbm), (kbuf, vbuf), sem, FEAT_PAGES, FEAT_PITCH)
    crows = CMP_STRIDE * N_KV

    def piece(buf):
        return lambda c, g: jnp.concatenate(
            [buf[slot, pl.ds(n * crows + c * N_KV + g, FEAT_PAGES, stride=FEAT_PITCH), :]
             for n in range(CHUNKS_PER_PAGE)], axis=0)

    for buf, w_ref, f_ref in ((kbuf, wk_ref, fk_ref), (vbuf, wv_ref, fv_ref)):
        f = _chunk_feats(piece(buf), w_ref[...])
        for n in range(CHUNKS_PER_PAGE):
            for cb in range(N_FEAT):
                f_ref[cb, pl.ds(n, FEAT_PAGES, stride=CHUNKS_PER_PAGE), :] = (
                    f[n * FEAT_PAGES:(n + 1) * FEAT_PAGES, cb * HEAD_DIM:(cb + 1) * HEAD_DIM])


def _feats_sample(page_table, cache_k, cache_v, wk, wv):
    n_pages = page_table.size
    n_phys = cache_k.shape[0]
    ck = cache_k.reshape(n_phys, PAGE_ROWS, HEAD_DIM)
    cv = cache_v.reshape(n_phys, PAGE_ROWS, HEAD_DIM)
    rows = FEAT_PAGES * CHUNKS_PER_PAGE
    brows = FEAT_PAGES * FEAT_PITCH
    wspec = pl.BlockSpec(wk.shape, lambda s, pt: (0, 0))
    ospec = pl.BlockSpec((N_FEAT, rows, HEAD_DIM), lambda s, pt: (0, s, 0))
    return pl.pallas_call(
        _feats_gather_kernel,
        out_shape=(jax.ShapeDtypeStruct((N_FEAT, n_pages * CHUNKS_PER_PAGE, HEAD_DIM), F32),) * 2,
        grid_spec=pltpu.PrefetchScalarGridSpec(
            num_scalar_prefetch=1,
            grid=(n_pages // FEAT_PAGES,),
            in_specs=[pl.BlockSpec(memory_space=pl.ANY), pl.BlockSpec(memory_space=pl.ANY), wspec, wspec],
            out_specs=(ospec, ospec),
            scratch_shapes=[pltpu.VMEM((2, brows, HEAD_DIM), F32), pltpu.VMEM((2, brows, HEAD_DIM), F32),
                            pltpu.SemaphoreType.DMA((2, 2))],
        ),
        compiler_params=pltpu.CompilerParams(dimension_semantics=("arbitrary",), vmem_limit_bytes=VMEM_LIMIT),
        name="feats_sample",
    )(page_table.reshape(-1), ck, cv, wk, wv)


def _cmp_finish_kernel(f_ref, w1_ref, pe_ref, w2_ref, g_ref, o_ref, *, nc, normalize):
    bias = _dot(pe_ref[...].astype(BF16), w1_ref[...])
    ba = bias[0:1, 0:HEAD_DIM]
    bb = bias[1:2, HEAD_DIM:2 * HEAD_DIM]
    row = _iota((nc, 1), 0)
    for g in range(N_KV):
        fa = f_ref[2 * g] + ba
        fb = f_ref[2 * g + 1] + bb
        hid = _silu(fa + pltpu.roll(fb, nc - 1, 0))
        o = _dot(hid.astype(BF16), w2_ref[...])
        if normalize:
            o = _rms(o, g_ref[...])
        o_ref[:, g * HEAD_DIM:(g + 1) * HEAD_DIM] = jnp.where(row < nc - 1, o, 0.0)


def _cmp_finish(feats, w1cat, pe2, w2_bf, gain, nc, normalize):
    nb = feats.shape[1] // nc
    full = lambda shape: pl.BlockSpec(shape, lambda b: (0, 0))
    return pl.pallas_call(
        functools.partial(_cmp_finish_kernel, nc=nc, normalize=normalize),
        out_shape=jax.ShapeDtypeStruct((nb * nc, N_KV * HEAD_DIM), F32),
        grid=(nb,),
        in_specs=[pl.BlockSpec((N_FEAT, nc, HEAD_DIM), lambda b: (0, b, 0)), full(w1cat.shape), full(pe2.shape),
                  full(w2_bf.shape), full((1, HEAD_DIM))],
        out_specs=pl.BlockSpec((nc, N_KV * HEAD_DIM), lambda b: (b, 0)),
        compiler_params=pltpu.CompilerParams(dimension_semantics=("arbitrary",), vmem_limit_bytes=VMEM_LIMIT),
        name="cmp_finish",
    )(feats, w1cat, pe2, w2_bf, gain.reshape(1, -1))


def _select_mask(imp, cur):
    shape = imp.shape
    j = _iota(shape, 1)
    jf = j.astype(F32)
    forced = (j == 0) | (j == cur) | (j == cur - 1)
    score0 = jnp.where(forced, 1e30, jnp.where(j <= cur, imp, -1.0))

    def body(_, carry):
        score, sel = carry
        m = jnp.max(score, axis=-1, keepdims=True)
        first = jnp.min(jnp.where(score == m, jf, 1e9), axis=-1, keepdims=True)
        hit = jf == first
        return jnp.where(hit, -2.0, score), jnp.where(hit, 1.0, sel)

    _, sel = lax.fori_loop(0, N_SELECT, body, (score0, jnp.zeros(shape, F32)))
    return sel


def _imp_matrix(n_cmp_pad, n_sel_pad, n_cmp, n_sel):
    r = SEL_BLOCK // CMP_STRIDE
    lead = CMP_BLOCK // CMP_STRIDE - 1
    m = np.zeros((n_cmp_pad, n_sel_pad), np.float32)
    for jb in range(n_sel):
        for o in range(-lead, r):
            start = o * CMP_STRIDE
            w = (min(start + CMP_BLOCK, SEL_BLOCK) - max(start, 0)) / CMP_BLOCK
            i = r * jb + o
            if 0 <= i < n_cmp:
                m[i, jb] += w
    return jnp.asarray(m)


def _cmp_select_kernel(q_ref, kc_ref, vc_ref, m_ref, oc_ref, sel_ref, any_ref, *, tq, nc):
    i = pl.program_id(0)
    qpos = i * tq + _iota((tq, 1), 0)
    end = _iota((1, nc), 1) * CMP_STRIDE + (CMP_BLOCK - 1)
    dist = qpos - end
    mask = dist >= 0
    distf = dist.astype(F32)
    cur = qpos >> 6
    imps = []
    for g in range(N_KV):
        kg = kc_ref[:, g * HEAD_DIM:(g + 1) * HEAD_DIM].astype(BF16)
        vg = vc_ref[:, g * HEAD_DIM:(g + 1) * HEAD_DIM].astype(BF16)
        psum = jnp.zeros((tq, nc), F32)
        for r in range(GROUP):
            h = g * GROUP + r
            qh = q_ref[:, h * HEAD_DIM:(h + 1) * HEAD_DIM].astype(BF16)
            s = _dot_nt(qh, kg) * SCALE - SLOPES[h] * distf
            s = jnp.where(mask, s, NEG)
            m = jnp.max(s, axis=-1, keepdims=True)
            e = jnp.where(mask, jnp.exp(s - m), 0.0)
            p = e * (1.0 / jnp.maximum(jnp.sum(e, axis=-1, keepdims=True), 1e-30))
            psum = psum + p
            oc_ref[:, h * HEAD_DIM:(h + 1) * HEAD_DIM] = _dot(p.astype(BF16), vg)
        imps.append(jnp.dot(psum, m_ref[...], precision=lax.Precision.HIGHEST, preferred_element_type=F32))
    sel_all = _select_mask(jnp.concatenate(imps, axis=0), jnp.concatenate([cur] * N_KV, axis=0))
    for g in range(N_KV):
        sel = sel_all[g * tq:(g + 1) * tq]
        sel_ref[:, g * 128:(g + 1) * 128] = sel
        any_ref[:, g * 128:(g + 1) * 128] = jnp.broadcast_to(jnp.max(sel, axis=0, keepdims=True), (8, 128))


def _cmp_select(z, kcmp, vcmp, imp_m, tq):
    t = z.shape[0]
    nc = kcmp.shape[0]
    n_sel = imp_m.shape[1]
    assert n_sel == 128
    full = lambda shape: pl.BlockSpec(shape, lambda i: (0, 0))
    return pl.pallas_call(
        functools.partial(_cmp_select_kernel, tq=tq, nc=nc),
        out_shape=(jax.ShapeDtypeStruct((t, N_HEADS * HEAD_DIM), F32), jax.ShapeDtypeStruct((t, N_KV * 128), F32),
                   jax.ShapeDtypeStruct((t // tq * 8, N_KV * 128), F32)),
        grid=(t // tq,),
        in_specs=[pl.BlockSpec((tq, N_HEADS * HEAD_DIM), lambda i: (i, COL_Q // (N_HEADS * HEAD_DIM))),
                  full(kcmp.shape), full(vcmp.shape), full(imp_m.shape)],
        out_specs=(pl.BlockSpec((tq, N_HEADS * HEAD_DIM), lambda i: (i, 0)),
                   pl.BlockSpec((tq, N_KV * 128), lambda i: (i, 0)),
                   pl.BlockSpec((8, N_KV * 128), lambda i: (i, 0))),
        compiler_params=pltpu.CompilerParams(dimension_semantics=("arbitrary",), vmem_limit_bytes=VMEM_LIMIT),
        name="cmp_select",
    )(z, kcmp, vcmp, imp_m)


def _flash_update(s, v_bf, m_ref, l_ref, acc_ref, rows):
    m_old = m_ref[rows, :]
    m_new = jnp.maximum(m_old, jnp.max(s, axis=-1, keepdims=True))
    alpha = jnp.exp(m_old - m_new)
    p = jnp.exp(s - m_new)
    l_ref[rows, :] = alpha * l_ref[rows, :] + jnp.sum(p, axis=-1, keepdims=True)
    acc_ref[rows, :] = alpha * acc_ref[rows, :] + _dot(p.astype(BF16), v_bf)
    m_ref[rows, :] = m_new


def _selected_prompt_kernel(flag_ref, q_ref, k_ref, v_ref, sel_ref, et_ref, o_ref,
                            kbf, vbf, qs, m_sc, l_sc, acc_sc, *, tq, tk):
    g = pl.program_id(0)
    qi = pl.program_id(1)
    nkt = kbf.shape[0] // tk
    step = g * pl.num_programs(1) + qi

    @pl.when(qi == 0)
    def _():
        kbf[...] = k_ref[...].astype(BF16)
        vbf[...] = v_ref[...].astype(BF16)

    for r in range(GROUP):
        qs[r * tq:(r + 1) * tq, :] = q_ref[:, r * HEAD_DIM:(r + 1) * HEAD_DIM].astype(BF16)
    m_sc[...] = jnp.full(m_sc.shape, M_INIT, F32)
    l_sc[...] = jnp.zeros(l_sc.shape, F32)
    acc_sc[...] = jnp.zeros(acc_sc.shape, F32)
    sel_bf = sel_ref[...].astype(BF16)
    qpos = qi * tq + _iota((tq, 1), 0)

    def body(ki, carry):
        @pl.when(flag_ref[step * nkt + ki] > 0)
        def _():
            k0 = pl.multiple_of(ki * tk, tk)
            k = kbf[pl.ds(k0, tk), :]
            v = vbf[pl.ds(k0, tk), :]
            selexp = _dot_nt(sel_bf, et_ref[pl.ds(k0, tk), :])
            kpos = ki * tk + _iota((1, tk), 1)
            mb = jnp.where((selexp > 0.5) & (kpos <= qpos), 0.0, NEG)
            krel = (kpos - qi * tq).astype(F32)
            for r in range(GROUP):
                slope = jnp.where(g == 0, SLOPES[r], SLOPES[GROUP + r])
                rows = slice(r * tq, (r + 1) * tq)
                s = _dot_nt(qs[rows, :], k) * SCALE + (mb + slope * krel)
                _flash_update(s, v, m_sc, l_sc, acc_sc, rows)
        return carry

    lax.fori_loop(0, (qi * tq + tq - 1) // tk + 1, body, 0)
    for r in range(GROUP):
        rows = slice(r * tq, (r + 1) * tq)
        o_ref[:, r * HEAD_DIM:(r + 1) * HEAD_DIM] = acc_sc[rows, :] * (1.0 / l_sc[rows, :])


def _tile_flags(anyb, t, tq, tk):
    nq, nkt = t // tq, t // tk
    a = anyb.reshape(nq, 8, N_KV, nkt, tk // SEL_BLOCK)[:, 0]
    return jnp.transpose(jnp.max(a, axis=-1) > 0.5, (1, 0, 2)).astype(jnp.int32).reshape(-1)


def _selected_prompt(z, sel, anyb, tq, tk=512):
    t = z.shape[0]
    gw = GROUP * HEAD_DIM
    flags = _tile_flags(anyb, t, tq, tk)
    et = jnp.asarray(np.arange(t)[:, None] // SEL_BLOCK == np.arange(128)[None, :], dtype=BF16)
    return pl.pallas_call(
        functools.partial(_selected_prompt_kernel, tq=tq, tk=tk),
        out_shape=jax.ShapeDtypeStruct((t, N_HEADS * HEAD_DIM), F32),
        grid_spec=pltpu.PrefetchScalarGridSpec(
            num_scalar_prefetch=1,
            grid=(N_KV, t // tq),
            in_specs=[
                pl.BlockSpec((tq, gw), lambda g, qi, f: (qi, COL_Q // gw + g)),
                pl.BlockSpec((t, HEAD_DIM), lambda g, qi, f: (0, COL_KS // HEAD_DIM + g)),
                pl.BlockSpec((t, HEAD_DIM), lambda g, qi, f: (0, COL_VS // HEAD_DIM + g)),
                pl.BlockSpec((tq, 128), lambda g, qi, f: (qi, g)),
                pl.BlockSpec((t, 128), lambda g, qi, f: (0, 0)),
            ],
            out_specs=pl.BlockSpec((tq, gw), lambda g, qi, f: (qi, g)),
            scratch_shapes=[pltpu.VMEM((t, HEAD_DIM), BF16), pltpu.VMEM((t, HEAD_DIM), BF16),
                            pltpu.VMEM((GROUP * tq, HEAD_DIM), BF16), pltpu.VMEM((GROUP * tq, 1), F32),
                            pltpu.VMEM((GROUP * tq, 1), F32), pltpu.VMEM((GROUP * tq, HEAD_DIM), F32)],
        ),
        compiler_params=pltpu.CompilerParams(
            dimension_semantics=("arbitrary", "arbitrary"), vmem_limit_bytes=VMEM_LIMIT),
        name="selected_prompt",
    )(flags, z, z, z, sel, et)


def _window_prompt_kernel(q_ref, kp_ref, kc_ref, vp_ref, vc_ref, o_ref, *, tq):
    g = pl.program_id(0)
    qi = pl.program_id(1)
    k = jnp.concatenate([kp_ref[...], kc_ref[...]], axis=0).astype(BF16)
    v = jnp.concatenate([vp_ref[...], vc_ref[...]], axis=0).astype(BF16)
    krel = _iota((1, 2 * tq), 1) - tq
    dist = _iota((tq, 1), 0) - krel
    mask = (dist >= 0) & (dist < WINDOW) & ((krel >= 0) | (qi > 0))
    distf = dist.astype(F32)
    for r in range(GROUP):
        slope = jnp.where(g == 0, SLOPES[r], SLOPES[GROUP + r])
        qh = q_ref[:, r * HEAD_DIM:(r + 1) * HEAD_DIM].astype(BF16)
        s = jnp.where(mask, _dot_nt(qh, k) * SCALE - slope * distf, NEG)
        m = jnp.max(s, axis=-1, keepdims=True)
        e = jnp.where(mask, jnp.exp(s - m), 0.0)
        p = e * (1.0 / jnp.maximum(jnp.sum(e, axis=-1, keepdims=True), 1e-30))
        o_ref[:, r * HEAD_DIM:(r + 1) * HEAD_DIM] = _dot(p.astype(BF16), v)


def _window_prompt(z, tq=WINDOW):
    t = z.shape[0]
    gw = GROUP * HEAD_DIM
    prev = lambda c: pl.BlockSpec((tq, HEAD_DIM), lambda g, qi: (jnp.maximum(qi - 1, 0), c // HEAD_DIM + g))
    cur = lambda c: pl.BlockSpec((tq, HEAD_DIM), lambda g, qi: (qi, c // HEAD_DIM + g))
    return pl.pallas_call(
        functools.partial(_window_prompt_kernel, tq=tq),
        out_shape=jax.ShapeDtypeStruct((t, N_HEADS * HEAD_DIM), F32),
        grid=(N_KV, t // tq),
        in_specs=[pl.BlockSpec((tq, gw), lambda g, qi: (qi, COL_Q // gw + g)),
                  prev(COL_KW), cur(COL_KW), prev(COL_VW), cur(COL_VW)],
        out_specs=pl.BlockSpec((tq, gw), lambda g, qi: (qi, g)),
        compiler_params=pltpu.CompilerParams(
            dimension_semantics=("arbitrary", "arbitrary"), vmem_limit_bytes=VMEM_LIMIT),
        name="window_prompt",
    )(z, z, z, z, z)


SEL_PAD = 384


def _slope_col(g, ds):
    return jnp.concatenate([jnp.full((ds, 1), SLOPES[g * GROUP + r], F32) for r in range(GROUP)], axis=0)


def _stack_heads(q_ref, g):
    return jnp.concatenate([q_ref[:, (g * GROUP + r) * HEAD_DIM:(g * GROUP + r + 1) * HEAD_DIM]
                            for r in range(GROUP)], axis=0).astype(BF16)


def _pad_rows(a, n):
    return jnp.concatenate([a, jnp.zeros((n - a.shape[0], a.shape[1]), a.dtype)], axis=0)


def _sample_small_kernel(q_ref, kc_ref, vc_ref, m_ref, sk_ref, sv_ref, nk_ref, nv_ref,
                         oc_ref, sel_ref, ow_ref, ko_ref, vo_ref, *, ds, nc, past, wb):
    rows = GROUP * ds
    qidx = _iota((rows, 1), 0) & (ds - 1)
    spos = past + qidx
    end = _iota((1, nc), 1) * CMP_STRIDE + (CMP_BLOCK - 1)
    dist_c = spos - end
    mask_c = dist_c >= 0
    cur = (past + _iota((ds, 1), 0)) >> 6
    ist = _iota((1, wb), 1)
    dist_s = wb + qidx - ist
    mask_s = dist_s < WINDOW
    jn = _iota((1, 128), 1)
    dist_n = qidx - jn
    mask_n = (dist_n >= 0) & (jn < ds)
    imps = []
    for g in range(N_KV):
        lanes = slice(g * HEAD_DIM, (g + 1) * HEAD_DIM)
        qs = _stack_heads(q_ref, g)
        slope = _slope_col(g, ds)
        s = _dot_nt(qs, kc_ref[:, lanes].astype(BF16)) * SCALE - slope * dist_c.astype(F32)
        s = jnp.where(mask_c, s, NEG)
        m = jnp.max(s, axis=-1, keepdims=True)
        e = jnp.where(mask_c, jnp.exp(s - m), 0.0)
        p = e * (1.0 / jnp.maximum(jnp.sum(e, axis=-1, keepdims=True), 1e-30))
        o = _dot(p.astype(BF16), vc_ref[:, lanes].astype(BF16))
        psum = p[0:ds]
        for r in range(1, GROUP):
            psum = psum + p[r * ds:(r + 1) * ds]
        imps.append(jnp.dot(psum, m_ref[...], precision=lax.Precision.HIGHEST, preferred_element_type=F32))
        kst = sk_ref[pl.ds(g, wb, stride=N_KV), :].astype(BF16)
        vst = sv_ref[pl.ds(g, wb, stride=N_KV), :].astype(BF16)
        kn = _pad_rows(nk_ref[pl.ds(g, ds, stride=N_KV), :], 128).astype(BF16)
        vn = _pad_rows(nv_ref[pl.ds(g, ds, stride=N_KV), :], 128).astype(BF16)
        s1 = jnp.where(mask_s, _dot_nt(qs, kst) * SCALE - slope * dist_s.astype(F32), NEG)
        s2 = jnp.where(mask_n, _dot_nt(qs, kn) * SCALE - slope * dist_n.astype(F32), NEG)
        mw = jnp.maximum(jnp.max(s1, axis=-1, keepdims=True), jnp.max(s2, axis=-1, keepdims=True))
        e1 = jnp.where(mask_s, jnp.exp(s1 - mw), 0.0)
        e2 = jnp.where(mask_n, jnp.exp(s2 - mw), 0.0)
        inv = 1.0 / jnp.maximum(jnp.sum(e1, axis=-1, keepdims=True) + jnp.sum(e2, axis=-1, keepdims=True), 1e-30)
        w = _dot((e1 * inv).astype(BF16), vst) + _dot((e2 * inv).astype(BF16), vn)
        for r in range(GROUP):
            h = g * GROUP + r
            oc_ref[:, h * HEAD_DIM:(h + 1) * HEAD_DIM] = o[r * ds:(r + 1) * ds]
            ow_ref[:, h * HEAD_DIM:(h + 1) * HEAD_DIM] = w[r * ds:(r + 1) * ds]
    sel_all = _select_mask(jnp.concatenate(imps, axis=0), jnp.concatenate([cur] * N_KV, axis=0))
    for g in range(N_KV):
        sel_ref[:, g * SEL_PAD:(g + 1) * SEL_PAD] = sel_all[g * ds:(g + 1) * ds]
    keep = (wb - ds) * N_KV
    for s_ref, n_ref, o_ref in ((sk_ref, nk_ref, ko_ref), (sv_ref, nv_ref, vo_ref)):
        o_ref[0:keep, :] = s_ref[ds * N_KV:wb * N_KV, :]
        o_ref[keep:wb * N_KV, :] = n_ref[...]


def _sample_small(z2, kw2, vw2, kcmp2, vcmp2, imp_m2, state_k, state_v, nb, past):
    t = z2.shape[0]
    ds = t // nb
    nc = kcmp2.shape[0] // nb
    wb = state_k.shape[0] // (nb * N_KV)
    qw = N_HEADS * HEAD_DIM
    kvw = N_KV * HEAD_DIM
    rows = lambda n: pl.BlockSpec((n * N_KV, HEAD_DIM), lambda b: (b, 0))
    return pl.pallas_call(
        functools.partial(_sample_small_kernel, ds=ds, nc=nc, past=past, wb=wb),
        out_shape=(jax.ShapeDtypeStruct((t, qw), F32), jax.ShapeDtypeStruct((t, N_KV * SEL_PAD), F32),
                   jax.ShapeDtypeStruct((t, qw), F32),
                   jax.ShapeDtypeStruct(state_k.shape, F32), jax.ShapeDtypeStruct(state_v.shape, F32)),
        grid=(nb,),
        in_specs=[pl.BlockSpec((ds, qw), lambda b: (b, COL_Q // qw)),
                  pl.BlockSpec((nc, kvw), lambda b: (b, 0)), pl.BlockSpec((nc, kvw), lambda b: (b, 0)),
                  pl.BlockSpec(imp_m2.shape, lambda b: (0, 0)),
                  rows(wb), rows(wb), rows(ds), rows(ds)],
        out_specs=(pl.BlockSpec((ds, qw), lambda b: (b, 0)), pl.BlockSpec((ds, N_KV * SEL_PAD), lambda b: (b, 0)),
                   pl.BlockSpec((ds, qw), lambda b: (b, 0)), rows(wb), rows(wb)),
        compiler_params=pltpu.CompilerParams(dimension_semantics=("arbitrary",), vmem_limit_bytes=VMEM_LIMIT),
        name="sample_small",
    )(z2, kcmp2, vcmp2, imp_m2, state_k, state_v, kw2, vw2)


SLC_PAGES = 32
SLC_KEYS = SLC_PAGES * PAGE_SIZE
SLC_BLOCKS = SLC_KEYS // SEL_BLOCK
WIN_STEPS = 128 // SLC_BLOCKS


def _expand_matrix():
    j = np.arange(128)[:, None]
    c = np.arange(SLC_KEYS)[None, :]
    e = np.concatenate([(j == w * SLC_BLOCKS + c // SEL_BLOCK) for w in range(WIN_STEPS)], axis=0)
    return jnp.asarray(e, dtype=BF16)


def _selected_sample_kernel(pt_ref, ck_hbm, cv_hbm, q_ref, sel0_ref, sel1_ref, e_ref, nk_ref, nv_ref, o_ref,
                            kbuf, vbuf, sem, m_sc, l_sc, acc_sc, *, ds, steps, past):
    slot = _gather_pipeline(pt_ref, (ck_hbm, cv_hbm), (kbuf, vbuf), sem, SLC_PAGES, PAGE_ROWS)
    kt = pl.program_id(0) % steps
    rows = GROUP * ds
    nk = SLC_KEYS
    qidx = _iota((rows, 1), 0) & (ds - 1)

    @pl.when(kt == 0)
    def _():
        m_sc[...] = jnp.full(m_sc.shape, M_INIT, F32)
        l_sc[...] = jnp.zeros(l_sc.shape, F32)
        acc_sc[...] = jnp.zeros(acc_sc.shape, F32)

    krel = (kt * nk - past + _iota((1, nk), 1)).astype(F32)
    e = e_ref[pl.ds(pl.multiple_of((kt % WIN_STEPS) * 128, 128), 128), :]
    for g, sel_ref in enumerate((sel0_ref, sel1_ref)):
        grows = slice(g * rows, (g + 1) * rows)
        qs = _stack_heads(q_ref, g)
        slope = _slope_col(g, ds)
        selexp = _dot(sel_ref[...].astype(BF16), e)
        mb = jnp.where(jnp.concatenate([selexp] * GROUP, axis=0) > 0.5, 0.0, NEG)
        kg = kbuf[slot, pl.ds(g, nk, stride=N_KV), :].astype(BF16)
        vg = vbuf[slot, pl.ds(g, nk, stride=N_KV), :].astype(BF16)
        s = _dot_nt(qs, kg) * SCALE + (mb + slope * krel)
        _flash_update(s, vg, m_sc, l_sc, acc_sc, grows)

    @pl.when(kt == steps - 1)
    def _():
        jn = _iota((1, 128), 1)
        mb_n = jnp.where((jn <= qidx) & (jn < ds), 0.0, NEG)
        for g in range(N_KV):
            grows = slice(g * rows, (g + 1) * rows)
            qs = _stack_heads(q_ref, g)
            kn = _pad_rows(nk_ref[pl.ds(g, ds, stride=N_KV), :], 128).astype(BF16)
            vn = _pad_rows(nv_ref[pl.ds(g, ds, stride=N_KV), :], 128).astype(BF16)
            s = _dot_nt(qs, kn) * SCALE + (mb_n + _slope_col(g, ds) * jn.astype(F32))
            _flash_update(s, vn, m_sc, l_sc, acc_sc, grows)
            o = acc_sc[grows, :] * (1.0 / l_sc[grows, :])
            for r in range(GROUP):
                h = g * GROUP + r
                o_ref[:, h * HEAD_DIM:(h + 1) * HEAD_DIM] = o[r * ds:(r + 1) * ds]


def _selected_sample(page_table, cache_k, cache_v, z2, ks2, vs2, sel2, nb, past):
    t = z2.shape[0]
    ds = t // nb
    n_phys = cache_k.shape[0]
    ck = cache_k.reshape(n_phys, PAGE_ROWS, HEAD_DIM)
    cv = cache_v.reshape(n_phys, PAGE_ROWS, HEAD_DIM)
    steps = page_table.shape[1] // SLC_PAGES
    qw = N_HEADS * HEAD_DIM
    rows = N_KV * GROUP * ds
    e = _expand_matrix()
    win = lambda g: pl.BlockSpec(
        (ds, 128), lambda s, pt: (s // steps, g * (SEL_PAD // 128) + (s % steps) // WIN_STEPS))
    new = pl.BlockSpec((ds * N_KV, HEAD_DIM), lambda s, pt: (s // steps, 0))
    return pl.pallas_call(
        functools.partial(_selected_sample_kernel, ds=ds, steps=steps, past=past),
        out_shape=jax.ShapeDtypeStruct((t, qw), F32),
        grid_spec=pltpu.PrefetchScalarGridSpec(
            num_scalar_prefetch=1,
            grid=(nb * steps,),
            in_specs=[pl.BlockSpec(memory_space=pl.ANY), pl.BlockSpec(memory_space=pl.ANY),
                      pl.BlockSpec((ds, qw), lambda s, pt: (s // steps, COL_Q // qw)),
                      win(0), win(1), pl.BlockSpec(e.shape, lambda s, pt: (0, 0)), new, new],
            out_specs=pl.BlockSpec((ds, qw), lambda s, pt: (s // steps, 0)),
            scratch_shapes=[pltpu.VMEM((2, SLC_PAGES * PAGE_ROWS, HEAD_DIM), F32),
                            pltpu.VMEM((2, SLC_PAGES * PAGE_ROWS, HEAD_DIM), F32),
                            pltpu.SemaphoreType.DMA((2, 2)),
                            pltpu.VMEM((rows, 1), F32), pltpu.VMEM((rows, 1), F32),
                            pltpu.VMEM((rows, HEAD_DIM), F32)],
        ),
        compiler_params=pltpu.CompilerParams(dimension_semantics=("arbitrary",), vmem_limit_bytes=VMEM_LIMIT),
        name="selected_sample",
    )(page_table.reshape(-1), ck, cv, z2, sel2, sel2, e, ks2, vs2)


def _merge_kernel(x_ref, conv_ref, oc_ref, os_ref, ow_ref, gt_ref, zn_ref, w_ref, y_ref):
    gt = gt_ref[...]
    parts = [conv_ref[...].astype(BF16)]
    for h in range(N_HEADS):
        lanes = slice(h * HEAD_DIM, (h + 1) * HEAD_DIM)
        o = (gt[:, 3 * h:3 * h + 1] * oc_ref[:, lanes] + gt[:, 3 * h + 1:3 * h + 2] * os_ref[:, lanes]
             + gt[:, 3 * h + 2:3 * h + 3] * ow_ref[:, lanes])
        parts.append((o * _silu(zn_ref[:, lanes])).astype(BF16))
    y_ref[...] = x_ref[...] + _dot(jnp.concatenate(parts, axis=1), w_ref[...])


def _merge(x, conv_o, o_c, o_s, o_w, z, w_out_bf, tm=256):
    t, d = x.shape
    qw = N_HEADS * HEAD_DIM
    rowblk = lambda w: pl.BlockSpec((tm, w), lambda i: (i, 0))
    return pl.pallas_call(
        _merge_kernel,
        out_shape=jax.ShapeDtypeStruct((t, d), F32),
        grid=(t // tm,),
        in_specs=[rowblk(d), rowblk(C_CONV), rowblk(qw), rowblk(qw), rowblk(qw),
                  pl.BlockSpec((tm, 128), lambda i: (i, COL_GT // 128)),
                  pl.BlockSpec((tm, qw), lambda i: (i, COL_ZN // qw)),
                  pl.BlockSpec(w_out_bf.shape, lambda i: (0, 0))],
        out_specs=rowblk(d),
        compiler_params=pltpu.CompilerParams(dimension_semantics=("arbitrary",), vmem_limit_bytes=VMEM_LIMIT),
        name="merge",
    )(x, conv_o, o_c, o_s, o_w, z, z, w_out_bf)


N_GT = 3 * N_HEADS


def _pad_w_kernel(w_ref, o_ref):
    o_ref[:, 0:COL_GT] = w_ref[:, 0:COL_GT].astype(BF16)
    o_ref[:, COL_GT:COL_ZN] = jnp.zeros((o_ref.shape[0], COL_ZN - COL_GT), BF16)
    o_ref[:, COL_GT:COL_GT + N_GT] = w_ref[:, COL_GT:COL_GT + N_GT].astype(BF16)
    o_ref[:, COL_ZN:ZW] = w_ref[:, COL_GT + N_GT:COL_GT + N_GT + ZW - COL_ZN].astype(BF16)


def _pad_w_in(w_in, tr=256):
    d, n = w_in.shape
    assert n == COL_GT + N_GT + ZW - COL_ZN
    return pl.pallas_call(
        _pad_w_kernel,
        out_shape=jax.ShapeDtypeStruct((d, ZW), BF16),
        grid=(d // tr,),
        in_specs=[pl.BlockSpec((tr, n), lambda i: (i, 0))],
        out_specs=pl.BlockSpec((tr, ZW), lambda i: (i, 0)),
        compiler_params=pltpu.CompilerParams(dimension_semantics=("arbitrary",), vmem_limit_bytes=VMEM_LIMIT),
        name="pad_w_in",
    )(w_in)


def _cmp_weights(pe, w1, w2):
    half = CMP_STRIDE
    k = half * HEAD_DIM
    w1cat = jnp.concatenate([w1[:half].reshape(k, -1), w1[half:].reshape(k, -1)], axis=1).astype(BF16)
    pe2 = jnp.concatenate([pe[:half].reshape(1, k), pe[half:].reshape(1, k), jnp.zeros((6, k), pe.dtype)], axis=0)
    return w1cat, pe2, w2.astype(BF16)


def kernel(x_prompt, x_sample, cache_k_cmp, cache_v_cmp, cache_k_slc, cache_v_slc, state_k_win, state_v_win,
           state_conv, page_table, g_norm, w_in, pe_cmp_k, w_cmp_k1, w_cmp_k2, pe_cmp_v, w_cmp_v1, w_cmp_v2,
           g_q, g_k_cmp, g_k_slc, g_k_win, w_dw, b_dw, ln_g, ln_b, w_pw2, b_pw2, w_out):
    _, t, d = x_prompt.shape
    db, ds, _ = x_sample.shape
    past = page_table.shape[1] * PAGE_SIZE
    wb = state_k_win.shape[1]
    kvw = N_KV * HEAD_DIM
    assert ds < CMP_STRIDE and wb == WINDOW and t % SEL_BLOCK == 0 and t // SEL_BLOCK == 128

    w_p = _pad_w_in(w_in)
    w_pw_bf = w_pw2.astype(BF16)
    w_out_bf = w_out.astype(BF16)
    wk1, pek, wk2 = _cmp_weights(pe_cmp_k, w_cmp_k1, w_cmp_k2)
    wv1, pev, wv2 = _cmp_weights(pe_cmp_v, w_cmp_v1, w_cmp_v2)
    ones = jnp.ones((HEAD_DIM,), F32)

    xp = x_prompt.reshape(t, d)
    z, kc, vc, ks, vs, kw, vw = _in_proj(xp, g_norm, w_p, g_q, g_k_slc, g_k_win, tm=1024)
    conv_o, conv_st = _conv_prompt(z, w_dw, b_dw, ln_g, ln_b, w_pw_bf, b_pw2)
    nc = t // CMP_STRIDE
    fk, fv = _feats_prompt(kc, vc, wk1, wv1)
    kcmp = _cmp_finish(fk, wk1, pek, wk2, g_k_cmp, nc, True)
    vcmp = _cmp_finish(fv, wv1, pev, wv2, ones, nc, False)
    imp_m = _imp_matrix(nc, t // SEL_BLOCK, nc - 1, t // SEL_BLOCK)
    o_c, sel, anyb = _cmp_select(z, kcmp, vcmp, imp_m, tq=256)
    o_s = _selected_prompt(z, sel, anyb, tq=256)
    o_w = _window_prompt(z)
    y_prompt = _merge(xp, conv_o, o_c, o_s, o_w, z, w_out_bf)

    xs = x_sample.reshape(db * ds, d)
    z2, kc2, vc2, ks2, vs2, kw2, vw2 = _in_proj(xs, g_norm, w_p, g_q, g_k_slc, g_k_win, tm=db * ds)
    conv_o2, u2 = _conv_sample(z2, state_conv, w_dw, b_dw, ln_g, ln_b, w_pw_bf, b_pw2)
    nc2 = past // CMP_STRIDE
    fk2, fv2 = _feats_sample(page_table, cache_k_cmp, cache_v_cmp, wk1, wv1)
    kcmp2 = _cmp_finish(fk2, wk1, pek, wk2, g_k_cmp, nc2, True)
    vcmp2 = _cmp_finish(fv2, wv1, pev, wv2, ones, nc2, False)
    n_sel2 = -(-(past + ds) // SEL_BLOCK)
    assert n_sel2 <= SEL_PAD
    imp_m2 = _imp_matrix(nc2, SEL_PAD, nc2 - 1, n_sel2)
    rows = lambda a: a.reshape(-1, HEAD_DIM)
    o_c2, sel2, o_w2, k_win, v_win = _sample_small(z2, kw2, vw2, kcmp2, vcmp2, imp_m2, rows(state_k_win),
                                                   rows(state_v_win), db, past)
    o_s2 = _selected_sample(page_table, cache_k_slc, cache_v_slc, z2, ks2, vs2, sel2, db, past)
    y_sample = _merge(xs, conv_o2, o_c2, o_s2, o_w2, z2, w_out_bf)

    kv4 = lambda a, b: a.reshape(b, -1, N_KV, HEAD_DIM)
    return (y_prompt.reshape(1, t, d), y_sample.reshape(db, ds, d),
            kv4(kc, 1), kv4(vc, 1), kv4(ks, 1), kv4(vs, 1), kv4(kw, 1)[:, t - wb:], kv4(vw, 1)[:, t - wb:],
            conv_st[HALO - (CONV_WIDTH - 1):][None],
            kv4(kc2, db), kv4(vc2, db), kv4(ks2, db), kv4(vs2, db), kv4(k_win, db), kv4(v_win, db),
            jnp.concatenate([state_conv[:, ds:], u2.reshape(db, ds, C_CONV)], axis=1))
```

```python
import functools

import numpy as np
import jax
import jax.numpy as jnp
from jax import lax
from jax.experimental import pallas as pl
from jax.experimental.pallas import tpu as pltpu

F32 = jnp.float32
BF16 = jnp.bfloat16

HEAD_DIM = 128
N_HEADS = 8
N_KV = 2
GROUP = 4
C_CONV = 1024
CONV_WIDTH = 31
CMP_STRIDE = 16
CMP_BLOCK = 32
SEL_BLOCK = 64
N_SELECT = 16
WINDOW = 512
PAGE_SIZE = 128
EPS = 1e-6
SCALE = HEAD_DIM ** -0.5
SLOPES = tuple(2.0 ** -(h + 1) for h in range(N_HEADS))

TN = 512
N_TILES = 14
ZW = TN * N_TILES
COL_UA, COL_UB, COL_ZC, COL_Q = 0, 1024, 2048, 3072
COL_KC, COL_VC, COL_KS, COL_VS, COL_KW, COL_VW = 4096, 4352, 4608, 4864, 5120, 5376
COL_GT, COL_ZN = 5632, 6144
TILE_Q0, TILE_Q1, TILE_CMP, TILE_SLC, TILE_WIN, TILE_GT = 6, 7, 8, 9, 10, 11

NEG = -1e30
M_INIT = -1e29
VMEM_LIMIT = 48 * 1024 * 1024
INPROJ_VMEM_LIMIT = 56 * 1024 * 1024


def _sigmoid(x):
    return 1.0 / (1.0 + jnp.exp(-x))


def _silu(x):
    return x * _sigmoid(x)


def _dot(a, b):
    return jnp.dot(a, b, preferred_element_type=F32)


def _dot_nt(a, b):
    return lax.dot_general(a, b, (((1,), (1,)), ((), ())), preferred_element_type=F32)


def _rms(a, g):
    return a * lax.rsqrt(jnp.mean(a * a, axis=-1, keepdims=True) + EPS) * g


def _iota(shape, dim):
    return lax.broadcasted_iota(jnp.int32, shape, dim)


def _inproj_kernel(x_ref, gn_ref, w_ref, gq_ref, gks_ref, gkw_ref,
                   z_ref, kc_o, vc_o, ks_o, vs_o, kw_o, vw_o, xn_ref, *, tm):
    j = pl.program_id(1)

    @pl.when(j == 0)
    def _():
        x = x_ref[...]
        ms = jnp.mean(x * x, axis=-1, keepdims=True)
        xn_ref[...] = (x * lax.rsqrt(ms + EPS) * gn_ref[...]).astype(BF16)

    z_ref[...] = _dot(xn_ref[...], w_ref[...])

    def kv_tile(g_ref, k_o, v_o):
        for c in range(TN // HEAD_DIM):
            lanes = slice(c * HEAD_DIM, (c + 1) * HEAD_DIM)
            a = z_ref[:, lanes]
            if c < N_KV and g_ref is not None:
                a = _rms(a, g_ref[...])
                z_ref[:, lanes] = a
            (k_o if c < N_KV else v_o)[pl.ds(c % N_KV, tm, stride=N_KV), :] = a

    @pl.when((j == TILE_Q0) | (j == TILE_Q1))
    def _():
        for c in range(TN // HEAD_DIM):
            lanes = slice(c * HEAD_DIM, (c + 1) * HEAD_DIM)
            z_ref[:, lanes] = _rms(z_ref[:, lanes], gq_ref[...])

    @pl.when(j == TILE_CMP)
    def _():
        kv_tile(None, kc_o, vc_o)

    @pl.when(j == TILE_SLC)
    def _():
        kv_tile(gks_ref, ks_o, vs_o)

    @pl.when(j == TILE_WIN)
    def _():
        kv_tile(gkw_ref, kw_o, vw_o)

    @pl.when(j == TILE_GT)
    def _():
        z_ref[...] = _sigmoid(z_ref[...])


def _in_proj(x, g_norm, w_p, g_q, g_ks, g_kw, tm):
    t, d = x.shape
    row = lambda a: a.reshape(1, -1)
    kv_shape = jax.ShapeDtypeStruct((t * N_KV, HEAD_DIM), F32)
    kv_spec = pl.BlockSpec((tm * N_KV, HEAD_DIM), lambda i, j: (i, 0))
    return pl.pallas_call(
        functools.partial(_inproj_kernel, tm=tm),
        out_shape=(jax.ShapeDtypeStruct((t, ZW), F32),) + (kv_shape,) * 6,
        grid=(t // tm, N_TILES),
        in_specs=[
            pl.BlockSpec((tm, d), lambda i, j: (i, 0)),
            pl.BlockSpec((1, d), lambda i, j: (0, 0)),
            pl.BlockSpec((d, TN), lambda i, j: (0, j)),
            pl.BlockSpec((1, HEAD_DIM), lambda i, j: (0, 0)),
            pl.BlockSpec((1, HEAD_DIM), lambda i, j: (0, 0)),
            pl.BlockSpec((1, HEAD_DIM), lambda i, j: (0, 0)),
        ],
        out_specs=(pl.BlockSpec((tm, TN), lambda i, j: (i, j)),) + (kv_spec,) * 6,
        scratch_shapes=[pltpu.VMEM((tm, d), BF16)],
        compiler_params=pltpu.CompilerParams(
            dimension_semantics=("arbitrary", "arbitrary"), vmem_limit_bytes=INPROJ_VMEM_LIMIT),
        name="in_proj",
    )(x, row(g_norm), w_p, row(g_q), row(g_ks), row(g_kw))


HALO = 32
CONV_RB = 64
CONV_CB = 128


def _conv_tail(y, zc, lng_ref, lnb_ref, wpw_ref, bpw_ref):
    mu = jnp.mean(y, axis=-1, keepdims=True)
    yc = y - mu
    var = jnp.mean(yc * yc, axis=-1, keepdims=True)
    yn = yc * lax.rsqrt(var + EPS) * lng_ref[...] + lnb_ref[...]
    act = _silu(yn).astype(BF16)
    return (_dot(act, wpw_ref[...]) + bpw_ref[...]) * _silu(zc)


def _conv_prompt_kernel(ua_ref, ub_ref, zc_ref, uah_ref, ubh_ref, wdw_ref, bdw_ref, lng_ref, lnb_ref,
                        wpw_ref, bpw_ref, o_ref, st_ref, buf, ybuf, *, tt):
    i = pl.program_id(0)
    uh = uah_ref[...] * _sigmoid(ubh_ref[...])
    buf[0:HALO, :] = jnp.where(i > 0, uh, 0.0)
    buf[HALO:HALO + tt, :] = ua_ref[...] * _sigmoid(ub_ref[...])
    off = HALO - (CONV_WIDTH - 1)
    for c0 in range(0, C_CONV, CONV_CB):
        lanes = slice(c0, c0 + CONV_CB)
        for r0 in range(0, tt, CONV_RB):
            acc = jnp.broadcast_to(bdw_ref[:, lanes], (CONV_RB, CONV_CB))
            for b in range(8):
                n = CONV_RB if b == 0 else CONV_RB + 8
                zb = None
                for a in range(-(-(off - b) // 8), (off + CONV_WIDTH - 1 - b) // 8 + 1):
                    k = 8 * a + b - off
                    term = wdw_ref[k:k + 1, lanes] * buf[r0 + 8 * a:r0 + 8 * a + n, lanes]
                    zb = term if zb is None else zb + term
                acc = acc + zb[b:b + CONV_RB]
            ybuf[r0:r0 + CONV_RB, lanes] = acc
    o_ref[...] = _conv_tail(ybuf[...], zc_ref[...], lng_ref, lnb_ref, wpw_ref, bpw_ref)

    @pl.when(i == pl.num_programs(0) - 1)
    def _():
        st_ref[...] = buf[tt:tt + HALO, :]


def _conv_prompt(z, w_dw, b_dw, ln_g, ln_b, w_pw_bf, b_pw, tt=256):
    t = z.shape[0]
    row = lambda a: a.reshape(1, -1)
    hb = tt // HALO
    cur = lambda c: pl.BlockSpec((tt, C_CONV), lambda i: (i, c))
    halo = lambda c: pl.BlockSpec((HALO, C_CONV), lambda i: (jnp.maximum(i * hb - 1, 0), c))
    full = lambda shape: pl.BlockSpec(shape, lambda i: (0, 0))
    return pl.pallas_call(
        functools.partial(_conv_prompt_kernel, tt=tt),
        out_shape=(jax.ShapeDtypeStruct((t, C_CONV), F32), jax.ShapeDtypeStruct((HALO, C_CONV), F32)),
        grid=(t // tt,),
        in_specs=[cur(0), cur(1), cur(2), halo(0), halo(1),
                  full((CONV_WIDTH, C_CONV)), full((1, C_CONV)), full((1, C_CONV)), full((1, C_CONV)),
                  full((C_CONV, C_CONV)), full((1, C_CONV))],
        out_specs=(pl.BlockSpec((tt, C_CONV), lambda i: (i, 0)), pl.BlockSpec((HALO, C_CONV), lambda i: (0, 0))),
        scratch_shapes=[pltpu.VMEM((HALO + tt, C_CONV), F32), pltpu.VMEM((tt, C_CONV), F32)],
        compiler_params=pltpu.CompilerParams(dimension_semantics=("arbitrary",), vmem_limit_bytes=VMEM_LIMIT),
        name="conv_prompt",
    )(z, z, z, z, z, w_dw, row(b_dw), row(ln_g), row(ln_b), w_pw_bf, row(b_pw))


ST_ROWS = 40


def _conv_sample_kernel(ua_ref, ub_ref, zc_ref, st_ref, wdw_ref, bdw_ref, lng_ref, lnb_ref, wpw_ref, bpw_ref,
                        o_ref, u_ref, fbuf, ybuf, *, nb, ds):
    u = ua_ref[...] * _sigmoid(ub_ref[...])
    u_ref[...] = u
    rows = _iota((nb * ds, 1), 0) & (ds - 1)
    acc_u = jnp.broadcast_to(bdw_ref[...], (nb * ds, C_CONV))
    for d in range(ds):
        sh = u if d == 0 else pltpu.roll(u, d, 0)
        acc_u = acc_u + jnp.where(rows >= d, sh, 0.0) * wdw_ref[CONV_WIDTH - 1 - d:CONV_WIDTH - d, :]
    ybuf[...] = acc_u
    fbuf[:, 24:ST_ROWS, :] = jnp.zeros((nb, ST_ROWS - 24, C_CONV), F32)
    fbuf[:, 0:CONV_WIDTH - 1, :] = st_ref[...]

    def body(b, carry):
        acc = jnp.zeros((ds, C_CONV), F32)
        for k in range(CONV_WIDTH - 1):
            acc = acc + wdw_ref[k:k + 1, :] * fbuf[b, k:k + ds, :]
        r = pl.multiple_of(b * ds, ds)
        ybuf[pl.ds(r, ds), :] = ybuf[pl.ds(r, ds), :] + acc
        return carry

    lax.fori_loop(0, nb, body, 0)
    o_ref[...] = _conv_tail(ybuf[...], zc_ref[...], lng_ref, lnb_ref, wpw_ref, bpw_ref)


def _conv_sample(z2, state_conv, w_dw, b_dw, ln_g, ln_b, w_pw_bf, b_pw):
    nb, sw, _ = state_conv.shape
    t = z2.shape[0]
    ds = t // nb
    assert sw == CONV_WIDTH - 1 and ds == 8
    row = lambda a: a.reshape(1, -1)
    col = lambda c: pl.BlockSpec((t, C_CONV), lambda i: (0, c))
    full = lambda shape: pl.BlockSpec(shape, lambda i: (0,) * len(shape))
    return pl.pallas_call(
        functools.partial(_conv_sample_kernel, nb=nb, ds=ds),
        out_shape=(jax.ShapeDtypeStruct((t, C_CONV), F32), jax.ShapeDtypeStruct((t, C_CONV), F32)),
        grid=(1,),
        in_specs=[col(0), col(1), col(2), full((nb, sw, C_CONV)),
                  full((CONV_WIDTH, C_CONV)), full((1, C_CONV)), full((1, C_CONV)), full((1, C_CONV)),
                  full((C_CONV, C_CONV)), full((1, C_CONV))],
        out_specs=(full((t, C_CONV)), full((t, C_CONV))),
        scratch_shapes=[pltpu.VMEM((nb, ST_ROWS, C_CONV), F32), pltpu.VMEM((t, C_CONV), F32)],
        compiler_params=pltpu.CompilerParams(dimension_semantics=("arbitrary",), vmem_limit_bytes=VMEM_LIMIT),
        name="conv_sample",
    )(z2, z2, z2, state_conv, w_dw, row(b_dw), row(ln_g), row(ln_b), w_pw_bf, row(b_pw))


FEAT_W = 2 * N_KV * HEAD_DIM


N_FEAT = FEAT_W // HEAD_DIM


def _chunk_feats(piece, w):
    outs = []
    for g in range(N_KV):
        xg = jnp.concatenate([piece(c, g).astype(BF16) for c in range(CMP_STRIDE)], axis=1)
        outs.append(_dot(xg, w))
    return jnp.concatenate(outs, axis=1)


CHUNK_ROWS = CMP_STRIDE * N_KV


def _feats_kernel(xk_ref, xv_ref, wk_ref, wv_ref, fk_ref, fv_ref, *, tm):
    for x_ref, w_ref, f_ref in ((xk_ref, wk_ref, fk_ref), (xv_ref, wv_ref, fv_ref)):
        f = _chunk_feats(lambda c, g: x_ref[pl.ds(c * N_KV + g, tm, stride=CHUNK_ROWS), :], w_ref[...])
        for cb in range(N_FEAT):
            f_ref[cb] = f[:, cb * HEAD_DIM:(cb + 1) * HEAD_DIM]


def _feats_prompt(xk, xv, wk, wv, tm=128):
    nc = xk.shape[0] // CHUNK_ROWS
    full = lambda shape: pl.BlockSpec(shape, lambda i: (0, 0))
    rows = pl.BlockSpec((tm * CHUNK_ROWS, HEAD_DIM), lambda i: (i, 0))
    ospec = pl.BlockSpec((N_FEAT, tm, HEAD_DIM), lambda i: (0, i, 0))
    return pl.pallas_call(
        functools.partial(_feats_kernel, tm=tm),
        out_shape=(jax.ShapeDtypeStruct((N_FEAT, nc, HEAD_DIM), F32),) * 2,
        grid=(nc // tm,),
        in_specs=[rows, rows, full(wk.shape), full(wv.shape)],
        out_specs=(ospec, ospec),
        compiler_params=pltpu.CompilerParams(dimension_semantics=("arbitrary",), vmem_limit_bytes=VMEM_LIMIT),
        name="feats_prompt",
    )(xk, xv, wk, wv)


FEAT_PAGES = 32
CHUNKS_PER_PAGE = PAGE_SIZE // CMP_STRIDE
PAGE_ROWS = PAGE_SIZE * N_KV
FEAT_PITCH = PAGE_ROWS + 8


def _page_copies(pt_ref, srcs, bufs, sem, step, slot, pages, pitch):
    out = []
    for p in range(pages):
        page = pt_ref[step * pages + p]
        for n, (src, buf) in enumerate(zip(srcs, bufs)):
            out.append(pltpu.make_async_copy(src.at[page], buf.at[slot, pl.ds(p * pitch, PAGE_ROWS)], sem.at[n, slot]))
    return out


def _gather_pipeline(pt_ref, srcs, bufs, sem, pages, pitch):
    s = pl.program_id(0)
    slot = s % 2

    @pl.when(s == 0)
    def _():
        for c in _page_copies(pt_ref, srcs, bufs, sem, 0, 0, pages, pitch):
            c.start()

    @pl.when(s + 1 < pl.num_programs(0))
    def _():
        for c in _page_copies(pt_ref, srcs, bufs, sem, s + 1, 1 - slot, pages, pitch):
            c.start()

    for c in _page_copies(pt_ref, srcs, bufs, sem, s, slot, pages, pitch):
        c.wait()
    return slot


def _feats_gather_kernel(pt_ref, ck_hbm, cv_hbm, wk_ref, wv_ref, fk_ref, fv_ref, kbuf, vbuf, sem):
    slot = _gather_pipeline(pt_ref, (ck_hbm, cv_hbm), (kbuf, vbuf), sem, FEAT_PAGES, FEAT_PITCH)
    crows = CMP_STRIDE * N_KV

    def piece(buf):
        return lambda c, g: jnp.concatenate(
            [buf[slot, pl.ds(n * crows + c * N_KV + g, FEAT_PAGES, stride=FEAT_PITCH), :]
             for n in range(CHUNKS_PER_PAGE)], axis=0)

    for buf, w_ref, f_ref in ((kbuf, wk_ref, fk_ref), (vbuf, wv_ref, fv_ref)):
        f = _chunk_feats(piece(buf), w_ref[...])
        for n in range(CHUNKS_PER_PAGE):
            for cb in range(N_FEAT):
                f_ref[cb, pl.ds(n, FEAT_PAGES, stride=CHUNKS_PER_PAGE), :] = (
                    f[n * FEAT_PAGES:(n + 1) * FEAT_PAGES, cb * HEAD_DIM:(cb + 1) * HEAD_DIM])


def _feats_sample(page_table, cache_k, cache_v, wk, wv):
    n_pages = page_table.size
    n_phys = cache_k.shape[0]
    ck = cache_k.reshape(n_phys, PAGE_ROWS, HEAD_DIM)
    cv = cache_v.reshape(n_phys, PAGE_ROWS, HEAD_DIM)
    rows = FEAT_PAGES * CHUNKS_PER_PAGE
    brows = FEAT_PAGES * FEAT_PITCH
    wspec = pl.BlockSpec(wk.shape, lambda s, pt: (0, 0))
    ospec = pl.BlockSpec((N_FEAT, rows, HEAD_DIM), lambda s, pt: (0, s, 0))
    return pl.pallas_call(
        _feats_gather_kernel,
        out_shape=(jax.ShapeDtypeStruct((N_FEAT, n_pages * CHUNKS_PER_PAGE, HEAD_DIM), F32),) * 2,
        grid_spec=pltpu.PrefetchScalarGridSpec(
            num_scalar_prefetch=1,
            grid=(n_pages // FEAT_PAGES,),
            in_specs=[pl.BlockSpec(memory_space=pl.ANY), pl.BlockSpec(memory_space=pl.ANY), wspec, wspec],
            out_specs=(ospec, ospec),
            scratch_shapes=[pltpu.VMEM((2, brows, HEAD_DIM), F32), pltpu.VMEM((2, brows, HEAD_DIM), F32),
                            pltpu.SemaphoreType.DMA((2, 2))],
        ),
        compiler_params=pltpu.CompilerParams(dimension_semantics=("arbitrary",), vmem_limit_bytes=VMEM_LIMIT),
        name="feats_sample",
    )(page_table.reshape(-1), ck, cv, wk, wv)


def _cmp_finish_kernel(f_ref, w1_ref, pe_ref, w2_ref, g_ref, o_ref, *, nc, normalize):
    bias = _dot(pe_ref[...].astype(BF16), w1_ref[...])
    ba = bias[0:1, 0:HEAD_DIM]
    bb = bias[1:2, HEAD_DIM:2 * HEAD_DIM]
    row = _iota((nc, 1), 0)
    for g in range(N_KV):
        fa = f_ref[2 * g] + ba
        fb = f_ref[2 * g + 1] + bb
        hid = _silu(fa + pltpu.roll(fb, nc - 1, 0))
        o = _dot(hid.astype(BF16), w2_ref[...])
        if normalize:
            o = _rms(o, g_ref[...])
        o_ref[:, g * HEAD_DIM:(g + 1) * HEAD_DIM] = jnp.where(row < nc - 1, o, 0.0)


def _cmp_finish(feats, w1cat, pe2, w2_bf, gain, nc, normalize):
    nb = feats.shape[1] // nc
    full = lambda shape: pl.BlockSpec(shape, lambda b: (0, 0))
    return pl.pallas_call(
        functools.partial(_cmp_finish_kernel, nc=nc, normalize=normalize),
        out_shape=jax.ShapeDtypeStruct((nb * nc, N_KV * HEAD_DIM), F32),
        grid=(nb,),
        in_specs=[pl.BlockSpec((N_FEAT, nc, HEAD_DIM), lambda b: (0, b, 0)), full(w1cat.shape), full(pe2.shape),
                  full(w2_bf.shape), full((1, HEAD_DIM))],
        out_specs=pl.BlockSpec((nc, N_KV * HEAD_DIM), lambda b: (b, 0)),
        compiler_params=pltpu.CompilerParams(dimension_semantics=("arbitrary",), vmem_limit_bytes=VMEM_LIMIT),
        name="cmp_finish",
    )(feats, w1cat, pe2, w2_bf, gain.reshape(1, -1))


SELECT_LANES = 128


def _select_kernel(imp_ref, cur_ref, sel_ref, *any_refs, any_rows):
    n_rows, n_blk = imp_ref.shape
    shape = (n_blk, SELECT_LANES)
    j = _iota(shape, 0)
    jf = j.astype(F32)

    def body(_, carry):
        score, sel_t = carry
        m = jnp.max(score, axis=0, keepdims=True)
        first = jnp.min(jnp.where(score == m, jf, 1e9), axis=0, keepdims=True)
        hit = jf == first
        return jnp.where(hit, -2.0, score), jnp.where(hit, 1.0, sel_t)

    for c in range(n_rows // SELECT_LANES):
        rows = slice(c * SELECT_LANES, (c + 1) * SELECT_LANES)
        imp_t = imp_ref[rows, :].T
        cur = cur_ref[:, rows]
        forced = (j == 0) | (j == cur) | (j == cur - 1)
        score0 = jnp.where(forced, 1e30, jnp.where(j <= cur, imp_t, -1.0))
        _, sel_t = lax.fori_loop(0, N_SELECT, body, (score0, jnp.zeros(shape, F32)))
        sel_ref[rows, :] = sel_t.T
    sel = sel_ref[...]
    if any_refs:
        for n in range(sel.shape[0] // any_rows):
            blk = jnp.max(sel[n * any_rows:(n + 1) * any_rows], axis=0, keepdims=True)
            any_refs[0][n * 8:(n + 1) * 8, :] = jnp.broadcast_to(blk, (8, sel.shape[1]))


def _select(imp, cur, n_lanes, tr, any_rows=None):
    r, w = imp.shape
    groups = w // n_lanes
    out_shape = [jax.ShapeDtypeStruct((r, w), F32)]
    out_specs = [pl.BlockSpec((tr, n_lanes), lambda i, g: (i, g))]
    if any_rows is not None:
        out_shape.append(jax.ShapeDtypeStruct((r // any_rows * 8, w), F32))
        out_specs.append(pl.BlockSpec((tr // any_rows * 8, n_lanes), lambda i, g: (i, g)))
    return pl.pallas_call(
        functools.partial(_select_kernel, any_rows=any_rows),
        out_shape=tuple(out_shape),
        grid=(r // tr, groups),
        in_specs=[pl.BlockSpec((tr, n_lanes), lambda i, g: (i, g)), pl.BlockSpec((1, tr), lambda i, g: (0, i))],
        out_specs=tuple(out_specs),
        compiler_params=pltpu.CompilerParams(
            dimension_semantics=("arbitrary", "arbitrary"), vmem_limit_bytes=VMEM_LIMIT),
        name="select",
    )(imp, cur)


def _imp_matrix(n_cmp_pad, n_sel_pad, n_cmp, n_sel):
    r = SEL_BLOCK // CMP_STRIDE
    lead = CMP_BLOCK // CMP_STRIDE - 1
    m = np.zeros((n_cmp_pad, n_sel_pad), np.float32)
    for jb in range(n_sel):
        for o in range(-lead, r):
            start = o * CMP_STRIDE
            w = (min(start + CMP_BLOCK, SEL_BLOCK) - max(start, 0)) / CMP_BLOCK
            i = r * jb + o
            if 0 <= i < n_cmp:
                m[i, jb] += w
    return jnp.asarray(m)


def _cmp_select_kernel(q_ref, kc_ref, vc_ref, m_ref, oc_ref, imp_ref, *, tq, nc):
    i = pl.program_id(0)
    qpos = i * tq + _iota((tq, 1), 0)
    end = _iota((1, nc), 1) * CMP_STRIDE + (CMP_BLOCK - 1)
    dist = qpos - end
    mask = dist >= 0
    distf = dist.astype(F32)
    for g in range(N_KV):
        kg = kc_ref[:, g * HEAD_DIM:(g + 1) * HEAD_DIM].astype(BF16)
        vg = vc_ref[:, g * HEAD_DIM:(g + 1) * HEAD_DIM].astype(BF16)
        psum = jnp.zeros((tq, nc), F32)
        for r in range(GROUP):
            h = g * GROUP + r
            qh = q_ref[:, h * HEAD_DIM:(h + 1) * HEAD_DIM].astype(BF16)
            s = _dot_nt(qh, kg) * SCALE - SLOPES[h] * distf
            s = jnp.where(mask, s, NEG)
            m = jnp.max(s, axis=-1, keepdims=True)
            e = jnp.where(mask, jnp.exp(s - m), 0.0)
            p = e * (1.0 / jnp.maximum(jnp.sum(e, axis=-1, keepdims=True), 1e-30))
            psum = psum + p
            oc_ref[:, h * HEAD_DIM:(h + 1) * HEAD_DIM] = _dot(p.astype(BF16), vg)
        imp_ref[:, g * 128:(g + 1) * 128] = jnp.dot(psum, m_ref[...], precision=lax.Precision.HIGHEST,
                                                    preferred_element_type=F32)


def _cmp_select(z, kcmp, vcmp, imp_m, tq):
    t = z.shape[0]
    nc = kcmp.shape[0]
    n_sel = imp_m.shape[1]
    assert n_sel == 128
    full = lambda shape: pl.BlockSpec(shape, lambda i: (0, 0))
    return pl.pallas_call(
        functools.partial(_cmp_select_kernel, tq=tq, nc=nc),
        out_shape=(jax.ShapeDtypeStruct((t, N_HEADS * HEAD_DIM), F32), jax.ShapeDtypeStruct((t, N_KV * 128), F32)),
        grid=(t // tq,),
        in_specs=[pl.BlockSpec((tq, N_HEADS * HEAD_DIM), lambda i: (i, COL_Q // (N_HEADS * HEAD_DIM))),
                  full(kcmp.shape), full(vcmp.shape), full(imp_m.shape)],
        out_specs=(pl.BlockSpec((tq, N_HEADS * HEAD_DIM), lambda i: (i, 0)),
                   pl.BlockSpec((tq, N_KV * 128), lambda i: (i, 0))),
        compiler_params=pltpu.CompilerParams(dimension_semantics=("arbitrary",), vmem_limit_bytes=VMEM_LIMIT),
        name="cmp_select",
    )(z, kcmp, vcmp, imp_m)


def _flash_update(s, v_bf, m_ref, l_ref, acc_ref, rows):
    m_old = m_ref[rows, :]
    m_new = jnp.maximum(m_old, jnp.max(s, axis=-1, keepdims=True))
    alpha = jnp.exp(m_old - m_new)
    p = jnp.exp(s - m_new)
    l_ref[rows, :] = alpha * l_ref[rows, :] + jnp.sum(p, axis=-1, keepdims=True)
    acc_ref[rows, :] = alpha * acc_ref[rows, :] + _dot(p.astype(BF16), v_bf)
    m_ref[rows, :] = m_new


def _selected_prompt_kernel(flag_ref, q_ref, k_ref, v_ref, sel_ref, et_ref, o_ref,
                            kbf, vbf, qs, m_sc, l_sc, acc_sc, *, tq, tk):
    g = pl.program_id(0)
    qi = pl.program_id(1)
    nkt = kbf.shape[0] // tk
    step = g * pl.num_programs(1) + qi

    @pl.when(qi == 0)
    def _():
        kbf[...] = k_ref[...].astype(BF16)
        vbf[...] = v_ref[...].astype(BF16)

    for r in range(GROUP):
        qs[r * tq:(r + 1) * tq, :] = q_ref[:, r * HEAD_DIM:(r + 1) * HEAD_DIM].astype(BF16)
    m_sc[...] = jnp.full(m_sc.shape, M_INIT, F32)
    l_sc[...] = jnp.zeros(l_sc.shape, F32)
    acc_sc[...] = jnp.zeros(acc_sc.shape, F32)
    sel_bf = sel_ref[...].astype(BF16)
    qpos = qi * tq + _iota((tq, 1), 0)

    def body(ki, carry):
        @pl.when(flag_ref[step * nkt + ki] > 0)
        def _():
            k0 = pl.multiple_of(ki * tk, tk)
            k = kbf[pl.ds(k0, tk), :]
            v = vbf[pl.ds(k0, tk), :]
            selexp = _dot_nt(sel_bf, et_ref[pl.ds(k0, tk), :])
            kpos = ki * tk + _iota((1, tk), 1)
            mb = jnp.where((selexp > 0.5) & (kpos <= qpos), 0.0, NEG)
            krel = (kpos - qi * tq).astype(F32)
            for r in range(GROUP):
                slope = jnp.where(g == 0, SLOPES[r], SLOPES[GROUP + r])
                rows = slice(r * tq, (r + 1) * tq)
                s = _dot_nt(qs[rows, :], k) * SCALE + (mb + slope * krel)
                _flash_update(s, v, m_sc, l_sc, acc_sc, rows)
        return carry

    lax.fori_loop(0, (qi * tq + tq - 1) // tk + 1, body, 0)
    for r in range(GROUP):
        rows = slice(r * tq, (r + 1) * tq)
        o_ref[:, r * HEAD_DIM:(r + 1) * HEAD_DIM] = acc_sc[rows, :] * (1.0 / l_sc[rows, :])


def _tile_flags(anyb, t, tq, tk):
    nq, nkt = t // tq, t // tk
    a = anyb.reshape(nq, 8, N_KV, nkt, tk // SEL_BLOCK)[:, 0]
    return jnp.transpose(jnp.max(a, axis=-1) > 0.5, (1, 0, 2)).astype(jnp.int32).reshape(-1)


def _selected_prompt(z, sel, anyb, tq, tk=512):
    t = z.shape[0]
    gw = GROUP * HEAD_DIM
    flags = _tile_flags(anyb, t, tq, tk)
    et = jnp.asarray(np.arange(t)[:, None] // SEL_BLOCK == np.arange(128)[None, :], dtype=BF16)
    return pl.pallas_call(
        functools.partial(_selected_prompt_kernel, tq=tq, tk=tk),
        out_shape=jax.ShapeDtypeStruct((t, N_HEADS * HEAD_DIM), F32),
        grid_spec=pltpu.PrefetchScalarGridSpec(
            num_scalar_prefetch=1,
            grid=(N_KV, t // tq),
            in_specs=[
                pl.BlockSpec((tq, gw), lambda g, qi, f: (qi, COL_Q // gw + g)),
                pl.BlockSpec((t, HEAD_DIM), lambda g, qi, f: (0, COL_KS // HEAD_DIM + g)),
                pl.BlockSpec((t, HEAD_DIM), lambda g, qi, f: (0, COL_VS // HEAD_DIM + g)),
                pl.BlockSpec((tq, 128), lambda g, qi, f: (qi, g)),
                pl.BlockSpec((t, 128), lambda g, qi, f: (0, 0)),
            ],
            out_specs=pl.BlockSpec((tq, gw), lambda g, qi, f: (qi, g)),
            scratch_shapes=[pltpu.VMEM((t, HEAD_DIM), BF16), pltpu.VMEM((t, HEAD_DIM), BF16),
                            pltpu.VMEM((GROUP * tq, HEAD_DIM), BF16), pltpu.VMEM((GROUP * tq, 1), F32),
                            pltpu.VMEM((GROUP * tq, 1), F32), pltpu.VMEM((GROUP * tq, HEAD_DIM), F32)],
        ),
        compiler_params=pltpu.CompilerParams(
            dimension_semantics=("arbitrary", "arbitrary"), vmem_limit_bytes=VMEM_LIMIT),
        name="selected_prompt",
    )(flags, z, z, z, sel, et)


def _window_prompt_kernel(q_ref, kp_ref, kc_ref, vp_ref, vc_ref, o_ref, *, tq):
    g = pl.program_id(0)
    qi = pl.program_id(1)
    k = jnp.concatenate([kp_ref[...], kc_ref[...]], axis=0).astype(BF16)
    v = jnp.concatenate([vp_ref[...], vc_ref[...]], axis=0).astype(BF16)
    krel = _iota((1, 2 * tq), 1) - tq
    dist = _iota((tq, 1), 0) - krel
    mask = (dist >= 0) & (dist < WINDOW) & ((krel >= 0) | (qi > 0))
    distf = dist.astype(F32)
    for r in range(GROUP):
        slope = jnp.where(g == 0, SLOPES[r], SLOPES[GROUP + r])
        qh = q_ref[:, r * HEAD_DIM:(r + 1) * HEAD_DIM].astype(BF16)
        s = jnp.where(mask, _dot_nt(qh, k) * SCALE - slope * distf, NEG)
        m = jnp.max(s, axis=-1, keepdims=True)
        e = jnp.where(mask, jnp.exp(s - m), 0.0)
        p = e * (1.0 / jnp.maximum(jnp.sum(e, axis=-1, keepdims=True), 1e-30))
        o_ref[:, r * HEAD_DIM:(r + 1) * HEAD_DIM] = _dot(p.astype(BF16), v)


def _window_prompt(z, tq=WINDOW):
    t = z.shape[0]
    gw = GROUP * HEAD_DIM
    prev = lambda c: pl.BlockSpec((tq, HEAD_DIM), lambda g, qi: (jnp.maximum(qi - 1, 0), c // HEAD_DIM + g))
    cur = lambda c: pl.BlockSpec((tq, HEAD_DIM), lambda g, qi: (qi, c // HEAD_DIM + g))
    return pl.pallas_call(
        functools.partial(_window_prompt_kernel, tq=tq),
        out_shape=jax.ShapeDtypeStruct((t, N_HEADS * HEAD_DIM), F32),
        grid=(N_KV, t // tq),
        in_specs=[pl.BlockSpec((tq, gw), lambda g, qi: (qi, COL_Q // gw + g)),
                  prev(COL_KW), cur(COL_KW), prev(COL_VW), cur(COL_VW)],
        out_specs=pl.BlockSpec((tq, gw), lambda g, qi: (qi, g)),
        compiler_params=pltpu.CompilerParams(
            dimension_semantics=("arbitrary", "arbitrary"), vmem_limit_bytes=VMEM_LIMIT),
        name="window_prompt",
    )(z, z, z, z, z)


SEL_PAD = 384


def _slope_col(g, ds):
    return jnp.concatenate([jnp.full((ds, 1), SLOPES[g * GROUP + r], F32) for r in range(GROUP)], axis=0)


def _stack_heads(q_ref, g):
    return jnp.concatenate([q_ref[:, (g * GROUP + r) * HEAD_DIM:(g * GROUP + r + 1) * HEAD_DIM]
                            for r in range(GROUP)], axis=0).astype(BF16)


def _pad_rows(a, n):
    return jnp.concatenate([a, jnp.zeros((n - a.shape[0], a.shape[1]), a.dtype)], axis=0)


def _sample_small_kernel(q_ref, kc_ref, vc_ref, m_ref, sk_ref, sv_ref, nk_ref, nv_ref,
                         oc_ref, imp_ref, ow_ref, ko_ref, vo_ref, *, ds, nc, past, wb):
    rows = GROUP * ds
    qidx = _iota((rows, 1), 0) & (ds - 1)
    spos = past + qidx
    end = _iota((1, nc), 1) * CMP_STRIDE + (CMP_BLOCK - 1)
    dist_c = spos - end
    mask_c = dist_c >= 0
    ist = _iota((1, wb), 1)
    dist_s = wb + qidx - ist
    mask_s = dist_s < WINDOW
    jn = _iota((1, 128), 1)
    dist_n = qidx - jn
    mask_n = (dist_n >= 0) & (jn < ds)
    for g in range(N_KV):
        lanes = slice(g * HEAD_DIM, (g + 1) * HEAD_DIM)
        qs = _stack_heads(q_ref, g)
        slope = _slope_col(g, ds)
        s = _dot_nt(qs, kc_ref[:, lanes].astype(BF16)) * SCALE - slope * dist_c.astype(F32)
        s = jnp.where(mask_c, s, NEG)
        m = jnp.max(s, axis=-1, keepdims=True)
        e = jnp.where(mask_c, jnp.exp(s - m), 0.0)
        p = e * (1.0 / jnp.maximum(jnp.sum(e, axis=-1, keepdims=True), 1e-30))
        o = _dot(p.astype(BF16), vc_ref[:, lanes].astype(BF16))
        psum = p[0:ds]
        for r in range(1, GROUP):
            psum = psum + p[r * ds:(r + 1) * ds]
        imp_ref[:, g * SEL_PAD:(g + 1) * SEL_PAD] = jnp.dot(psum, m_ref[...], precision=lax.Precision.HIGHEST,
                                                            preferred_element_type=F32)
        kst = sk_ref[pl.ds(g, wb, stride=N_KV), :].astype(BF16)
        vst = sv_ref[pl.ds(g, wb, stride=N_KV), :].astype(BF16)
        kn = _pad_rows(nk_ref[pl.ds(g, ds, stride=N_KV), :], 128).astype(BF16)
        vn = _pad_rows(nv_ref[pl.ds(g, ds, stride=N_KV), :], 128).astype(BF16)
        s1 = jnp.where(mask_s, _dot_nt(qs, kst) * SCALE - slope * dist_s.astype(F32), NEG)
        s2 = jnp.where(mask_n, _dot_nt(qs, kn) * SCALE - slope * dist_n.astype(F32), NEG)
        mw = jnp.maximum(jnp.max(s1, axis=-1, keepdims=True), jnp.max(s2, axis=-1, keepdims=True))
        e1 = jnp.where(mask_s, jnp.exp(s1 - mw), 0.0)
        e2 = jnp.where(mask_n, jnp.exp(s2 - mw), 0.0)
        inv = 1.0 / jnp.maximum(jnp.sum(e1, axis=-1, keepdims=True) + jnp.sum(e2, axis=-1, keepdims=True), 1e-30)
        w = _dot((e1 * inv).astype(BF16), vst) + _dot((e2 * inv).astype(BF16), vn)
        for r in range(GROUP):
            h = g * GROUP + r
            oc_ref[:, h * HEAD_DIM:(h + 1) * HEAD_DIM] = o[r * ds:(r + 1) * ds]
            ow_ref[:, h * HEAD_DIM:(h + 1) * HEAD_DIM] = w[r * ds:(r + 1) * ds]
    keep = (wb - ds) * N_KV
    for s_ref, n_ref, o_ref in ((sk_ref, nk_ref, ko_ref), (sv_ref, nv_ref, vo_ref)):
        o_ref[0:keep, :] = s_ref[ds * N_KV:wb * N_KV, :]
        o_ref[keep:wb * N_KV, :] = n_ref[...]


def _sample_small(z2, kw2, vw2, kcmp2, vcmp2, imp_m2, state_k, state_v, nb, past):
    t = z2.shape[0]
    ds = t // nb
    nc = kcmp2.shape[0] // nb
    wb = state_k.shape[0] // (nb * N_KV)
    qw = N_HEADS * HEAD_DIM
    kvw = N_KV * HEAD_DIM
    rows = lambda n: pl.BlockSpec((n * N_KV, HEAD_DIM), lambda b: (b, 0))
    return pl.pallas_call(
        functools.partial(_sample_small_kernel, ds=ds, nc=nc, past=past, wb=wb),
        out_shape=(jax.ShapeDtypeStruct((t, qw), F32), jax.ShapeDtypeStruct((t, N_KV * SEL_PAD), F32),
                   jax.ShapeDtypeStruct((t, qw), F32),
                   jax.ShapeDtypeStruct(state_k.shape, F32), jax.ShapeDtypeStruct(state_v.shape, F32)),
        grid=(nb,),
        in_specs=[pl.BlockSpec((ds, qw), lambda b: (b, COL_Q // qw)),
                  pl.BlockSpec((nc, kvw), lambda b: (b, 0)), pl.BlockSpec((nc, kvw), lambda b: (b, 0)),
                  pl.BlockSpec(imp_m2.shape, lambda b: (0, 0)),
                  rows(wb), rows(wb), rows(ds), rows(ds)],
        out_specs=(pl.BlockSpec((ds, qw), lambda b: (b, 0)), pl.BlockSpec((ds, N_KV * SEL_PAD), lambda b: (b, 0)),
                   pl.BlockSpec((ds, qw), lambda b: (b, 0)), rows(wb), rows(wb)),
        compiler_params=pltpu.CompilerParams(dimension_semantics=("arbitrary",), vmem_limit_bytes=VMEM_LIMIT),
        name="sample_small",
    )(z2, kcmp2, vcmp2, imp_m2, state_k, state_v, kw2, vw2)


SLC_PAGES = 32
SLC_KEYS = SLC_PAGES * PAGE_SIZE
SLC_BLOCKS = SLC_KEYS // SEL_BLOCK
WIN_STEPS = 128 // SLC_BLOCKS


def _expand_matrix():
    j = np.arange(128)[:, None]
    c = np.arange(SLC_KEYS)[None, :]
    e = np.concatenate([(j == w * SLC_BLOCKS + c // SEL_BLOCK) for w in range(WIN_STEPS)], axis=0)
    return jnp.asarray(e, dtype=BF16)


def _selected_sample_kernel(pt_ref, ck_hbm, cv_hbm, q_ref, sel0_ref, sel1_ref, e_ref, nk_ref, nv_ref, o_ref,
                            kbuf, vbuf, sem, m_sc, l_sc, acc_sc, *, ds, steps, past):
    slot = _gather_pipeline(pt_ref, (ck_hbm, cv_hbm), (kbuf, vbuf), sem, SLC_PAGES, PAGE_ROWS)
    kt = pl.program_id(0) % steps
    rows = GROUP * ds
    nk = SLC_KEYS
    qidx = _iota((rows, 1), 0) & (ds - 1)

    @pl.when(kt == 0)
    def _():
        m_sc[...] = jnp.full(m_sc.shape, M_INIT, F32)
        l_sc[...] = jnp.zeros(l_sc.shape, F32)
        acc_sc[...] = jnp.zeros(acc_sc.shape, F32)

    krel = (kt * nk - past + _iota((1, nk), 1)).astype(F32)
    e = e_ref[pl.ds(pl.multiple_of((kt % WIN_STEPS) * 128, 128), 128), :]
    for g, sel_ref in enumerate((sel0_ref, sel1_ref)):
        grows = slice(g * rows, (g + 1) * rows)
        qs = _stack_heads(q_ref, g)
        slope = _slope_col(g, ds)
        selexp = _dot(sel_ref[...].astype(BF16), e)
        mb = jnp.where(jnp.concatenate([selexp] * GROUP, axis=0) > 0.5, 0.0, NEG)
        kg = kbuf[slot, pl.ds(g, nk, stride=N_KV), :].astype(BF16)
        vg = vbuf[slot, pl.ds(g, nk, stride=N_KV), :].astype(BF16)
        s = _dot_nt(qs, kg) * SCALE + (mb + slope * krel)
        _flash_update(s, vg, m_sc, l_sc, acc_sc, grows)

    @pl.when(kt == steps - 1)
    def _():
        jn = _iota((1, 128), 1)
        mb_n = jnp.where((jn <= qidx) & (jn < ds), 0.0, NEG)
        for g in range(N_KV):
            grows = slice(g * rows, (g + 1) * rows)
            qs = _stack_heads(q_ref, g)
            kn = _pad_rows(nk_ref[pl.ds(g, ds, stride=N_KV), :], 128).astype(BF16)
            vn = _pad_rows(nv_ref[pl.ds(g, ds, stride=N_KV), :], 128).astype(BF16)
            s = _dot_nt(qs, kn) * SCALE + (mb_n + _slope_col(g, ds) * jn.astype(F32))
            _flash_update(s, vn, m_sc, l_sc, acc_sc, grows)
            o = acc_sc[grows, :] * (1.0 / l_sc[grows, :])
            for r in range(GROUP):
                h = g * GROUP + r
                o_ref[:, h * HEAD_DIM:(h + 1) * HEAD_DIM] = o[r * ds:(r + 1) * ds]


def _selected_sample(page_table, cache_k, cache_v, z2, ks2, vs2, sel2, nb, past):
    t = z2.shape[0]
    ds = t // nb
    n_phys = cache_k.shape[0]
    ck = cache_k.reshape(n_phys, PAGE_ROWS, HEAD_DIM)
    cv = cache_v.reshape(n_phys, PAGE_ROWS, HEAD_DIM)
    steps = page_table.shape[1] // SLC_PAGES
    qw = N_HEADS * HEAD_DIM
    rows = N_KV * GROUP * ds
    e = _expand_matrix()
    win = lambda g: pl.BlockSpec(
        (ds, 128), lambda s, pt: (s // steps, g * (SEL_PAD // 128) + (s % steps) // WIN_STEPS))
    new = pl.BlockSpec((ds * N_KV, HEAD_DIM), lambda s, pt: (s // steps, 0))
    return pl.pallas_call(
        functools.partial(_selected_sample_kernel, ds=ds, steps=steps, past=past),
        out_shape=jax.ShapeDtypeStruct((t, qw), F32),
        grid_spec=pltpu.PrefetchScalarGridSpec(
            num_scalar_prefetch=1,
            grid=(nb * steps,),
            in_specs=[pl.BlockSpec(memory_space=pl.ANY), pl.BlockSpec(memory_space=pl.ANY),
                      pl.BlockSpec((ds, qw), lambda s, pt: (s // steps, COL_Q // qw)),
                      win(0), win(1), pl.BlockSpec(e.shape, lambda s, pt: (0, 0)), new, new],
            out_specs=pl.BlockSpec((ds, qw), lambda s, pt: (s // steps, 0)),
            scratch_shapes=[pltpu.VMEM((2, SLC_PAGES * PAGE_ROWS, HEAD_DIM), F32),
                            pltpu.VMEM((2, SLC_PAGES * PAGE_ROWS, HEAD_DIM), F32),
                            pltpu.SemaphoreType.DMA((2, 2)),
                            pltpu.VMEM((rows, 1), F32), pltpu.VMEM((rows, 1), F32),
                            pltpu.VMEM((rows, HEAD_DIM), F32)],
        ),
        compiler_params=pltpu.CompilerParams(dimension_semantics=("arbitrary",), vmem_limit_bytes=VMEM_LIMIT),
        name="selected_sample",
    )(page_table.reshape(-1), ck, cv, z2, sel2, sel2, e, ks2, vs2)


def _merge_kernel(x_ref, conv_ref, oc_ref, os_ref, ow_ref, gt_ref, zn_ref, w_ref, y_ref):
    gt = gt_ref[...]
    parts = [conv_ref[...].astype(BF16)]
    for h in range(N_HEADS):
        lanes = slice(h * HEAD_DIM, (h + 1) * HEAD_DIM)
        o = (gt[:, 3 * h:3 * h + 1] * oc_ref[:, lanes] + gt[:, 3 * h + 1:3 * h + 2] * os_ref[:, lanes]
             + gt[:, 3 * h + 2:3 * h + 3] * ow_ref[:, lanes])
        parts.append((o * _silu(zn_ref[:, lanes])).astype(BF16))
    y_ref[...] = x_ref[...] + _dot(jnp.concatenate(parts, axis=1), w_ref[...])


def _merge(x, conv_o, o_c, o_s, o_w, z, w_out_bf, tm=256):
    t, d = x.shape
    qw = N_HEADS * HEAD_DIM
    rowblk = lambda w: pl.BlockSpec((tm, w), lambda i: (i, 0))
    return pl.pallas_call(
        _merge_kernel,
        out_shape=jax.ShapeDtypeStruct((t, d), F32),
        grid=(t // tm,),
        in_specs=[rowblk(d), rowblk(C_CONV), rowblk(qw), rowblk(qw), rowblk(qw),
                  pl.BlockSpec((tm, 128), lambda i: (i, COL_GT // 128)),
                  pl.BlockSpec((tm, qw), lambda i: (i, COL_ZN // qw)),
                  pl.BlockSpec(w_out_bf.shape, lambda i: (0, 0))],
        out_specs=rowblk(d),
        compiler_params=pltpu.CompilerParams(dimension_semantics=("arbitrary",), vmem_limit_bytes=VMEM_LIMIT),
        name="merge",
    )(x, conv_o, o_c, o_s, o_w, z, z, w_out_bf)


N_GT = 3 * N_HEADS


def _pad_w_kernel(wt_hbm, o_ref, buf, sem):
    j = pl.program_id(0)
    src = jnp.where(j <= TILE_GT, j * TN, j * TN - COL_ZN + COL_GT + N_GT)
    copy = pltpu.make_async_copy(wt_hbm.at[pl.ds(pl.multiple_of(src, 8), TN)], buf, sem)
    copy.start()
    copy.wait()
    w = buf[...]
    keep = (j != TILE_GT) | (_iota((TN, 1), 0) < N_GT)
    o_ref[...] = jnp.where(keep, w, 0.0).T.astype(BF16)


def _pad_w_in(w_in):
    d, n = w_in.shape
    assert n == COL_GT + N_GT + ZW - COL_ZN and COL_ZN == (TILE_GT + 1) * TN
    return pl.pallas_call(
        _pad_w_kernel,
        out_shape=jax.ShapeDtypeStruct((d, ZW), BF16),
        grid=(N_TILES,),
        in_specs=[pl.BlockSpec(memory_space=pl.ANY)],
        out_specs=pl.BlockSpec((d, TN), lambda j: (0, j)),
        scratch_shapes=[pltpu.VMEM((TN, d), F32), pltpu.SemaphoreType.DMA(())],
        compiler_params=pltpu.CompilerParams(dimension_semantics=("arbitrary",), vmem_limit_bytes=VMEM_LIMIT),
        name="pad_w_in",
    )(w_in.T)


def _cmp_weights(pe, w1, w2):
    half = CMP_STRIDE
    k = half * HEAD_DIM
    w1cat = jnp.concatenate([w1[:half].reshape(k, -1), w1[half:].reshape(k, -1)], axis=1).astype(BF16)
    pe2 = jnp.concatenate([pe[:half].reshape(1, k), pe[half:].reshape(1, k), jnp.zeros((6, k), pe.dtype)], axis=0)
    return w1cat, pe2, w2.astype(BF16)


def kernel(x_prompt, x_sample, cache_k_cmp, cache_v_cmp, cache_k_slc, cache_v_slc, state_k_win, state_v_win,
           state_conv, page_table, g_norm, w_in, pe_cmp_k, w_cmp_k1, w_cmp_k2, pe_cmp_v, w_cmp_v1, w_cmp_v2,
           g_q, g_k_cmp, g_k_slc, g_k_win, w_dw, b_dw, ln_g, ln_b, w_pw2, b_pw2, w_out):
    _, t, d = x_prompt.shape
    db, ds, _ = x_sample.shape
    past = page_table.shape[1] * PAGE_SIZE
    wb = state_k_win.shape[1]
    kvw = N_KV * HEAD_DIM
    assert ds < CMP_STRIDE and wb == WINDOW and t % SEL_BLOCK == 0 and t // SEL_BLOCK == 128

    w_p = _pad_w_in(w_in)
    w_pw_bf = w_pw2.astype(BF16)
    w_out_bf = w_out.astype(BF16)
    wk1, pek, wk2 = _cmp_weights(pe_cmp_k, w_cmp_k1, w_cmp_k2)
    wv1, pev, wv2 = _cmp_weights(pe_cmp_v, w_cmp_v1, w_cmp_v2)
    ones = jnp.ones((HEAD_DIM,), F32)

    xp = x_prompt.reshape(t, d)
    z, kc, vc, ks, vs, kw, vw = _in_proj(xp, g_norm, w_p, g_q, g_k_slc, g_k_win, tm=1024)
    conv_o, conv_st = _conv_prompt(z, w_dw, b_dw, ln_g, ln_b, w_pw_bf, b_pw2)
    nc = t // CMP_STRIDE
    fk, fv = _feats_prompt(kc, vc, wk1, wv1)
    kcmp = _cmp_finish(fk, wk1, pek, wk2, g_k_cmp, nc, True)
    vcmp = _cmp_finish(fv, wv1, pev, wv2, ones, nc, False)
    imp_m = _imp_matrix(nc, t // SEL_BLOCK, nc - 1, t // SEL_BLOCK)
    o_c, imp = _cmp_select(z, kcmp, vcmp, imp_m, tq=256)
    cur = (jnp.arange(t, dtype=jnp.int32) // SEL_BLOCK).reshape(1, t)
    sel, anyb = _select(imp, cur, 128, tr=512, any_rows=256)
    o_s = _selected_prompt(z, sel, anyb, tq=256)
    o_w = _window_prompt(z)
    y_prompt = _merge(xp, conv_o, o_c, o_s, o_w, z, w_out_bf)

    xs = x_sample.reshape(db * ds, d)
    z2, kc2, vc2, ks2, vs2, kw2, vw2 = _in_proj(xs, g_norm, w_p, g_q, g_k_slc, g_k_win, tm=db * ds)
    conv_o2, u2 = _conv_sample(z2, state_conv, w_dw, b_dw, ln_g, ln_b, w_pw_bf, b_pw2)
    nc2 = past // CMP_STRIDE
    fk2, fv2 = _feats_sample(page_table, cache_k_cmp, cache_v_cmp, wk1, wv1)
    kcmp2 = _cmp_finish(fk2, wk1, pek, wk2, g_k_cmp, nc2, True)
    vcmp2 = _cmp_finish(fv2, wv1, pev, wv2, ones, nc2, False)
    n_sel2 = -(-(past + ds) // SEL_BLOCK)
    assert n_sel2 <= SEL_PAD
    imp_m2 = _imp_matrix(nc2, SEL_PAD, nc2 - 1, n_sel2)
    rows = lambda a: a.reshape(-1, HEAD_DIM)
    o_c2, imp2, o_w2, k_win, v_win = _sample_small(z2, kw2, vw2, kcmp2, vcmp2, imp_m2, rows(state_k_win),
                                                   rows(state_v_win), db, past)
    cur2 = ((past + jnp.arange(db * ds, dtype=jnp.int32) % ds) // SEL_BLOCK).reshape(1, db * ds)
    sel2, = _select(imp2, cur2, SEL_PAD, tr=db * ds)
    o_s2 = _selected_sample(page_table, cache_k_slc, cache_v_slc, z2, ks2, vs2, sel2, db, past)
    y_sample = _merge(xs, conv_o2, o_c2, o_s2, o_w2, z2, w_out_bf)

    kv4 = lambda a, b: a.reshape(b, -1, N_KV, HEAD_DIM)
    return (y_prompt.reshape(1, t, d), y_sample.reshape(db, ds, d),
            kv4(kc, 1), kv4(vc, 1), kv4(ks, 1), kv4(vs, 1), kv4(kw, 1)[:, t - wb:], kv4(vw, 1)[:, t - wb:],
            conv_st[HALO - (CONV_WIDTH - 1):][None],
            kv4(kc2, db), kv4(vc2, db), kv4(ks2, db), kv4(vs2, db), kv4(k_win, db), kv4(v_win, db),
            jnp.concatenate([state_conv[:, ds:], u2.reshape(db, ds, C_CONV)], axis=1))
```

```python
import functools

import numpy as np
import jax
import jax.numpy as jnp
from jax import lax
from jax.experimental import pallas as pl
from jax.experimental.pallas import tpu as pltpu

F32 = jnp.float32
BF16 = jnp.bfloat16

HEAD_DIM = 128
N_HEADS = 8
N_KV = 2
GROUP = 4
C_CONV = 1024
CONV_WIDTH = 31
CMP_STRIDE = 16
CMP_BLOCK = 32
SEL_BLOCK = 64
N_SELECT = 16
WINDOW = 512
PAGE_SIZE = 128
EPS = 1e-6
SCALE = HEAD_DIM ** -0.5
SLOPES = tuple(2.0 ** -(h + 1) for h in range(N_HEADS))

TN = 512
N_TILES = 14
ZW = TN * N_TILES
COL_UA, COL_UB, COL_ZC, COL_Q = 0, 1024, 2048, 3072
COL_KC, COL_VC, COL_KS, COL_VS, COL_KW, COL_VW = 4096, 4352, 4608, 4864, 5120, 5376
COL_GT, COL_ZN = 5632, 6144
TILE_Q0, TILE_Q1, TILE_CMP, TILE_SLC, TILE_WIN, TILE_GT = 6, 7, 8, 9, 10, 11

NEG = -1e30
M_INIT = -1e29
VMEM_LIMIT = 48 * 1024 * 1024
INPROJ_VMEM_LIMIT = 56 * 1024 * 1024


def _sigmoid(x):
    return 1.0 / (1.0 + jnp.exp(-x))


def _silu(x):
    return x * _sigmoid(x)


def _dot(a, b):
    return jnp.dot(a, b, preferred_element_type=F32)


def _dot_nt(a, b):
    return lax.dot_general(a, b, (((1,), (1,)), ((), ())), preferred_element_type=F32)


def _rms(a, g):
    return a * lax.rsqrt(jnp.mean(a * a, axis=-1, keepdims=True) + EPS) * g


def _iota(shape, dim):
    return lax.broadcasted_iota(jnp.int32, shape, dim)


def _inproj_kernel(x_ref, gn_ref, w_ref, gq_ref, gks_ref, gkw_ref,
                   z_ref, kc_o, vc_o, ks_o, vs_o, kw_o, vw_o, xn_ref, *, tm):
    j = pl.program_id(1)

    @pl.when(j == 0)
    def _():
        x = x_ref[...]
        ms = jnp.mean(x * x, axis=-1, keepdims=True)
        xn_ref[...] = (x * lax.rsqrt(ms + EPS) * gn_ref[...]).astype(BF16)

    z_ref[...] = _dot(xn_ref[...], w_ref[...])

    def kv_tile(g_ref, k_o, v_o):
        for c in range(TN // HEAD_DIM):
            lanes = slice(c * HEAD_DIM, (c + 1) * HEAD_DIM)
            a = z_ref[:, lanes]
            if c < N_KV and g_ref is not None:
                a = _rms(a, g_ref[...])
                z_ref[:, lanes] = a
            (k_o if c < N_KV else v_o)[pl.ds(c % N_KV, tm, stride=N_KV), :] = a

    @pl.when((j == TILE_Q0) | (j == TILE_Q1))
    def _():
        for c in range(TN // HEAD_DIM):
            lanes = slice(c * HEAD_DIM, (c + 1) * HEAD_DIM)
            z_ref[:, lanes] = _rms(z_ref[:, lanes], gq_ref[...])

    @pl.when(j == TILE_CMP)
    def _():
        kv_tile(None, kc_o, vc_o)

    @pl.when(j == TILE_SLC)
    def _():
        kv_tile(gks_ref, ks_o, vs_o)

    @pl.when(j == TILE_WIN)
    def _():
        kv_tile(gkw_ref, kw_o, vw_o)

    @pl.when(j == TILE_GT)
    def _():
        z_ref[...] = _sigmoid(z_ref[...])


def _in_proj(x, g_norm, w_p, g_q, g_ks, g_kw, tm):
    t, d = x.shape
    row = lambda a: a.reshape(1, -1)
    kv_shape = jax.ShapeDtypeStruct((t * N_KV, HEAD_DIM), F32)
    kv_spec = pl.BlockSpec((tm * N_KV, HEAD_DIM), lambda i, j: (i, 0))
    return pl.pallas_call(
        functools.partial(_inproj_kernel, tm=tm),
        out_shape=(jax.ShapeDtypeStruct((t, ZW), F32),) + (kv_shape,) * 6,
        grid=(t // tm, N_TILES),
        in_specs=[
            pl.BlockSpec((tm, d), lambda i, j: (i, 0)),
            pl.BlockSpec((1, d), lambda i, j: (0, 0)),
            pl.BlockSpec((d, TN), lambda i, j: (0, j)),
            pl.BlockSpec((1, HEAD_DIM), lambda i, j: (0, 0)),
            pl.BlockSpec((1, HEAD_DIM), lambda i, j: (0, 0)),
            pl.BlockSpec((1, HEAD_DIM), lambda i, j: (0, 0)),
        ],
        out_specs=(pl.BlockSpec((tm, TN), lambda i, j: (i, j)),) + (kv_spec,) * 6,
        scratch_shapes=[pltpu.VMEM((tm, d), BF16)],
        compiler_params=pltpu.CompilerParams(
            dimension_semantics=("arbitrary", "arbitrary"), vmem_limit_bytes=INPROJ_VMEM_LIMIT),
        name="in_proj",
    )(x, row(g_norm), w_p, row(g_q), row(g_ks), row(g_kw))


HALO = 32
CONV_RB = 64
CONV_CB = 128


def _conv_tail(y, zc, lng_ref, lnb_ref, wpw_ref, bpw_ref):
    mu = jnp.mean(y, axis=-1, keepdims=True)
    yc = y - mu
    var = jnp.mean(yc * yc, axis=-1, keepdims=True)
    yn = yc * lax.rsqrt(var + EPS) * lng_ref[...] + lnb_ref[...]
    act = _silu(yn).astype(BF16)
    return (_dot(act, wpw_ref[...]) + bpw_ref[...]) * _silu(zc)


def _conv_prompt_kernel(ua_ref, ub_ref, zc_ref, uah_ref, ubh_ref, wdw_ref, bdw_ref, lng_ref, lnb_ref,
                        wpw_ref, bpw_ref, o_ref, st_ref, buf, ybuf, *, tt):
    i = pl.program_id(0)
    uh = uah_ref[...] * _sigmoid(ubh_ref[...])
    buf[0:HALO, :] = jnp.where(i > 0, uh, 0.0)
    buf[HALO:HALO + tt, :] = ua_ref[...] * _sigmoid(ub_ref[...])
    off = HALO - (CONV_WIDTH - 1)
    for c0 in range(0, C_CONV, CONV_CB):
        lanes = slice(c0, c0 + CONV_CB)
        for r0 in range(0, tt, CONV_RB):
            acc = jnp.broadcast_to(bdw_ref[:, lanes], (CONV_RB, CONV_CB))
            for b in range(8):
                n = CONV_RB if b == 0 else CONV_RB + 8
                zb = None
                for a in range(-(-(off - b) // 8), (off + CONV_WIDTH - 1 - b) // 8 + 1):
                    k = 8 * a + b - off
                    term = wdw_ref[k:k + 1, lanes] * buf[r0 + 8 * a:r0 + 8 * a + n, lanes]
                    zb = term if zb is None else zb + term
                acc = acc + zb[b:b + CONV_RB]
            ybuf[r0:r0 + CONV_RB, lanes] = acc
    o_ref[...] = _conv_tail(ybuf[...], zc_ref[...], lng_ref, lnb_ref, wpw_ref, bpw_ref)

    @pl.when(i == pl.num_programs(0) - 1)
    def _():
        st_ref[...] = buf[tt:tt + HALO, :]


def _conv_prompt(z, w_dw, b_dw, ln_g, ln_b, w_pw_bf, b_pw, tt=256):
    t = z.shape[0]
    row = lambda a: a.reshape(1, -1)
    hb = tt // HALO
    cur = lambda c: pl.BlockSpec((tt, C_CONV), lambda i: (i, c))
    halo = lambda c: pl.BlockSpec((HALO, C_CONV), lambda i: (jnp.maximum(i * hb - 1, 0), c))
    full = lambda shape: pl.BlockSpec(shape, lambda i: (0, 0))
    return pl.pallas_call(
        functools.partial(_conv_prompt_kernel, tt=tt),
        out_shape=(jax.ShapeDtypeStruct((t, C_CONV), F32), jax.ShapeDtypeStruct((HALO, C_CONV), F32)),
        grid=(t // tt,),
        in_specs=[cur(0), cur(1), cur(2), halo(0), halo(1),
                  full((CONV_WIDTH, C_CONV)), full((1, C_CONV)), full((1, C_CONV)), full((1, C_CONV)),
                  full((C_CONV, C_CONV)), full((1, C_CONV))],
        out_specs=(pl.BlockSpec((tt, C_CONV), lambda i: (i, 0)), pl.BlockSpec((HALO, C_CONV), lambda i: (0, 0))),
        scratch_shapes=[pltpu.VMEM((HALO + tt, C_CONV), F32), pltpu.VMEM((tt, C_CONV), F32)],
        compiler_params=pltpu.CompilerParams(dimension_semantics=("arbitrary",), vmem_limit_bytes=VMEM_LIMIT),
        name="conv_prompt",
    )(z, z, z, z, z, w_dw, row(b_dw), row(ln_g), row(ln_b), w_pw_bf, row(b_pw))


ST_ROWS = 40


def _conv_sample_kernel(ua_ref, ub_ref, zc_ref, st_ref, wdw_ref, bdw_ref, lng_ref, lnb_ref, wpw_ref, bpw_ref,
                        o_ref, u_ref, fbuf, ybuf, *, nb, ds):
    u = ua_ref[...] * _sigmoid(ub_ref[...])
    u_ref[...] = u
    rows = _iota((nb * ds, 1), 0) & (ds - 1)
    acc_u = jnp.broadcast_to(bdw_ref[...], (nb * ds, C_CONV))
    for d in range(ds):
        sh = u if d == 0 else pltpu.roll(u, d, 0)
        acc_u = acc_u + jnp.where(rows >= d, sh, 0.0) * wdw_ref[CONV_WIDTH - 1 - d:CONV_WIDTH - d, :]
    ybuf[...] = acc_u
    fbuf[:, 24:ST_ROWS, :] = jnp.zeros((nb, ST_ROWS - 24, C_CONV), F32)
    fbuf[:, 0:CONV_WIDTH - 1, :] = st_ref[...]

    def body(b, carry):
        acc = jnp.zeros((ds, C_CONV), F32)
        for k in range(CONV_WIDTH - 1):
            acc = acc + wdw_ref[k:k + 1, :] * fbuf[b, k:k + ds, :]
        r = pl.multiple_of(b * ds, ds)
        ybuf[pl.ds(r, ds), :] = ybuf[pl.ds(r, ds), :] + acc
        return carry

    lax.fori_loop(0, nb, body, 0)
    o_ref[...] = _conv_tail(ybuf[...], zc_ref[...], lng_ref, lnb_ref, wpw_ref, bpw_ref)


def _conv_sample(z2, state_conv, w_dw, b_dw, ln_g, ln_b, w_pw_bf, b_pw):
    nb, sw, _ = state_conv.shape
    t = z2.shape[0]
    ds = t // nb
    assert sw == CONV_WIDTH - 1 and ds == 8
    row = lambda a: a.reshape(1, -1)
    col = lambda c: pl.BlockSpec((t, C_CONV), lambda i: (0, c))
    full = lambda shape: pl.BlockSpec(shape, lambda i: (0,) * len(shape))
    return pl.pallas_call(
        functools.partial(_conv_sample_kernel, nb=nb, ds=ds),
        out_shape=(jax.ShapeDtypeStruct((t, C_CONV), F32), jax.ShapeDtypeStruct((t, C_CONV), F32)),
        grid=(1,),
        in_specs=[col(0), col(1), col(2), full((nb, sw, C_CONV)),
                  full((CONV_WIDTH, C_CONV)), full((1, C_CONV)), full((1, C_CONV)), full((1, C_CONV)),
                  full((C_CONV, C_CONV)), full((1, C_CONV))],
        out_specs=(full((t, C_CONV)), full((t, C_CONV))),
        scratch_shapes=[pltpu.VMEM((nb, ST_ROWS, C_CONV), F32), pltpu.VMEM((t, C_CONV), F32)],
        compiler_params=pltpu.CompilerParams(dimension_semantics=("arbitrary",), vmem_limit_bytes=VMEM_LIMIT),
        name="conv_sample",
    )(z2, z2, z2, state_conv, w_dw, row(b_dw), row(ln_g), row(ln_b), w_pw_bf, row(b_pw))


FEAT_W = 2 * N_KV * HEAD_DIM


N_FEAT = FEAT_W // HEAD_DIM


def _chunk_feats(piece, w):
    outs = []
    for g in range(N_KV):
        xg = jnp.concatenate([piece(c, g).astype(BF16) for c in range(CMP_STRIDE)], axis=1)
        outs.append(_dot(xg, w))
    return jnp.concatenate(outs, axis=1)


CHUNK_ROWS = CMP_STRIDE * N_KV


def _feats_kernel(xk_ref, xv_ref, wk_ref, wv_ref, fk_ref, fv_ref, *, tm):
    for x_ref, w_ref, f_ref in ((xk_ref, wk_ref, fk_ref), (xv_ref, wv_ref, fv_ref)):
        f = _chunk_feats(lambda c, g: x_ref[pl.ds(c * N_KV + g, tm, stride=CHUNK_ROWS), :], w_ref[...])
        for cb in range(N_FEAT):
            f_ref[cb] = f[:, cb * HEAD_DIM:(cb + 1) * HEAD_DIM]


def _feats_prompt(xk, xv, wk, wv, tm=128):
    nc = xk.shape[0] // CHUNK_ROWS
    full = lambda shape: pl.BlockSpec(shape, lambda i: (0, 0))
    rows = pl.BlockSpec((tm * CHUNK_ROWS, HEAD_DIM), lambda i: (i, 0))
    ospec = pl.BlockSpec((N_FEAT, tm, HEAD_DIM), lambda i: (0, i, 0))
    return pl.pallas_call(
        functools.partial(_feats_kernel, tm=tm),
        out_shape=(jax.ShapeDtypeStruct((N_FEAT, nc, HEAD_DIM), F32),) * 2,
        grid=(nc // tm,),
        in_specs=[rows, rows, full(wk.shape), full(wv.shape)],
        out_specs=(ospec, ospec),
        compiler_params=pltpu.CompilerParams(dimension_semantics=("arbitrary",), vmem_limit_bytes=VMEM_LIMIT),
        name="feats_prompt",
    )(xk, xv, wk, wv)


FEAT_PAGES = 32
CHUNKS_PER_PAGE = PAGE_SIZE // CMP_STRIDE
PAGE_ROWS = PAGE_SIZE * N_KV
FEAT_PITCH = PAGE_ROWS + 8


def _page_copies(pt_ref, srcs, bufs, sem, step, slot, pages, pitch):
    out = []
    for p in range(pages):
        page = pt_ref[step * pages + p]
        for n, (src, buf) in enumerate(zip(srcs, bufs)):
            out.append(pltpu.make_async_copy(src.at[page], buf.at[slot, pl.ds(p * pitch, PAGE_ROWS)], sem.at[n, slot]))
    return out


def _gather_pipeline(pt_ref, srcs, bufs, sem, pages, pitch):
    s = pl.program_id(0)
    slot = s % 2

    @pl.when(s == 0)
    def _():
        for c in _page_copies(pt_ref, srcs, bufs, sem, 0, 0, pages, pitch):
            c.start()

    @pl.when(s + 1 < pl.num_programs(0))
    def _():
        for c in _page_copies(pt_ref, srcs, bufs, sem, s + 1, 1 - slot, pages, pitch):
            c.start()

    for c in _page_copies(pt_ref, srcs, bufs, sem, s, slot, pages, pitch):
        c.wait()
    return slot


def _feats_gather_kernel(pt_ref, ck_hbm, cv_hbm, wk_ref, wv_ref, fk_ref, fv_ref, kbuf, vbuf, sem):
    slot = _gather_pipeline(pt_ref, (ck_hbm, cv_hbm), (kbuf, vbuf), sem, FEAT_PAGES, FEAT_PITCH)
    crows = CMP_STRIDE * N_KV

    def piece(buf):
        return lambda c, g: jnp.concatenate(
            [buf[slot, pl.ds(n * crows + c * N_KV + g, FEAT_PAGES, stride=FEAT_PITCH), :]
             for n in range(CHUNKS_PER_PAGE)], axis=0)

    for buf, w_ref, f_ref in ((kbuf, wk_ref, fk_ref), (vbuf, wv_ref, fv_ref)):
        f = _chunk_feats(piece(buf), w_ref[...])
        for n in range(CHUNKS_PER_PAGE):
            for cb in range(N_FEAT):
                f_ref[cb, pl.ds(n, FEAT_PAGES, stride=CHUNKS_PER_PAGE), :] = (
                    f[n * FEAT_PAGES:(n + 1) * FEAT_PAGES, cb * HEAD_DIM:(cb + 1) * HEAD_DIM])


def _feats_sample(page_table, cache_k, cache_v, wk, wv):
    n_pages = page_table.size
    n_phys = cache_k.shape[0]
    ck = cache_k.reshape(n_phys, PAGE_ROWS, HEAD_DIM)
    cv = cache_v.reshape(n_phys, PAGE_ROWS, HEAD_DIM)
    rows = FEAT_PAGES * CHUNKS_PER_PAGE
    brows = FEAT_PAGES * FEAT_PITCH
    wspec = pl.BlockSpec(wk.shape, lambda s, pt: (0, 0))
    ospec = pl.BlockSpec((N_FEAT, rows, HEAD_DIM), lambda s, pt: (0, s, 0))
    return pl.pallas_call(
        _feats_gather_kernel,
        out_shape=(jax.ShapeDtypeStruct((N_FEAT, n_pages * CHUNKS_PER_PAGE, HEAD_DIM), F32),) * 2,
        grid_spec=pltpu.PrefetchScalarGridSpec(
            num_scalar_prefetch=1,
            grid=(n_pages // FEAT_PAGES,),
            in_specs=[pl.BlockSpec(memory_space=pl.ANY), pl.BlockSpec(memory_space=pl.ANY), wspec, wspec],
            out_specs=(ospec, ospec),
            scratch_shapes=[pltpu.VMEM((2, brows, HEAD_DIM), F32), pltpu.VMEM((2, brows, HEAD_DIM), F32),
                            pltpu.SemaphoreType.DMA((2, 2))],
        ),
        compiler_params=pltpu.CompilerParams(dimension_semantics=("arbitrary",), vmem_limit_bytes=VMEM_LIMIT),
        name="feats_sample",
    )(page_table.reshape(-1), ck, cv, wk, wv)


def _cmp_finish_one(f_ref, w1_ref, pe_ref, w2_ref, g_ref, o_ref, nc):
    bias = _dot(pe_ref[...].astype(BF16), w1_ref[...])
    ba = bias[0:1, 0:HEAD_DIM]
    bb = bias[1:2, HEAD_DIM:2 * HEAD_DIM]
    row = _iota((nc, 1), 0)
    for g in range(N_KV):
        fa = f_ref[2 * g] + ba
        fb = f_ref[2 * g + 1] + bb
        hid = _silu(fa + pltpu.roll(fb, nc - 1, 0))
        o = _dot(hid.astype(BF16), w2_ref[...])
        if g_ref is not None:
            o = _rms(o, g_ref[...])
        o_ref[:, g * HEAD_DIM:(g + 1) * HEAD_DIM] = jnp.where(row < nc - 1, o, 0.0)


def _cmp_finish_kernel(fk_ref, fv_ref, wk1_ref, pek_ref, wk2_ref, gk_ref, wv1_ref, pev_ref, wv2_ref,
                       ok_ref, ov_ref, *, nc):
    _cmp_finish_one(fk_ref, wk1_ref, pek_ref, wk2_ref, gk_ref, ok_ref, nc)
    _cmp_finish_one(fv_ref, wv1_ref, pev_ref, wv2_ref, None, ov_ref, nc)


def _cmp_finish(fk, fv, k_weights, gain_k, v_weights, nc):
    nb = fk.shape[1] // nc
    full = lambda a: pl.BlockSpec(a.shape, lambda b: (0, 0))
    fspec = pl.BlockSpec((N_FEAT, nc, HEAD_DIM), lambda b: (0, b, 0))
    ospec = pl.BlockSpec((nc, N_KV * HEAD_DIM), lambda b: (b, 0))
    gk = gain_k.reshape(1, -1)
    return pl.pallas_call(
        functools.partial(_cmp_finish_kernel, nc=nc),
        out_shape=(jax.ShapeDtypeStruct((nb * nc, N_KV * HEAD_DIM), F32),) * 2,
        grid=(nb,),
        in_specs=[fspec, fspec] + [full(a) for a in k_weights] + [full(gk)] + [full(a) for a in v_weights],
        out_specs=(ospec, ospec),
        compiler_params=pltpu.CompilerParams(dimension_semantics=("arbitrary",), vmem_limit_bytes=VMEM_LIMIT),
        name="cmp_finish",
    )(fk, fv, *k_weights, gk, *v_weights)


SELECT_LANES = 128


def _select_kernel(imp_ref, cur_ref, sel_ref, *any_refs, any_rows):
    n_rows, n_blk = imp_ref.shape
    shape = (n_blk, SELECT_LANES)
    j = _iota(shape, 0)
    jf = j.astype(F32)

    def body(_, carry):
        score, sel_t = carry
        m = jnp.max(score, axis=0, keepdims=True)
        first = jnp.min(jnp.where(score == m, jf, 1e9), axis=0, keepdims=True)
        hit = jf == first
        return jnp.where(hit, -2.0, score), jnp.where(hit, 1.0, sel_t)

    for c in range(n_rows // SELECT_LANES):
        rows = slice(c * SELECT_LANES, (c + 1) * SELECT_LANES)
        imp_t = imp_ref[rows, :].T
        cur = cur_ref[:, rows]
        forced = (j == 0) | (j == cur) | (j == cur - 1)
        score0 = jnp.where(forced, 1e30, jnp.where(j <= cur, imp_t, -1.0))
        _, sel_t = lax.fori_loop(0, N_SELECT, body, (score0, jnp.zeros(shape, F32)))
        sel_ref[rows, :] = sel_t.T
    sel = sel_ref[...]
    if any_refs:
        for n in range(sel.shape[0] // any_rows):
            blk = jnp.max(sel[n * any_rows:(n + 1) * any_rows], axis=0, keepdims=True)
            any_refs[0][n * 8:(n + 1) * 8, :] = jnp.broadcast_to(blk, (8, sel.shape[1]))


def _select(imp, cur, n_lanes, tr, any_rows=None):
    r, w = imp.shape
    groups = w // n_lanes
    out_shape = [jax.ShapeDtypeStruct((r, w), F32)]
    out_specs = [pl.BlockSpec((tr, n_lanes), lambda i, g: (i, g))]
    if any_rows is not None:
        out_shape.append(jax.ShapeDtypeStruct((r // any_rows * 8, w), F32))
        out_specs.append(pl.BlockSpec((tr // any_rows * 8, n_lanes), lambda i, g: (i, g)))
    return pl.pallas_call(
        functools.partial(_select_kernel, any_rows=any_rows),
        out_shape=tuple(out_shape),
        grid=(r // tr, groups),
        in_specs=[pl.BlockSpec((tr, n_lanes), lambda i, g: (i, g)), pl.BlockSpec((1, tr), lambda i, g: (0, i))],
        out_specs=tuple(out_specs),
        compiler_params=pltpu.CompilerParams(
            dimension_semantics=("arbitrary", "arbitrary"), vmem_limit_bytes=VMEM_LIMIT),
        name="select",
    )(imp, cur)


def _imp_matrix(n_cmp_pad, n_sel_pad, n_cmp, n_sel):
    r = SEL_BLOCK // CMP_STRIDE
    lead = CMP_BLOCK // CMP_STRIDE - 1
    m = np.zeros((n_cmp_pad, n_sel_pad), np.float32)
    for jb in range(n_sel):
        for o in range(-lead, r):
            start = o * CMP_STRIDE
            w = (min(start + CMP_BLOCK, SEL_BLOCK) - max(start, 0)) / CMP_BLOCK
            i = r * jb + o
            if 0 <= i < n_cmp:
                m[i, jb] += w
    return jnp.asarray(m)


def _cmp_select_kernel(q_ref, kc_ref, vc_ref, m_ref, oc_ref, imp_ref, *, tq, nc):
    i = pl.program_id(0)
    qpos = i * tq + _iota((tq, 1), 0)
    end = _iota((1, nc), 1) * CMP_STRIDE + (CMP_BLOCK - 1)
    dist = qpos - end
    mask = dist >= 0
    distf = dist.astype(F32)
    for g in range(N_KV):
        kg = kc_ref[:, g * HEAD_DIM:(g + 1) * HEAD_DIM].astype(BF16)
        vg = vc_ref[:, g * HEAD_DIM:(g + 1) * HEAD_DIM].astype(BF16)
        psum = jnp.zeros((tq, nc), F32)
        for r in range(GROUP):
            h = g * GROUP + r
            qh = q_ref[:, h * HEAD_DIM:(h + 1) * HEAD_DIM].astype(BF16)
            s = _dot_nt(qh, kg) * SCALE - SLOPES[h] * distf
            s = jnp.where(mask, s, NEG)
            m = jnp.max(s, axis=-1, keepdims=True)
            e = jnp.where(mask, jnp.exp(s - m), 0.0)
            p = e * (1.0 / jnp.maximum(jnp.sum(e, axis=-1, keepdims=True), 1e-30))
            psum = psum + p
            oc_ref[:, h * HEAD_DIM:(h + 1) * HEAD_DIM] = _dot(p.astype(BF16), vg)
        imp_ref[:, g * 128:(g + 1) * 128] = jnp.dot(psum, m_ref[...], precision=lax.Precision.HIGHEST,
                                                    preferred_element_type=F32)


def _cmp_select(z, kcmp, vcmp, imp_m, tq):
    t = z.shape[0]
    nc = kcmp.shape[0]
    n_sel = imp_m.shape[1]
    assert n_sel == 128
    full = lambda shape: pl.BlockSpec(shape, lambda i: (0, 0))
    return pl.pallas_call(
        functools.partial(_cmp_select_kernel, tq=tq, nc=nc),
        out_shape=(jax.ShapeDtypeStruct((t, N_HEADS * HEAD_DIM), F32), jax.ShapeDtypeStruct((t, N_KV * 128), F32)),
        grid=(t // tq,),
        in_specs=[pl.BlockSpec((tq, N_HEADS * HEAD_DIM), lambda i: (i, COL_Q // (N_HEADS * HEAD_DIM))),
                  full(kcmp.shape), full(vcmp.shape), full(imp_m.shape)],
        out_specs=(pl.BlockSpec((tq, N_HEADS * HEAD_DIM), lambda i: (i, 0)),
                   pl.BlockSpec((tq, N_KV * 128), lambda i: (i, 0))),
        compiler_params=pltpu.CompilerParams(dimension_semantics=("arbitrary",), vmem_limit_bytes=VMEM_LIMIT),
        name="cmp_select",
    )(z, kcmp, vcmp, imp_m)


def _flash_update(s, v_bf, m_ref, l_ref, acc_ref, rows):
    m_old = m_ref[rows, :]
    m_new = jnp.maximum(m_old, jnp.max(s, axis=-1, keepdims=True))
    alpha = jnp.exp(m_old - m_new)
    p = jnp.exp(s - m_new)
    l_ref[rows, :] = alpha * l_ref[rows, :] + jnp.sum(p, axis=-1, keepdims=True)
    acc_ref[rows, :] = alpha * acc_ref[rows, :] + _dot(p.astype(BF16), v_bf)
    m_ref[rows, :] = m_new


LOG2E = 1.4426950408889634
V_ROWS = 2 * HEAD_DIM


def _selected_prompt_kernel(flag_ref, q_ref, k_ref, v_ref, sel_ref, et_ref, o_ref,
                            kbf, vaug, qs, m_sc, acc_sc, *, tq, tk):
    g = pl.program_id(0)
    qi = pl.program_id(1)
    nkt = kbf.shape[0]
    step = g * pl.num_programs(1) + qi

    @pl.when(qi == 0)
    def _():
        ones_row = jnp.where(_iota((V_ROWS - HEAD_DIM, tk), 0) == 0, 1.0, 0.0).astype(BF16)
        for ki in range(nkt):
            rows = slice(ki * tk, (ki + 1) * tk)
            kbf[ki] = k_ref[rows, :].astype(BF16)
            vaug[ki, 0:HEAD_DIM, :] = v_ref[rows, :].T.astype(BF16)
            vaug[ki, HEAD_DIM:V_ROWS, :] = ones_row

    for r in range(GROUP):
        qs[r] = (q_ref[:, r * HEAD_DIM:(r + 1) * HEAD_DIM] * (SCALE * LOG2E)).astype(BF16)
    m_sc[...] = jnp.full(m_sc.shape, M_INIT, F32)
    acc_sc[...] = jnp.zeros(acc_sc.shape, F32)
    sel_bf = sel_ref[...].astype(BF16)
    qpos = qi * tq + _iota((1, tq), 1)

    def body(ki, carry):
        @pl.when(flag_ref[step * nkt + ki] > 0)
        def _():
            k = kbf[ki]
            va = vaug[ki]
            sel_t = _dot_nt(et_ref[pl.ds(pl.multiple_of(ki * tk, tk), tk), :], sel_bf)
            kpos = ki * tk + _iota((tk, tq), 0)
            mb = jnp.where((sel_t > 0.5) & (kpos <= qpos), 0.0, NEG)
            krel = (kpos - qi * tq).astype(F32)
            for r in range(GROUP):
                slope = jnp.where(g == 0, SLOPES[r], SLOPES[GROUP + r]) * LOG2E
                s = _dot_nt(k, qs[r]) + (mb + slope * krel)
                m_old = m_sc[r]
                m_new = jnp.maximum(m_old, jnp.max(s, axis=0, keepdims=True))
                p = jnp.exp2(s - m_new).astype(BF16)
                acc_sc[r] = acc_sc[r] * jnp.exp2(m_old - m_new) + _dot(va, p)
                m_sc[r] = m_new
        return carry

    lax.fori_loop(0, (qi * tq + tq - 1) // tk + 1, body, 0)
    for r in range(GROUP):
        o_t = acc_sc[r, 0:HEAD_DIM, :] * (1.0 / acc_sc[r, HEAD_DIM:HEAD_DIM + 1, :])
        o_ref[:, r * HEAD_DIM:(r + 1) * HEAD_DIM] = o_t.T


def _tile_flags(anyb, t, tq, tk):
    nq, nkt = t // tq, t // tk
    a = anyb.reshape(nq, 8, N_KV, nkt, tk // SEL_BLOCK)[:, 0]
    return jnp.transpose(jnp.max(a, axis=-1) > 0.5, (1, 0, 2)).astype(jnp.int32).reshape(-1)


def _selected_prompt(z, sel, anyb, tq, tk=512):
    t = z.shape[0]
    gw = GROUP * HEAD_DIM
    flags = _tile_flags(anyb, t, tq, tk)
    et = jnp.asarray(np.arange(t)[:, None] // SEL_BLOCK == np.arange(128)[None, :], dtype=BF16)
    return pl.pallas_call(
        functools.partial(_selected_prompt_kernel, tq=tq, tk=tk),
        out_shape=jax.ShapeDtypeStruct((t, N_HEADS * HEAD_DIM), F32),
        grid_spec=pltpu.PrefetchScalarGridSpec(
            num_scalar_prefetch=1,
            grid=(N_KV, t // tq),
            in_specs=[
                pl.BlockSpec((tq, gw), lambda g, qi, f: (qi, COL_Q // gw + g)),
                pl.BlockSpec((t, HEAD_DIM), lambda g, qi, f: (0, COL_KS // HEAD_DIM + g)),
                pl.BlockSpec((t, HEAD_DIM), lambda g, qi, f: (0, COL_VS // HEAD_DIM + g)),
                pl.BlockSpec((tq, 128), lambda g, qi, f: (qi, g)),
                pl.BlockSpec((t, 128), lambda g, qi, f: (0, 0)),
            ],
            out_specs=pl.BlockSpec((tq, gw), lambda g, qi, f: (qi, g)),
            scratch_shapes=[pltpu.VMEM((t // tk, tk, HEAD_DIM), BF16), pltpu.VMEM((t // tk, V_ROWS, tk), BF16),
                            pltpu.VMEM((GROUP, tq, HEAD_DIM), BF16), pltpu.VMEM((GROUP, 1, tq), F32),
                            pltpu.VMEM((GROUP, V_ROWS, tq), F32)],
        ),
        compiler_params=pltpu.CompilerParams(
            dimension_semantics=("arbitrary", "arbitrary"), vmem_limit_bytes=VMEM_LIMIT),
        name="selected_prompt",
    )(flags, z, z, z, sel, et)


def _window_prompt_kernel(q_ref, kp_ref, kc_ref, vp_ref, vc_ref, o_ref, *, tq):
    g = pl.program_id(0)
    qi = pl.program_id(1)
    k = jnp.concatenate([kp_ref[...], kc_ref[...]], axis=0).astype(BF16)
    v = jnp.concatenate([vp_ref[...], vc_ref[...]], axis=0).astype(BF16)
    krel = _iota((1, 2 * tq), 1) - tq
    dist = _iota((tq, 1), 0) - krel
    mask = (dist >= 0) & (dist < WINDOW) & ((krel >= 0) | (qi > 0))
    distf = dist.astype(F32)
    for r in range(GROUP):
        slope = jnp.where(g == 0, SLOPES[r], SLOPES[GROUP + r])
        qh = q_ref[:, r * HEAD_DIM:(r + 1) * HEAD_DIM].astype(BF16)
        s = jnp.where(mask, _dot_nt(qh, k) * SCALE - slope * distf, NEG)
        m = jnp.max(s, axis=-1, keepdims=True)
        e = jnp.where(mask, jnp.exp(s - m), 0.0)
        p = e * (1.0 / jnp.maximum(jnp.sum(e, axis=-1, keepdims=True), 1e-30))
        o_ref[:, r * HEAD_DIM:(r + 1) * HEAD_DIM] = _dot(p.astype(BF16), v)


def _window_prompt(z, tq=WINDOW):
    t = z.shape[0]
    gw = GROUP * HEAD_DIM
    prev = lambda c: pl.BlockSpec((tq, HEAD_DIM), lambda g, qi: (jnp.maximum(qi - 1, 0), c // HEAD_DIM + g))
    cur = lambda c: pl.BlockSpec((tq, HEAD_DIM), lambda g, qi: (qi, c // HEAD_DIM + g))
    return pl.pallas_call(
        functools.partial(_window_prompt_kernel, tq=tq),
        out_shape=jax.ShapeDtypeStruct((t, N_HEADS * HEAD_DIM), F32),
        grid=(N_KV, t // tq),
        in_specs=[pl.BlockSpec((tq, gw), lambda g, qi: (qi, COL_Q // gw + g)),
                  prev(COL_KW), cur(COL_KW), prev(COL_VW), cur(COL_VW)],
        out_specs=pl.BlockSpec((tq, gw), lambda g, qi: (qi, g)),
        compiler_params=pltpu.CompilerParams(
            dimension_semantics=("arbitrary", "arbitrary"), vmem_limit_bytes=VMEM_LIMIT),
        name="window_prompt",
    )(z, z, z, z, z)


SEL_PAD = 384


def _slope_col(g, ds):
    return jnp.concatenate([jnp.full((ds, 1), SLOPES[g * GROUP + r], F32) for r in range(GROUP)], axis=0)


def _stack_heads(q_ref, g):
    return jnp.concatenate([q_ref[:, (g * GROUP + r) * HEAD_DIM:(g * GROUP + r + 1) * HEAD_DIM]
                            for r in range(GROUP)], axis=0).astype(BF16)


def _pad_rows(a, n):
    return jnp.concatenate([a, jnp.zeros((n - a.shape[0], a.shape[1]), a.dtype)], axis=0)


def _sample_small_kernel(q_ref, kc_ref, vc_ref, m_ref, sk_ref, sv_ref, nk_ref, nv_ref,
                         oc_ref, imp_ref, ow_ref, ko_ref, vo_ref, *, ds, nc, past, wb):
    rows = GROUP * ds
    qidx = _iota((rows, 1), 0) & (ds - 1)
    spos = past + qidx
    end = _iota((1, nc), 1) * CMP_STRIDE + (CMP_BLOCK - 1)
    dist_c = spos - end
    mask_c = dist_c >= 0
    ist = _iota((1, wb), 1)
    dist_s = wb + qidx - ist
    mask_s = dist_s < WINDOW
    jn = _iota((1, 128), 1)
    dist_n = qidx - jn
    mask_n = (dist_n >= 0) & (jn < ds)
    for g in range(N_KV):
        lanes = slice(g * HEAD_DIM, (g + 1) * HEAD_DIM)
        qs = _stack_heads(q_ref, g)
        slope = _slope_col(g, ds)
        s = _dot_nt(qs, kc_ref[:, lanes].astype(BF16)) * SCALE - slope * dist_c.astype(F32)
        s = jnp.where(mask_c, s, NEG)
        m = jnp.max(s, axis=-1, keepdims=True)
        e = jnp.where(mask_c, jnp.exp(s - m), 0.0)
        p = e * (1.0 / jnp.maximum(jnp.sum(e, axis=-1, keepdims=True), 1e-30))
        o = _dot(p.astype(BF16), vc_ref[:, lanes].astype(BF16))
        psum = p[0:ds]
        for r in range(1, GROUP):
            psum = psum + p[r * ds:(r + 1) * ds]
        imp_ref[:, g * SEL_PAD:(g + 1) * SEL_PAD] = jnp.dot(psum, m_ref[...], precision=lax.Precision.HIGHEST,
                                                            preferred_element_type=F32)
        kst = sk_ref[pl.ds(g, wb, stride=N_KV), :].astype(BF16)
        vst = sv_ref[pl.ds(g, wb, stride=N_KV), :].astype(BF16)
        kn = _pad_rows(nk_ref[pl.ds(g, ds, stride=N_KV), :], 128).astype(BF16)
        vn = _pad_rows(nv_ref[pl.ds(g, ds, stride=N_KV), :], 128).astype(BF16)
        s1 = jnp.where(mask_s, _dot_nt(qs, kst) * SCALE - slope * dist_s.astype(F32), NEG)
        s2 = jnp.where(mask_n, _dot_nt(qs, kn) * SCALE - slope * dist_n.astype(F32), NEG)
        mw = jnp.maximum(jnp.max(s1, axis=-1, keepdims=True), jnp.max(s2, axis=-1, keepdims=True))
        e1 = jnp.where(mask_s, jnp.exp(s1 - mw), 0.0)
        e2 = jnp.where(mask_n, jnp.exp(s2 - mw), 0.0)
        inv = 1.0 / jnp.maximum(jnp.sum(e1, axis=-1, keepdims=True) + jnp.sum(e2, axis=-1, keepdims=True), 1e-30)
        w = _dot((e1 * inv).astype(BF16), vst) + _dot((e2 * inv).astype(BF16), vn)
        for r in range(GROUP):
            h = g * GROUP + r
            oc_ref[:, h * HEAD_DIM:(h + 1) * HEAD_DIM] = o[r * ds:(r + 1) * ds]
            ow_ref[:, h * HEAD_DIM:(h + 1) * HEAD_DIM] = w[r * ds:(r + 1) * ds]
    keep = (wb - ds) * N_KV
    for s_ref, n_ref, o_ref in ((sk_ref, nk_ref, ko_ref), (sv_ref, nv_ref, vo_ref)):
        o_ref[0:keep, :] = s_ref[ds * N_KV:wb * N_KV, :]
        o_ref[keep:wb * N_KV, :] = n_ref[...]


def _sample_small(z2, kw2, vw2, kcmp2, vcmp2, imp_m2, state_k, state_v, nb, past):
    t = z2.shape[0]
    ds = t // nb
    nc = kcmp2.shape[0] // nb
    wb = state_k.shape[0] // (nb * N_KV)
    qw = N_HEADS * HEAD_DIM
    kvw = N_KV * HEAD_DIM
    rows = lambda n: pl.BlockSpec((n * N_KV, HEAD_DIM), lambda b: (b, 0))
    return pl.pallas_call(
        functools.partial(_sample_small_kernel, ds=ds, nc=nc, past=past, wb=wb),
        out_shape=(jax.ShapeDtypeStruct((t, qw), F32), jax.ShapeDtypeStruct((t, N_KV * SEL_PAD), F32),
                   jax.ShapeDtypeStruct((t, qw), F32),
                   jax.ShapeDtypeStruct(state_k.shape, F32), jax.ShapeDtypeStruct(state_v.shape, F32)),
        grid=(nb,),
        in_specs=[pl.BlockSpec((ds, qw), lambda b: (b, COL_Q // qw)),
                  pl.BlockSpec((nc, kvw), lambda b: (b, 0)), pl.BlockSpec((nc, kvw), lambda b: (b, 0)),
                  pl.BlockSpec(imp_m2.shape, lambda b: (0, 0)),
                  rows(wb), rows(wb), rows(ds), rows(ds)],
        out_specs=(pl.BlockSpec((ds, qw), lambda b: (b, 0)), pl.BlockSpec((ds, N_KV * SEL_PAD), lambda b: (b, 0)),
                   pl.BlockSpec((ds, qw), lambda b: (b, 0)), rows(wb), rows(wb)),
        compiler_params=pltpu.CompilerParams(dimension_semantics=("arbitrary",), vmem_limit_bytes=VMEM_LIMIT),
        name="sample_small",
    )(z2, kcmp2, vcmp2, imp_m2, state_k, state_v, kw2, vw2)


SLC_PAGES = 32
SLC_KEYS = SLC_PAGES * PAGE_SIZE
SLC_BLOCKS = SLC_KEYS // SEL_BLOCK
WIN_STEPS = 128 // SLC_BLOCKS


def _expand_matrix():
    j = np.arange(128)[:, None]
    c = np.arange(SLC_KEYS)[None, :]
    e = np.concatenate([(j == w * SLC_BLOCKS + c // SEL_BLOCK) for w in range(WIN_STEPS)], axis=0)
    return jnp.asarray(e, dtype=BF16)


def _selected_sample_kernel(pt_ref, ck_hbm, cv_hbm, q_ref, sel0_ref, sel1_ref, e_ref, nk_ref, nv_ref, o_ref,
                            kbuf, vbuf, sem, m_sc, l_sc, acc_sc, *, ds, steps, past):
    slot = _gather_pipeline(pt_ref, (ck_hbm, cv_hbm), (kbuf, vbuf), sem, SLC_PAGES, PAGE_ROWS)
    kt = pl.program_id(0) % steps
    rows = GROUP * ds
    nk = SLC_KEYS
    qidx = _iota((rows, 1), 0) & (ds - 1)

    @pl.when(kt == 0)
    def _():
        m_sc[...] = jnp.full(m_sc.shape, M_INIT, F32)
        l_sc[...] = jnp.zeros(l_sc.shape, F32)
        acc_sc[...] = jnp.zeros(acc_sc.shape, F32)

    krel = (kt * nk - past + _iota((1, nk), 1)).astype(F32)
    e = e_ref[pl.ds(pl.multiple_of((kt % WIN_STEPS) * 128, 128), 128), :]
    for g, sel_ref in enumerate((sel0_ref, sel1_ref)):
        grows = slice(g * rows, (g + 1) * rows)
        qs = _stack_heads(q_ref, g)
        slope = _slope_col(g, ds)
        selexp = _dot(sel_ref[...].astype(BF16), e)
        mb = jnp.where(jnp.concatenate([selexp] * GROUP, axis=0) > 0.5, 0.0, NEG)
        kg = kbuf[slot, pl.ds(g, nk, stride=N_KV), :].astype(BF16)
        vg = vbuf[slot, pl.ds(g, nk, stride=N_KV), :].astype(BF16)
        s = _dot_nt(qs, kg) * SCALE + (mb + slope * krel)
        _flash_update(s, vg, m_sc, l_sc, acc_sc, grows)

    @pl.when(kt == steps - 1)
    def _():
        jn = _iota((1, 128), 1)
        mb_n = jnp.where((jn <= qidx) & (jn < ds), 0.0, NEG)
        for g in range(N_KV):
            grows = slice(g * rows, (g + 1) * rows)
            qs = _stack_heads(q_ref, g)
            kn = _pad_rows(nk_ref[pl.ds(g, ds, stride=N_KV), :], 128).astype(BF16)
            vn = _pad_rows(nv_ref[pl.ds(g, ds, stride=N_KV), :], 128).astype(BF16)
            s = _dot_nt(qs, kn) * SCALE + (mb_n + _slope_col(g, ds) * jn.astype(F32))
            _flash_update(s, vn, m_sc, l_sc, acc_sc, grows)
            o = acc_sc[grows, :] * (1.0 / l_sc[grows, :])
            for r in range(GROUP):
                h = g * GROUP + r
                o_ref[:, h * HEAD_DIM:(h + 1) * HEAD_DIM] = o[r * ds:(r + 1) * ds]


def _selected_sample(page_table, cache_k, cache_v, z2, ks2, vs2, sel2, nb, past):
    t = z2.shape[0]
    ds = t // nb
    n_phys = cache_k.shape[0]
    ck = cache_k.reshape(n_phys, PAGE_ROWS, HEAD_DIM)
    cv = cache_v.reshape(n_phys, PAGE_ROWS, HEAD_DIM)
    steps = page_table.shape[1] // SLC_PAGES
    qw = N_HEADS * HEAD_DIM
    rows = N_KV * GROUP * ds
    e = _expand_matrix()
    win = lambda g: pl.BlockSpec(
        (ds, 128), lambda s, pt: (s // steps, g * (SEL_PAD // 128) + (s % steps) // WIN_STEPS))
    new = pl.BlockSpec((ds * N_KV, HEAD_DIM), lambda s, pt: (s // steps, 0))
    return pl.pallas_call(
        functools.partial(_selected_sample_kernel, ds=ds, steps=steps, past=past),
        out_shape=jax.ShapeDtypeStruct((t, qw), F32),
        grid_spec=pltpu.PrefetchScalarGridSpec(
            num_scalar_prefetch=1,
            grid=(nb * steps,),
            in_specs=[pl.BlockSpec(memory_space=pl.ANY), pl.BlockSpec(memory_space=pl.ANY),
                      pl.BlockSpec((ds, qw), lambda s, pt: (s // steps, COL_Q // qw)),
                      win(0), win(1), pl.BlockSpec(e.shape, lambda s, pt: (0, 0)), new, new],
            out_specs=pl.BlockSpec((ds, qw), lambda s, pt: (s // steps, 0)),
            scratch_shapes=[pltpu.VMEM((2, SLC_PAGES * PAGE_ROWS, HEAD_DIM), F32),
                            pltpu.VMEM((2, SLC_PAGES * PAGE_ROWS, HEAD_DIM), F32),
                            pltpu.SemaphoreType.DMA((2, 2)),
                            pltpu.VMEM((rows, 1), F32), pltpu.VMEM((rows, 1), F32),
                            pltpu.VMEM((rows, HEAD_DIM), F32)],
        ),
        compiler_params=pltpu.CompilerParams(dimension_semantics=("arbitrary",), vmem_limit_bytes=VMEM_LIMIT),
        name="selected_sample",
    )(page_table.reshape(-1), ck, cv, z2, sel2, sel2, e, ks2, vs2)


def _merge_kernel(x_ref, conv_ref, oc_ref, os_ref, ow_ref, gt_ref, zn_ref, w_ref, y_ref):
    gt = gt_ref[...]
    parts = [conv_ref[...].astype(BF16)]
    for h in range(N_HEADS):
        lanes = slice(h * HEAD_DIM, (h + 1) * HEAD_DIM)
        o = (gt[:, 3 * h:3 * h + 1] * oc_ref[:, lanes] + gt[:, 3 * h + 1:3 * h + 2] * os_ref[:, lanes]
             + gt[:, 3 * h + 2:3 * h + 3] * ow_ref[:, lanes])
        parts.append((o * _silu(zn_ref[:, lanes])).astype(BF16))
    y_ref[...] = x_ref[...] + _dot(jnp.concatenate(parts, axis=1), w_ref[...])


def _merge(x, conv_o, o_c, o_s, o_w, z, w_out_bf, tm=256):
    t, d = x.shape
    qw = N_HEADS * HEAD_DIM
    rowblk = lambda w: pl.BlockSpec((tm, w), lambda i: (i, 0))
    return pl.pallas_call(
        _merge_kernel,
        out_shape=jax.ShapeDtypeStruct((t, d), F32),
        grid=(t // tm,),
        in_specs=[rowblk(d), rowblk(C_CONV), rowblk(qw), rowblk(qw), rowblk(qw),
                  pl.BlockSpec((tm, 128), lambda i: (i, COL_GT // 128)),
                  pl.BlockSpec((tm, qw), lambda i: (i, COL_ZN // qw)),
                  pl.BlockSpec(w_out_bf.shape, lambda i: (0, 0))],
        out_specs=rowblk(d),
        compiler_params=pltpu.CompilerParams(dimension_semantics=("arbitrary",), vmem_limit_bytes=VMEM_LIMIT),
        name="merge",
    )(x, conv_o, o_c, o_s, o_w, z, z, w_out_bf)


N_GT = 3 * N_HEADS


def _pad_w_kernel(wt_hbm, o_ref, buf, sem):
    j = pl.program_id(0)
    src = jnp.where(j <= TILE_GT, j * TN, j * TN - COL_ZN + COL_GT + N_GT)
    copy = pltpu.make_async_copy(wt_hbm.at[pl.ds(pl.multiple_of(src, 8), TN)], buf, sem)
    copy.start()
    copy.wait()
    w = buf[...]
    keep = (j != TILE_GT) | (_iota((TN, 1), 0) < N_GT)
    o_ref[...] = jnp.where(keep, w, 0.0).T.astype(BF16)


def _pad_w_in(w_in):
    d, n = w_in.shape
    assert n == COL_GT + N_GT + ZW - COL_ZN and COL_ZN == (TILE_GT + 1) * TN
    return pl.pallas_call(
        _pad_w_kernel,
        out_shape=jax.ShapeDtypeStruct((d, ZW), BF16),
        grid=(N_TILES,),
        in_specs=[pl.BlockSpec(memory_space=pl.ANY)],
        out_specs=pl.BlockSpec((d, TN), lambda j: (0, j)),
        scratch_shapes=[pltpu.VMEM((TN, d), F32), pltpu.SemaphoreType.DMA(())],
        compiler_params=pltpu.CompilerParams(dimension_semantics=("arbitrary",), vmem_limit_bytes=VMEM_LIMIT),
        name="pad_w_in",
    )(w_in.T)


def _cmp_weights(pe, w1, w2):
    half = CMP_STRIDE
    k = half * HEAD_DIM
    w1cat = jnp.concatenate([w1[:half].reshape(k, -1), w1[half:].reshape(k, -1)], axis=1).astype(BF16)
    pe2 = jnp.concatenate([pe[:half].reshape(1, k), pe[half:].reshape(1, k), jnp.zeros((6, k), pe.dtype)], axis=0)
    return w1cat, pe2, w2.astype(BF16)


def kernel(x_prompt, x_sample, cache_k_cmp, cache_v_cmp, cache_k_slc, cache_v_slc, state_k_win, state_v_win,
           state_conv, page_table, g_norm, w_in, pe_cmp_k, w_cmp_k1, w_cmp_k2, pe_cmp_v, w_cmp_v1, w_cmp_v2,
           g_q, g_k_cmp, g_k_slc, g_k_win, w_dw, b_dw, ln_g, ln_b, w_pw2, b_pw2, w_out):
    _, t, d = x_prompt.shape
    db, ds, _ = x_sample.shape
    past = page_table.shape[1] * PAGE_SIZE
    wb = state_k_win.shape[1]
    kvw = N_KV * HEAD_DIM
    assert ds < CMP_STRIDE and wb == WINDOW and t % SEL_BLOCK == 0 and t // SEL_BLOCK == 128

    w_p = _pad_w_in(w_in)
    w_pw_bf = w_pw2.astype(BF16)
    w_out_bf = w_out.astype(BF16)
    wk1, pek, wk2 = _cmp_weights(pe_cmp_k, w_cmp_k1, w_cmp_k2)
    wv1, pev, wv2 = _cmp_weights(pe_cmp_v, w_cmp_v1, w_cmp_v2)

    xp = x_prompt.reshape(t, d)
    z, kc, vc, ks, vs, kw, vw = _in_proj(xp, g_norm, w_p, g_q, g_k_slc, g_k_win, tm=1024)
    conv_o, conv_st = _conv_prompt(z, w_dw, b_dw, ln_g, ln_b, w_pw_bf, b_pw2)
    nc = t // CMP_STRIDE
    fk, fv = _feats_prompt(kc, vc, wk1, wv1)
    kcmp, vcmp = _cmp_finish(fk, fv, (wk1, pek, wk2), g_k_cmp, (wv1, pev, wv2), nc)
    imp_m = _imp_matrix(nc, t // SEL_BLOCK, nc - 1, t // SEL_BLOCK)
    o_c, imp = _cmp_select(z, kcmp, vcmp, imp_m, tq=256)
    cur = (jnp.arange(t, dtype=jnp.int32) // SEL_BLOCK).reshape(1, t)
    sel, anyb = _select(imp, cur, 128, tr=512, any_rows=256)
    o_s = _selected_prompt(z, sel, anyb, tq=256)
    o_w = _window_prompt(z)
    y_prompt = _merge(xp, conv_o, o_c, o_s, o_w, z, w_out_bf)

    xs = x_sample.reshape(db * ds, d)
    z2, kc2, vc2, ks2, vs2, kw2, vw2 = _in_proj(xs, g_norm, w_p, g_q, g_k_slc, g_k_win, tm=db * ds)
    conv_o2, u2 = _conv_sample(z2, state_conv, w_dw, b_dw, ln_g, ln_b, w_pw_bf, b_pw2)
    nc2 = past // CMP_STRIDE
    fk2, fv2 = _feats_sample(page_table, cache_k_cmp, cache_v_cmp, wk1, wv1)
    kcmp2, vcmp2 = _cmp_finish(fk2, fv2, (wk1, pek, wk2), g_k_cmp, (wv1, pev, wv2), nc2)
    n_sel2 = -(-(past + ds) // SEL_BLOCK)
    assert n_sel2 <= SEL_PAD
    imp_m2 = _imp_matrix(nc2, SEL_PAD, nc2 - 1, n_sel2)
    rows = lambda a: a.reshape(-1, HEAD_DIM)
    o_c2, imp2, o_w2, k_win, v_win = _sample_small(z2, kw2, vw2, kcmp2, vcmp2, imp_m2, rows(state_k_win),
                                                   rows(state_v_win), db, past)
    cur2 = ((past + jnp.arange(db * ds, dtype=jnp.int32) % ds) // SEL_BLOCK).reshape(1, db * ds)
    sel2, = _select(imp2, cur2, SEL_PAD, tr=db * ds)
    o_s2 = _selected_sample(page_table, cache_k_slc, cache_v_slc, z2, ks2, vs2, sel2, db, past)
    y_sample = _merge(xs, conv_o2, o_c2, o_s2, o_w2, z2, w_out_bf)

    kv4 = lambda a, b: a.reshape(b, -1, N_KV, HEAD_DIM)
    return (y_prompt.reshape(1, t, d), y_sample.reshape(db, ds, d),
            kv4(kc, 1), kv4(vc, 1), kv4(ks, 1), kv4(vs, 1), kv4(kw, 1)[:, t - wb:], kv4(vw, 1)[:, t - wb:],
            conv_st[HALO - (CONV_WIDTH - 1):][None],
            kv4(kc2, db), kv4(vc2, db), kv4(ks2, db), kv4(vs2, db), kv4(k_win, db), kv4(v_win, db),
            jnp.concatenate([state_conv[:, ds:], u2.reshape(db, ds, C_CONV)], axis=1))
```

```python
import functools

import numpy as np
import jax
import jax.numpy as jnp
from jax import lax
from jax.experimental import pallas as pl
from jax.experimental.pallas import tpu as pltpu

F32 = jnp.float32
BF16 = jnp.bfloat16

HEAD_DIM = 128
N_HEADS = 8
N_KV = 2
GROUP = 4
C_CONV = 1024
CONV_WIDTH = 31
CMP_STRIDE = 16
CMP_BLOCK = 32
SEL_BLOCK = 64
N_SELECT = 16
WINDOW = 512
PAGE_SIZE = 128
EPS = 1e-6
SCALE = HEAD_DIM ** -0.5
SLOPES = tuple(2.0 ** -(h + 1) for h in range(N_HEADS))

TN = 512
N_TILES = 14
ZW = TN * N_TILES
COL_UA, COL_UB, COL_ZC, COL_Q = 0, 1024, 2048, 3072
COL_KC, COL_VC, COL_KS, COL_VS, COL_KW, COL_VW = 4096, 4352, 4608, 4864, 5120, 5376
COL_GT, COL_ZN = 5632, 6144
TILE_Q0, TILE_Q1, TILE_CMP, TILE_SLC, TILE_WIN, TILE_GT = 6, 7, 8, 9, 10, 11

NEG = -1e30
M_INIT = -1e29
VMEM_LIMIT = 48 * 1024 * 1024
INPROJ_VMEM_LIMIT = 56 * 1024 * 1024


def _sigmoid(x):
    return 1.0 / (1.0 + jnp.exp(-x))


def _silu(x):
    return x * _sigmoid(x)


def _dot(a, b):
    return jnp.dot(a, b, preferred_element_type=F32)


def _dot_nt(a, b):
    return lax.dot_general(a, b, (((1,), (1,)), ((), ())), preferred_element_type=F32)


def _rms(a, g):
    return a * lax.rsqrt(jnp.mean(a * a, axis=-1, keepdims=True) + EPS) * g


def _iota(shape, dim):
    return lax.broadcasted_iota(jnp.int32, shape, dim)


def _inproj_kernel(x_ref, gn_ref, w_ref, gq_ref, gks_ref, gkw_ref,
                   z_ref, kc_o, vc_o, ks_o, vs_o, kw_o, vw_o, xn_ref, *, tm):
    j = pl.program_id(1)

    @pl.when(j == 0)
    def _():
        x = x_ref[...]
        ms = jnp.mean(x * x, axis=-1, keepdims=True)
        xn_ref[...] = (x * lax.rsqrt(ms + EPS) * gn_ref[...]).astype(BF16)

    z_ref[...] = _dot(xn_ref[...], w_ref[...])

    def kv_tile(g_ref, k_o, v_o):
        for c in range(TN // HEAD_DIM):
            lanes = slice(c * HEAD_DIM, (c + 1) * HEAD_DIM)
            a = z_ref[:, lanes]
            if c < N_KV and g_ref is not None:
                a = _rms(a, g_ref[...])
                z_ref[:, lanes] = a
            (k_o if c < N_KV else v_o)[pl.ds(c % N_KV, tm, stride=N_KV), :] = a

    @pl.when((j == TILE_Q0) | (j == TILE_Q1))
    def _():
        for c in range(TN // HEAD_DIM):
            lanes = slice(c * HEAD_DIM, (c + 1) * HEAD_DIM)
            z_ref[:, lanes] = _rms(z_ref[:, lanes], gq_ref[...])

    @pl.when(j == TILE_CMP)
    def _():
        kv_tile(None, kc_o, vc_o)

    @pl.when(j == TILE_SLC)
    def _():
        kv_tile(gks_ref, ks_o, vs_o)

    @pl.when(j == TILE_WIN)
    def _():
        kv_tile(gkw_ref, kw_o, vw_o)

    @pl.when(j == TILE_GT)
    def _():
        z_ref[...] = _sigmoid(z_ref[...])


def _in_proj(x, g_norm, w_p, g_q, g_ks, g_kw, tm):
    t, d = x.shape
    row = lambda a: a.reshape(1, -1)
    kv_shape = jax.ShapeDtypeStruct((t * N_KV, HEAD_DIM), F32)
    kv_spec = pl.BlockSpec((tm * N_KV, HEAD_DIM), lambda i, j: (i, 0))
    return pl.pallas_call(
        functools.partial(_inproj_kernel, tm=tm),
        out_shape=(jax.ShapeDtypeStruct((t, ZW), F32),) + (kv_shape,) * 6,
        grid=(t // tm, N_TILES),
        in_specs=[
            pl.BlockSpec((tm, d), lambda i, j: (i, 0)),
            pl.BlockSpec((1, d), lambda i, j: (0, 0)),
            pl.BlockSpec((d, TN), lambda i, j: (0, j)),
            pl.BlockSpec((1, HEAD_DIM), lambda i, j: (0, 0)),
            pl.BlockSpec((1, HEAD_DIM), lambda i, j: (0, 0)),
            pl.BlockSpec((1, HEAD_DIM), lambda i, j: (0, 0)),
        ],
        out_specs=(pl.BlockSpec((tm, TN), lambda i, j: (i, j)),) + (kv_spec,) * 6,
        scratch_shapes=[pltpu.VMEM((tm, d), BF16)],
        compiler_params=pltpu.CompilerParams(
            dimension_semantics=("arbitrary", "arbitrary"), vmem_limit_bytes=INPROJ_VMEM_LIMIT),
        name="in_proj",
    )(x, row(g_norm), w_p, row(g_q), row(g_ks), row(g_kw))


HALO = 32
CONV_RB = 64
CONV_CB = 128


def _conv_tail(y, zc, lng_ref, lnb_ref, wpw_ref, bpw_ref):
    mu = jnp.mean(y, axis=-1, keepdims=True)
    yc = y - mu
    var = jnp.mean(yc * yc, axis=-1, keepdims=True)
    yn = yc * lax.rsqrt(var + EPS) * lng_ref[...] + lnb_ref[...]
    act = _silu(yn).astype(BF16)
    return (_dot(act, wpw_ref[...]) + bpw_ref[...]) * _silu(zc)


def _conv_prompt_kernel(ua_ref, ub_ref, zc_ref, uah_ref, ubh_ref, wdw_ref, bdw_ref, lng_ref, lnb_ref,
                        wpw_ref, bpw_ref, o_ref, st_ref, buf, ybuf, *, tt):
    i = pl.program_id(0)
    uh = uah_ref[...] * _sigmoid(ubh_ref[...])
    buf[0:HALO, :] = jnp.where(i > 0, uh, 0.0)
    buf[HALO:HALO + tt, :] = ua_ref[...] * _sigmoid(ub_ref[...])
    off = HALO - (CONV_WIDTH - 1)
    for c0 in range(0, C_CONV, CONV_CB):
        lanes = slice(c0, c0 + CONV_CB)
        for r0 in range(0, tt, CONV_RB):
            acc = jnp.broadcast_to(bdw_ref[:, lanes], (CONV_RB, CONV_CB))
            for b in range(8):
                n = CONV_RB if b == 0 else CONV_RB + 8
                zb = None
                for a in range(-(-(off - b) // 8), (off + CONV_WIDTH - 1 - b) // 8 + 1):
                    k = 8 * a + b - off
                    term = wdw_ref[k:k + 1, lanes] * buf[r0 + 8 * a:r0 + 8 * a + n, lanes]
                    zb = term if zb is None else zb + term
                acc = acc + zb[b:b + CONV_RB]
            ybuf[r0:r0 + CONV_RB, lanes] = acc
    o_ref[...] = _conv_tail(ybuf[...], zc_ref[...], lng_ref, lnb_ref, wpw_ref, bpw_ref)

    @pl.when(i == pl.num_programs(0) - 1)
    def _():
        st_ref[...] = buf[tt:tt + HALO, :]


def _conv_prompt(z, w_dw, b_dw, ln_g, ln_b, w_pw_bf, b_pw, tt=256):
    t = z.shape[0]
    row = lambda a: a.reshape(1, -1)
    hb = tt // HALO
    cur = lambda c: pl.BlockSpec((tt, C_CONV), lambda i: (i, c))
    halo = lambda c: pl.BlockSpec((HALO, C_CONV), lambda i: (jnp.maximum(i * hb - 1, 0), c))
    full = lambda shape: pl.BlockSpec(shape, lambda i: (0, 0))
    return pl.pallas_call(
        functools.partial(_conv_prompt_kernel, tt=tt),
        out_shape=(jax.ShapeDtypeStruct((t, C_CONV), F32), jax.ShapeDtypeStruct((HALO, C_CONV), F32)),
        grid=(t // tt,),
        in_specs=[cur(0), cur(1), cur(2), halo(0), halo(1),
                  full((CONV_WIDTH, C_CONV)), full((1, C_CONV)), full((1, C_CONV)), full((1, C_CONV)),
                  full((C_CONV, C_CONV)), full((1, C_CONV))],
        out_specs=(pl.BlockSpec((tt, C_CONV), lambda i: (i, 0)), pl.BlockSpec((HALO, C_CONV), lambda i: (0, 0))),
        scratch_shapes=[pltpu.VMEM((HALO + tt, C_CONV), F32), pltpu.VMEM((tt, C_CONV), F32)],
        compiler_params=pltpu.CompilerParams(dimension_semantics=("arbitrary",), vmem_limit_bytes=VMEM_LIMIT),
        name="conv_prompt",
    )(z, z, z, z, z, w_dw, row(b_dw), row(ln_g), row(ln_b), w_pw_bf, row(b_pw))


ST_ROWS = 40


def _conv_sample_kernel(ua_ref, ub_ref, zc_ref, st_ref, wdw_ref, bdw_ref, lng_ref, lnb_ref, wpw_ref, bpw_ref,
                        o_ref, u_ref, fbuf, ybuf, *, nb, ds):
    u = ua_ref[...] * _sigmoid(ub_ref[...])
    u_ref[...] = u
    rows = _iota((nb * ds, 1), 0) & (ds - 1)
    acc_u = jnp.broadcast_to(bdw_ref[...], (nb * ds, C_CONV))
    for d in range(ds):
        sh = u if d == 0 else pltpu.roll(u, d, 0)
        acc_u = acc_u + jnp.where(rows >= d, sh, 0.0) * wdw_ref[CONV_WIDTH - 1 - d:CONV_WIDTH - d, :]
    ybuf[...] = acc_u
    fbuf[:, 24:ST_ROWS, :] = jnp.zeros((nb, ST_ROWS - 24, C_CONV), F32)
    fbuf[:, 0:CONV_WIDTH - 1, :] = st_ref[...]

    def body(b, carry):
        acc = jnp.zeros((ds, C_CONV), F32)
        for k in range(CONV_WIDTH - 1):
            acc = acc + wdw_ref[k:k + 1, :] * fbuf[b, k:k + ds, :]
        r = pl.multiple_of(b * ds, ds)
        ybuf[pl.ds(r, ds), :] = ybuf[pl.ds(r, ds), :] + acc
        return carry

    lax.fori_loop(0, nb, body, 0)
    o_ref[...] = _conv_tail(ybuf[...], zc_ref[...], lng_ref, lnb_ref, wpw_ref, bpw_ref)


def _conv_sample(z2, state_conv, w_dw, b_dw, ln_g, ln_b, w_pw_bf, b_pw):
    nb, sw, _ = state_conv.shape
    t = z2.shape[0]
    ds = t // nb
    assert sw == CONV_WIDTH - 1 and ds == 8
    row = lambda a: a.reshape(1, -1)
    col = lambda c: pl.BlockSpec((t, C_CONV), lambda i: (0, c))
    full = lambda shape: pl.BlockSpec(shape, lambda i: (0,) * len(shape))
    return pl.pallas_call(
        functools.partial(_conv_sample_kernel, nb=nb, ds=ds),
        out_shape=(jax.ShapeDtypeStruct((t, C_CONV), F32), jax.ShapeDtypeStruct((t, C_CONV), F32)),
        grid=(1,),
        in_specs=[col(0), col(1), col(2), full((nb, sw, C_CONV)),
                  full((CONV_WIDTH, C_CONV)), full((1, C_CONV)), full((1, C_CONV)), full((1, C_CONV)),
                  full((C_CONV, C_CONV)), full((1, C_CONV))],
        out_specs=(full((t, C_CONV)), full((t, C_CONV))),
        scratch_shapes=[pltpu.VMEM((nb, ST_ROWS, C_CONV), F32), pltpu.VMEM((t, C_CONV), F32)],
        compiler_params=pltpu.CompilerParams(dimension_semantics=("arbitrary",), vmem_limit_bytes=VMEM_LIMIT),
        name="conv_sample",
    )(z2, z2, z2, state_conv, w_dw, row(b_dw), row(ln_g), row(ln_b), w_pw_bf, row(b_pw))


FEAT_W = 2 * N_KV * HEAD_DIM


N_FEAT = FEAT_W // HEAD_DIM


def _chunk_feats(piece, w):
    outs = []
    for g in range(N_KV):
        xg = jnp.concatenate([piece(c, g).astype(BF16) for c in range(CMP_STRIDE)], axis=1)
        outs.append(_dot(xg, w))
    return jnp.concatenate(outs, axis=1)


CHUNK_ROWS = CMP_STRIDE * N_KV


def _feats_kernel(xk_ref, xv_ref, wk_ref, wv_ref, fk_ref, fv_ref, *, tm):
    for x_ref, w_ref, f_ref in ((xk_ref, wk_ref, fk_ref), (xv_ref, wv_ref, fv_ref)):
        f = _chunk_feats(lambda c, g: x_ref[pl.ds(c * N_KV + g, tm, stride=CHUNK_ROWS), :], w_ref[...])
        for cb in range(N_FEAT):
            f_ref[cb] = f[:, cb * HEAD_DIM:(cb + 1) * HEAD_DIM]


def _feats_prompt(xk, xv, wk, wv, tm=128):
    nc = xk.shape[0] // CHUNK_ROWS
    full = lambda shape: pl.BlockSpec(shape, lambda i: (0, 0))
    rows = pl.BlockSpec((tm * CHUNK_ROWS, HEAD_DIM), lambda i: (i, 0))
    ospec = pl.BlockSpec((N_FEAT, tm, HEAD_DIM), lambda i: (0, i, 0))
    return pl.pallas_call(
        functools.partial(_feats_kernel, tm=tm),
        out_shape=(jax.ShapeDtypeStruct((N_FEAT, nc, HEAD_DIM), F32),) * 2,
        grid=(nc // tm,),
        in_specs=[rows, rows, full(wk.shape), full(wv.shape)],
        out_specs=(ospec, ospec),
        compiler_params=pltpu.CompilerParams(dimension_semantics=("arbitrary",), vmem_limit_bytes=VMEM_LIMIT),
        name="feats_prompt",
    )(xk, xv, wk, wv)


FEAT_PAGES = 32
CHUNKS_PER_PAGE = PAGE_SIZE // CMP_STRIDE
PAGE_ROWS = PAGE_SIZE * N_KV
FEAT_PITCH = PAGE_ROWS + 8


def _page_copies(pt_ref, srcs, bufs, sem, step, slot, pages, pitch):
    out = []
    for p in range(pages):
        page = pt_ref[step * pages + p]
        for n, (src, buf) in enumerate(zip(srcs, bufs)):
            out.append(pltpu.make_async_copy(src.at[page], buf.at[slot, pl.ds(p * pitch, PAGE_ROWS)], sem.at[n, slot]))
    return out


def _gather_pipeline(pt_ref, srcs, bufs, sem, pages, pitch):
    s = pl.program_id(0)
    slot = s % 2

    def start_all(copies):
        for n, c in enumerate(copies):
            c.start(priority=n % 2)

    @pl.when(s == 0)
    def _():
        start_all(_page_copies(pt_ref, srcs, bufs, sem, 0, 0, pages, pitch))

    @pl.when(s + 1 < pl.num_programs(0))
    def _():
        start_all(_page_copies(pt_ref, srcs, bufs, sem, s + 1, 1 - slot, pages, pitch))

    for c in _page_copies(pt_ref, srcs, bufs, sem, s, slot, pages, pitch):
        c.wait()
    return slot


def _feats_gather_kernel(pt_ref, ck_hbm, cv_hbm, wk_ref, wv_ref, fk_ref, fv_ref, kbuf, vbuf, sem):
    slot = _gather_pipeline(pt_ref, (ck_hbm, cv_hbm), (kbuf, vbuf), sem, FEAT_PAGES, FEAT_PITCH)
    crows = CMP_STRIDE * N_KV

    def piece(buf):
        return lambda c, g: jnp.concatenate(
            [buf[slot, pl.ds(n * crows + c * N_KV + g, FEAT_PAGES, stride=FEAT_PITCH), :]
             for n in range(CHUNKS_PER_PAGE)], axis=0)

    for buf, w_ref, f_ref in ((kbuf, wk_ref, fk_ref), (vbuf, wv_ref, fv_ref)):
        f = _chunk_feats(piece(buf), w_ref[...])
        for n in range(CHUNKS_PER_PAGE):
            for cb in range(N_FEAT):
                f_ref[cb, pl.ds(n, FEAT_PAGES, stride=CHUNKS_PER_PAGE), :] = (
                    f[n * FEAT_PAGES:(n + 1) * FEAT_PAGES, cb * HEAD_DIM:(cb + 1) * HEAD_DIM])


def _feats_sample(page_table, cache_k, cache_v, wk, wv):
    n_pages = page_table.size
    n_phys = cache_k.shape[0]
    ck = cache_k.reshape(n_phys, PAGE_ROWS, HEAD_DIM)
    cv = cache_v.reshape(n_phys, PAGE_ROWS, HEAD_DIM)
    rows = FEAT_PAGES * CHUNKS_PER_PAGE
    brows = FEAT_PAGES * FEAT_PITCH
    wspec = pl.BlockSpec(wk.shape, lambda s, pt: (0, 0))
    ospec = pl.BlockSpec((N_FEAT, rows, HEAD_DIM), lambda s, pt: (0, s, 0))
    return pl.pallas_call(
        _feats_gather_kernel,
        out_shape=(jax.ShapeDtypeStruct((N_FEAT, n_pages * CHUNKS_PER_PAGE, HEAD_DIM), F32),) * 2,
        grid_spec=pltpu.PrefetchScalarGridSpec(
            num_scalar_prefetch=1,
            grid=(n_pages // FEAT_PAGES,),
            in_specs=[pl.BlockSpec(memory_space=pl.ANY), pl.BlockSpec(memory_space=pl.ANY), wspec, wspec],
            out_specs=(ospec, ospec),
            scratch_shapes=[pltpu.VMEM((2, brows, HEAD_DIM), F32), pltpu.VMEM((2, brows, HEAD_DIM), F32),
                            pltpu.SemaphoreType.DMA((2, 2))],
        ),
        compiler_params=pltpu.CompilerParams(dimension_semantics=("arbitrary",), vmem_limit_bytes=VMEM_LIMIT),
        name="feats_sample",
    )(page_table.reshape(-1), ck, cv, wk, wv)


def _cmp_finish_one(f_ref, w1_ref, pe_ref, w2_ref, g_ref, o_ref, nc):
    bias = _dot(pe_ref[...].astype(BF16), w1_ref[...])
    ba = bias[0:1, 0:HEAD_DIM]
    bb = bias[1:2, HEAD_DIM:2 * HEAD_DIM]
    row = _iota((nc, 1), 0)
    for g in range(N_KV):
        fa = f_ref[2 * g] + ba
        fb = f_ref[2 * g + 1] + bb
        hid = _silu(fa + pltpu.roll(fb, nc - 1, 0))
        o = _dot(hid.astype(BF16), w2_ref[...])
        if g_ref is not None:
            o = _rms(o, g_ref[...])
        o_ref[:, g * HEAD_DIM:(g + 1) * HEAD_DIM] = jnp.where(row < nc - 1, o, 0.0)


def _cmp_finish_kernel(fk_ref, fv_ref, wk1_ref, pek_ref, wk2_ref, gk_ref, wv1_ref, pev_ref, wv2_ref,
                       ok_ref, ov_ref, *, nc):
    _cmp_finish_one(fk_ref, wk1_ref, pek_ref, wk2_ref, gk_ref, ok_ref, nc)
    _cmp_finish_one(fv_ref, wv1_ref, pev_ref, wv2_ref, None, ov_ref, nc)


def _cmp_finish(fk, fv, k_weights, gain_k, v_weights, nc):
    nb = fk.shape[1] // nc
    full = lambda a: pl.BlockSpec(a.shape, lambda b: (0, 0))
    fspec = pl.BlockSpec((N_FEAT, nc, HEAD_DIM), lambda b: (0, b, 0))
    ospec = pl.BlockSpec((nc, N_KV * HEAD_DIM), lambda b: (b, 0))
    gk = gain_k.reshape(1, -1)
    return pl.pallas_call(
        functools.partial(_cmp_finish_kernel, nc=nc),
        out_shape=(jax.ShapeDtypeStruct((nb * nc, N_KV * HEAD_DIM), F32),) * 2,
        grid=(nb,),
        in_specs=[fspec, fspec] + [full(a) for a in k_weights] + [full(gk)] + [full(a) for a in v_weights],
        out_specs=(ospec, ospec),
        compiler_params=pltpu.CompilerParams(dimension_semantics=("arbitrary",), vmem_limit_bytes=VMEM_LIMIT),
        name="cmp_finish",
    )(fk, fv, *k_weights, gk, *v_weights)


SELECT_LANES = 128


def _select_kernel(imp_ref, cur_ref, sel_ref, *any_refs, any_rows):
    n_rows, n_blk = imp_ref.shape
    shape = (n_blk, SELECT_LANES)
    j = _iota(shape, 0)
    jf = j.astype(F32)

    def body(_, carry):
        score, sel_t = carry
        m = jnp.max(score, axis=0, keepdims=True)
        first = jnp.min(jnp.where(score == m, jf, 1e9), axis=0, keepdims=True)
        hit = jf == first
        return jnp.where(hit, -2.0, score), jnp.where(hit, 1.0, sel_t)

    for c in range(n_rows // SELECT_LANES):
        rows = slice(c * SELECT_LANES, (c + 1) * SELECT_LANES)
        imp_t = imp_ref[rows, :].T
        cur = cur_ref[:, rows]
        forced = (j == 0) | (j == cur) | (j == cur - 1)
        score0 = jnp.where(forced, 1e30, jnp.where(j <= cur, imp_t, -1.0))
        _, sel_t = lax.fori_loop(0, N_SELECT, body, (score0, jnp.zeros(shape, F32)))
        sel_ref[rows, :] = sel_t.T
    sel = sel_ref[...]
    if any_refs:
        for n in range(sel.shape[0] // any_rows):
            blk = jnp.max(sel[n * any_rows:(n + 1) * any_rows], axis=0, keepdims=True)
            any_refs[0][n * 8:(n + 1) * 8, :] = jnp.broadcast_to(blk, (8, sel.shape[1]))


def _select(imp, cur, n_lanes, tr, any_rows=None):
    r, w = imp.shape
    groups = w // n_lanes
    out_shape = [jax.ShapeDtypeStruct((r, w), F32)]
    out_specs = [pl.BlockSpec((tr, n_lanes), lambda i, g: (i, g))]
    if any_rows is not None:
        out_shape.append(jax.ShapeDtypeStruct((r // any_rows * 8, w), F32))
        out_specs.append(pl.BlockSpec((tr // any_rows * 8, n_lanes), lambda i, g: (i, g)))
    return pl.pallas_call(
        functools.partial(_select_kernel, any_rows=any_rows),
        out_shape=tuple(out_shape),
        grid=(r // tr, groups),
        in_specs=[pl.BlockSpec((tr, n_lanes), lambda i, g: (i, g)), pl.BlockSpec((1, tr), lambda i, g: (0, i))],
        out_specs=tuple(out_specs),
        compiler_params=pltpu.CompilerParams(
            dimension_semantics=("arbitrary", "arbitrary"), vmem_limit_bytes=VMEM_LIMIT),
        name="select",
    )(imp, cur)


def _imp_matrix(n_cmp_pad, n_sel_pad, n_cmp, n_sel):
    r = SEL_BLOCK // CMP_STRIDE
    lead = CMP_BLOCK // CMP_STRIDE - 1
    m = np.zeros((n_cmp_pad, n_sel_pad), np.float32)
    for jb in range(n_sel):
        for o in range(-lead, r):
            start = o * CMP_STRIDE
            w = (min(start + CMP_BLOCK, SEL_BLOCK) - max(start, 0)) / CMP_BLOCK
            i = r * jb + o
            if 0 <= i < n_cmp:
                m[i, jb] += w
    return jnp.asarray(m)


CMP_KEY_STEP = 128


def _cmp_select_tile(q_ref, kc_ref, vc_ref, m_ref, oc_ref, imp_ref, i, tq, nk):
    qpos = i * tq + _iota((tq, 1), 0)
    end = _iota((1, nk), 1) * CMP_STRIDE + (CMP_BLOCK - 1)
    dist = qpos - end
    mask = dist >= 0
    distf = dist.astype(F32)
    for g in range(N_KV):
        kg = kc_ref[0:nk, g * HEAD_DIM:(g + 1) * HEAD_DIM].astype(BF16)
        vg = vc_ref[0:nk, g * HEAD_DIM:(g + 1) * HEAD_DIM].astype(BF16)
        psum = jnp.zeros((tq, nk), F32)
        for r in range(GROUP):
            h = g * GROUP + r
            qh = q_ref[:, h * HEAD_DIM:(h + 1) * HEAD_DIM].astype(BF16)
            s = _dot_nt(qh, kg) * SCALE - SLOPES[h] * distf
            s = jnp.where(mask, s, NEG)
            m = jnp.max(s, axis=-1, keepdims=True)
            e = jnp.where(mask, jnp.exp(s - m), 0.0)
            p = e * (1.0 / jnp.maximum(jnp.sum(e, axis=-1, keepdims=True), 1e-30))
            psum = psum + p
            oc_ref[:, h * HEAD_DIM:(h + 1) * HEAD_DIM] = _dot(p.astype(BF16), vg)
        imp_ref[:, g * 128:(g + 1) * 128] = jnp.dot(psum, m_ref[0:nk, :], precision=lax.Precision.HIGHEST,
                                                    preferred_element_type=F32)


def _cmp_select_kernel(q_ref, kc_ref, vc_ref, m_ref, oc_ref, imp_ref, *, tq, nc):
    i = pl.program_id(0)
    reach = (i * tq + tq - 1 - (CMP_BLOCK - 1)) // CMP_STRIDE + 1
    for nk in range(CMP_KEY_STEP, nc + 1, CMP_KEY_STEP):
        covers = reach <= nk if nk == CMP_KEY_STEP else (reach > nk - CMP_KEY_STEP) & (reach <= nk)
        if nk == nc:
            covers = reach > nk - CMP_KEY_STEP

        @pl.when(covers)
        def _(nk=nk):
            _cmp_select_tile(q_ref, kc_ref, vc_ref, m_ref, oc_ref, imp_ref, i, tq, nk)


def _cmp_select(z, kcmp, vcmp, imp_m, tq):
    t = z.shape[0]
    nc = kcmp.shape[0]
    n_sel = imp_m.shape[1]
    assert n_sel == 128
    full = lambda shape: pl.BlockSpec(shape, lambda i: (0, 0))
    return pl.pallas_call(
        functools.partial(_cmp_select_kernel, tq=tq, nc=nc),
        out_shape=(jax.ShapeDtypeStruct((t, N_HEADS * HEAD_DIM), F32), jax.ShapeDtypeStruct((t, N_KV * 128), F32)),
        grid=(t // tq,),
        in_specs=[pl.BlockSpec((tq, N_HEADS * HEAD_DIM), lambda i: (i, COL_Q // (N_HEADS * HEAD_DIM))),
                  full(kcmp.shape), full(vcmp.shape), full(imp_m.shape)],
        out_specs=(pl.BlockSpec((tq, N_HEADS * HEAD_DIM), lambda i: (i, 0)),
                   pl.BlockSpec((tq, N_KV * 128), lambda i: (i, 0))),
        compiler_params=pltpu.CompilerParams(dimension_semantics=("arbitrary",), vmem_limit_bytes=VMEM_LIMIT),
        name="cmp_select",
    )(z, kcmp, vcmp, imp_m)


def _flash_update(s, v_bf, m_ref, l_ref, acc_ref, rows):
    m_old = m_ref[rows, :]
    m_new = jnp.maximum(m_old, jnp.max(s, axis=-1, keepdims=True))
    alpha = jnp.exp(m_old - m_new)
    p = jnp.exp(s - m_new)
    l_ref[rows, :] = alpha * l_ref[rows, :] + jnp.sum(p, axis=-1, keepdims=True)
    acc_ref[rows, :] = alpha * acc_ref[rows, :] + _dot(p.astype(BF16), v_bf)
    m_ref[rows, :] = m_new


LOG2E = 1.4426950408889634
V_ROWS = 2 * HEAD_DIM


def _selected_prompt_kernel(flag_ref, q_ref, k_ref, v_ref, sel_ref, et_ref, o_ref,
                            kbf, vaug, qs, m_sc, acc_sc, *, tq, tk):
    g = pl.program_id(0)
    qi = pl.program_id(1)
    nkt = kbf.shape[0]
    step = g * pl.num_programs(1) + qi

    @pl.when(qi == 0)
    def _():
        ones_row = jnp.where(_iota((V_ROWS - HEAD_DIM, tk), 0) == 0, 1.0, 0.0).astype(BF16)
        for ki in range(nkt):
            rows = slice(ki * tk, (ki + 1) * tk)
            kbf[ki] = k_ref[rows, :].astype(BF16)
            vaug[ki, 0:HEAD_DIM, :] = v_ref[rows, :].T.astype(BF16)
            vaug[ki, HEAD_DIM:V_ROWS, :] = ones_row

    for r in range(GROUP):
        qs[r] = (q_ref[:, r * HEAD_DIM:(r + 1) * HEAD_DIM] * (SCALE * LOG2E)).astype(BF16)
    m_sc[...] = jnp.full(m_sc.shape, M_INIT, F32)
    acc_sc[...] = jnp.zeros(acc_sc.shape, F32)
    sel_bf = sel_ref[...].astype(BF16)
    qpos = qi * tq + _iota((1, tq), 1)

    def body(ki, carry):
        @pl.when(flag_ref[step * nkt + ki] > 0)
        def _():
            k = kbf[ki]
            va = vaug[ki]
            sel_t = _dot_nt(et_ref[pl.ds(pl.multiple_of(ki * tk, tk), tk), :], sel_bf)
            kpos = ki * tk + _iota((tk, tq), 0)
            mb = jnp.where((sel_t > 0.5) & (kpos <= qpos), 0.0, NEG)
            krel = (kpos - qi * tq).astype(F32)
            for r in range(GROUP):
                slope = jnp.where(g == 0, SLOPES[r], SLOPES[GROUP + r]) * LOG2E
                s = _dot_nt(k, qs[r]) + (mb + slope * krel)
                m_old = m_sc[r]
                m_new = jnp.maximum(m_old, jnp.max(s, axis=0, keepdims=True))
                p = jnp.exp2(s - m_new).astype(BF16)
                acc_sc[r] = acc_sc[r] * jnp.exp2(m_old - m_new) + _dot(va, p)
                m_sc[r] = m_new
        return carry

    lax.fori_loop(0, (qi * tq + tq - 1) // tk + 1, body, 0)
    for r in range(GROUP):
        o_t = acc_sc[r, 0:HEAD_DIM, :] * (1.0 / acc_sc[r, HEAD_DIM:HEAD_DIM + 1, :])
        o_ref[:, r * HEAD_DIM:(r + 1) * HEAD_DIM] = o_t.T


def _tile_flags(anyb, t, tq, tk):
    nq, nkt = t // tq, t // tk
    a = anyb.reshape(nq, 8, N_KV, nkt, tk // SEL_BLOCK)[:, 0]
    return jnp.transpose(jnp.max(a, axis=-1) > 0.5, (1, 0, 2)).astype(jnp.int32).reshape(-1)


def _selected_prompt(z, sel, anyb, tq, tk=512):
    t = z.shape[0]
    gw = GROUP * HEAD_DIM
    flags = _tile_flags(anyb, t, tq, tk)
    et = jnp.asarray(np.arange(t)[:, None] // SEL_BLOCK == np.arange(128)[None, :], dtype=BF16)
    return pl.pallas_call(
        functools.partial(_selected_prompt_kernel, tq=tq, tk=tk),
        out_shape=jax.ShapeDtypeStruct((t, N_HEADS * HEAD_DIM), F32),
        grid_spec=pltpu.PrefetchScalarGridSpec(
            num_scalar_prefetch=1,
            grid=(N_KV, t // tq),
            in_specs=[
                pl.BlockSpec((tq, gw), lambda g, qi, f: (qi, COL_Q // gw + g)),
                pl.BlockSpec((t, HEAD_DIM), lambda g, qi, f: (0, COL_KS // HEAD_DIM + g)),
                pl.BlockSpec((t, HEAD_DIM), lambda g, qi, f: (0, COL_VS // HEAD_DIM + g)),
                pl.BlockSpec((tq, 128), lambda g, qi, f: (qi, g)),
                pl.BlockSpec((t, 128), lambda g, qi, f: (0, 0)),
            ],
            out_specs=pl.BlockSpec((tq, gw), lambda g, qi, f: (qi, g)),
            scratch_shapes=[pltpu.VMEM((t // tk, tk, HEAD_DIM), BF16), pltpu.VMEM((t // tk, V_ROWS, tk), BF16),
                            pltpu.VMEM((GROUP, tq, HEAD_DIM), BF16), pltpu.VMEM((GROUP, 1, tq), F32),
                            pltpu.VMEM((GROUP, V_ROWS, tq), F32)],
        ),
        compiler_params=pltpu.CompilerParams(
            dimension_semantics=("arbitrary", "arbitrary"), vmem_limit_bytes=VMEM_LIMIT),
        name="selected_prompt",
    )(flags, z, z, z, sel, et)


def _window_prompt_kernel(q_ref, kp_ref, kc_ref, vp_ref, vc_ref, o_ref, *, tq):
    g = pl.program_id(0)
    qi = pl.program_id(1)
    k = jnp.concatenate([kp_ref[...], kc_ref[...]], axis=0).astype(BF16)
    v = jnp.concatenate([vp_ref[...], vc_ref[...]], axis=0).astype(BF16)
    krel = _iota((1, 2 * tq), 1) - tq
    dist = _iota((tq, 1), 0) - krel
    mask = (dist >= 0) & (dist < WINDOW) & ((krel >= 0) | (qi > 0))
    distf = dist.astype(F32)
    for r in range(GROUP):
        slope = jnp.where(g == 0, SLOPES[r], SLOPES[GROUP + r])
        qh = q_ref[:, r * HEAD_DIM:(r + 1) * HEAD_DIM].astype(BF16)
        s = jnp.where(mask, _dot_nt(qh, k) * SCALE - slope * distf, NEG)
        m = jnp.max(s, axis=-1, keepdims=True)
        e = jnp.where(mask, jnp.exp(s - m), 0.0)
        p = e * (1.0 / jnp.maximum(jnp.sum(e, axis=-1, keepdims=True), 1e-30))
        o_ref[:, r * HEAD_DIM:(r + 1) * HEAD_DIM] = _dot(p.astype(BF16), v)


def _window_prompt(z, tq=WINDOW):
    t = z.shape[0]
    gw = GROUP * HEAD_DIM
    prev = lambda c: pl.BlockSpec((tq, HEAD_DIM), lambda g, qi: (jnp.maximum(qi - 1, 0), c // HEAD_DIM + g))
    cur = lambda c: pl.BlockSpec((tq, HEAD_DIM), lambda g, qi: (qi, c // HEAD_DIM + g))
    return pl.pallas_call(
        functools.partial(_window_prompt_kernel, tq=tq),
        out_shape=jax.ShapeDtypeStruct((t, N_HEADS * HEAD_DIM), F32),
        grid=(N_KV, t // tq),
        in_specs=[pl.BlockSpec((tq, gw), lambda g, qi: (qi, COL_Q // gw + g)),
                  prev(COL_KW), cur(COL_KW), prev(COL_VW), cur(COL_VW)],
        out_specs=pl.BlockSpec((tq, gw), lambda g, qi: (qi, g)),
        compiler_params=pltpu.CompilerParams(
            dimension_semantics=("arbitrary", "arbitrary"), vmem_limit_bytes=VMEM_LIMIT),
        name="window_prompt",
    )(z, z, z, z, z)


SEL_PAD = 384


def _slope_col(g, ds):
    return jnp.concatenate([jnp.full((ds, 1), SLOPES[g * GROUP + r], F32) for r in range(GROUP)], axis=0)


def _stack_heads(q_ref, g):
    return jnp.concatenate([q_ref[:, (g * GROUP + r) * HEAD_DIM:(g * GROUP + r + 1) * HEAD_DIM]
                            for r in range(GROUP)], axis=0).astype(BF16)


def _pad_rows(a, n):
    return jnp.concatenate([a, jnp.zeros((n - a.shape[0], a.shape[1]), a.dtype)], axis=0)


def _sample_small_kernel(q_ref, kc_ref, vc_ref, m_ref, sk_ref, sv_ref, nk_ref, nv_ref,
                         oc_ref, imp_ref, ow_ref, ko_ref, vo_ref, *, ds, nc, past, wb):
    rows = GROUP * ds
    qidx = _iota((rows, 1), 0) & (ds - 1)
    spos = past + qidx
    end = _iota((1, nc), 1) * CMP_STRIDE + (CMP_BLOCK - 1)
    dist_c = spos - end
    mask_c = dist_c >= 0
    ist = _iota((1, wb), 1)
    dist_s = wb + qidx - ist
    mask_s = dist_s < WINDOW
    jn = _iota((1, 128), 1)
    dist_n = qidx - jn
    mask_n = (dist_n >= 0) & (jn < ds)
    for g in range(N_KV):
        lanes = slice(g * HEAD_DIM, (g + 1) * HEAD_DIM)
        qs = _stack_heads(q_ref, g)
        slope = _slope_col(g, ds)
        s = _dot_nt(qs, kc_ref[:, lanes].astype(BF16)) * SCALE - slope * dist_c.astype(F32)
        s = jnp.where(mask_c, s, NEG)
        m = jnp.max(s, axis=-1, keepdims=True)
        e = jnp.where(mask_c, jnp.exp(s - m), 0.0)
        p = e * (1.0 / jnp.maximum(jnp.sum(e, axis=-1, keepdims=True), 1e-30))
        o = _dot(p.astype(BF16), vc_ref[:, lanes].astype(BF16))
        psum = p[0:ds]
        for r in range(1, GROUP):
            psum = psum + p[r * ds:(r + 1) * ds]
        imp_ref[:, g * SEL_PAD:(g + 1) * SEL_PAD] = jnp.dot(psum, m_ref[...], precision=lax.Precision.HIGHEST,
                                                            preferred_element_type=F32)
        kst = sk_ref[pl.ds(g, wb, stride=N_KV), :].astype(BF16)
        vst = sv_ref[pl.ds(g, wb, stride=N_KV), :].astype(BF16)
        kn = _pad_rows(nk_ref[pl.ds(g, ds, stride=N_KV), :], 128).astype(BF16)
        vn = _pad_rows(nv_ref[pl.ds(g, ds, stride=N_KV), :], 128).astype(BF16)
        s1 = jnp.where(mask_s, _dot_nt(qs, kst) * SCALE - slope * dist_s.astype(F32), NEG)
        s2 = jnp.where(mask_n, _dot_nt(qs, kn) * SCALE - slope * dist_n.astype(F32), NEG)
        mw = jnp.maximum(jnp.max(s1, axis=-1, keepdims=True), jnp.max(s2, axis=-1, keepdims=True))
        e1 = jnp.where(mask_s, jnp.exp(s1 - mw), 0.0)
        e2 = jnp.where(mask_n, jnp.exp(s2 - mw), 0.0)
        inv = 1.0 / jnp.maximum(jnp.sum(e1, axis=-1, keepdims=True) + jnp.sum(e2, axis=-1, keepdims=True), 1e-30)
        w = _dot((e1 * inv).astype(BF16), vst) + _dot((e2 * inv).astype(BF16), vn)
        for r in range(GROUP):
            h = g * GROUP + r
            oc_ref[:, h * HEAD_DIM:(h + 1) * HEAD_DIM] = o[r * ds:(r + 1) * ds]
            ow_ref[:, h * HEAD_DIM:(h + 1) * HEAD_DIM] = w[r * ds:(r + 1) * ds]
    keep = (wb - ds) * N_KV
    for s_ref, n_ref, o_ref in ((sk_ref, nk_ref, ko_ref), (sv_ref, nv_ref, vo_ref)):
        o_ref[0:keep, :] = s_ref[ds * N_KV:wb * N_KV, :]
        o_ref[keep:wb * N_KV, :] = n_ref[...]


def _sample_small(z2, kw2, vw2, kcmp2, vcmp2, imp_m2, state_k, state_v, nb, past):
    t = z2.shape[0]
    ds = t // nb
    nc = kcmp2.shape[0] // nb
    wb = state_k.shape[0] // (nb * N_KV)
    qw = N_HEADS * HEAD_DIM
    kvw = N_KV * HEAD_DIM
    rows = lambda n: pl.BlockSpec((n * N_KV, HEAD_DIM), lambda b: (b, 0))
    return pl.pallas_call(
        functools.partial(_sample_small_kernel, ds=ds, nc=nc, past=past, wb=wb),
        out_shape=(jax.ShapeDtypeStruct((t, qw), F32), jax.ShapeDtypeStruct((t, N_KV * SEL_PAD), F32),
                   jax.ShapeDtypeStruct((t, qw), F32),
                   jax.ShapeDtypeStruct(state_k.shape, F32), jax.ShapeDtypeStruct(state_v.shape, F32)),
        grid=(nb,),
        in_specs=[pl.BlockSpec((ds, qw), lambda b: (b, COL_Q // qw)),
                  pl.BlockSpec((nc, kvw), lambda b: (b, 0)), pl.BlockSpec((nc, kvw), lambda b: (b, 0)),
                  pl.BlockSpec(imp_m2.shape, lambda b: (0, 0)),
                  rows(wb), rows(wb), rows(ds), rows(ds)],
        out_specs=(pl.BlockSpec((ds, qw), lambda b: (b, 0)), pl.BlockSpec((ds, N_KV * SEL_PAD), lambda b: (b, 0)),
                   pl.BlockSpec((ds, qw), lambda b: (b, 0)), rows(wb), rows(wb)),
        compiler_params=pltpu.CompilerParams(dimension_semantics=("arbitrary",), vmem_limit_bytes=VMEM_LIMIT),
        name="sample_small",
    )(z2, kcmp2, vcmp2, imp_m2, state_k, state_v, kw2, vw2)


SLC_PAGES = 32
SLC_KEYS = SLC_PAGES * PAGE_SIZE
SLC_BLOCKS = SLC_KEYS // SEL_BLOCK
WIN_STEPS = 128 // SLC_BLOCKS


def _expand_matrix():
    j = np.arange(128)[:, None]
    c = np.arange(SLC_KEYS)[None, :]
    e = np.concatenate([(j == w * SLC_BLOCKS + c // SEL_BLOCK) for w in range(WIN_STEPS)], axis=0)
    return jnp.asarray(e, dtype=BF16)


def _selected_sample_kernel(pt_ref, ck_hbm, cv_hbm, q_ref, sel0_ref, sel1_ref, e_ref, nk_ref, nv_ref, o_ref,
                            kbuf, vbuf, sem, m_sc, l_sc, acc_sc, *, ds, steps, past):
    slot = _gather_pipeline(pt_ref, (ck_hbm, cv_hbm), (kbuf, vbuf), sem, SLC_PAGES, PAGE_ROWS)
    kt = pl.program_id(0) % steps
    rows = GROUP * ds
    nk = SLC_KEYS
    qidx = _iota((rows, 1), 0) & (ds - 1)

    @pl.when(kt == 0)
    def _():
        m_sc[...] = jnp.full(m_sc.shape, M_INIT, F32)
        l_sc[...] = jnp.zeros(l_sc.shape, F32)
        acc_sc[...] = jnp.zeros(acc_sc.shape, F32)

    krel = (kt * nk - past + _iota((1, nk), 1)).astype(F32)
    e = e_ref[pl.ds(pl.multiple_of((kt % WIN_STEPS) * 128, 128), 128), :]
    for g, sel_ref in enumerate((sel0_ref, sel1_ref)):
        grows = slice(g * rows, (g + 1) * rows)
        qs = _stack_heads(q_ref, g)
        slope = _slope_col(g, ds)
        selexp = _dot(sel_ref[...].astype(BF16), e)
        mb = jnp.where(jnp.concatenate([selexp] * GROUP, axis=0) > 0.5, 0.0, NEG)
        kg = kbuf[slot, pl.ds(g, nk, stride=N_KV), :].astype(BF16)
        vg = vbuf[slot, pl.ds(g, nk, stride=N_KV), :].astype(BF16)
        s = _dot_nt(qs, kg) * SCALE + (mb + slope * krel)
        _flash_update(s, vg, m_sc, l_sc, acc_sc, grows)

    @pl.when(kt == steps - 1)
    def _():
        jn = _iota((1, 128), 1)
        mb_n = jnp.where((jn <= qidx) & (jn < ds), 0.0, NEG)
        for g in range(N_KV):
            grows = slice(g * rows, (g + 1) * rows)
            qs = _stack_heads(q_ref, g)
            kn = _pad_rows(nk_ref[pl.ds(g, ds, stride=N_KV), :], 128).astype(BF16)
            vn = _pad_rows(nv_ref[pl.ds(g, ds, stride=N_KV), :], 128).astype(BF16)
            s = _dot_nt(qs, kn) * SCALE + (mb_n + _slope_col(g, ds) * jn.astype(F32))
            _flash_update(s, vn, m_sc, l_sc, acc_sc, grows)
            o = acc_sc[grows, :] * (1.0 / l_sc[grows, :])
            for r in range(GROUP):
                h = g * GROUP + r
                o_ref[:, h * HEAD_DIM:(h + 1) * HEAD_DIM] = o[r * ds:(r + 1) * ds]


def _selected_sample(page_table, cache_k, cache_v, z2, ks2, vs2, sel2, nb, past):
    t = z2.shape[0]
    ds = t // nb
    n_phys = cache_k.shape[0]
    ck = cache_k.reshape(n_phys, PAGE_ROWS, HEAD_DIM)
    cv = cache_v.reshape(n_phys, PAGE_ROWS, HEAD_DIM)
    steps = page_table.shape[1] // SLC_PAGES
    qw = N_HEADS * HEAD_DIM
    rows = N_KV * GROUP * ds
    e = _expand_matrix()
    win = lambda g: pl.BlockSpec(
        (ds, 128), lambda s, pt: (s // steps, g * (SEL_PAD // 128) + (s % steps) // WIN_STEPS))
    new = pl.BlockSpec((ds * N_KV, HEAD_DIM), lambda s, pt: (s // steps, 0))
    return pl.pallas_call(
        functools.partial(_selected_sample_kernel, ds=ds, steps=steps, past=past),
        out_shape=jax.ShapeDtypeStruct((t, qw), F32),
        grid_spec=pltpu.PrefetchScalarGridSpec(
            num_scalar_prefetch=1,
            grid=(nb * steps,),
            in_specs=[pl.BlockSpec(memory_space=pl.ANY), pl.BlockSpec(memory_space=pl.ANY),
                      pl.BlockSpec((ds, qw), lambda s, pt: (s // steps, COL_Q // qw)),
                      win(0), win(1), pl.BlockSpec(e.shape, lambda s, pt: (0, 0)), new, new],
            out_specs=pl.BlockSpec((ds, qw), lambda s, pt: (s // steps, 0)),
            scratch_shapes=[pltpu.VMEM((2, SLC_PAGES * PAGE_ROWS, HEAD_DIM), F32),
                            pltpu.VMEM((2, SLC_PAGES * PAGE_ROWS, HEAD_DIM), F32),
                            pltpu.SemaphoreType.DMA((2, 2)),
                            pltpu.VMEM((rows, 1), F32), pltpu.VMEM((rows, 1), F32),
                            pltpu.VMEM((rows, HEAD_DIM), F32)],
        ),
        compiler_params=pltpu.CompilerParams(dimension_semantics=("arbitrary",), vmem_limit_bytes=VMEM_LIMIT),
        name="selected_sample",
    )(page_table.reshape(-1), ck, cv, z2, sel2, sel2, e, ks2, vs2)


def _merge_kernel(x_ref, conv_ref, oc_ref, os_ref, ow_ref, gt_ref, zn_ref, w_ref, y_ref):
    gt = gt_ref[...]
    parts = [conv_ref[...].astype(BF16)]
    for h in range(N_HEADS):
        lanes = slice(h * HEAD_DIM, (h + 1) * HEAD_DIM)
        o = (gt[:, 3 * h:3 * h + 1] * oc_ref[:, lanes] + gt[:, 3 * h + 1:3 * h + 2] * os_ref[:, lanes]
             + gt[:, 3 * h + 2:3 * h + 3] * ow_ref[:, lanes])
        parts.append((o * _silu(zn_ref[:, lanes])).astype(BF16))
    y_ref[...] = x_ref[...] + _dot(jnp.concatenate(parts, axis=1), w_ref[...])


def _merge(x, conv_o, o_c, o_s, o_w, z, w_out_bf, tm=256):
    t, d = x.shape
    qw = N_HEADS * HEAD_DIM
    rowblk = lambda w: pl.BlockSpec((tm, w), lambda i: (i, 0))
    return pl.pallas_call(
        _merge_kernel,
        out_shape=jax.ShapeDtypeStruct((t, d), F32),
        grid=(t // tm,),
        in_specs=[rowblk(d), rowblk(C_CONV), rowblk(qw), rowblk(qw), rowblk(qw),
                  pl.BlockSpec((tm, 128), lambda i: (i, COL_GT // 128)),
                  pl.BlockSpec((tm, qw), lambda i: (i, COL_ZN // qw)),
                  pl.BlockSpec(w_out_bf.shape, lambda i: (0, 0))],
        out_specs=rowblk(d),
        compiler_params=pltpu.CompilerParams(dimension_semantics=("arbitrary",), vmem_limit_bytes=VMEM_LIMIT),
        name="merge",
    )(x, conv_o, o_c, o_s, o_w, z, z, w_out_bf)


N_GT = 3 * N_HEADS


def _pad_w_kernel(wt_hbm, o_ref, buf, sem):
    j = pl.program_id(0)
    slot = j % 2

    def tile_copy(jj, sl):
        src = jnp.where(jj <= TILE_GT, jj * TN, jj * TN - COL_ZN + COL_GT + N_GT)
        return pltpu.make_async_copy(wt_hbm.at[pl.ds(pl.multiple_of(src, 8), TN)], buf.at[sl], sem.at[sl])

    @pl.when(j == 0)
    def _():
        tile_copy(0, 0).start()

    @pl.when(j + 1 < pl.num_programs(0))
    def _():
        tile_copy(j + 1, 1 - slot).start()

    tile_copy(j, slot).wait()
    w = buf[slot]
    keep = (j != TILE_GT) | (_iota((TN, 1), 0) < N_GT)
    o_ref[...] = jnp.where(keep, w, 0.0).T.astype(BF16)


def _pad_w_in(w_in):
    d, n = w_in.shape
    assert n == COL_GT + N_GT + ZW - COL_ZN and COL_ZN == (TILE_GT + 1) * TN
    return pl.pallas_call(
        _pad_w_kernel,
        out_shape=jax.ShapeDtypeStruct((d, ZW), BF16),
        grid=(N_TILES,),
        in_specs=[pl.BlockSpec(memory_space=pl.ANY)],
        out_specs=pl.BlockSpec((d, TN), lambda j: (0, j)),
        scratch_shapes=[pltpu.VMEM((2, TN, d), F32), pltpu.SemaphoreType.DMA((2,))],
        compiler_params=pltpu.CompilerParams(dimension_semantics=("arbitrary",), vmem_limit_bytes=VMEM_LIMIT),
        name="pad_w_in",
    )(w_in.T)


def _cmp_weights(pe, w1, w2):
    half = CMP_STRIDE
    k = half * HEAD_DIM
    w1cat = jnp.concatenate([w1[:half].reshape(k, -1), w1[half:].reshape(k, -1)], axis=1).astype(BF16)
    pe2 = jnp.concatenate([pe[:half].reshape(1, k), pe[half:].reshape(1, k), jnp.zeros((6, k), pe.dtype)], axis=0)
    return w1cat, pe2, w2.astype(BF16)


def kernel(x_prompt, x_sample, cache_k_cmp, cache_v_cmp, cache_k_slc, cache_v_slc, state_k_win, state_v_win,
           state_conv, page_table, g_norm, w_in, pe_cmp_k, w_cmp_k1, w_cmp_k2, pe_cmp_v, w_cmp_v1, w_cmp_v2,
           g_q, g_k_cmp, g_k_slc, g_k_win, w_dw, b_dw, ln_g, ln_b, w_pw2, b_pw2, w_out):
    _, t, d = x_prompt.shape
    db, ds, _ = x_sample.shape
    past = page_table.shape[1] * PAGE_SIZE
    wb = state_k_win.shape[1]
    kvw = N_KV * HEAD_DIM
    assert ds < CMP_STRIDE and wb == WINDOW and t % SEL_BLOCK == 0 and t // SEL_BLOCK == 128

    w_p = _pad_w_in(w_in)
    w_pw_bf = w_pw2.astype(BF16)
    w_out_bf = w_out.astype(BF16)
    wk1, pek, wk2 = _cmp_weights(pe_cmp_k, w_cmp_k1, w_cmp_k2)
    wv1, pev, wv2 = _cmp_weights(pe_cmp_v, w_cmp_v1, w_cmp_v2)

    xp = x_prompt.reshape(t, d)
    z, kc, vc, ks, vs, kw, vw = _in_proj(xp, g_norm, w_p, g_q, g_k_slc, g_k_win, tm=1024)
    conv_o, conv_st = _conv_prompt(z, w_dw, b_dw, ln_g, ln_b, w_pw_bf, b_pw2)
    nc = t // CMP_STRIDE
    fk, fv = _feats_prompt(kc, vc, wk1, wv1)
    kcmp, vcmp = _cmp_finish(fk, fv, (wk1, pek, wk2), g_k_cmp, (wv1, pev, wv2), nc)
    imp_m = _imp_matrix(nc, t // SEL_BLOCK, nc - 1, t // SEL_BLOCK)
    o_c, imp = _cmp_select(z, kcmp, vcmp, imp_m, tq=256)
    cur = (jnp.arange(t, dtype=jnp.int32) // SEL_BLOCK).reshape(1, t)
    sel, anyb = _select(imp, cur, 128, tr=512, any_rows=256)
    o_s = _selected_prompt(z, sel, anyb, tq=256)
    o_w = _window_prompt(z)
    y_prompt = _merge(xp, conv_o, o_c, o_s, o_w, z, w_out_bf)

    xs = x_sample.reshape(db * ds, d)
    z2, kc2, vc2, ks2, vs2, kw2, vw2 = _in_proj(xs, g_norm, w_p, g_q, g_k_slc, g_k_win, tm=db * ds)
    conv_o2, u2 = _conv_sample(z2, state_conv, w_dw, b_dw, ln_g, ln_b, w_pw_bf, b_pw2)
    nc2 = past // CMP_STRIDE
    fk2, fv2 = _feats_sample(page_table, cache_k_cmp, cache_v_cmp, wk1, wv1)
    kcmp2, vcmp2 = _cmp_finish(fk2, fv2, (wk1, pek, wk2), g_k_cmp, (wv1, pev, wv2), nc2)
    n_sel2 = -(-(past + ds) // SEL_BLOCK)
    assert n_sel2 <= SEL_PAD
    imp_m2 = _imp_matrix(nc2, SEL_PAD, nc2 - 1, n_sel2)
    rows = lambda a: a.reshape(-1, HEAD_DIM)
    o_c2, imp2, o_w2, k_win, v_win = _sample_small(z2, kw2, vw2, kcmp2, vcmp2, imp_m2, rows(state_k_win),
                                                   rows(state_v_win), db, past)
    cur2 = ((past + jnp.arange(db * ds, dtype=jnp.int32) % ds) // SEL_BLOCK).reshape(1, db * ds)
    sel2, = _select(imp2, cur2, SEL_PAD, tr=db * ds)
    o_s2 = _selected_sample(page_table, cache_k_slc, cache_v_slc, z2, ks2, vs2, sel2, db, past)
    y_sample = _merge(xs, conv_o2, o_c2, o_s2, o_w2, z2, w_out_bf)

    kv4 = lambda a, b: a.reshape(b, -1, N_KV, HEAD_DIM)
    return (y_prompt.reshape(1, t, d), y_sample.reshape(db, ds, d),
            kv4(kc, 1), kv4(vc, 1), kv4(ks, 1), kv4(vs, 1), kv4(kw, 1)[:, t - wb:], kv4(vw, 1)[:, t - wb:],
            conv_st[HALO - (CONV_WIDTH - 1):][None],
            kv4(kc2, db), kv4(vc2, db), kv4(ks2, db), kv4(vs2, db), kv4(k_win, db), kv4(v_win, db),
            jnp.concatenate([state_conv[:, ds:], u2.reshape(db, ds, C_CONV)], axis=1))
```

```python
import functools

import numpy as np
import jax
import jax.numpy as jnp
from jax import lax
from jax.experimental import pallas as pl
from jax.experimental.pallas import tpu as pltpu

F32 = jnp.float32
BF16 = jnp.bfloat16

HEAD_DIM = 128
N_HEADS = 8
N_KV = 2
GROUP = 4
C_CONV = 1024
CONV_WIDTH = 31
CMP_STRIDE = 16
CMP_BLOCK = 32
SEL_BLOCK = 64
N_SELECT = 16
WINDOW = 512
PAGE_SIZE = 128
EPS = 1e-6
SCALE = HEAD_DIM ** -0.5
LOG2E = 1.4426950408889634
SLOPES = tuple(2.0 ** -(h + 1) for h in range(N_HEADS))

TN = 512
N_TILES = 14
ZW = TN * N_TILES
COL_UA, COL_UB, COL_ZC, COL_Q = 0, 1024, 2048, 3072
COL_KC, COL_VC, COL_KS, COL_VS, COL_KW, COL_VW = 4096, 4352, 4608, 4864, 5120, 5376
COL_GT, COL_ZN = 5632, 6144
TILE_Q0, TILE_Q1, TILE_CMP, TILE_SLC, TILE_WIN, TILE_GT = 6, 7, 8, 9, 10, 11

NEG = -1e30
M_INIT = -1e29
VMEM_LIMIT = 48 * 1024 * 1024
INPROJ_VMEM_LIMIT = 56 * 1024 * 1024


def _sigmoid(x):
    return 1.0 / (1.0 + jnp.exp(-x))


def _silu(x):
    return x * _sigmoid(x)


def _dot(a, b):
    return jnp.dot(a, b, preferred_element_type=F32)


def _dot_nt(a, b):
    return lax.dot_general(a, b, (((1,), (1,)), ((), ())), preferred_element_type=F32)


def _rms(a, g):
    return a * lax.rsqrt(jnp.mean(a * a, axis=-1, keepdims=True) + EPS) * g


def _iota(shape, dim):
    return lax.broadcasted_iota(jnp.int32, shape, dim)


def _inproj_kernel(x_ref, gn_ref, w_ref, gq_ref, gks_ref, gkw_ref,
                   z_ref, kc_o, vc_o, ks_o, vs_o, kw_o, vw_o, xn_ref, *, tm):
    j = pl.program_id(1)

    @pl.when(j == 0)
    def _():
        x = x_ref[...]
        ms = jnp.mean(x * x, axis=-1, keepdims=True)
        xn_ref[...] = (x * lax.rsqrt(ms + EPS) * gn_ref[...]).astype(BF16)

    z_ref[...] = _dot(xn_ref[...], w_ref[...])

    def kv_tile(g_ref, k_o, v_o):
        for c in range(TN // HEAD_DIM):
            lanes = slice(c * HEAD_DIM, (c + 1) * HEAD_DIM)
            a = z_ref[:, lanes]
            if c < N_KV and g_ref is not None:
                a = _rms(a, g_ref[...])
                z_ref[:, lanes] = a
            (k_o if c < N_KV else v_o)[pl.ds(c % N_KV, tm, stride=N_KV), :] = a

    @pl.when((j == TILE_Q0) | (j == TILE_Q1))
    def _():
        for c in range(TN // HEAD_DIM):
            lanes = slice(c * HEAD_DIM, (c + 1) * HEAD_DIM)
            z_ref[:, lanes] = _rms(z_ref[:, lanes], gq_ref[...])

    @pl.when(j == TILE_CMP)
    def _():
        kv_tile(None, kc_o, vc_o)

    @pl.when(j == TILE_SLC)
    def _():
        kv_tile(gks_ref, ks_o, vs_o)

    @pl.when(j == TILE_WIN)
    def _():
        kv_tile(gkw_ref, kw_o, vw_o)

    @pl.when(j == TILE_GT)
    def _():
        z_ref[...] = _sigmoid(z_ref[...])


def _in_proj(x, g_norm, w_p, g_q, g_ks, g_kw, tm):
    t, d = x.shape
    row = lambda a: a.reshape(1, -1)
    kv_shape = jax.ShapeDtypeStruct((t * N_KV, HEAD_DIM), F32)
    kv_spec = pl.BlockSpec((tm * N_KV, HEAD_DIM), lambda i, j: (i, 0))
    return pl.pallas_call(
        functools.partial(_inproj_kernel, tm=tm),
        out_shape=(jax.ShapeDtypeStruct((t, ZW), F32),) + (kv_shape,) * 6,
        grid=(t // tm, N_TILES),
        in_specs=[
            pl.BlockSpec((tm, d), lambda i, j: (i, 0)),
            pl.BlockSpec((1, d), lambda i, j: (0, 0)),
            pl.BlockSpec((d, TN), lambda i, j: (0, j)),
            pl.BlockSpec((1, HEAD_DIM), lambda i, j: (0, 0)),
            pl.BlockSpec((1, HEAD_DIM), lambda i, j: (0, 0)),
            pl.BlockSpec((1, HEAD_DIM), lambda i, j: (0, 0)),
        ],
        out_specs=(pl.BlockSpec((tm, TN), lambda i, j: (i, j)),) + (kv_spec,) * 6,
        scratch_shapes=[pltpu.VMEM((tm, d), BF16)],
        compiler_params=pltpu.CompilerParams(
            dimension_semantics=("arbitrary", "arbitrary"), vmem_limit_bytes=INPROJ_VMEM_LIMIT),
        name="in_proj",
    )(x, row(g_norm), w_p, row(g_q), row(g_ks), row(g_kw))


HALO = 32
CONV_RB = 64
CONV_CB = 128


def _conv_tail(y, zc, lng_ref, lnb_ref, wpw_ref, bpw_ref):
    mu = jnp.mean(y, axis=-1, keepdims=True)
    yc = y - mu
    var = jnp.mean(yc * yc, axis=-1, keepdims=True)
    yn = yc * lax.rsqrt(var + EPS) * lng_ref[...] + lnb_ref[...]
    act = _silu(yn).astype(BF16)
    return (_dot(act, wpw_ref[...]) + bpw_ref[...]) * _silu(zc)


def _conv_prompt_kernel(ua_ref, ub_ref, zc_ref, uah_ref, ubh_ref, wdw_ref, bdw_ref, lng_ref, lnb_ref,
                        wpw_ref, bpw_ref, o_ref, st_ref, buf, ybuf, *, tt):
    i = pl.program_id(0)
    uh = uah_ref[...] * _sigmoid(ubh_ref[...])
    buf[0:HALO, :] = jnp.where(i > 0, uh, 0.0)
    buf[HALO:HALO + tt, :] = ua_ref[...] * _sigmoid(ub_ref[...])
    off = HALO - (CONV_WIDTH - 1)
    for c0 in range(0, C_CONV, CONV_CB):
        lanes = slice(c0, c0 + CONV_CB)
        for r0 in range(0, tt, CONV_RB):
            acc = jnp.broadcast_to(bdw_ref[:, lanes], (CONV_RB, CONV_CB))
            for b in range(8):
                n = CONV_RB if b == 0 else CONV_RB + 8
                zb = None
                for a in range(-(-(off - b) // 8), (off + CONV_WIDTH - 1 - b) // 8 + 1):
                    k = 8 * a + b - off
                    term = wdw_ref[k:k + 1, lanes] * buf[r0 + 8 * a:r0 + 8 * a + n, lanes]
                    zb = term if zb is None else zb + term
                acc = acc + zb[b:b + CONV_RB]
            ybuf[r0:r0 + CONV_RB, lanes] = acc
    o_ref[...] = _conv_tail(ybuf[...], zc_ref[...], lng_ref, lnb_ref, wpw_ref, bpw_ref)

    @pl.when(i == pl.num_programs(0) - 1)
    def _():
        st_ref[...] = buf[tt:tt + HALO, :]


def _conv_prompt(z, w_dw, b_dw, ln_g, ln_b, w_pw_bf, b_pw, tt=256):
    t = z.shape[0]
    row = lambda a: a.reshape(1, -1)
    hb = tt // HALO
    cur = lambda c: pl.BlockSpec((tt, C_CONV), lambda i: (i, c))
    halo = lambda c: pl.BlockSpec((HALO, C_CONV), lambda i: (jnp.maximum(i * hb - 1, 0), c))
    full = lambda shape: pl.BlockSpec(shape, lambda i: (0, 0))
    return pl.pallas_call(
        functools.partial(_conv_prompt_kernel, tt=tt),
        out_shape=(jax.ShapeDtypeStruct((t, C_CONV), F32), jax.ShapeDtypeStruct((HALO, C_CONV), F32)),
        grid=(t // tt,),
        in_specs=[cur(0), cur(1), cur(2), halo(0), halo(1),
                  full((CONV_WIDTH, C_CONV)), full((1, C_CONV)), full((1, C_CONV)), full((1, C_CONV)),
                  full((C_CONV, C_CONV)), full((1, C_CONV))],
        out_specs=(pl.BlockSpec((tt, C_CONV), lambda i: (i, 0)), pl.BlockSpec((HALO, C_CONV), lambda i: (0, 0))),
        scratch_shapes=[pltpu.VMEM((HALO + tt, C_CONV), F32), pltpu.VMEM((tt, C_CONV), F32)],
        compiler_params=pltpu.CompilerParams(dimension_semantics=("arbitrary",), vmem_limit_bytes=VMEM_LIMIT),
        name="conv_prompt",
    )(z, z, z, z, z, w_dw, row(b_dw), row(ln_g), row(ln_b), w_pw_bf, row(b_pw))


ST_ROWS = 40


def _conv_sample_kernel(ua_ref, ub_ref, zc_ref, st_ref, wdw_ref, bdw_ref, lng_ref, lnb_ref, wpw_ref, bpw_ref,
                        o_ref, u_ref, fbuf, ybuf, *, nb, ds):
    u = ua_ref[...] * _sigmoid(ub_ref[...])
    u_ref[...] = u
    rows = _iota((nb * ds, 1), 0) & (ds - 1)
    acc_u = jnp.broadcast_to(bdw_ref[...], (nb * ds, C_CONV))
    for d in range(ds):
        sh = u if d == 0 else pltpu.roll(u, d, 0)
        acc_u = acc_u + jnp.where(rows >= d, sh, 0.0) * wdw_ref[CONV_WIDTH - 1 - d:CONV_WIDTH - d, :]
    ybuf[...] = acc_u
    fbuf[:, 24:ST_ROWS, :] = jnp.zeros((nb, ST_ROWS - 24, C_CONV), F32)
    fbuf[:, 0:CONV_WIDTH - 1, :] = st_ref[...]

    def body(b, carry):
        acc = jnp.zeros((ds, C_CONV), F32)
        for k in range(CONV_WIDTH - 1):
            acc = acc + wdw_ref[k:k + 1, :] * fbuf[b, k:k + ds, :]
        r = pl.multiple_of(b * ds, ds)
        ybuf[pl.ds(r, ds), :] = ybuf[pl.ds(r, ds), :] + acc
        return carry

    lax.fori_loop(0, nb, body, 0)
    o_ref[...] = _conv_tail(ybuf[...], zc_ref[...], lng_ref, lnb_ref, wpw_ref, bpw_ref)


def _conv_sample(z2, state_conv, w_dw, b_dw, ln_g, ln_b, w_pw_bf, b_pw):
    nb, sw, _ = state_conv.shape
    t = z2.shape[0]
    ds = t // nb
    assert sw == CONV_WIDTH - 1 and ds == 8
    row = lambda a: a.reshape(1, -1)
    col = lambda c: pl.BlockSpec((t, C_CONV), lambda i: (0, c))
    full = lambda shape: pl.BlockSpec(shape, lambda i: (0,) * len(shape))
    return pl.pallas_call(
        functools.partial(_conv_sample_kernel, nb=nb, ds=ds),
        out_shape=(jax.ShapeDtypeStruct((t, C_CONV), F32), jax.ShapeDtypeStruct((t, C_CONV), F32)),
        grid=(1,),
        in_specs=[col(0), col(1), col(2), full((nb, sw, C_CONV)),
                  full((CONV_WIDTH, C_CONV)), full((1, C_CONV)), full((1, C_CONV)), full((1, C_CONV)),
                  full((C_CONV, C_CONV)), full((1, C_CONV))],
        out_specs=(full((t, C_CONV)), full((t, C_CONV))),
        scratch_shapes=[pltpu.VMEM((nb, ST_ROWS, C_CONV), F32), pltpu.VMEM((t, C_CONV), F32)],
        compiler_params=pltpu.CompilerParams(dimension_semantics=("arbitrary",), vmem_limit_bytes=VMEM_LIMIT),
        name="conv_sample",
    )(z2, z2, z2, state_conv, w_dw, row(b_dw), row(ln_g), row(ln_b), w_pw_bf, row(b_pw))


FEAT_W = 2 * N_KV * HEAD_DIM


N_FEAT = FEAT_W // HEAD_DIM


def _chunk_feats(piece, w):
    outs = []
    for g in range(N_KV):
        xg = jnp.concatenate([piece(c, g).astype(BF16) for c in range(CMP_STRIDE)], axis=1)
        outs.append(_dot(xg, w))
    return jnp.concatenate(outs, axis=1)


CHUNK_ROWS = CMP_STRIDE * N_KV


def _feats_kernel(xk_ref, xv_ref, wk_ref, wv_ref, fk_ref, fv_ref, *, tm):
    for x_ref, w_ref, f_ref in ((xk_ref, wk_ref, fk_ref), (xv_ref, wv_ref, fv_ref)):
        f = _chunk_feats(lambda c, g: x_ref[pl.ds(c * N_KV + g, tm, stride=CHUNK_ROWS), :], w_ref[...])
        for cb in range(N_FEAT):
            f_ref[cb] = f[:, cb * HEAD_DIM:(cb + 1) * HEAD_DIM]


def _feats_prompt(xk, xv, wk, wv, tm=128):
    nc = xk.shape[0] // CHUNK_ROWS
    full = lambda shape: pl.BlockSpec(shape, lambda i: (0, 0))
    rows = pl.BlockSpec((tm * CHUNK_ROWS, HEAD_DIM), lambda i: (i, 0))
    ospec = pl.BlockSpec((N_FEAT, tm, HEAD_DIM), lambda i: (0, i, 0))
    return pl.pallas_call(
        functools.partial(_feats_kernel, tm=tm),
        out_shape=(jax.ShapeDtypeStruct((N_FEAT, nc, HEAD_DIM), F32),) * 2,
        grid=(nc // tm,),
        in_specs=[rows, rows, full(wk.shape), full(wv.shape)],
        out_specs=(ospec, ospec),
        compiler_params=pltpu.CompilerParams(dimension_semantics=("arbitrary",), vmem_limit_bytes=VMEM_LIMIT),
        name="feats_prompt",
    )(xk, xv, wk, wv)


FEAT_PAGES = 32
CHUNKS_PER_PAGE = PAGE_SIZE // CMP_STRIDE
PAGE_ROWS = PAGE_SIZE * N_KV
FEAT_PITCH = PAGE_ROWS + 8


def _page_copies(pt_ref, srcs, bufs, sem, step, slot, pages, pitch, look):
    out = []
    last = pt_ref.shape[0] - 1
    for p in range(pages + look):
        page = pt_ref[jnp.minimum(step * pages + p, last)] if look else pt_ref[step * pages + p]
        for n, (src, buf) in enumerate(zip(srcs, bufs)):
            out.append(pltpu.make_async_copy(src.at[page], buf.at[slot, pl.ds(p * pitch, PAGE_ROWS)], sem.at[n, slot]))
    return out


def _gather_pipeline(pt_ref, srcs, bufs, sem, pages, pitch, look=0):
    s = pl.program_id(0)
    slot = s % 2

    @pl.when(s == 0)
    def _():
        for c in _page_copies(pt_ref, srcs, bufs, sem, 0, 0, pages, pitch, look):
            c.start()

    @pl.when(s + 1 < pl.num_programs(0))
    def _():
        for c in _page_copies(pt_ref, srcs, bufs, sem, s + 1, 1 - slot, pages, pitch, look):
            c.start()

    for c in _page_copies(pt_ref, srcs, bufs, sem, s, slot, pages, pitch, look):
        c.wait()
    return slot


FEAT_CHUNKS = FEAT_PAGES * CHUNKS_PER_PAGE


def _cmp_sample_kernel(pt_ref, ck_hbm, cv_hbm, wk1_ref, pek_ref, wk2_ref, gk_ref, wv1_ref, pev_ref, wv2_ref,
                       ok_ref, ov_ref, kbuf, vbuf, sem, fsc, *, steps_per_seq):
    slot = _gather_pipeline(pt_ref, (ck_hbm, cv_hbm), (kbuf, vbuf), sem, FEAT_PAGES, FEAT_PITCH, look=1)
    last_of_seq = (pl.program_id(0) % steps_per_seq) == steps_per_seq - 1
    row = _iota((FEAT_CHUNKS, 1), 0)
    keep = jnp.logical_not(last_of_seq & (row == FEAT_CHUNKS - 1))

    def piece(buf):
        def get(c, g):
            parts = [buf[slot, pl.ds(n * CHUNK_ROWS + c * N_KV + g, FEAT_PAGES, stride=FEAT_PITCH), :]
                     for n in range(CHUNKS_PER_PAGE)]
            parts.append(buf[slot, pl.ds(FEAT_PAGES * FEAT_PITCH + c * N_KV + g, 8, stride=8), :])
            return jnp.concatenate(parts, axis=0)
        return get

    for buf, w1_ref, pe_ref, w2_ref, g_ref, o_ref in ((kbuf, wk1_ref, pek_ref, wk2_ref, gk_ref, ok_ref),
                                                      (vbuf, wv1_ref, pev_ref, wv2_ref, None, ov_ref)):
        f = _chunk_feats(piece(buf), w1_ref[...])
        for cb in range(N_FEAT):
            lanes = slice(cb * HEAD_DIM, (cb + 1) * HEAD_DIM)
            for n in range(CHUNKS_PER_PAGE):
                fsc[cb, pl.ds(n, FEAT_PAGES, stride=CHUNKS_PER_PAGE), :] = f[n * FEAT_PAGES:(n + 1) * FEAT_PAGES, lanes]
            fsc[cb, FEAT_CHUNKS:FEAT_CHUNKS + 8, :] = f[FEAT_CHUNKS:FEAT_CHUNKS + 8, lanes]
        bias = _dot(pe_ref[...].astype(BF16), w1_ref[...])
        ba = bias[0:1, 0:HEAD_DIM]
        bb = bias[1:2, HEAD_DIM:2 * HEAD_DIM]
        for g in range(N_KV):
            fa = fsc[2 * g, 0:FEAT_CHUNKS, :] + ba
            fb_next = fsc[2 * g + 1, 1:FEAT_CHUNKS + 1, :] + bb
            o = _dot(_silu(fa + fb_next).astype(BF16), w2_ref[...])
            if g_ref is not None:
                o = _rms(o, g_ref[...])
            o_ref[:, g * HEAD_DIM:(g + 1) * HEAD_DIM] = jnp.where(keep, o, 0.0)


def _cmp_sample(page_table, cache_k, cache_v, k_weights, gain_k, v_weights):
    n_seq, pages_per_seq = page_table.shape
    n_pages = page_table.size
    n_phys = cache_k.shape[0]
    ck = cache_k.reshape(n_phys, PAGE_ROWS, HEAD_DIM)
    cv = cache_v.reshape(n_phys, PAGE_ROWS, HEAD_DIM)
    brows = (FEAT_PAGES + 1) * FEAT_PITCH
    full = lambda a: pl.BlockSpec(a.shape, lambda s, pt: (0, 0))
    ospec = pl.BlockSpec((FEAT_CHUNKS, N_KV * HEAD_DIM), lambda s, pt: (s, 0))
    gk = gain_k.reshape(1, -1)
    return pl.pallas_call(
        functools.partial(_cmp_sample_kernel, steps_per_seq=pages_per_seq // FEAT_PAGES),
        out_shape=(jax.ShapeDtypeStruct((n_pages * CHUNKS_PER_PAGE, N_KV * HEAD_DIM), F32),) * 2,
        grid_spec=pltpu.PrefetchScalarGridSpec(
            num_scalar_prefetch=1,
            grid=(n_pages // FEAT_PAGES,),
            in_specs=[pl.BlockSpec(memory_space=pl.ANY), pl.BlockSpec(memory_space=pl.ANY)]
            + [full(a) for a in k_weights] + [full(gk)] + [full(a) for a in v_weights],
            out_specs=(ospec, ospec),
            scratch_shapes=[pltpu.VMEM((2, brows, HEAD_DIM), F32), pltpu.VMEM((2, brows, HEAD_DIM), F32),
                            pltpu.SemaphoreType.DMA((2, 2)),
                            pltpu.VMEM((N_FEAT, FEAT_CHUNKS + 8, HEAD_DIM), F32)],
        ),
        compiler_params=pltpu.CompilerParams(dimension_semantics=("arbitrary",), vmem_limit_bytes=VMEM_LIMIT),
        name="cmp_sample",
    )(page_table.reshape(-1), ck, cv, *k_weights, gk, *v_weights)


def _cmp_finish_one(f_ref, w1_ref, pe_ref, w2_ref, g_ref, o_ref, nc):
    bias = _dot(pe_ref[...].astype(BF16), w1_ref[...])
    ba = bias[0:1, 0:HEAD_DIM]
    bb = bias[1:2, HEAD_DIM:2 * HEAD_DIM]
    row = _iota((nc, 1), 0)
    for g in range(N_KV):
        fa = f_ref[2 * g] + ba
        fb = f_ref[2 * g + 1] + bb
        hid = _silu(fa + pltpu.roll(fb, nc - 1, 0))
        o = _dot(hid.astype(BF16), w2_ref[...])
        if g_ref is not None:
            o = _rms(o, g_ref[...])
        o_ref[:, g * HEAD_DIM:(g + 1) * HEAD_DIM] = jnp.where(row < nc - 1, o, 0.0)


def _cmp_finish_kernel(fk_ref, fv_ref, wk1_ref, pek_ref, wk2_ref, gk_ref, wv1_ref, pev_ref, wv2_ref,
                       ok_ref, ov_ref, *, nc):
    _cmp_finish_one(fk_ref, wk1_ref, pek_ref, wk2_ref, gk_ref, ok_ref, nc)
    _cmp_finish_one(fv_ref, wv1_ref, pev_ref, wv2_ref, None, ov_ref, nc)


def _cmp_finish(fk, fv, k_weights, gain_k, v_weights, nc):
    nb = fk.shape[1] // nc
    full = lambda a: pl.BlockSpec(a.shape, lambda b: (0, 0))
    fspec = pl.BlockSpec((N_FEAT, nc, HEAD_DIM), lambda b: (0, b, 0))
    ospec = pl.BlockSpec((nc, N_KV * HEAD_DIM), lambda b: (b, 0))
    gk = gain_k.reshape(1, -1)
    return pl.pallas_call(
        functools.partial(_cmp_finish_kernel, nc=nc),
        out_shape=(jax.ShapeDtypeStruct((nb * nc, N_KV * HEAD_DIM), F32),) * 2,
        grid=(nb,),
        in_specs=[fspec, fspec] + [full(a) for a in k_weights] + [full(gk)] + [full(a) for a in v_weights],
        out_specs=(ospec, ospec),
        compiler_params=pltpu.CompilerParams(dimension_semantics=("arbitrary",), vmem_limit_bytes=VMEM_LIMIT),
        name="cmp_finish",
    )(fk, fv, *k_weights, gk, *v_weights)


SELECT_LANES = 128


def _select_kernel(imp_ref, cur_ref, sel_ref, *any_refs, any_rows):
    n_rows, n_blk = imp_ref.shape
    shape = (n_blk, SELECT_LANES)
    j = _iota(shape, 0)
    jf = j.astype(F32)

    def body(_, carry):
        score, sel_t = carry
        m = jnp.max(score, axis=0, keepdims=True)
        first = jnp.min(jnp.where(score == m, jf, 1e9), axis=0, keepdims=True)
        hit = jf == first
        return jnp.where(hit, -2.0, score), jnp.where(hit, 1.0, sel_t)

    for c in range(n_rows // SELECT_LANES):
        rows = slice(c * SELECT_LANES, (c + 1) * SELECT_LANES)
        imp_t = imp_ref[rows, :].T
        cur = cur_ref[:, rows]
        forced = (j == 0) | (j == cur) | (j == cur - 1)
        score0 = jnp.where(forced, 1e30, jnp.where(j <= cur, imp_t, -1.0))
        _, sel_t = lax.fori_loop(0, N_SELECT, body, (score0, jnp.zeros(shape, F32)))
        sel_ref[rows, :] = sel_t.T
    sel = sel_ref[...]
    if any_refs:
        for n in range(sel.shape[0] // any_rows):
            blk = jnp.max(sel[n * any_rows:(n + 1) * any_rows], axis=0, keepdims=True)
            any_refs[0][n * 8:(n + 1) * 8, :] = jnp.broadcast_to(blk, (8, sel.shape[1]))


def _select(imp, cur, n_lanes, tr, any_rows=None):
    r, w = imp.shape
    groups = w // n_lanes
    out_shape = [jax.ShapeDtypeStruct((r, w), F32)]
    out_specs = [pl.BlockSpec((tr, n_lanes), lambda i, g: (i, g))]
    if any_rows is not None:
        out_shape.append(jax.ShapeDtypeStruct((r // any_rows * 8, w), F32))
        out_specs.append(pl.BlockSpec((tr // any_rows * 8, n_lanes), lambda i, g: (i, g)))
    return pl.pallas_call(
        functools.partial(_select_kernel, any_rows=any_rows),
        out_shape=tuple(out_shape),
        grid=(r // tr, groups),
        in_specs=[pl.BlockSpec((tr, n_lanes), lambda i, g: (i, g)), pl.BlockSpec((1, tr), lambda i, g: (0, i))],
        out_specs=tuple(out_specs),
        compiler_params=pltpu.CompilerParams(
            dimension_semantics=("arbitrary", "arbitrary"), vmem_limit_bytes=VMEM_LIMIT),
        name="select",
    )(imp, cur)


def _imp_matrix(n_cmp_pad, n_sel_pad, n_cmp, n_sel):
    r = SEL_BLOCK // CMP_STRIDE
    lead = CMP_BLOCK // CMP_STRIDE - 1
    m = np.zeros((n_cmp_pad, n_sel_pad), np.float32)
    for jb in range(n_sel):
        for o in range(-lead, r):
            start = o * CMP_STRIDE
            w = (min(start + CMP_BLOCK, SEL_BLOCK) - max(start, 0)) / CMP_BLOCK
            i = r * jb + o
            if 0 <= i < n_cmp:
                m[i, jb] += w
    return jnp.asarray(m)


CMP_KEY_STEP = 128


def _cmp_select_tile(q_ref, kc_ref, vc_ref, m_ref, oc_ref, imp_ref, i, tq, nk):
    qpos = i * tq + _iota((tq, 1), 0)
    end = _iota((1, nk), 1) * CMP_STRIDE + (CMP_BLOCK - 1)
    dist = qpos - end
    mask = dist >= 0
    distf = dist.astype(F32)
    for g in range(N_KV):
        kg = kc_ref[0:nk, g * HEAD_DIM:(g + 1) * HEAD_DIM].astype(BF16)
        vg = vc_ref[0:nk, g * HEAD_DIM:(g + 1) * HEAD_DIM].astype(BF16)
        psum = jnp.zeros((tq, nk), F32)
        for r in range(GROUP):
            h = g * GROUP + r
            qh = q_ref[:, h * HEAD_DIM:(h + 1) * HEAD_DIM].astype(BF16)
            s = _dot_nt(qh, kg) * SCALE - SLOPES[h] * distf
            s = jnp.where(mask, s, NEG)
            m = jnp.max(s, axis=-1, keepdims=True)
            e = jnp.where(mask, jnp.exp(s - m), 0.0)
            p = e * (1.0 / jnp.maximum(jnp.sum(e, axis=-1, keepdims=True), 1e-30))
            psum = psum + p
            oc_ref[:, h * HEAD_DIM:(h + 1) * HEAD_DIM] = _dot(p.astype(BF16), vg)
        imp_ref[:, g * 128:(g + 1) * 128] = jnp.dot(psum, m_ref[0:nk, :], precision=lax.Precision.HIGHEST,
                                                    preferred_element_type=F32)


def _cmp_select_kernel(q_ref, kc_ref, vc_ref, m_ref, oc_ref, imp_ref, *, tq, nc):
    i = pl.program_id(0)
    reach = (i * tq + tq - 1 - (CMP_BLOCK - 1)) // CMP_STRIDE + 1
    for nk in range(CMP_KEY_STEP, nc + 1, CMP_KEY_STEP):
        covers = reach <= nk if nk == CMP_KEY_STEP else (reach > nk - CMP_KEY_STEP) & (reach <= nk)
        if nk == nc:
            covers = reach > nk - CMP_KEY_STEP

        @pl.when(covers)
        def _(nk=nk):
            _cmp_select_tile(q_ref, kc_ref, vc_ref, m_ref, oc_ref, imp_ref, i, tq, nk)


def _cmp_select(z, kcmp, vcmp, imp_m, tq):
    t = z.shape[0]
    nc = kcmp.shape[0]
    n_sel = imp_m.shape[1]
    assert n_sel == 128
    full = lambda shape: pl.BlockSpec(shape, lambda i: (0, 0))
    return pl.pallas_call(
        functools.partial(_cmp_select_kernel, tq=tq, nc=nc),
        out_shape=(jax.ShapeDtypeStruct((t, N_HEADS * HEAD_DIM), F32), jax.ShapeDtypeStruct((t, N_KV * 128), F32)),
        grid=(t // tq,),
        in_specs=[pl.BlockSpec((tq, N_HEADS * HEAD_DIM), lambda i: (i, COL_Q // (N_HEADS * HEAD_DIM))),
                  full(kcmp.shape), full(vcmp.shape), full(imp_m.shape)],
        out_specs=(pl.BlockSpec((tq, N_HEADS * HEAD_DIM), lambda i: (i, 0)),
                   pl.BlockSpec((tq, N_KV * 128), lambda i: (i, 0))),
        compiler_params=pltpu.CompilerParams(dimension_semantics=("arbitrary",), vmem_limit_bytes=VMEM_LIMIT),
        name="cmp_select",
    )(z, kcmp, vcmp, imp_m)


def _flash_update(s, v_bf, m_ref, l_ref, acc_ref, rows):
    m_old = m_ref[rows, :]
    m_new = jnp.maximum(m_old, jnp.max(s, axis=-1, keepdims=True))
    alpha = jnp.exp(m_old - m_new)
    p = jnp.exp(s - m_new)
    l_ref[rows, :] = alpha * l_ref[rows, :] + jnp.sum(p, axis=-1, keepdims=True)
    acc_ref[rows, :] = alpha * acc_ref[rows, :] + _dot(p.astype(BF16), v_bf)
    m_ref[rows, :] = m_new


V_ROWS = 2 * HEAD_DIM


def _selected_prompt_kernel(flag_ref, q_ref, k_ref, v_ref, sel_ref, et_ref, o_ref,
                            kbf, vaug, qs, m_sc, acc_sc, *, tq, tk):
    g = pl.program_id(0)
    qi = pl.program_id(1)
    nkt = kbf.shape[0]
    step = g * pl.num_programs(1) + qi

    @pl.when(qi == 0)
    def _():
        ones_row = jnp.where(_iota((V_ROWS - HEAD_DIM, tk), 0) == 0, 1.0, 0.0).astype(BF16)
        for ki in range(nkt):
            rows = slice(ki * tk, (ki + 1) * tk)
            kbf[ki] = k_ref[rows, :].astype(BF16)
            vaug[ki, 0:HEAD_DIM, :] = v_ref[rows, :].T.astype(BF16)
            vaug[ki, HEAD_DIM:V_ROWS, :] = ones_row

    for r in range(GROUP):
        qs[r] = (q_ref[:, r * HEAD_DIM:(r + 1) * HEAD_DIM] * (SCALE * LOG2E)).astype(BF16)
    m_sc[...] = jnp.full(m_sc.shape, M_INIT, F32)
    acc_sc[...] = jnp.zeros(acc_sc.shape, F32)
    sel_bf = sel_ref[...].astype(BF16)
    qpos = qi * tq + _iota((1, tq), 1)

    def body(ki, carry):
        @pl.when(flag_ref[step * nkt + ki] > 0)
        def _():
            k = kbf[ki]
            va = vaug[ki]
            sel_t = _dot_nt(et_ref[pl.ds(pl.multiple_of(ki * tk, tk), tk), :], sel_bf)
            kpos = ki * tk + _iota((tk, tq), 0)
            mb = jnp.where((sel_t > 0.5) & (kpos <= qpos), 0.0, NEG)
            krel = (kpos - qi * tq).astype(F32)
            for r in range(GROUP):
                slope = jnp.where(g == 0, SLOPES[r], SLOPES[GROUP + r]) * LOG2E
                s = _dot_nt(k, qs[r]) + (mb + slope * krel)
                m_old = m_sc[r]
                m_new = jnp.maximum(m_old, jnp.max(s, axis=0, keepdims=True))
                p = jnp.exp2(s - m_new).astype(BF16)
                acc_sc[r] = acc_sc[r] * jnp.exp2(m_old - m_new) + _dot(va, p)
                m_sc[r] = m_new
        return carry

    lax.fori_loop(0, (qi * tq + tq - 1) // tk + 1, body, 0)
    for r in range(GROUP):
        o_t = acc_sc[r, 0:HEAD_DIM, :] * (1.0 / acc_sc[r, HEAD_DIM:HEAD_DIM + 1, :])
        o_ref[:, r * HEAD_DIM:(r + 1) * HEAD_DIM] = o_t.T


def _tile_flags(anyb, t, tq, tk):
    nq, nkt = t // tq, t // tk
    a = anyb.reshape(nq, 8, N_KV, nkt, tk // SEL_BLOCK)[:, 0]
    return jnp.transpose(jnp.max(a, axis=-1) > 0.5, (1, 0, 2)).astype(jnp.int32).reshape(-1)


def _selected_prompt(z, sel, anyb, tq, tk=512):
    t = z.shape[0]
    gw = GROUP * HEAD_DIM
    flags = _tile_flags(anyb, t, tq, tk)
    et = jnp.asarray(np.arange(t)[:, None] // SEL_BLOCK == np.arange(128)[None, :], dtype=BF16)
    return pl.pallas_call(
        functools.partial(_selected_prompt_kernel, tq=tq, tk=tk),
        out_shape=jax.ShapeDtypeStruct((t, N_HEADS * HEAD_DIM), F32),
        grid_spec=pltpu.PrefetchScalarGridSpec(
            num_scalar_prefetch=1,
            grid=(N_KV, t // tq),
            in_specs=[
                pl.BlockSpec((tq, gw), lambda g, qi, f: (qi, COL_Q // gw + g)),
                pl.BlockSpec((t, HEAD_DIM), lambda g, qi, f: (0, COL_KS // HEAD_DIM + g)),
                pl.BlockSpec((t, HEAD_DIM), lambda g, qi, f: (0, COL_VS // HEAD_DIM + g)),
                pl.BlockSpec((tq, 128), lambda g, qi, f: (qi, g)),
                pl.BlockSpec((t, 128), lambda g, qi, f: (0, 0)),
            ],
            out_specs=pl.BlockSpec((tq, gw), lambda g, qi, f: (qi, g)),
            scratch_shapes=[pltpu.VMEM((t // tk, tk, HEAD_DIM), BF16), pltpu.VMEM((t // tk, V_ROWS, tk), BF16),
                            pltpu.VMEM((GROUP, tq, HEAD_DIM), BF16), pltpu.VMEM((GROUP, 1, tq), F32),
                            pltpu.VMEM((GROUP, V_ROWS, tq), F32)],
        ),
        compiler_params=pltpu.CompilerParams(
            dimension_semantics=("arbitrary", "arbitrary"), vmem_limit_bytes=VMEM_LIMIT),
        name="selected_prompt",
    )(flags, z, z, z, sel, et)


def _window_prompt_kernel(q_ref, kp_ref, kc_ref, vp_ref, vc_ref, o_ref, *, tq):
    g = pl.program_id(0)
    qi = pl.program_id(1)
    k = jnp.concatenate([kp_ref[...], kc_ref[...]], axis=0).astype(BF16)
    v = jnp.concatenate([vp_ref[...], vc_ref[...]], axis=0).astype(BF16)
    krel = _iota((1, 2 * tq), 1) - tq
    dist = _iota((tq, 1), 0) - krel
    mask = (dist >= 0) & (dist < WINDOW) & ((krel >= 0) | (qi > 0))
    distf = dist.astype(F32)
    for r in range(GROUP):
        slope = jnp.where(g == 0, SLOPES[r], SLOPES[GROUP + r])
        qh = q_ref[:, r * HEAD_DIM:(r + 1) * HEAD_DIM].astype(BF16)
        s = jnp.where(mask, _dot_nt(qh, k) * SCALE - slope * distf, NEG)
        m = jnp.max(s, axis=-1, keepdims=True)
        e = jnp.where(mask, jnp.exp(s - m), 0.0)
        p = e * (1.0 / jnp.maximum(jnp.sum(e, axis=-1, keepdims=True), 1e-30))
        o_ref[:, r * HEAD_DIM:(r + 1) * HEAD_DIM] = _dot(p.astype(BF16), v)


def _window_prompt(z, tq=WINDOW):
    t = z.shape[0]
    gw = GROUP * HEAD_DIM
    prev = lambda c: pl.BlockSpec((tq, HEAD_DIM), lambda g, qi: (jnp.maximum(qi - 1, 0), c // HEAD_DIM + g))
    cur = lambda c: pl.BlockSpec((tq, HEAD_DIM), lambda g, qi: (qi, c // HEAD_DIM + g))
    return pl.pallas_call(
        functools.partial(_window_prompt_kernel, tq=tq),
        out_shape=jax.ShapeDtypeStruct((t, N_HEADS * HEAD_DIM), F32),
        grid=(N_KV, t // tq),
        in_specs=[pl.BlockSpec((tq, gw), lambda g, qi: (qi, COL_Q // gw + g)),
                  prev(COL_KW), cur(COL_KW), prev(COL_VW), cur(COL_VW)],
        out_specs=pl.BlockSpec((tq, gw), lambda g, qi: (qi, g)),
        compiler_params=pltpu.CompilerParams(
            dimension_semantics=("arbitrary", "arbitrary"), vmem_limit_bytes=VMEM_LIMIT),
        name="window_prompt",
    )(z, z, z, z, z)


SEL_PAD = 384


def _slope_col(g, ds):
    return jnp.concatenate([jnp.full((ds, 1), SLOPES[g * GROUP + r], F32) for r in range(GROUP)], axis=0)


def _stack_heads(q_ref, g):
    return jnp.concatenate([q_ref[:, (g * GROUP + r) * HEAD_DIM:(g * GROUP + r + 1) * HEAD_DIM]
                            for r in range(GROUP)], axis=0).astype(BF16)


def _pad_rows(a, n):
    return jnp.concatenate([a, jnp.zeros((n - a.shape[0], a.shape[1]), a.dtype)], axis=0)


def _sample_small_kernel(q_ref, kc_ref, vc_ref, m_ref, sk_ref, sv_ref, nk_ref, nv_ref,
                         oc_ref, imp_ref, ow_ref, ko_ref, vo_ref, *, ds, nc, past, wb):
    rows = GROUP * ds
    qidx = _iota((rows, 1), 0) & (ds - 1)
    spos = past + qidx
    end = _iota((1, nc), 1) * CMP_STRIDE + (CMP_BLOCK - 1)
    dist_c = spos - end
    mask_c = dist_c >= 0
    ist = _iota((1, wb), 1)
    dist_s = wb + qidx - ist
    mask_s = dist_s < WINDOW
    jn = _iota((1, 128), 1)
    dist_n = qidx - jn
    mask_n = (dist_n >= 0) & (jn < ds)
    for g in range(N_KV):
        lanes = slice(g * HEAD_DIM, (g + 1) * HEAD_DIM)
        qs = _stack_heads(q_ref, g)
        slope = _slope_col(g, ds)
        s = _dot_nt(qs, kc_ref[:, lanes].astype(BF16)) * SCALE - slope * dist_c.astype(F32)
        s = jnp.where(mask_c, s, NEG)
        m = jnp.max(s, axis=-1, keepdims=True)
        e = jnp.where(mask_c, jnp.exp(s - m), 0.0)
        p = e * (1.0 / jnp.maximum(jnp.sum(e, axis=-1, keepdims=True), 1e-30))
        o = _dot(p.astype(BF16), vc_ref[:, lanes].astype(BF16))
        psum = p[0:ds]
        for r in range(1, GROUP):
            psum = psum + p[r * ds:(r + 1) * ds]
        imp_ref[:, g * SEL_PAD:(g + 1) * SEL_PAD] = jnp.dot(psum, m_ref[...], precision=lax.Precision.HIGHEST,
                                                            preferred_element_type=F32)
        kst = sk_ref[pl.ds(g, wb, stride=N_KV), :].astype(BF16)
        vst = sv_ref[pl.ds(g, wb, stride=N_KV), :].astype(BF16)
        kn = _pad_rows(nk_ref[pl.ds(g, ds, stride=N_KV), :], 128).astype(BF16)
        vn = _pad_rows(nv_ref[pl.ds(g, ds, stride=N_KV), :], 128).astype(BF16)
        s1 = jnp.where(mask_s, _dot_nt(qs, kst) * SCALE - slope * dist_s.astype(F32), NEG)
        s2 = jnp.where(mask_n, _dot_nt(qs, kn) * SCALE - slope * dist_n.astype(F32), NEG)
        mw = jnp.maximum(jnp.max(s1, axis=-1, keepdims=True), jnp.max(s2, axis=-1, keepdims=True))
        e1 = jnp.where(mask_s, jnp.exp(s1 - mw), 0.0)
        e2 = jnp.where(mask_n, jnp.exp(s2 - mw), 0.0)
        inv = 1.0 / jnp.maximum(jnp.sum(e1, axis=-1, keepdims=True) + jnp.sum(e2, axis=-1, keepdims=True), 1e-30)
        w = _dot((e1 * inv).astype(BF16), vst) + _dot((e2 * inv).astype(BF16), vn)
        for r in range(GROUP):
            h = g * GROUP + r
            oc_ref[:, h * HEAD_DIM:(h + 1) * HEAD_DIM] = o[r * ds:(r + 1) * ds]
            ow_ref[:, h * HEAD_DIM:(h + 1) * HEAD_DIM] = w[r * ds:(r + 1) * ds]
    keep = (wb - ds) * N_KV
    for s_ref, n_ref, o_ref in ((sk_ref, nk_ref, ko_ref), (sv_ref, nv_ref, vo_ref)):
        o_ref[0:keep, :] = s_ref[ds * N_KV:wb * N_KV, :]
        o_ref[keep:wb * N_KV, :] = n_ref[...]


def _sample_small(z2, kw2, vw2, kcmp2, vcmp2, imp_m2, state_k, state_v, nb, past):
    t = z2.shape[0]
    ds = t // nb
    nc = kcmp2.shape[0] // nb
    wb = state_k.shape[0] // (nb * N_KV)
    qw = N_HEADS * HEAD_DIM
    kvw = N_KV * HEAD_DIM
    rows = lambda n: pl.BlockSpec((n * N_KV, HEAD_DIM), lambda b: (b, 0))
    return pl.pallas_call(
        functools.partial(_sample_small_kernel, ds=ds, nc=nc, past=past, wb=wb),
        out_shape=(jax.ShapeDtypeStruct((t, qw), F32), jax.ShapeDtypeStruct((t, N_KV * SEL_PAD), F32),
                   jax.ShapeDtypeStruct((t, qw), F32),
                   jax.ShapeDtypeStruct(state_k.shape, F32), jax.ShapeDtypeStruct(state_v.shape, F32)),
        grid=(nb,),
        in_specs=[pl.BlockSpec((ds, qw), lambda b: (b, COL_Q // qw)),
                  pl.BlockSpec((nc, kvw), lambda b: (b, 0)), pl.BlockSpec((nc, kvw), lambda b: (b, 0)),
                  pl.BlockSpec(imp_m2.shape, lambda b: (0, 0)),
                  rows(wb), rows(wb), rows(ds), rows(ds)],
        out_specs=(pl.BlockSpec((ds, qw), lambda b: (b, 0)), pl.BlockSpec((ds, N_KV * SEL_PAD), lambda b: (b, 0)),
                   pl.BlockSpec((ds, qw), lambda b: (b, 0)), rows(wb), rows(wb)),
        compiler_params=pltpu.CompilerParams(dimension_semantics=("arbitrary",), vmem_limit_bytes=VMEM_LIMIT),
        name="sample_small",
    )(z2, kcmp2, vcmp2, imp_m2, state_k, state_v, kw2, vw2)


SLC_PAGES = 32
SLC_KEYS = SLC_PAGES * PAGE_SIZE
SLC_BLOCKS = SLC_KEYS // SEL_BLOCK
WIN_STEPS = 128 // SLC_BLOCKS


def _expand_matrix():
    j = np.arange(128)[:, None]
    c = np.arange(SLC_KEYS)[None, :]
    e = np.concatenate([(j == w * SLC_BLOCKS + c // SEL_BLOCK) for w in range(WIN_STEPS)], axis=0)
    return jnp.asarray(e, dtype=BF16)


def _selected_sample_kernel(pt_ref, ck_hbm, cv_hbm, q_ref, sel0_ref, sel1_ref, e_ref, nk_ref, nv_ref, o_ref,
                            kbuf, vbuf, sem, m_sc, l_sc, acc_sc, *, ds, steps, past):
    slot = _gather_pipeline(pt_ref, (ck_hbm, cv_hbm), (kbuf, vbuf), sem, SLC_PAGES, PAGE_ROWS)
    kt = pl.program_id(0) % steps
    rows = GROUP * ds
    nk = SLC_KEYS
    qidx = _iota((rows, 1), 0) & (ds - 1)

    @pl.when(kt == 0)
    def _():
        m_sc[...] = jnp.full(m_sc.shape, M_INIT, F32)
        l_sc[...] = jnp.zeros(l_sc.shape, F32)
        acc_sc[...] = jnp.zeros(acc_sc.shape, F32)

    krel = (kt * nk - past + _iota((1, nk), 1)).astype(F32)
    e = e_ref[pl.ds(pl.multiple_of((kt % WIN_STEPS) * 128, 128), 128), :]
    for g, sel_ref in enumerate((sel0_ref, sel1_ref)):
        grows = slice(g * rows, (g + 1) * rows)
        qs = _stack_heads(q_ref, g)
        slope = _slope_col(g, ds)
        selexp = _dot(sel_ref[...].astype(BF16), e)
        mb = jnp.where(jnp.concatenate([selexp] * GROUP, axis=0) > 0.5, 0.0, NEG)
        kg = kbuf[slot, pl.ds(g, nk, stride=N_KV), :].astype(BF16)
        vg = vbuf[slot, pl.ds(g, nk, stride=N_KV), :].astype(BF16)
        s = _dot_nt(qs, kg) * SCALE + (mb + slope * krel)
        _flash_update(s, vg, m_sc, l_sc, acc_sc, grows)

    @pl.when(kt == steps - 1)
    def _():
        jn = _iota((1, 128), 1)
        mb_n = jnp.where((jn <= qidx) & (jn < ds), 0.0, NEG)
        for g in range(N_KV):
            grows = slice(g * rows, (g + 1) * rows)
            qs = _stack_heads(q_ref, g)
            kn = _pad_rows(nk_ref[pl.ds(g, ds, stride=N_KV), :], 128).astype(BF16)
            vn = _pad_rows(nv_ref[pl.ds(g, ds, stride=N_KV), :], 128).astype(BF16)
            s = _dot_nt(qs, kn) * SCALE + (mb_n + _slope_col(g, ds) * jn.astype(F32))
            _flash_update(s, vn, m_sc, l_sc, acc_sc, grows)
            o = acc_sc[grows, :] * (1.0 / l_sc[grows, :])
            for r in range(GROUP):
                h = g * GROUP + r
                o_ref[:, h * HEAD_DIM:(h + 1) * HEAD_DIM] = o[r * ds:(r + 1) * ds]


def _selected_sample(page_table, cache_k, cache_v, z2, ks2, vs2, sel2, nb, past):
    t = z2.shape[0]
    ds = t // nb
    n_phys = cache_k.shape[0]
    ck = cache_k.reshape(n_phys, PAGE_ROWS, HEAD_DIM)
    cv = cache_v.reshape(n_phys, PAGE_ROWS, HEAD_DIM)
    steps = page_table.shape[1] // SLC_PAGES
    qw = N_HEADS * HEAD_DIM
    rows = N_KV * GROUP * ds
    e = _expand_matrix()
    win = lambda g: pl.BlockSpec(
        (ds, 128), lambda s, pt: (s // steps, g * (SEL_PAD // 128) + (s % steps) // WIN_STEPS))
    new = pl.BlockSpec((ds * N_KV, HEAD_DIM), lambda s, pt: (s // steps, 0))
    return pl.pallas_call(
        functools.partial(_selected_sample_kernel, ds=ds, steps=steps, past=past),
        out_shape=jax.ShapeDtypeStruct((t, qw), F32),
        grid_spec=pltpu.PrefetchScalarGridSpec(
            num_scalar_prefetch=1,
            grid=(nb * steps,),
            in_specs=[pl.BlockSpec(memory_space=pl.ANY), pl.BlockSpec(memory_space=pl.ANY),
                      pl.BlockSpec((ds, qw), lambda s, pt: (s // steps, COL_Q // qw)),
                      win(0), win(1), pl.BlockSpec(e.shape, lambda s, pt: (0, 0)), new, new],
            out_specs=pl.BlockSpec((ds, qw), lambda s, pt: (s // steps, 0)),
            scratch_shapes=[pltpu.VMEM((2, SLC_PAGES * PAGE_ROWS, HEAD_DIM), F32),
                            pltpu.VMEM((2, SLC_PAGES * PAGE_ROWS, HEAD_DIM), F32),
                            pltpu.SemaphoreType.DMA((2, 2)),
                            pltpu.VMEM((rows, 1), F32), pltpu.VMEM((rows, 1), F32),
                            pltpu.VMEM((rows, HEAD_DIM), F32)],
        ),
        compiler_params=pltpu.CompilerParams(dimension_semantics=("arbitrary",), vmem_limit_bytes=VMEM_LIMIT),
        name="selected_sample",
    )(page_table.reshape(-1), ck, cv, z2, sel2, sel2, e, ks2, vs2)


def _merge_kernel(x_ref, conv_ref, oc_ref, os_ref, ow_ref, gt_ref, zn_ref, w_ref, y_ref):
    gt = gt_ref[...]
    parts = [conv_ref[...].astype(BF16)]
    for h in range(N_HEADS):
        lanes = slice(h * HEAD_DIM, (h + 1) * HEAD_DIM)
        o = (gt[:, 3 * h:3 * h + 1] * oc_ref[:, lanes] + gt[:, 3 * h + 1:3 * h + 2] * os_ref[:, lanes]
             + gt[:, 3 * h + 2:3 * h + 3] * ow_ref[:, lanes])
        parts.append((o * _silu(zn_ref[:, lanes])).astype(BF16))
    y_ref[...] = x_ref[...] + _dot(jnp.concatenate(parts, axis=1), w_ref[...])


def _merge(x, conv_o, o_c, o_s, o_w, z, w_out_bf, tm=256):
    t, d = x.shape
    qw = N_HEADS * HEAD_DIM
    rowblk = lambda w: pl.BlockSpec((tm, w), lambda i: (i, 0))
    return pl.pallas_call(
        _merge_kernel,
        out_shape=jax.ShapeDtypeStruct((t, d), F32),
        grid=(t // tm,),
        in_specs=[rowblk(d), rowblk(C_CONV), rowblk(qw), rowblk(qw), rowblk(qw),
                  pl.BlockSpec((tm, 128), lambda i: (i, COL_GT // 128)),
                  pl.BlockSpec((tm, qw), lambda i: (i, COL_ZN // qw)),
                  pl.BlockSpec(w_out_bf.shape, lambda i: (0, 0))],
        out_specs=rowblk(d),
        compiler_params=pltpu.CompilerParams(dimension_semantics=("arbitrary",), vmem_limit_bytes=VMEM_LIMIT),
        name="merge",
    )(x, conv_o, o_c, o_s, o_w, z, z, w_out_bf)


N_GT = 3 * N_HEADS


def _pad_w_kernel(wt_hbm, o_ref, buf, sem):
    j = pl.program_id(0)
    slot = j % 2

    def tile_copy(jj, sl):
        src = jnp.where(jj <= TILE_GT, jj * TN, jj * TN - COL_ZN + COL_GT + N_GT)
        return pltpu.make_async_copy(wt_hbm.at[pl.ds(pl.multiple_of(src, 8), TN)], buf.at[sl], sem.at[sl])

    @pl.when(j == 0)
    def _():
        tile_copy(0, 0).start()

    @pl.when(j + 1 < pl.num_programs(0))
    def _():
        tile_copy(j + 1, 1 - slot).start()

    tile_copy(j, slot).wait()
    w = buf[slot]
    keep = (j != TILE_GT) | (_iota((TN, 1), 0) < N_GT)
    o_ref[...] = jnp.where(keep, w, 0.0).T.astype(BF16)


def _pad_w_in(w_in):
    d, n = w_in.shape
    assert n == COL_GT + N_GT + ZW - COL_ZN and COL_ZN == (TILE_GT + 1) * TN
    return pl.pallas_call(
        _pad_w_kernel,
        out_shape=jax.ShapeDtypeStruct((d, ZW), BF16),
        grid=(N_TILES,),
        in_specs=[pl.BlockSpec(memory_space=pl.ANY)],
        out_specs=pl.BlockSpec((d, TN), lambda j: (0, j)),
        scratch_shapes=[pltpu.VMEM((2, TN, d), F32), pltpu.SemaphoreType.DMA((2,))],
        compiler_params=pltpu.CompilerParams(dimension_semantics=("arbitrary",), vmem_limit_bytes=VMEM_LIMIT),
        name="pad_w_in",
    )(w_in.T)


def _cmp_weights(pe, w1, w2):
    half = CMP_STRIDE
    k = half * HEAD_DIM
    w1cat = jnp.concatenate([w1[:half].reshape(k, -1), w1[half:].reshape(k, -1)], axis=1).astype(BF16)
    pe2 = jnp.concatenate([pe[:half].reshape(1, k), pe[half:].reshape(1, k), jnp.zeros((6, k), pe.dtype)], axis=0)
    return w1cat, pe2, w2.astype(BF16)


def kernel(x_prompt, x_sample, cache_k_cmp, cache_v_cmp, cache_k_slc, cache_v_slc, state_k_win, state_v_win,
           state_conv, page_table, g_norm, w_in, pe_cmp_k, w_cmp_k1, w_cmp_k2, pe_cmp_v, w_cmp_v1, w_cmp_v2,
           g_q, g_k_cmp, g_k_slc, g_k_win, w_dw, b_dw, ln_g, ln_b, w_pw2, b_pw2, w_out):
    _, t, d = x_prompt.shape
    db, ds, _ = x_sample.shape
    past = page_table.shape[1] * PAGE_SIZE
    wb = state_k_win.shape[1]
    kvw = N_KV * HEAD_DIM
    assert ds < CMP_STRIDE and wb == WINDOW and t % SEL_BLOCK == 0 and t // SEL_BLOCK == 128

    w_p = _pad_w_in(w_in)
    w_pw_bf = w_pw2.astype(BF16)
    w_out_bf = w_out.astype(BF16)
    wk1, pek, wk2 = _cmp_weights(pe_cmp_k, w_cmp_k1, w_cmp_k2)
    wv1, pev, wv2 = _cmp_weights(pe_cmp_v, w_cmp_v1, w_cmp_v2)

    xp = x_prompt.reshape(t, d)
    z, kc, vc, ks, vs, kw, vw = _in_proj(xp, g_norm, w_p, g_q, g_k_slc, g_k_win, tm=1024)
    conv_o, conv_st = _conv_prompt(z, w_dw, b_dw, ln_g, ln_b, w_pw_bf, b_pw2)
    nc = t // CMP_STRIDE
    fk, fv = _feats_prompt(kc, vc, wk1, wv1)
    kcmp, vcmp = _cmp_finish(fk, fv, (wk1, pek, wk2), g_k_cmp, (wv1, pev, wv2), nc)
    imp_m = _imp_matrix(nc, t // SEL_BLOCK, nc - 1, t // SEL_BLOCK)
    o_c, imp = _cmp_select(z, kcmp, vcmp, imp_m, tq=256)
    cur = (jnp.arange(t, dtype=jnp.int32) // SEL_BLOCK).reshape(1, t)
    sel, anyb = _select(imp, cur, 128, tr=512, any_rows=256)
    o_s = _selected_prompt(z, sel, anyb, tq=256)
    o_w = _window_prompt(z)
    y_prompt = _merge(xp, conv_o, o_c, o_s, o_w, z, w_out_bf)

    xs = x_sample.reshape(db * ds, d)
    z2, kc2, vc2, ks2, vs2, kw2, vw2 = _in_proj(xs, g_norm, w_p, g_q, g_k_slc, g_k_win, tm=db * ds)
    conv_o2, u2 = _conv_sample(z2, state_conv, w_dw, b_dw, ln_g, ln_b, w_pw_bf, b_pw2)
    nc2 = past // CMP_STRIDE
    kcmp2, vcmp2 = _cmp_sample(page_table, cache_k_cmp, cache_v_cmp, (wk1, pek, wk2), g_k_cmp, (wv1, pev, wv2))
    n_sel2 = -(-(past + ds) // SEL_BLOCK)
    assert n_sel2 <= SEL_PAD
    imp_m2 = _imp_matrix(nc2, SEL_PAD, nc2 - 1, n_sel2)
    rows = lambda a: a.reshape(-1, HEAD_DIM)
    o_c2, imp2, o_w2, k_win, v_win = _sample_small(z2, kw2, vw2, kcmp2, vcmp2, imp_m2, rows(state_k_win),
                                                   rows(state_v_win), db, past)
    cur2 = ((past + jnp.arange(db * ds, dtype=jnp.int32) % ds) // SEL_BLOCK).reshape(1, db * ds)
    sel2, = _select(imp2, cur2, SEL_PAD, tr=db * ds)
    o_s2 = _selected_sample(page_table, cache_k_slc, cache_v_slc, z2, ks2, vs2, sel2, db, past)
    y_sample = _merge(xs, conv_o2, o_c2, o_s2, o_w2, z2, w_out_bf)

    kv4 = lambda a, b: a.reshape(b, -1, N_KV, HEAD_DIM)
    return (y_prompt.reshape(1, t, d), y_sample.reshape(db, ds, d),
            kv4(kc, 1), kv4(vc, 1), kv4(ks, 1), kv4(vs, 1), kv4(kw, 1)[:, t - wb:], kv4(vw, 1)[:, t - wb:],
            conv_st[HALO - (CONV_WIDTH - 1):][None],
            kv4(kc2, db), kv4(vc2, db), kv4(ks2, db), kv4(vs2, db), kv4(k_win, db), kv4(v_win, db),
            jnp.concatenate([state_conv[:, ds:], u2.reshape(db, ds, C_CONV)], axis=1))
```

```python
import functools

import numpy as np
import jax
import jax.numpy as jnp
from jax import lax
from jax.experimental import pallas as pl
from jax.experimental.pallas import tpu as pltpu

F32 = jnp.float32
BF16 = jnp.bfloat16

HEAD_DIM = 128
N_HEADS = 8
N_KV = 2
GROUP = 4
C_CONV = 1024
CONV_WIDTH = 31
CMP_STRIDE = 16
CMP_BLOCK = 32
SEL_BLOCK = 64
N_SELECT = 16
WINDOW = 512
PAGE_SIZE = 128
EPS = 1e-6
SCALE = HEAD_DIM ** -0.5
LOG2E = 1.4426950408889634
SLOPES = tuple(2.0 ** -(h + 1) for h in range(N_HEADS))

TN = 512
N_TILES = 14
ZW = TN * N_TILES
COL_UA, COL_UB, COL_ZC, COL_Q = 0, 1024, 2048, 3072
COL_KC, COL_VC, COL_KS, COL_VS, COL_KW, COL_VW = 4096, 4352, 4608, 4864, 5120, 5376
COL_GT, COL_ZN = 5632, 6144
TILE_Q0, TILE_Q1, TILE_CMP, TILE_SLC, TILE_WIN, TILE_GT = 6, 7, 8, 9, 10, 11

NEG = -1e30
M_INIT = -1e29
VMEM_LIMIT = 48 * 1024 * 1024
INPROJ_VMEM_LIMIT = 56 * 1024 * 1024


def _sigmoid(x):
    return 1.0 / (1.0 + jnp.exp(-x))


def _silu(x):
    return x * _sigmoid(x)


def _dot(a, b):
    return jnp.dot(a, b, preferred_element_type=F32)


def _dot_nt(a, b):
    return lax.dot_general(a, b, (((1,), (1,)), ((), ())), preferred_element_type=F32)


def _rms(a, g):
    return a * lax.rsqrt(jnp.mean(a * a, axis=-1, keepdims=True) + EPS) * g


def _iota(shape, dim):
    return lax.broadcasted_iota(jnp.int32, shape, dim)


def _inproj_kernel(x_ref, gn_ref, w_ref, gq_ref, gks_ref, gkw_ref,
                   z_ref, kc_o, vc_o, ks_o, vs_o, kw_o, vw_o, xn_ref, *, tm):
    j = pl.program_id(1)

    @pl.when(j == 0)
    def _():
        x = x_ref[...]
        ms = jnp.mean(x * x, axis=-1, keepdims=True)
        xn_ref[...] = (x * lax.rsqrt(ms + EPS) * gn_ref[...]).astype(BF16)

    z_ref[...] = _dot(xn_ref[...], w_ref[...])

    def kv_tile(g_ref, k_o, v_o):
        for c in range(TN // HEAD_DIM):
            lanes = slice(c * HEAD_DIM, (c + 1) * HEAD_DIM)
            a = z_ref[:, lanes]
            if c < N_KV and g_ref is not None:
                a = _rms(a, g_ref[...])
                z_ref[:, lanes] = a
            (k_o if c < N_KV else v_o)[pl.ds(c % N_KV, tm, stride=N_KV), :] = a

    @pl.when((j == TILE_Q0) | (j == TILE_Q1))
    def _():
        for c in range(TN // HEAD_DIM):
            lanes = slice(c * HEAD_DIM, (c + 1) * HEAD_DIM)
            z_ref[:, lanes] = _rms(z_ref[:, lanes], gq_ref[...])

    @pl.when(j == TILE_CMP)
    def _():
        kv_tile(None, kc_o, vc_o)

    @pl.when(j == TILE_SLC)
    def _():
        kv_tile(gks_ref, ks_o, vs_o)

    @pl.when(j == TILE_WIN)
    def _():
        kv_tile(gkw_ref, kw_o, vw_o)

    @pl.when(j == TILE_GT)
    def _():
        z_ref[...] = _sigmoid(z_ref[...])


def _in_proj(x, g_norm, w_p, g_q, g_ks, g_kw, tm):
    t, d = x.shape
    row = lambda a: a.reshape(1, -1)
    kv_shape = jax.ShapeDtypeStruct((t * N_KV, HEAD_DIM), F32)
    kv_spec = pl.BlockSpec((tm * N_KV, HEAD_DIM), lambda i, j: (i, 0))
    return pl.pallas_call(
        functools.partial(_inproj_kernel, tm=tm),
        out_shape=(jax.ShapeDtypeStruct((t, ZW), F32),) + (kv_shape,) * 6,
        grid=(t // tm, N_TILES),
        in_specs=[
            pl.BlockSpec((tm, d), lambda i, j: (i, 0)),
            pl.BlockSpec((1, d), lambda i, j: (0, 0)),
            pl.BlockSpec((d, TN), lambda i, j: (0, j)),
            pl.BlockSpec((1, HEAD_DIM), lambda i, j: (0, 0)),
            pl.BlockSpec((1, HEAD_DIM), lambda i, j: (0, 0)),
            pl.BlockSpec((1, HEAD_DIM), lambda i, j: (0, 0)),
        ],
        out_specs=(pl.BlockSpec((tm, TN), lambda i, j: (i, j)),) + (kv_spec,) * 6,
        scratch_shapes=[pltpu.VMEM((tm, d), BF16)],
        compiler_params=pltpu.CompilerParams(
            dimension_semantics=("arbitrary", "arbitrary"), vmem_limit_bytes=INPROJ_VMEM_LIMIT),
        name="in_proj",
    )(x, row(g_norm), w_p, row(g_q), row(g_ks), row(g_kw))


HALO = 32
CONV_RB = 64
CONV_CB = 128


def _conv_tail(y, zc, lng_ref, lnb_ref, wpw_ref, bpw_ref):
    mu = jnp.mean(y, axis=-1, keepdims=True)
    yc = y - mu
    var = jnp.mean(yc * yc, axis=-1, keepdims=True)
    yn = yc * lax.rsqrt(var + EPS) * lng_ref[...] + lnb_ref[...]
    act = _silu(yn).astype(BF16)
    return (_dot(act, wpw_ref[...]) + bpw_ref[...]) * _silu(zc)


def _conv_prompt_kernel(ua_ref, ub_ref, zc_ref, uah_ref, ubh_ref, wdw_ref, bdw_ref, lng_ref, lnb_ref,
                        wpw_ref, bpw_ref, o_ref, st_ref, buf, ybuf, *, tt):
    i = pl.program_id(0)
    uh = uah_ref[...] * _sigmoid(ubh_ref[...])
    buf[0:HALO, :] = jnp.where(i > 0, uh, 0.0)
    buf[HALO:HALO + tt, :] = ua_ref[...] * _sigmoid(ub_ref[...])
    off = HALO - (CONV_WIDTH - 1)
    for c0 in range(0, C_CONV, CONV_CB):
        lanes = slice(c0, c0 + CONV_CB)
        for r0 in range(0, tt, CONV_RB):
            acc = jnp.broadcast_to(bdw_ref[:, lanes], (CONV_RB, CONV_CB))
            for b in range(8):
                n = CONV_RB if b == 0 else CONV_RB + 8
                zb = None
                for a in range(-(-(off - b) // 8), (off + CONV_WIDTH - 1 - b) // 8 + 1):
                    k = 8 * a + b - off
                    term = wdw_ref[k:k + 1, lanes] * buf[r0 + 8 * a:r0 + 8 * a + n, lanes]
                    zb = term if zb is None else zb + term
                acc = acc + zb[b:b + CONV_RB]
            ybuf[r0:r0 + CONV_RB, lanes] = acc
    o_ref[...] = _conv_tail(ybuf[...], zc_ref[...], lng_ref, lnb_ref, wpw_ref, bpw_ref)

    @pl.when(i == pl.num_programs(0) - 1)
    def _():
        st_ref[...] = buf[tt:tt + HALO, :]


def _conv_prompt(z, w_dw, b_dw, ln_g, ln_b, w_pw_bf, b_pw, tt=256):
    t = z.shape[0]
    row = lambda a: a.reshape(1, -1)
    hb = tt // HALO
    cur = lambda c: pl.BlockSpec((tt, C_CONV), lambda i: (i, c))
    halo = lambda c: pl.BlockSpec((HALO, C_CONV), lambda i: (jnp.maximum(i * hb - 1, 0), c))
    full = lambda shape: pl.BlockSpec(shape, lambda i: (0, 0))
    return pl.pallas_call(
        functools.partial(_conv_prompt_kernel, tt=tt),
        out_shape=(jax.ShapeDtypeStruct((t, C_CONV), F32), jax.ShapeDtypeStruct((HALO, C_CONV), F32)),
        grid=(t // tt,),
        in_specs=[cur(0), cur(1), cur(2), halo(0), halo(1),
                  full((CONV_WIDTH, C_CONV)), full((1, C_CONV)), full((1, C_CONV)), full((1, C_CONV)),
                  full((C_CONV, C_CONV)), full((1, C_CONV))],
        out_specs=(pl.BlockSpec((tt, C_CONV), lambda i: (i, 0)), pl.BlockSpec((HALO, C_CONV), lambda i: (0, 0))),
        scratch_shapes=[pltpu.VMEM((HALO + tt, C_CONV), F32), pltpu.VMEM((tt, C_CONV), F32)],
        compiler_params=pltpu.CompilerParams(dimension_semantics=("arbitrary",), vmem_limit_bytes=VMEM_LIMIT),
        name="conv_prompt",
    )(z, z, z, z, z, w_dw, row(b_dw), row(ln_g), row(ln_b), w_pw_bf, row(b_pw))


ST_ROWS = 40


def _conv_sample_kernel(ua_ref, ub_ref, zc_ref, st_ref, wdw_ref, bdw_ref, lng_ref, lnb_ref, wpw_ref, bpw_ref,
                        o_ref, u_ref, fbuf, ybuf, *, nb, ds):
    u = ua_ref[...] * _sigmoid(ub_ref[...])
    u_ref[...] = u
    rows = _iota((nb * ds, 1), 0) & (ds - 1)
    acc_u = jnp.broadcast_to(bdw_ref[...], (nb * ds, C_CONV))
    for d in range(ds):
        sh = u if d == 0 else pltpu.roll(u, d, 0)
        acc_u = acc_u + jnp.where(rows >= d, sh, 0.0) * wdw_ref[CONV_WIDTH - 1 - d:CONV_WIDTH - d, :]
    ybuf[...] = acc_u
    fbuf[:, 24:ST_ROWS, :] = jnp.zeros((nb, ST_ROWS - 24, C_CONV), F32)
    fbuf[:, 0:CONV_WIDTH - 1, :] = st_ref[...]

    def body(b, carry):
        acc = jnp.zeros((ds, C_CONV), F32)
        for k in range(CONV_WIDTH - 1):
            acc = acc + wdw_ref[k:k + 1, :] * fbuf[b, k:k + ds, :]
        r = pl.multiple_of(b * ds, ds)
        ybuf[pl.ds(r, ds), :] = ybuf[pl.ds(r, ds), :] + acc
        return carry

    lax.fori_loop(0, nb, body, 0)
    o_ref[...] = _conv_tail(ybuf[...], zc_ref[...], lng_ref, lnb_ref, wpw_ref, bpw_ref)


def _conv_sample(z2, state_conv, w_dw, b_dw, ln_g, ln_b, w_pw_bf, b_pw):
    nb, sw, _ = state_conv.shape
    t = z2.shape[0]
    ds = t // nb
    assert sw == CONV_WIDTH - 1 and ds == 8
    row = lambda a: a.reshape(1, -1)
    col = lambda c: pl.BlockSpec((t, C_CONV), lambda i: (0, c))
    full = lambda shape: pl.BlockSpec(shape, lambda i: (0,) * len(shape))
    return pl.pallas_call(
        functools.partial(_conv_sample_kernel, nb=nb, ds=ds),
        out_shape=(jax.ShapeDtypeStruct((t, C_CONV), F32), jax.ShapeDtypeStruct((t, C_CONV), F32)),
        grid=(1,),
        in_specs=[col(0), col(1), col(2), full((nb, sw, C_CONV)),
                  full((CONV_WIDTH, C_CONV)), full((1, C_CONV)), full((1, C_CONV)), full((1, C_CONV)),
                  full((C_CONV, C_CONV)), full((1, C_CONV))],
        out_specs=(full((t, C_CONV)), full((t, C_CONV))),
        scratch_shapes=[pltpu.VMEM((nb, ST_ROWS, C_CONV), F32), pltpu.VMEM((t, C_CONV), F32)],
        compiler_params=pltpu.CompilerParams(dimension_semantics=("arbitrary",), vmem_limit_bytes=VMEM_LIMIT),
        name="conv_sample",
    )(z2, z2, z2, state_conv, w_dw, row(b_dw), row(ln_g), row(ln_b), w_pw_bf, row(b_pw))


FEAT_W = 2 * N_KV * HEAD_DIM


N_FEAT = FEAT_W // HEAD_DIM


def _chunk_feats(piece, w):
    outs = []
    for g in range(N_KV):
        xg = jnp.concatenate([piece(c, g).astype(BF16) for c in range(CMP_STRIDE)], axis=1)
        outs.append(_dot(xg, w))
    return jnp.concatenate(outs, axis=1)


CHUNK_ROWS = CMP_STRIDE * N_KV


def _feats_kernel(xk_ref, xv_ref, wk_ref, wv_ref, fk_ref, fv_ref, *, tm):
    for x_ref, w_ref, f_ref in ((xk_ref, wk_ref, fk_ref), (xv_ref, wv_ref, fv_ref)):
        f = _chunk_feats(lambda c, g: x_ref[pl.ds(c * N_KV + g, tm, stride=CHUNK_ROWS), :], w_ref[...])
        for cb in range(N_FEAT):
            f_ref[cb] = f[:, cb * HEAD_DIM:(cb + 1) * HEAD_DIM]


def _feats_prompt(xk, xv, wk, wv, tm=128):
    nc = xk.shape[0] // CHUNK_ROWS
    full = lambda shape: pl.BlockSpec(shape, lambda i: (0, 0))
    rows = pl.BlockSpec((tm * CHUNK_ROWS, HEAD_DIM), lambda i: (i, 0))
    ospec = pl.BlockSpec((N_FEAT, tm, HEAD_DIM), lambda i: (0, i, 0))
    return pl.pallas_call(
        functools.partial(_feats_kernel, tm=tm),
        out_shape=(jax.ShapeDtypeStruct((N_FEAT, nc, HEAD_DIM), F32),) * 2,
        grid=(nc // tm,),
        in_specs=[rows, rows, full(wk.shape), full(wv.shape)],
        out_specs=(ospec, ospec),
        compiler_params=pltpu.CompilerParams(dimension_semantics=("arbitrary",), vmem_limit_bytes=VMEM_LIMIT),
        name="feats_prompt",
    )(xk, xv, wk, wv)


FEAT_PAGES = 32
CHUNKS_PER_PAGE = PAGE_SIZE // CMP_STRIDE
PAGE_ROWS = PAGE_SIZE * N_KV
FEAT_PITCH = PAGE_ROWS + 8


def _page_copies(pt_ref, srcs, bufs, sem, step, slot, pages, pitch, look):
    out = []
    last = pt_ref.shape[0] - 1
    for p in range(pages + look):
        page = pt_ref[jnp.minimum(step * pages + p, last)] if look else pt_ref[step * pages + p]
        for n, (src, buf) in enumerate(zip(srcs, bufs)):
            out.append(pltpu.make_async_copy(src.at[page], buf.at[slot, pl.ds(p * pitch, PAGE_ROWS)], sem.at[n, slot]))
    return out


def _gather_pipeline(pt_ref, srcs, bufs, sem, pages, pitch, look=0):
    s = pl.program_id(0)
    slot = s % 2

    @pl.when(s == 0)
    def _():
        for c in _page_copies(pt_ref, srcs, bufs, sem, 0, 0, pages, pitch, look):
            c.start()

    @pl.when(s + 1 < pl.num_programs(0))
    def _():
        for c in _page_copies(pt_ref, srcs, bufs, sem, s + 1, 1 - slot, pages, pitch, look):
            c.start()

    for c in _page_copies(pt_ref, srcs, bufs, sem, s, slot, pages, pitch, look):
        c.wait()
    return slot


FEAT_CHUNKS = FEAT_PAGES * CHUNKS_PER_PAGE


def _cmp_sample_kernel(pt_ref, ck_hbm, cv_hbm, wk1_ref, pek_ref, wk2_ref, gk_ref, wv1_ref, pev_ref, wv2_ref,
                       ok_ref, ov_ref, kbuf, vbuf, sem, fsc, bias_sc, *, steps_per_seq):
    slot = _gather_pipeline(pt_ref, (ck_hbm, cv_hbm), (kbuf, vbuf), sem, FEAT_PAGES, FEAT_PITCH, look=1)
    last_of_seq = (pl.program_id(0) % steps_per_seq) == steps_per_seq - 1
    row = _iota((FEAT_CHUNKS, 1), 0)
    keep = jnp.logical_not(last_of_seq & (row == FEAT_CHUNKS - 1))

    def piece(buf):
        def get(c, g):
            parts = [buf[slot, pl.ds(n * CHUNK_ROWS + c * N_KV + g, FEAT_PAGES, stride=FEAT_PITCH), :]
                     for n in range(CHUNKS_PER_PAGE)]
            parts.append(buf[slot, pl.ds(FEAT_PAGES * FEAT_PITCH + c * N_KV + g, 8, stride=8), :])
            return jnp.concatenate(parts, axis=0)
        return get

    @pl.when(pl.program_id(0) == 0)
    def _():
        for n, (pe_ref, w1_ref) in enumerate(((pek_ref, wk1_ref), (pev_ref, wv1_ref))):
            bias_sc[n] = _dot(pe_ref[...].astype(BF16), w1_ref[...])

    for kv, (buf, w1_ref, w2_ref, g_ref, o_ref) in enumerate(((kbuf, wk1_ref, wk2_ref, gk_ref, ok_ref),
                                                              (vbuf, wv1_ref, wv2_ref, None, ov_ref))):
        f = _chunk_feats(piece(buf), w1_ref[...])
        for cb in range(N_FEAT):
            lanes = slice(cb * HEAD_DIM, (cb + 1) * HEAD_DIM)
            for n in range(CHUNKS_PER_PAGE):
                fsc[cb, pl.ds(n, FEAT_PAGES, stride=CHUNKS_PER_PAGE), :] = f[n * FEAT_PAGES:(n + 1) * FEAT_PAGES, lanes]
            fsc[cb, FEAT_CHUNKS:FEAT_CHUNKS + 8, :] = f[FEAT_CHUNKS:FEAT_CHUNKS + 8, lanes]
        ba = bias_sc[kv, 0:1, 0:HEAD_DIM]
        bb = bias_sc[kv, 1:2, HEAD_DIM:2 * HEAD_DIM]
        for g in range(N_KV):
            fa = fsc[2 * g, 0:FEAT_CHUNKS, :] + ba
            fb_next = fsc[2 * g + 1, 1:FEAT_CHUNKS + 1, :] + bb
            o = _dot(_silu(fa + fb_next).astype(BF16), w2_ref[...])
            if g_ref is not None:
                o = _rms(o, g_ref[...])
            o_ref[:, g * HEAD_DIM:(g + 1) * HEAD_DIM] = jnp.where(keep, o, 0.0)


def _cmp_sample(page_table, cache_k, cache_v, k_weights, gain_k, v_weights):
    n_seq, pages_per_seq = page_table.shape
    n_pages = page_table.size
    n_phys = cache_k.shape[0]
    ck = cache_k.reshape(n_phys, PAGE_ROWS, HEAD_DIM)
    cv = cache_v.reshape(n_phys, PAGE_ROWS, HEAD_DIM)
    brows = (FEAT_PAGES + 1) * FEAT_PITCH
    full = lambda a: pl.BlockSpec(a.shape, lambda s, pt: (0, 0))
    ospec = pl.BlockSpec((FEAT_CHUNKS, N_KV * HEAD_DIM), lambda s, pt: (s, 0))
    gk = gain_k.reshape(1, -1)
    return pl.pallas_call(
        functools.partial(_cmp_sample_kernel, steps_per_seq=pages_per_seq // FEAT_PAGES),
        out_shape=(jax.ShapeDtypeStruct((n_pages * CHUNKS_PER_PAGE, N_KV * HEAD_DIM), F32),) * 2,
        grid_spec=pltpu.PrefetchScalarGridSpec(
            num_scalar_prefetch=1,
            grid=(n_pages // FEAT_PAGES,),
            in_specs=[pl.BlockSpec(memory_space=pl.ANY), pl.BlockSpec(memory_space=pl.ANY)]
            + [full(a) for a in k_weights] + [full(gk)] + [full(a) for a in v_weights],
            out_specs=(ospec, ospec),
            scratch_shapes=[pltpu.VMEM((2, brows, HEAD_DIM), F32), pltpu.VMEM((2, brows, HEAD_DIM), F32),
                            pltpu.SemaphoreType.DMA((2, 2)),
                            pltpu.VMEM((N_FEAT, FEAT_CHUNKS + 8, HEAD_DIM), F32),
                            pltpu.VMEM((2, 8, 2 * HEAD_DIM), F32)],
        ),
        compiler_params=pltpu.CompilerParams(dimension_semantics=("arbitrary",), vmem_limit_bytes=VMEM_LIMIT),
        name="cmp_sample",
    )(page_table.reshape(-1), ck, cv, *k_weights, gk, *v_weights)


def _cmp_finish_one(f_ref, w1_ref, pe_ref, w2_ref, g_ref, o_ref, nc):
    bias = _dot(pe_ref[...].astype(BF16), w1_ref[...])
    ba = bias[0:1, 0:HEAD_DIM]
    bb = bias[1:2, HEAD_DIM:2 * HEAD_DIM]
    row = _iota((nc, 1), 0)
    for g in range(N_KV):
        fa = f_ref[2 * g] + ba
        fb = f_ref[2 * g + 1] + bb
        hid = _silu(fa + pltpu.roll(fb, nc - 1, 0))
        o = _dot(hid.astype(BF16), w2_ref[...])
        if g_ref is not None:
            o = _rms(o, g_ref[...])
        o_ref[:, g * HEAD_DIM:(g + 1) * HEAD_DIM] = jnp.where(row < nc - 1, o, 0.0)


def _cmp_finish_kernel(fk_ref, fv_ref, wk1_ref, pek_ref, wk2_ref, gk_ref, wv1_ref, pev_ref, wv2_ref,
                       ok_ref, ov_ref, *, nc):
    _cmp_finish_one(fk_ref, wk1_ref, pek_ref, wk2_ref, gk_ref, ok_ref, nc)
    _cmp_finish_one(fv_ref, wv1_ref, pev_ref, wv2_ref, None, ov_ref, nc)


def _cmp_finish(fk, fv, k_weights, gain_k, v_weights, nc):
    nb = fk.shape[1] // nc
    full = lambda a: pl.BlockSpec(a.shape, lambda b: (0, 0))
    fspec = pl.BlockSpec((N_FEAT, nc, HEAD_DIM), lambda b: (0, b, 0))
    ospec = pl.BlockSpec((nc, N_KV * HEAD_DIM), lambda b: (b, 0))
    gk = gain_k.reshape(1, -1)
    return pl.pallas_call(
        functools.partial(_cmp_finish_kernel, nc=nc),
        out_shape=(jax.ShapeDtypeStruct((nb * nc, N_KV * HEAD_DIM), F32),) * 2,
        grid=(nb,),
        in_specs=[fspec, fspec] + [full(a) for a in k_weights] + [full(gk)] + [full(a) for a in v_weights],
        out_specs=(ospec, ospec),
        compiler_params=pltpu.CompilerParams(dimension_semantics=("arbitrary",), vmem_limit_bytes=VMEM_LIMIT),
        name="cmp_finish",
    )(fk, fv, *k_weights, gk, *v_weights)


SELECT_LANES = 128


def _select_kernel(imp_ref, cur_ref, sel_ref, *any_refs, any_rows):
    n_rows, n_blk = imp_ref.shape
    shape = (n_blk, SELECT_LANES)
    j = _iota(shape, 0)
    jf = j.astype(F32)

    def body(_, carry):
        score, sel_t = carry
        m = jnp.max(score, axis=0, keepdims=True)
        first = jnp.min(jnp.where(score == m, jf, 1e9), axis=0, keepdims=True)
        hit = jf == first
        return jnp.where(hit, -2.0, score), jnp.where(hit, 1.0, sel_t)

    for c in range(n_rows // SELECT_LANES):
        rows = slice(c * SELECT_LANES, (c + 1) * SELECT_LANES)
        imp_t = imp_ref[rows, :].T
        cur = cur_ref[:, rows]
        forced = (j == 0) | (j == cur) | (j == cur - 1)
        score0 = jnp.where(forced, 1e30, jnp.where(j <= cur, imp_t, -1.0))
        _, sel_t = lax.fori_loop(0, N_SELECT, body, (score0, jnp.zeros(shape, F32)))
        sel_ref[rows, :] = sel_t.T
    sel = sel_ref[...]
    if any_refs:
        for n in range(sel.shape[0] // any_rows):
            blk = jnp.max(sel[n * any_rows:(n + 1) * any_rows], axis=0, keepdims=True)
            any_refs[0][n * 8:(n + 1) * 8, :] = jnp.broadcast_to(blk, (8, sel.shape[1]))


def _select(imp, cur, n_lanes, tr, any_rows=None):
    r, w = imp.shape
    groups = w // n_lanes
    out_shape = [jax.ShapeDtypeStruct((r, w), F32)]
    out_specs = [pl.BlockSpec((tr, n_lanes), lambda i, g: (i, g))]
    if any_rows is not None:
        out_shape.append(jax.ShapeDtypeStruct((r // any_rows * 8, w), F32))
        out_specs.append(pl.BlockSpec((tr // any_rows * 8, n_lanes), lambda i, g: (i, g)))
    return pl.pallas_call(
        functools.partial(_select_kernel, any_rows=any_rows),
        out_shape=tuple(out_shape),
        grid=(r // tr, groups),
        in_specs=[pl.BlockSpec((tr, n_lanes), lambda i, g: (i, g)), pl.BlockSpec((1, tr), lambda i, g: (0, i))],
        out_specs=tuple(out_specs),
        compiler_params=pltpu.CompilerParams(
            dimension_semantics=("arbitrary", "arbitrary"), vmem_limit_bytes=VMEM_LIMIT),
        name="select",
    )(imp, cur)


def _imp_matrix(n_cmp_pad, n_sel_pad, n_cmp, n_sel):
    r = SEL_BLOCK // CMP_STRIDE
    lead = CMP_BLOCK // CMP_STRIDE - 1
    m = np.zeros((n_cmp_pad, n_sel_pad), np.float32)
    for jb in range(n_sel):
        for o in range(-lead, r):
            start = o * CMP_STRIDE
            w = (min(start + CMP_BLOCK, SEL_BLOCK) - max(start, 0)) / CMP_BLOCK
            i = r * jb + o
            if 0 <= i < n_cmp:
                m[i, jb] += w
    return jnp.asarray(m)


CMP_KEY_STEP = 128


def _cmp_select_tile(q_ref, kc_ref, vc_ref, m_ref, oc_ref, imp_ref, i, tq, nk):
    qpos = i * tq + _iota((tq, 1), 0)
    end = _iota((1, nk), 1) * CMP_STRIDE + (CMP_BLOCK - 1)
    dist = qpos - end
    mask = dist >= 0
    distf = dist.astype(F32)
    for g in range(N_KV):
        kg = kc_ref[0:nk, g * HEAD_DIM:(g + 1) * HEAD_DIM].astype(BF16)
        vg = vc_ref[0:nk, g * HEAD_DIM:(g + 1) * HEAD_DIM].astype(BF16)
        psum = jnp.zeros((tq, nk), F32)
        for r in range(GROUP):
            h = g * GROUP + r
            qh = q_ref[:, h * HEAD_DIM:(h + 1) * HEAD_DIM].astype(BF16)
            s = _dot_nt(qh, kg) * SCALE - SLOPES[h] * distf
            s = jnp.where(mask, s, NEG)
            m = jnp.max(s, axis=-1, keepdims=True)
            e = jnp.where(mask, jnp.exp(s - m), 0.0)
            p = e * (1.0 / jnp.maximum(jnp.sum(e, axis=-1, keepdims=True), 1e-30))
            psum = psum + p
            oc_ref[:, h * HEAD_DIM:(h + 1) * HEAD_DIM] = _dot(p.astype(BF16), vg)
        imp_ref[:, g * 128:(g + 1) * 128] = jnp.dot(psum, m_ref[0:nk, :], precision=lax.Precision.HIGHEST,
                                                    preferred_element_type=F32)


def _cmp_select_kernel(q_ref, kc_ref, vc_ref, m_ref, oc_ref, imp_ref, *, tq, nc):
    i = pl.program_id(0)
    reach = (i * tq + tq - 1 - (CMP_BLOCK - 1)) // CMP_STRIDE + 1
    for nk in range(CMP_KEY_STEP, nc + 1, CMP_KEY_STEP):
        covers = reach <= nk if nk == CMP_KEY_STEP else (reach > nk - CMP_KEY_STEP) & (reach <= nk)
        if nk == nc:
            covers = reach > nk - CMP_KEY_STEP

        @pl.when(covers)
        def _(nk=nk):
            _cmp_select_tile(q_ref, kc_ref, vc_ref, m_ref, oc_ref, imp_ref, i, tq, nk)


def _cmp_select(z, kcmp, vcmp, imp_m, tq):
    t = z.shape[0]
    nc = kcmp.shape[0]
    n_sel = imp_m.shape[1]
    assert n_sel == 128
    full = lambda shape: pl.BlockSpec(shape, lambda i: (0, 0))
    return pl.pallas_call(
        functools.partial(_cmp_select_kernel, tq=tq, nc=nc),
        out_shape=(jax.ShapeDtypeStruct((t, N_HEADS * HEAD_DIM), F32), jax.ShapeDtypeStruct((t, N_KV * 128), F32)),
        grid=(t // tq,),
        in_specs=[pl.BlockSpec((tq, N_HEADS * HEAD_DIM), lambda i: (i, COL_Q // (N_HEADS * HEAD_DIM))),
                  full(kcmp.shape), full(vcmp.shape), full(imp_m.shape)],
        out_specs=(pl.BlockSpec((tq, N_HEADS * HEAD_DIM), lambda i: (i, 0)),
                   pl.BlockSpec((tq, N_KV * 128), lambda i: (i, 0))),
        compiler_params=pltpu.CompilerParams(dimension_semantics=("arbitrary",), vmem_limit_bytes=VMEM_LIMIT),
        name="cmp_select",
    )(z, kcmp, vcmp, imp_m)


def _flash_update(s, v_bf, m_ref, l_ref, acc_ref, rows):
    m_old = m_ref[rows, :]
    m_new = jnp.maximum(m_old, jnp.max(s, axis=-1, keepdims=True))
    alpha = jnp.exp(m_old - m_new)
    p = jnp.exp(s - m_new)
    l_ref[rows, :] = alpha * l_ref[rows, :] + jnp.sum(p, axis=-1, keepdims=True)
    acc_ref[rows, :] = alpha * acc_ref[rows, :] + _dot(p.astype(BF16), v_bf)
    m_ref[rows, :] = m_new


V_ROWS = 2 * HEAD_DIM


def _selected_prompt_kernel(flag_ref, q_ref, k_ref, v_ref, sel_ref, et_ref, o_ref,
                            kbf, vaug, qs, m_sc, acc_sc, *, tq, tk):
    g = pl.program_id(0)
    qi = pl.program_id(1)
    nkt = kbf.shape[0]
    step = g * pl.num_programs(1) + qi

    @pl.when(qi == 0)
    def _():
        ones_row = jnp.where(_iota((V_ROWS - HEAD_DIM, tk), 0) == 0, 1.0, 0.0).astype(BF16)
        for ki in range(nkt):
            rows = slice(ki * tk, (ki + 1) * tk)
            kbf[ki] = k_ref[rows, :].astype(BF16)
            vaug[ki, 0:HEAD_DIM, :] = v_ref[rows, :].T.astype(BF16)
            vaug[ki, HEAD_DIM:V_ROWS, :] = ones_row

    for r in range(GROUP):
        qs[r * tq:(r + 1) * tq, :] = (q_ref[:, r * HEAD_DIM:(r + 1) * HEAD_DIM] * (SCALE * LOG2E)).astype(BF16)
    m_sc[...] = jnp.full(m_sc.shape, M_INIT, F32)
    acc_sc[...] = jnp.zeros(acc_sc.shape, F32)
    sel_bf = sel_ref[...].astype(BF16)
    qpos = qi * tq + _iota((1, tq), 1)

    def body(ki, carry):
        @pl.when(flag_ref[step * nkt + ki] > 0)
        def _():
            k = kbf[ki]
            va = vaug[ki]
            sel_t = _dot_nt(et_ref[pl.ds(pl.multiple_of(ki * tk, tk), tk), :], sel_bf)
            kpos = ki * tk + _iota((tk, tq), 0)
            mb = jnp.where((sel_t > 0.5) & (kpos <= qpos), 0.0, NEG)
            krel = (kpos - qi * tq).astype(F32)
            s_all = _dot_nt(k, qs[...])
            for r in range(GROUP):
                slope = jnp.where(g == 0, SLOPES[r], SLOPES[GROUP + r]) * LOG2E
                s = s_all[:, r * tq:(r + 1) * tq] + (mb + slope * krel)
                m_old = m_sc[r]
                m_new = jnp.maximum(m_old, jnp.max(s, axis=0, keepdims=True))
                p = jnp.exp2(s - m_new).astype(BF16)
                acc_sc[r] = acc_sc[r] * jnp.exp2(m_old - m_new) + _dot(va, p)
                m_sc[r] = m_new
        return carry

    lax.fori_loop(0, (qi * tq + tq - 1) // tk + 1, body, 0)
    for r in range(GROUP):
        o_t = acc_sc[r, 0:HEAD_DIM, :] * (1.0 / acc_sc[r, HEAD_DIM:HEAD_DIM + 1, :])
        o_ref[:, r * HEAD_DIM:(r + 1) * HEAD_DIM] = o_t.T


def _tile_flags(anyb, t, tq, tk):
    nq, nkt = t // tq, t // tk
    a = anyb.reshape(nq, 8, N_KV, nkt, tk // SEL_BLOCK)[:, 0]
    return jnp.transpose(jnp.max(a, axis=-1) > 0.5, (1, 0, 2)).astype(jnp.int32).reshape(-1)


def _selected_prompt(z, sel, anyb, tq, tk=512):
    t = z.shape[0]
    gw = GROUP * HEAD_DIM
    flags = _tile_flags(anyb, t, tq, tk)
    et = jnp.asarray(np.arange(t)[:, None] // SEL_BLOCK == np.arange(128)[None, :], dtype=BF16)
    return pl.pallas_call(
        functools.partial(_selected_prompt_kernel, tq=tq, tk=tk),
        out_shape=jax.ShapeDtypeStruct((t, N_HEADS * HEAD_DIM), F32),
        grid_spec=pltpu.PrefetchScalarGridSpec(
            num_scalar_prefetch=1,
            grid=(N_KV, t // tq),
            in_specs=[
                pl.BlockSpec((tq, gw), lambda g, qi, f: (qi, COL_Q // gw + g)),
                pl.BlockSpec((t, HEAD_DIM), lambda g, qi, f: (0, COL_KS // HEAD_DIM + g)),
                pl.BlockSpec((t, HEAD_DIM), lambda g, qi, f: (0, COL_VS // HEAD_DIM + g)),
                pl.BlockSpec((tq, 128), lambda g, qi, f: (qi, g)),
                pl.BlockSpec((t, 128), lambda g, qi, f: (0, 0)),
            ],
            out_specs=pl.BlockSpec((tq, gw), lambda g, qi, f: (qi, g)),
            scratch_shapes=[pltpu.VMEM((t // tk, tk, HEAD_DIM), BF16), pltpu.VMEM((t // tk, V_ROWS, tk), BF16),
                            pltpu.VMEM((GROUP * tq, HEAD_DIM), BF16), pltpu.VMEM((GROUP, 1, tq), F32),
                            pltpu.VMEM((GROUP, V_ROWS, tq), F32)],
        ),
        compiler_params=pltpu.CompilerParams(
            dimension_semantics=("arbitrary", "arbitrary"), vmem_limit_bytes=VMEM_LIMIT),
        name="selected_prompt",
    )(flags, z, z, z, sel, et)


def _window_prompt_kernel(q_ref, kp_ref, kc_ref, vp_ref, vc_ref, o_ref, *, tq):
    g = pl.program_id(0)
    qi = pl.program_id(1)
    k = jnp.concatenate([kp_ref[...], kc_ref[...]], axis=0).astype(BF16)
    v = jnp.concatenate([vp_ref[...], vc_ref[...]], axis=0).astype(BF16)
    krel = _iota((1, 2 * tq), 1) - tq
    dist = _iota((tq, 1), 0) - krel
    mask = (dist >= 0) & (dist < WINDOW) & ((krel >= 0) | (qi > 0))
    distf = dist.astype(F32)
    for r in range(GROUP):
        slope = jnp.where(g == 0, SLOPES[r], SLOPES[GROUP + r])
        qh = q_ref[:, r * HEAD_DIM:(r + 1) * HEAD_DIM].astype(BF16)
        s = jnp.where(mask, _dot_nt(qh, k) * SCALE - slope * distf, NEG)
        m = jnp.max(s, axis=-1, keepdims=True)
        e = jnp.where(mask, jnp.exp(s - m), 0.0)
        p = e * (1.0 / jnp.maximum(jnp.sum(e, axis=-1, keepdims=True), 1e-30))
        o_ref[:, r * HEAD_DIM:(r + 1) * HEAD_DIM] = _dot(p.astype(BF16), v)


def _window_prompt(z, tq=WINDOW):
    t = z.shape[0]
    gw = GROUP * HEAD_DIM
    prev = lambda c: pl.BlockSpec((tq, HEAD_DIM), lambda g, qi: (jnp.maximum(qi - 1, 0), c // HEAD_DIM + g))
    cur = lambda c: pl.BlockSpec((tq, HEAD_DIM), lambda g, qi: (qi, c // HEAD_DIM + g))
    return pl.pallas_call(
        functools.partial(_window_prompt_kernel, tq=tq),
        out_shape=jax.ShapeDtypeStruct((t, N_HEADS * HEAD_DIM), F32),
        grid=(N_KV, t // tq),
        in_specs=[pl.BlockSpec((tq, gw), lambda g, qi: (qi, COL_Q // gw + g)),
                  prev(COL_KW), cur(COL_KW), prev(COL_VW), cur(COL_VW)],
        out_specs=pl.BlockSpec((tq, gw), lambda g, qi: (qi, g)),
        compiler_params=pltpu.CompilerParams(
            dimension_semantics=("arbitrary", "arbitrary"), vmem_limit_bytes=VMEM_LIMIT),
        name="window_prompt",
    )(z, z, z, z, z)


SEL_PAD = 384


def _slope_col(g, ds):
    return jnp.concatenate([jnp.full((ds, 1), SLOPES[g * GROUP + r], F32) for r in range(GROUP)], axis=0)


def _stack_heads(q_ref, g):
    return jnp.concatenate([q_ref[:, (g * GROUP + r) * HEAD_DIM:(g * GROUP + r + 1) * HEAD_DIM]
                            for r in range(GROUP)], axis=0).astype(BF16)


def _pad_rows(a, n):
    return jnp.concatenate([a, jnp.zeros((n - a.shape[0], a.shape[1]), a.dtype)], axis=0)


def _sample_small_kernel(q_ref, kc_ref, vc_ref, m_ref, sk_ref, sv_ref, nk_ref, nv_ref,
                         oc_ref, imp_ref, ow_ref, ko_ref, vo_ref, *, ds, nc, past, wb):
    rows = GROUP * ds
    qidx = _iota((rows, 1), 0) & (ds - 1)
    spos = past + qidx
    end = _iota((1, nc), 1) * CMP_STRIDE + (CMP_BLOCK - 1)
    dist_c = spos - end
    mask_c = dist_c >= 0
    ist = _iota((1, wb), 1)
    dist_s = wb + qidx - ist
    mask_s = dist_s < WINDOW
    jn = _iota((1, 128), 1)
    dist_n = qidx - jn
    mask_n = (dist_n >= 0) & (jn < ds)
    for g in range(N_KV):
        lanes = slice(g * HEAD_DIM, (g + 1) * HEAD_DIM)
        qs = _stack_heads(q_ref, g)
        slope = _slope_col(g, ds)
        s = _dot_nt(qs, kc_ref[:, lanes].astype(BF16)) * SCALE - slope * dist_c.astype(F32)
        s = jnp.where(mask_c, s, NEG)
        m = jnp.max(s, axis=-1, keepdims=True)
        e = jnp.where(mask_c, jnp.exp(s - m), 0.0)
        p = e * (1.0 / jnp.maximum(jnp.sum(e, axis=-1, keepdims=True), 1e-30))
        o = _dot(p.astype(BF16), vc_ref[:, lanes].astype(BF16))
        psum = p[0:ds]
        for r in range(1, GROUP):
            psum = psum + p[r * ds:(r + 1) * ds]
        imp_ref[:, g * SEL_PAD:(g + 1) * SEL_PAD] = jnp.dot(psum, m_ref[...], precision=lax.Precision.HIGHEST,
                                                            preferred_element_type=F32)
        kst = sk_ref[pl.ds(g, wb, stride=N_KV), :].astype(BF16)
        vst = sv_ref[pl.ds(g, wb, stride=N_KV), :].astype(BF16)
        kn = _pad_rows(nk_ref[pl.ds(g, ds, stride=N_KV), :], 128).astype(BF16)
        vn = _pad_rows(nv_ref[pl.ds(g, ds, stride=N_KV), :], 128).astype(BF16)
        s1 = jnp.where(mask_s, _dot_nt(qs, kst) * SCALE - slope * dist_s.astype(F32), NEG)
        s2 = jnp.where(mask_n, _dot_nt(qs, kn) * SCALE - slope * dist_n.astype(F32), NEG)
        mw = jnp.maximum(jnp.max(s1, axis=-1, keepdims=True), jnp.max(s2, axis=-1, keepdims=True))
        e1 = jnp.where(mask_s, jnp.exp(s1 - mw), 0.0)
        e2 = jnp.where(mask_n, jnp.exp(s2 - mw), 0.0)
        inv = 1.0 / jnp.maximum(jnp.sum(e1, axis=-1, keepdims=True) + jnp.sum(e2, axis=-1, keepdims=True), 1e-30)
        w = _dot((e1 * inv).astype(BF16), vst) + _dot((e2 * inv).astype(BF16), vn)
        for r in range(GROUP):
            h = g * GROUP + r
            oc_ref[:, h * HEAD_DIM:(h + 1) * HEAD_DIM] = o[r * ds:(r + 1) * ds]
            ow_ref[:, h * HEAD_DIM:(h + 1) * HEAD_DIM] = w[r * ds:(r + 1) * ds]
    keep = (wb - ds) * N_KV
    for s_ref, n_ref, o_ref in ((sk_ref, nk_ref, ko_ref), (sv_ref, nv_ref, vo_ref)):
        o_ref[0:keep, :] = s_ref[ds * N_KV:wb * N_KV, :]
        o_ref[keep:wb * N_KV, :] = n_ref[...]


def _sample_small(z2, kw2, vw2, kcmp2, vcmp2, imp_m2, state_k, state_v, nb, past):
    t = z2.shape[0]
    ds = t // nb
    nc = kcmp2.shape[0] // nb
    wb = state_k.shape[0] // (nb * N_KV)
    qw = N_HEADS * HEAD_DIM
    kvw = N_KV * HEAD_DIM
    rows = lambda n: pl.BlockSpec((n * N_KV, HEAD_DIM), lambda b: (b, 0))
    return pl.pallas_call(
        functools.partial(_sample_small_kernel, ds=ds, nc=nc, past=past, wb=wb),
        out_shape=(jax.ShapeDtypeStruct((t, qw), F32), jax.ShapeDtypeStruct((t, N_KV * SEL_PAD), F32),
                   jax.ShapeDtypeStruct((t, qw), F32),
                   jax.ShapeDtypeStruct(state_k.shape, F32), jax.ShapeDtypeStruct(state_v.shape, F32)),
        grid=(nb,),
        in_specs=[pl.BlockSpec((ds, qw), lambda b: (b, COL_Q // qw)),
                  pl.BlockSpec((nc, kvw), lambda b: (b, 0)), pl.BlockSpec((nc, kvw), lambda b: (b, 0)),
                  pl.BlockSpec(imp_m2.shape, lambda b: (0, 0)),
                  rows(wb), rows(wb), rows(ds), rows(ds)],
        out_specs=(pl.BlockSpec((ds, qw), lambda b: (b, 0)), pl.BlockSpec((ds, N_KV * SEL_PAD), lambda b: (b, 0)),
                   pl.BlockSpec((ds, qw), lambda b: (b, 0)), rows(wb), rows(wb)),
        compiler_params=pltpu.CompilerParams(dimension_semantics=("arbitrary",), vmem_limit_bytes=VMEM_LIMIT),
        name="sample_small",
    )(z2, kcmp2, vcmp2, imp_m2, state_k, state_v, kw2, vw2)


SLC_PAGES = 32
SLC_KEYS = SLC_PAGES * PAGE_SIZE
SLC_BLOCKS = SLC_KEYS // SEL_BLOCK
WIN_STEPS = 128 // SLC_BLOCKS


def _expand_matrix():
    j = np.arange(128)[:, None]
    c = np.arange(SLC_KEYS)[None, :]
    e = np.concatenate([(j == w * SLC_BLOCKS + c // SEL_BLOCK) for w in range(WIN_STEPS)], axis=0)
    return jnp.asarray(e, dtype=BF16)


def _selected_sample_kernel(pt_ref, ck_hbm, cv_hbm, q_ref, sel0_ref, sel1_ref, e_ref, nk_ref, nv_ref, o_ref,
                            kbuf, vbuf, sem, m_sc, l_sc, acc_sc, *, ds, steps, past):
    slot = _gather_pipeline(pt_ref, (ck_hbm, cv_hbm), (kbuf, vbuf), sem, SLC_PAGES, PAGE_ROWS)
    kt = pl.program_id(0) % steps
    rows = GROUP * ds
    nk = SLC_KEYS
    qidx = _iota((rows, 1), 0) & (ds - 1)

    @pl.when(kt == 0)
    def _():
        m_sc[...] = jnp.full(m_sc.shape, M_INIT, F32)
        l_sc[...] = jnp.zeros(l_sc.shape, F32)
        acc_sc[...] = jnp.zeros(acc_sc.shape, F32)

    krel = (kt * nk - past + _iota((1, nk), 1)).astype(F32)
    e = e_ref[pl.ds(pl.multiple_of((kt % WIN_STEPS) * 128, 128), 128), :]
    for g, sel_ref in enumerate((sel0_ref, sel1_ref)):
        grows = slice(g * rows, (g + 1) * rows)
        qs = _stack_heads(q_ref, g)
        slope = _slope_col(g, ds)
        selexp = _dot(sel_ref[...].astype(BF16), e)
        mb = jnp.where(jnp.concatenate([selexp] * GROUP, axis=0) > 0.5, 0.0, NEG)
        kg = kbuf[slot, pl.ds(g, nk, stride=N_KV), :].astype(BF16)
        vg = vbuf[slot, pl.ds(g, nk, stride=N_KV), :].astype(BF16)
        s = _dot_nt(qs, kg) * SCALE + (mb + slope * krel)
        _flash_update(s, vg, m_sc, l_sc, acc_sc, grows)

    @pl.when(kt == steps - 1)
    def _():
        jn = _iota((1, 128), 1)
        mb_n = jnp.where((jn <= qidx) & (jn < ds), 0.0, NEG)
        for g in range(N_KV):
            grows = slice(g * rows, (g + 1) * rows)
            qs = _stack_heads(q_ref, g)
            kn = _pad_rows(nk_ref[pl.ds(g, ds, stride=N_KV), :], 128).astype(BF16)
            vn = _pad_rows(nv_ref[pl.ds(g, ds, stride=N_KV), :], 128).astype(BF16)
            s = _dot_nt(qs, kn) * SCALE + (mb_n + _slope_col(g, ds) * jn.astype(F32))
            _flash_update(s, vn, m_sc, l_sc, acc_sc, grows)
            o = acc_sc[grows, :] * (1.0 / l_sc[grows, :])
            for r in range(GROUP):
                h = g * GROUP + r
                o_ref[:, h * HEAD_DIM:(h + 1) * HEAD_DIM] = o[r * ds:(r + 1) * ds]


def _selected_sample(page_table, cache_k, cache_v, z2, ks2, vs2, sel2, nb, past):
    t = z2.shape[0]
    ds = t // nb
    n_phys = cache_k.shape[0]
    ck = cache_k.reshape(n_phys, PAGE_ROWS, HEAD_DIM)
    cv = cache_v.reshape(n_phys, PAGE_ROWS, HEAD_DIM)
    steps = page_table.shape[1] // SLC_PAGES
    qw = N_HEADS * HEAD_DIM
    rows = N_KV * GROUP * ds
    e = _expand_matrix()
    win = lambda g: pl.BlockSpec(
        (ds, 128), lambda s, pt: (s // steps, g * (SEL_PAD // 128) + (s % steps) // WIN_STEPS))
    new = pl.BlockSpec((ds * N_KV, HEAD_DIM), lambda s, pt: (s // steps, 0))
    return pl.pallas_call(
        functools.partial(_selected_sample_kernel, ds=ds, steps=steps, past=past),
        out_shape=jax.ShapeDtypeStruct((t, qw), F32),
        grid_spec=pltpu.PrefetchScalarGridSpec(
            num_scalar_prefetch=1,
            grid=(nb * steps,),
            in_specs=[pl.BlockSpec(memory_space=pl.ANY), pl.BlockSpec(memory_space=pl.ANY),
                      pl.BlockSpec((ds, qw), lambda s, pt: (s // steps, COL_Q // qw)),
                      win(0), win(1), pl.BlockSpec(e.shape, lambda s, pt: (0, 0)), new, new],
            out_specs=pl.BlockSpec((ds, qw), lambda s, pt: (s // steps, 0)),
            scratch_shapes=[pltpu.VMEM((2, SLC_PAGES * PAGE_ROWS, HEAD_DIM), F32),
                            pltpu.VMEM((2, SLC_PAGES * PAGE_ROWS, HEAD_DIM), F32),
                            pltpu.SemaphoreType.DMA((2, 2)),
                            pltpu.VMEM((rows, 1), F32), pltpu.VMEM((rows, 1), F32),
                            pltpu.VMEM((rows, HEAD_DIM), F32)],
        ),
        compiler_params=pltpu.CompilerParams(dimension_semantics=("arbitrary",), vmem_limit_bytes=VMEM_LIMIT),
        name="selected_sample",
    )(page_table.reshape(-1), ck, cv, z2, sel2, sel2, e, ks2, vs2)


def _merge_kernel(x_ref, conv_ref, oc_ref, os_ref, ow_ref, gt_ref, zn_ref, w_ref, y_ref):
    gt = gt_ref[...]
    parts = [conv_ref[...].astype(BF16)]
    for h in range(N_HEADS):
        lanes = slice(h * HEAD_DIM, (h + 1) * HEAD_DIM)
        o = (gt[:, 3 * h:3 * h + 1] * oc_ref[:, lanes] + gt[:, 3 * h + 1:3 * h + 2] * os_ref[:, lanes]
             + gt[:, 3 * h + 2:3 * h + 3] * ow_ref[:, lanes])
        parts.append((o * _silu(zn_ref[:, lanes])).astype(BF16))
    y_ref[...] = x_ref[...] + _dot(jnp.concatenate(parts, axis=1), w_ref[...])


def _merge(x, conv_o, o_c, o_s, o_w, z, w_out_bf, tm=256):
    t, d = x.shape
    qw = N_HEADS * HEAD_DIM
    rowblk = lambda w: pl.BlockSpec((tm, w), lambda i: (i, 0))
    return pl.pallas_call(
        _merge_kernel,
        out_shape=jax.ShapeDtypeStruct((t, d), F32),
        grid=(t // tm,),
        in_specs=[rowblk(d), rowblk(C_CONV), rowblk(qw), rowblk(qw), rowblk(qw),
                  pl.BlockSpec((tm, 128), lambda i: (i, COL_GT // 128)),
                  pl.BlockSpec((tm, qw), lambda i: (i, COL_ZN // qw)),
                  pl.BlockSpec(w_out_bf.shape, lambda i: (0, 0))],
        out_specs=rowblk(d),
        compiler_params=pltpu.CompilerParams(dimension_semantics=("arbitrary",), vmem_limit_bytes=VMEM_LIMIT),
        name="merge",
    )(x, conv_o, o_c, o_s, o_w, z, z, w_out_bf)


N_GT = 3 * N_HEADS


def _pad_w_kernel(wt_hbm, o_ref, buf, sem):
    j = pl.program_id(0)
    slot = j % 2

    def tile_copy(jj, sl):
        src = jnp.where(jj <= TILE_GT, jj * TN, jj * TN - COL_ZN + COL_GT + N_GT)
        return pltpu.make_async_copy(wt_hbm.at[pl.ds(pl.multiple_of(src, 8), TN)], buf.at[sl], sem.at[sl])

    @pl.when(j == 0)
    def _():
        tile_copy(0, 0).start()

    @pl.when(j + 1 < pl.num_programs(0))
    def _():
        tile_copy(j + 1, 1 - slot).start()

    tile_copy(j, slot).wait()
    w = buf[slot]
    keep = (j != TILE_GT) | (_iota((TN, 1), 0) < N_GT)
    o_ref[...] = jnp.where(keep, w, 0.0).T.astype(BF16)


def _pad_w_in(w_in):
    d, n = w_in.shape
    assert n == COL_GT + N_GT + ZW - COL_ZN and COL_ZN == (TILE_GT + 1) * TN
    return pl.pallas_call(
        _pad_w_kernel,
        out_shape=jax.ShapeDtypeStruct((d, ZW), BF16),
        grid=(N_TILES,),
        in_specs=[pl.BlockSpec(memory_space=pl.ANY)],
        out_specs=pl.BlockSpec((d, TN), lambda j: (0, j)),
        scratch_shapes=[pltpu.VMEM((2, TN, d), F32), pltpu.SemaphoreType.DMA((2,))],
        compiler_params=pltpu.CompilerParams(dimension_semantics=("arbitrary",), vmem_limit_bytes=VMEM_LIMIT),
        name="pad_w_in",
    )(w_in.T)


def _cmp_weights(pe, w1, w2):
    half = CMP_STRIDE
    k = half * HEAD_DIM
    w1cat = jnp.concatenate([w1[:half].reshape(k, -1), w1[half:].reshape(k, -1)], axis=1).astype(BF16)
    pe2 = jnp.concatenate([pe[:half].reshape(1, k), pe[half:].reshape(1, k), jnp.zeros((6, k), pe.dtype)], axis=0)
    return w1cat, pe2, w2.astype(BF16)


def kernel(x_prompt, x_sample, cache_k_cmp, cache_v_cmp, cache_k_slc, cache_v_slc, state_k_win, state_v_win,
           state_conv, page_table, g_norm, w_in, pe_cmp_k, w_cmp_k1, w_cmp_k2, pe_cmp_v, w_cmp_v1, w_cmp_v2,
           g_q, g_k_cmp, g_k_slc, g_k_win, w_dw, b_dw, ln_g, ln_b, w_pw2, b_pw2, w_out):
    _, t, d = x_prompt.shape
    db, ds, _ = x_sample.shape
    past = page_table.shape[1] * PAGE_SIZE
    wb = state_k_win.shape[1]
    kvw = N_KV * HEAD_DIM
    assert ds < CMP_STRIDE and wb == WINDOW and t % SEL_BLOCK == 0 and t // SEL_BLOCK == 128

    w_p = _pad_w_in(w_in)
    w_pw_bf = w_pw2.astype(BF16)
    w_out_bf = w_out.astype(BF16)
    wk1, pek, wk2 = _cmp_weights(pe_cmp_k, w_cmp_k1, w_cmp_k2)
    wv1, pev, wv2 = _cmp_weights(pe_cmp_v, w_cmp_v1, w_cmp_v2)

    xp = x_prompt.reshape(t, d)
    z, kc, vc, ks, vs, kw, vw = _in_proj(xp, g_norm, w_p, g_q, g_k_slc, g_k_win, tm=1024)
    conv_o, conv_st = _conv_prompt(z, w_dw, b_dw, ln_g, ln_b, w_pw_bf, b_pw2)
    nc = t // CMP_STRIDE
    fk, fv = _feats_prompt(kc, vc, wk1, wv1)
    kcmp, vcmp = _cmp_finish(fk, fv, (wk1, pek, wk2), g_k_cmp, (wv1, pev, wv2), nc)
    imp_m = _imp_matrix(nc, t // SEL_BLOCK, nc - 1, t // SEL_BLOCK)
    o_c, imp = _cmp_select(z, kcmp, vcmp, imp_m, tq=256)
    cur = (jnp.arange(t, dtype=jnp.int32) // SEL_BLOCK).reshape(1, t)
    sel, anyb = _select(imp, cur, 128, tr=512, any_rows=256)
    o_s = _selected_prompt(z, sel, anyb, tq=256)
    o_w = _window_prompt(z)
    y_prompt = _merge(xp, conv_o, o_c, o_s, o_w, z, w_out_bf)

    xs = x_sample.reshape(db * ds, d)
    z2, kc2, vc2, ks2, vs2, kw2, vw2 = _in_proj(xs, g_norm, w_p, g_q, g_k_slc, g_k_win, tm=db * ds)
    conv_o2, u2 = _conv_sample(z2, state_conv, w_dw, b_dw, ln_g, ln_b, w_pw_bf, b_pw2)
    nc2 = past // CMP_STRIDE
    kcmp2, vcmp2 = _cmp_sample(page_table, cache_k_cmp, cache_v_cmp, (wk1, pek, wk2), g_k_cmp, (wv1, pev, wv2))
    n_sel2 = -(-(past + ds) // SEL_BLOCK)
    assert n_sel2 <= SEL_PAD
    imp_m2 = _imp_matrix(nc2, SEL_PAD, nc2 - 1, n_sel2)
    rows = lambda a: a.reshape(-1, HEAD_DIM)
    o_c2, imp2, o_w2, k_win, v_win = _sample_small(z2, kw2, vw2, kcmp2, vcmp2, imp_m2, rows(state_k_win),
                                                   rows(state_v_win), db, past)
    cur2 = ((past + jnp.arange(db * ds, dtype=jnp.int32) % ds) // SEL_BLOCK).reshape(1, db * ds)
    sel2, = _select(imp2, cur2, SEL_PAD, tr=db * ds)
    o_s2 = _selected_sample(page_table, cache_k_slc, cache_v_slc, z2, ks2, vs2, sel2, db, past)
    y_sample = _merge(xs, conv_o2, o_c2, o_s2, o_w2, z2, w_out_bf)

    kv4 = lambda a, b: a.reshape(b, -1, N_KV, HEAD_DIM)
    return (y_prompt.reshape(1, t, d), y_sample.reshape(db, ds, d),
            kv4(kc, 1), kv4(vc, 1), kv4(ks, 1), kv4(vs, 1), kv4(kw, 1)[:, t - wb:], kv4(vw, 1)[:, t - wb:],
            conv_st[HALO - (CONV_WIDTH - 1):][None],
            kv4(kc2, db), kv4(vc2, db), kv4(ks2, db), kv4(vs2, db), kv4(k_win, db), kv4(v_win, db),
            jnp.concatenate([state_conv[:, ds:], u2.reshape(db, ds, C_CONV)], axis=1))
```

```python
import functools

import numpy as np
import jax
import jax.numpy as jnp
from jax import lax
from jax.experimental import pallas as pl
from jax.experimental.pallas import tpu as pltpu

F32 = jnp.float32
BF16 = jnp.bfloat16

HEAD_DIM = 128
N_HEADS = 8
N_KV = 2
GROUP = 4
C_CONV = 1024
CONV_WIDTH = 31
CMP_STRIDE = 16
CMP_BLOCK = 32
SEL_BLOCK = 64
N_SELECT = 16
WINDOW = 512
PAGE_SIZE = 128
EPS = 1e-6
SCALE = HEAD_DIM ** -0.5
LOG2E = 1.4426950408889634
SLOPES = tuple(2.0 ** -(h + 1) for h in range(N_HEADS))

TN = 512
N_TILES = 14
ZW = TN * N_TILES
COL_UA, COL_UB, COL_ZC, COL_Q = 0, 1024, 2048, 3072
COL_KC, COL_VC, COL_KS, COL_VS, COL_KW, COL_VW = 4096, 4352, 4608, 4864, 5120, 5376
COL_GT, COL_ZN = 5632, 6144
TILE_Q0, TILE_Q1, TILE_CMP, TILE_SLC, TILE_WIN, TILE_GT = 6, 7, 8, 9, 10, 11

NEG = -1e30
M_INIT = -1e29
VMEM_LIMIT = 48 * 1024 * 1024
INPROJ_VMEM_LIMIT = 56 * 1024 * 1024


def _sigmoid(x):
    return 1.0 / (1.0 + jnp.exp(-x))


def _silu(x):
    return x * _sigmoid(x)


def _dot(a, b):
    return jnp.dot(a, b, preferred_element_type=F32)


def _dot_nt(a, b):
    return lax.dot_general(a, b, (((1,), (1,)), ((), ())), preferred_element_type=F32)


def _rms(a, g):
    return a * lax.rsqrt(jnp.mean(a * a, axis=-1, keepdims=True) + EPS) * g


def _iota(shape, dim):
    return lax.broadcasted_iota(jnp.int32, shape, dim)


def _inproj_kernel(x_ref, gn_ref, w_ref, gq_ref, gks_ref, gkw_ref,
                   z_ref, kc_o, vc_o, ks_o, vs_o, kw_o, vw_o, xn_ref, *, tm):
    j = pl.program_id(1)

    @pl.when(j == 0)
    def _():
        x = x_ref[...]
        ms = jnp.mean(x * x, axis=-1, keepdims=True)
        xn_ref[...] = (x * lax.rsqrt(ms + EPS) * gn_ref[...]).astype(BF16)

    z_ref[...] = _dot(xn_ref[...], w_ref[...])

    def kv_tile(g_ref, k_o, v_o):
        for c in range(TN // HEAD_DIM):
            lanes = slice(c * HEAD_DIM, (c + 1) * HEAD_DIM)
            a = z_ref[:, lanes]
            if c < N_KV and g_ref is not None:
                a = _rms(a, g_ref[...])
                z_ref[:, lanes] = a
            (k_o if c < N_KV else v_o)[pl.ds(c % N_KV, tm, stride=N_KV), :] = a

    @pl.when((j == TILE_Q0) | (j == TILE_Q1))
    def _():
        for c in range(TN // HEAD_DIM):
            lanes = slice(c * HEAD_DIM, (c + 1) * HEAD_DIM)
            z_ref[:, lanes] = _rms(z_ref[:, lanes], gq_ref[...])

    @pl.when(j == TILE_CMP)
    def _():
        kv_tile(None, kc_o, vc_o)

    @pl.when(j == TILE_SLC)
    def _():
        kv_tile(gks_ref, ks_o, vs_o)

    @pl.when(j == TILE_WIN)
    def _():
        kv_tile(gkw_ref, kw_o, vw_o)

    @pl.when(j == TILE_GT)
    def _():
        z_ref[...] = _sigmoid(z_ref[...])


def _in_proj(x, g_norm, w_p, g_q, g_ks, g_kw, tm):
    t, d = x.shape
    row = lambda a: a.reshape(1, -1)
    kv_shape = jax.ShapeDtypeStruct((t * N_KV, HEAD_DIM), F32)
    kv_spec = pl.BlockSpec((tm * N_KV, HEAD_DIM), lambda i, j: (i, 0))
    return pl.pallas_call(
        functools.partial(_inproj_kernel, tm=tm),
        out_shape=(jax.ShapeDtypeStruct((t, ZW), F32),) + (kv_shape,) * 6,
        grid=(t // tm, N_TILES),
        in_specs=[
            pl.BlockSpec((tm, d), lambda i, j: (i, 0)),
            pl.BlockSpec((1, d), lambda i, j: (0, 0)),
            pl.BlockSpec((d, TN), lambda i, j: (0, j)),
            pl.BlockSpec((1, HEAD_DIM), lambda i, j: (0, 0)),
            pl.BlockSpec((1, HEAD_DIM), lambda i, j: (0, 0)),
            pl.BlockSpec((1, HEAD_DIM), lambda i, j: (0, 0)),
        ],
        out_specs=(pl.BlockSpec((tm, TN), lambda i, j: (i, j)),) + (kv_spec,) * 6,
        scratch_shapes=[pltpu.VMEM((tm, d), BF16)],
        compiler_params=pltpu.CompilerParams(
            dimension_semantics=("arbitrary", "arbitrary"), vmem_limit_bytes=INPROJ_VMEM_LIMIT),
        name="in_proj",
    )(x, row(g_norm), w_p, row(g_q), row(g_ks), row(g_kw))


HALO = 32
CONV_RB = 64
CONV_CB = 128


def _conv_tail(y, zc, lng_ref, lnb_ref, wpw_ref, bpw_ref):
    mu = jnp.mean(y, axis=-1, keepdims=True)
    yc = y - mu
    var = jnp.mean(yc * yc, axis=-1, keepdims=True)
    yn = yc * lax.rsqrt(var + EPS) * lng_ref[...] + lnb_ref[...]
    act = _silu(yn).astype(BF16)
    return (_dot(act, wpw_ref[...]) + bpw_ref[...]) * _silu(zc)


def _conv_prompt_kernel(ua_ref, ub_ref, zc_ref, uah_ref, ubh_ref, wdw_ref, bdw_ref, lng_ref, lnb_ref,
                        wpw_ref, bpw_ref, o_ref, st_ref, buf, ybuf, *, tt):
    i = pl.program_id(0)
    uh = uah_ref[...] * _sigmoid(ubh_ref[...])
    buf[0:HALO, :] = jnp.where(i > 0, uh, 0.0)
    buf[HALO:HALO + tt, :] = ua_ref[...] * _sigmoid(ub_ref[...])
    off = HALO - (CONV_WIDTH - 1)
    for c0 in range(0, C_CONV, CONV_CB):
        lanes = slice(c0, c0 + CONV_CB)
        for r0 in range(0, tt, CONV_RB):
            acc = jnp.broadcast_to(bdw_ref[:, lanes], (CONV_RB, CONV_CB))
            for b in range(8):
                n = CONV_RB if b == 0 else CONV_RB + 8
                zb = None
                for a in range(-(-(off - b) // 8), (off + CONV_WIDTH - 1 - b) // 8 + 1):
                    k = 8 * a + b - off
                    term = wdw_ref[k:k + 1, lanes] * buf[r0 + 8 * a:r0 + 8 * a + n, lanes]
                    zb = term if zb is None else zb + term
                acc = acc + zb[b:b + CONV_RB]
            ybuf[r0:r0 + CONV_RB, lanes] = acc
    o_ref[...] = _conv_tail(ybuf[...], zc_ref[...], lng_ref, lnb_ref, wpw_ref, bpw_ref)

    @pl.when(i == pl.num_programs(0) - 1)
    def _():
        st_ref[...] = buf[tt:tt + HALO, :]


def _conv_prompt(z, w_dw, b_dw, ln_g, ln_b, w_pw_bf, b_pw, tt=256):
    t = z.shape[0]
    row = lambda a: a.reshape(1, -1)
    hb = tt // HALO
    cur = lambda c: pl.BlockSpec((tt, C_CONV), lambda i: (i, c))
    halo = lambda c: pl.BlockSpec((HALO, C_CONV), lambda i: (jnp.maximum(i * hb - 1, 0), c))
    full = lambda shape: pl.BlockSpec(shape, lambda i: (0, 0))
    return pl.pallas_call(
        functools.partial(_conv_prompt_kernel, tt=tt),
        out_shape=(jax.ShapeDtypeStruct((t, C_CONV), F32), jax.ShapeDtypeStruct((HALO, C_CONV), F32)),
        grid=(t // tt,),
        in_specs=[cur(0), cur(1), cur(2), halo(0), halo(1),
                  full((CONV_WIDTH, C_CONV)), full((1, C_CONV)), full((1, C_CONV)), full((1, C_CONV)),
                  full((C_CONV, C_CONV)), full((1, C_CONV))],
        out_specs=(pl.BlockSpec((tt, C_CONV), lambda i: (i, 0)), pl.BlockSpec((HALO, C_CONV), lambda i: (0, 0))),
        scratch_shapes=[pltpu.VMEM((HALO + tt, C_CONV), F32), pltpu.VMEM((tt, C_CONV), F32)],
        compiler_params=pltpu.CompilerParams(dimension_semantics=("arbitrary",), vmem_limit_bytes=VMEM_LIMIT),
        name="conv_prompt",
    )(z, z, z, z, z, w_dw, row(b_dw), row(ln_g), row(ln_b), w_pw_bf, row(b_pw))


ST_ROWS = 40


def _conv_sample_kernel(ua_ref, ub_ref, zc_ref, st_ref, wdw_ref, bdw_ref, lng_ref, lnb_ref, wpw_ref, bpw_ref,
                        o_ref, u_ref, fbuf, ybuf, *, nb, ds):
    u = ua_ref[...] * _sigmoid(ub_ref[...])
    u_ref[...] = u
    rows = _iota((nb * ds, 1), 0) & (ds - 1)
    acc_u = jnp.broadcast_to(bdw_ref[...], (nb * ds, C_CONV))
    for d in range(ds):
        sh = u if d == 0 else pltpu.roll(u, d, 0)
        acc_u = acc_u + jnp.where(rows >= d, sh, 0.0) * wdw_ref[CONV_WIDTH - 1 - d:CONV_WIDTH - d, :]
    ybuf[...] = acc_u
    fbuf[:, 24:ST_ROWS, :] = jnp.zeros((nb, ST_ROWS - 24, C_CONV), F32)
    fbuf[:, 0:CONV_WIDTH - 1, :] = st_ref[...]

    def body(b, carry):
        acc = jnp.zeros((ds, C_CONV), F32)
        for k in range(CONV_WIDTH - 1):
            acc = acc + wdw_ref[k:k + 1, :] * fbuf[b, k:k + ds, :]
        r = pl.multiple_of(b * ds, ds)
        ybuf[pl.ds(r, ds), :] = ybuf[pl.ds(r, ds), :] + acc
        return carry

    lax.fori_loop(0, nb, body, 0)
    o_ref[...] = _conv_tail(ybuf[...], zc_ref[...], lng_ref, lnb_ref, wpw_ref, bpw_ref)


def _conv_sample(z2, state_conv, w_dw, b_dw, ln_g, ln_b, w_pw_bf, b_pw):
    nb, sw, _ = state_conv.shape
    t = z2.shape[0]
    ds = t // nb
    assert sw == CONV_WIDTH - 1 and ds == 8
    row = lambda a: a.reshape(1, -1)
    col = lambda c: pl.BlockSpec((t, C_CONV), lambda i: (0, c))
    full = lambda shape: pl.BlockSpec(shape, lambda i: (0,) * len(shape))
    return pl.pallas_call(
        functools.partial(_conv_sample_kernel, nb=nb, ds=ds),
        out_shape=(jax.ShapeDtypeStruct((t, C_CONV), F32), jax.ShapeDtypeStruct((t, C_CONV), F32)),
        grid=(1,),
        in_specs=[col(0), col(1), col(2), full((nb, sw, C_CONV)),
                  full((CONV_WIDTH, C_CONV)), full((1, C_CONV)), full((1, C_CONV)), full((1, C_CONV)),
                  full((C_CONV, C_CONV)), full((1, C_CONV))],
        out_specs=(full((t, C_CONV)), full((t, C_CONV))),
        scratch_shapes=[pltpu.VMEM((nb, ST_ROWS, C_CONV), F32), pltpu.VMEM((t, C_CONV), F32)],
        compiler_params=pltpu.CompilerParams(dimension_semantics=("arbitrary",), vmem_limit_bytes=VMEM_LIMIT),
        name="conv_sample",
    )(z2, z2, z2, state_conv, w_dw, row(b_dw), row(ln_g), row(ln_b), w_pw_bf, row(b_pw))


FEAT_W = 2 * N_KV * HEAD_DIM


N_FEAT = FEAT_W // HEAD_DIM


def _chunk_feats(piece, w):
    outs = []
    for g in range(N_KV):
        xg = jnp.concatenate([piece(c, g).astype(BF16) for c in range(CMP_STRIDE)], axis=1)
        outs.append(_dot(xg, w))
    return jnp.concatenate(outs, axis=1)


CHUNK_ROWS = CMP_STRIDE * N_KV


def _feats_kernel(xk_ref, xv_ref, wk_ref, wv_ref, fk_ref, fv_ref, *, tm):
    for x_ref, w_ref, f_ref in ((xk_ref, wk_ref, fk_ref), (xv_ref, wv_ref, fv_ref)):
        f = _chunk_feats(lambda c, g: x_ref[pl.ds(c * N_KV + g, tm, stride=CHUNK_ROWS), :], w_ref[...])
        for cb in range(N_FEAT):
            f_ref[cb] = f[:, cb * HEAD_DIM:(cb + 1) * HEAD_DIM]


def _feats_prompt(xk, xv, wk, wv, tm=128):
    nc = xk.shape[0] // CHUNK_ROWS
    full = lambda shape: pl.BlockSpec(shape, lambda i: (0, 0))
    rows = pl.BlockSpec((tm * CHUNK_ROWS, HEAD_DIM), lambda i: (i, 0))
    ospec = pl.BlockSpec((N_FEAT, tm, HEAD_DIM), lambda i: (0, i, 0))
    return pl.pallas_call(
        functools.partial(_feats_kernel, tm=tm),
        out_shape=(jax.ShapeDtypeStruct((N_FEAT, nc, HEAD_DIM), F32),) * 2,
        grid=(nc // tm,),
        in_specs=[rows, rows, full(wk.shape), full(wv.shape)],
        out_specs=(ospec, ospec),
        compiler_params=pltpu.CompilerParams(dimension_semantics=("arbitrary",), vmem_limit_bytes=VMEM_LIMIT),
        name="feats_prompt",
    )(xk, xv, wk, wv)


FEAT_PAGES = 32
CHUNKS_PER_PAGE = PAGE_SIZE // CMP_STRIDE
PAGE_ROWS = PAGE_SIZE * N_KV
FEAT_PITCH = PAGE_ROWS + 8


def _page_copies(pt_ref, srcs, bufs, sem, step, slot, pages, pitch, look):
    out = []
    last = pt_ref.shape[0] - 1
    for p in range(pages + look):
        page = pt_ref[jnp.minimum(step * pages + p, last)] if look else pt_ref[step * pages + p]
        for n, (src, buf) in enumerate(zip(srcs, bufs)):
            out.append(pltpu.make_async_copy(src.at[page], buf.at[slot, pl.ds(p * pitch, PAGE_ROWS)], sem.at[n, slot]))
    return out


def _gather_pipeline(pt_ref, srcs, bufs, sem, pages, pitch, look=0):
    s = pl.program_id(0)
    slot = s % 2

    @pl.when(s == 0)
    def _():
        for c in _page_copies(pt_ref, srcs, bufs, sem, 0, 0, pages, pitch, look):
            c.start()

    @pl.when(s + 1 < pl.num_programs(0))
    def _():
        for c in _page_copies(pt_ref, srcs, bufs, sem, s + 1, 1 - slot, pages, pitch, look):
            c.start()

    for c in _page_copies(pt_ref, srcs, bufs, sem, s, slot, pages, pitch, look):
        c.wait()
    return slot


FEAT_CHUNKS = FEAT_PAGES * CHUNKS_PER_PAGE


def _cmp_sample_kernel(pt_ref, ck_hbm, cv_hbm, wk1_ref, pek_ref, wk2_ref, gk_ref, wv1_ref, pev_ref, wv2_ref,
                       ok_ref, ov_ref, kbuf, vbuf, sem, fsc, bias_sc, *, steps_per_seq):
    slot = _gather_pipeline(pt_ref, (ck_hbm, cv_hbm), (kbuf, vbuf), sem, FEAT_PAGES, FEAT_PITCH, look=1)
    last_of_seq = (pl.program_id(0) % steps_per_seq) == steps_per_seq - 1
    row = _iota((FEAT_CHUNKS, 1), 0)
    keep = jnp.logical_not(last_of_seq & (row == FEAT_CHUNKS - 1))

    def piece(buf):
        def get(c, g):
            parts = [buf[slot, pl.ds(n * CHUNK_ROWS + c * N_KV + g, FEAT_PAGES, stride=FEAT_PITCH), :]
                     for n in range(CHUNKS_PER_PAGE)]
            parts.append(buf[slot, pl.ds(FEAT_PAGES * FEAT_PITCH + c * N_KV + g, 8, stride=8), :])
            return jnp.concatenate(parts, axis=0)
        return get

    @pl.when(pl.program_id(0) == 0)
    def _():
        for n, (pe_ref, w1_ref) in enumerate(((pek_ref, wk1_ref), (pev_ref, wv1_ref))):
            bias_sc[n] = _dot(pe_ref[...].astype(BF16), w1_ref[...])

    for kv, (buf, w1_ref, w2_ref, g_ref, o_ref) in enumerate(((kbuf, wk1_ref, wk2_ref, gk_ref, ok_ref),
                                                              (vbuf, wv1_ref, wv2_ref, None, ov_ref))):
        f = _chunk_feats(piece(buf), w1_ref[...])
        for cb in range(N_FEAT):
            lanes = slice(cb * HEAD_DIM, (cb + 1) * HEAD_DIM)
            for n in range(CHUNKS_PER_PAGE):
                fsc[cb, pl.ds(n, FEAT_PAGES, stride=CHUNKS_PER_PAGE), :] = f[n * FEAT_PAGES:(n + 1) * FEAT_PAGES, lanes]
            fsc[cb, FEAT_CHUNKS:FEAT_CHUNKS + 8, :] = f[FEAT_CHUNKS:FEAT_CHUNKS + 8, lanes]
        ba = bias_sc[kv, 0:1, 0:HEAD_DIM]
        bb = bias_sc[kv, 1:2, HEAD_DIM:2 * HEAD_DIM]
        for g in range(N_KV):
            fa = fsc[2 * g, 0:FEAT_CHUNKS, :] + ba
            fb_next = fsc[2 * g + 1, 1:FEAT_CHUNKS + 1, :] + bb
            o = _dot(_silu(fa + fb_next).astype(BF16), w2_ref[...])
            if g_ref is not None:
                o = _rms(o, g_ref[...])
            o_ref[:, g * HEAD_DIM:(g + 1) * HEAD_DIM] = jnp.where(keep, o, 0.0)


def _cmp_sample(page_table, cache_k, cache_v, k_weights, gain_k, v_weights):
    n_seq, pages_per_seq = page_table.shape
    n_pages = page_table.size
    n_phys = cache_k.shape[0]
    ck = cache_k.reshape(n_phys, PAGE_ROWS, HEAD_DIM)
    cv = cache_v.reshape(n_phys, PAGE_ROWS, HEAD_DIM)
    brows = (FEAT_PAGES + 1) * FEAT_PITCH
    full = lambda a: pl.BlockSpec(a.shape, lambda s, pt: (0, 0))
    ospec = pl.BlockSpec((FEAT_CHUNKS, N_KV * HEAD_DIM), lambda s, pt: (s, 0))
    gk = gain_k.reshape(1, -1)
    return pl.pallas_call(
        functools.partial(_cmp_sample_kernel, steps_per_seq=pages_per_seq // FEAT_PAGES),
        out_shape=(jax.ShapeDtypeStruct((n_pages * CHUNKS_PER_PAGE, N_KV * HEAD_DIM), F32),) * 2,
        grid_spec=pltpu.PrefetchScalarGridSpec(
            num_scalar_prefetch=1,
            grid=(n_pages // FEAT_PAGES,),
            in_specs=[pl.BlockSpec(memory_space=pl.ANY), pl.BlockSpec(memory_space=pl.ANY)]
            + [full(a) for a in k_weights] + [full(gk)] + [full(a) for a in v_weights],
            out_specs=(ospec, ospec),
            scratch_shapes=[pltpu.VMEM((2, brows, HEAD_DIM), F32), pltpu.VMEM((2, brows, HEAD_DIM), F32),
                            pltpu.SemaphoreType.DMA((2, 2)),
                            pltpu.VMEM((N_FEAT, FEAT_CHUNKS + 8, HEAD_DIM), F32),
                            pltpu.VMEM((2, 8, 2 * HEAD_DIM), F32)],
        ),
        compiler_params=pltpu.CompilerParams(dimension_semantics=("arbitrary",), vmem_limit_bytes=VMEM_LIMIT),
        name="cmp_sample",
    )(page_table.reshape(-1), ck, cv, *k_weights, gk, *v_weights)


def _cmp_finish_one(f_ref, w1_ref, pe_ref, w2_ref, g_ref, o_ref, nc):
    bias = _dot(pe_ref[...].astype(BF16), w1_ref[...])
    ba = bias[0:1, 0:HEAD_DIM]
    bb = bias[1:2, HEAD_DIM:2 * HEAD_DIM]
    row = _iota((nc, 1), 0)
    for g in range(N_KV):
        fa = f_ref[2 * g] + ba
        fb = f_ref[2 * g + 1] + bb
        hid = _silu(fa + pltpu.roll(fb, nc - 1, 0))
        o = _dot(hid.astype(BF16), w2_ref[...])
        if g_ref is not None:
            o = _rms(o, g_ref[...])
        o_ref[:, g * HEAD_DIM:(g + 1) * HEAD_DIM] = jnp.where(row < nc - 1, o, 0.0)


def _cmp_finish_kernel(fk_ref, fv_ref, wk1_ref, pek_ref, wk2_ref, gk_ref, wv1_ref, pev_ref, wv2_ref,
                       ok_ref, ov_ref, *, nc):
    _cmp_finish_one(fk_ref, wk1_ref, pek_ref, wk2_ref, gk_ref, ok_ref, nc)
    _cmp_finish_one(fv_ref, wv1_ref, pev_ref, wv2_ref, None, ov_ref, nc)


def _cmp_finish(fk, fv, k_weights, gain_k, v_weights, nc):
    nb = fk.shape[1] // nc
    full = lambda a: pl.BlockSpec(a.shape, lambda b: (0, 0))
    fspec = pl.BlockSpec((N_FEAT, nc, HEAD_DIM), lambda b: (0, b, 0))
    ospec = pl.BlockSpec((nc, N_KV * HEAD_DIM), lambda b: (b, 0))
    gk = gain_k.reshape(1, -1)
    return pl.pallas_call(
        functools.partial(_cmp_finish_kernel, nc=nc),
        out_shape=(jax.ShapeDtypeStruct((nb * nc, N_KV * HEAD_DIM), F32),) * 2,
        grid=(nb,),
        in_specs=[fspec, fspec] + [full(a) for a in k_weights] + [full(gk)] + [full(a) for a in v_weights],
        out_specs=(ospec, ospec),
        compiler_params=pltpu.CompilerParams(dimension_semantics=("arbitrary",), vmem_limit_bytes=VMEM_LIMIT),
        name="cmp_finish",
    )(fk, fv, *k_weights, gk, *v_weights)


SELECT_LANES = 128


def _select_kernel(imp_ref, cur_ref, sel_ref, *any_refs, any_rows):
    n_rows, n_blk = imp_ref.shape
    shape = (n_blk, SELECT_LANES)
    j = _iota(shape, 0)
    jf = j.astype(F32)

    def pick(score, sel_t):
        m = jnp.max(score, axis=0, keepdims=True)
        first = jnp.min(jnp.where(score == m, jf, 1e9), axis=0, keepdims=True)
        hit = jf == first
        return jnp.where(hit, -2.0, score), jnp.where(hit, 1.0, sel_t)

    def body(_, carry):
        return pick(*carry[0:2]) + pick(*carry[2:4])

    def start(c):
        rows = slice(c * SELECT_LANES, (c + 1) * SELECT_LANES)
        imp_t = imp_ref[rows, :].T
        cur = cur_ref[:, rows]
        forced = (j == 0) | (j == cur) | (j == cur - 1)
        return jnp.where(forced, 1e30, jnp.where(j <= cur, imp_t, -1.0)), jnp.zeros(shape, F32)

    for c in range(0, n_rows // SELECT_LANES, 2):
        out = lax.fori_loop(0, N_SELECT, body, start(c) + start(c + 1))
        sel_ref[c * SELECT_LANES:(c + 1) * SELECT_LANES, :] = out[1].T
        sel_ref[(c + 1) * SELECT_LANES:(c + 2) * SELECT_LANES, :] = out[3].T
    sel = sel_ref[...]
    if any_refs:
        for n in range(sel.shape[0] // any_rows):
            blk = jnp.max(sel[n * any_rows:(n + 1) * any_rows], axis=0, keepdims=True)
            any_refs[0][n * 8:(n + 1) * 8, :] = jnp.broadcast_to(blk, (8, sel.shape[1]))


def _select(imp, cur, n_lanes, tr, any_rows=None):
    r, w = imp.shape
    groups = w // n_lanes
    out_shape = [jax.ShapeDtypeStruct((r, w), F32)]
    out_specs = [pl.BlockSpec((tr, n_lanes), lambda i, g: (i, g))]
    if any_rows is not None:
        out_shape.append(jax.ShapeDtypeStruct((r // any_rows * 8, w), F32))
        out_specs.append(pl.BlockSpec((tr // any_rows * 8, n_lanes), lambda i, g: (i, g)))
    return pl.pallas_call(
        functools.partial(_select_kernel, any_rows=any_rows),
        out_shape=tuple(out_shape),
        grid=(r // tr, groups),
        in_specs=[pl.BlockSpec((tr, n_lanes), lambda i, g: (i, g)), pl.BlockSpec((1, tr), lambda i, g: (0, i))],
        out_specs=tuple(out_specs),
        compiler_params=pltpu.CompilerParams(
            dimension_semantics=("arbitrary", "arbitrary"), vmem_limit_bytes=VMEM_LIMIT),
        name="select",
    )(imp, cur)


def _imp_matrix(n_cmp_pad, n_sel_pad, n_cmp, n_sel):
    r = SEL_BLOCK // CMP_STRIDE
    lead = CMP_BLOCK // CMP_STRIDE - 1
    m = np.zeros((n_cmp_pad, n_sel_pad), np.float32)
    for jb in range(n_sel):
        for o in range(-lead, r):
            start = o * CMP_STRIDE
            w = (min(start + CMP_BLOCK, SEL_BLOCK) - max(start, 0)) / CMP_BLOCK
            i = r * jb + o
            if 0 <= i < n_cmp:
                m[i, jb] += w
    return jnp.asarray(m)


CMP_KEY_STEP = 128


def _cmp_select_tile(q_ref, kc_ref, vc_ref, m_ref, oc_ref, imp_ref, i, tq, nk):
    qpos = i * tq + _iota((tq, 1), 0)
    end = _iota((1, nk), 1) * CMP_STRIDE + (CMP_BLOCK - 1)
    dist = qpos - end
    mask = dist >= 0
    distf = dist.astype(F32)
    for g in range(N_KV):
        kg = kc_ref[0:nk, g * HEAD_DIM:(g + 1) * HEAD_DIM].astype(BF16)
        vg = vc_ref[0:nk, g * HEAD_DIM:(g + 1) * HEAD_DIM].astype(BF16)
        psum = jnp.zeros((tq, nk), F32)
        for r in range(GROUP):
            h = g * GROUP + r
            qh = q_ref[:, h * HEAD_DIM:(h + 1) * HEAD_DIM].astype(BF16)
            s = _dot_nt(qh, kg) * SCALE - SLOPES[h] * distf
            s = jnp.where(mask, s, NEG)
            m = jnp.max(s, axis=-1, keepdims=True)
            e = jnp.where(mask, jnp.exp(s - m), 0.0)
            p = e * (1.0 / jnp.maximum(jnp.sum(e, axis=-1, keepdims=True), 1e-30))
            psum = psum + p
            oc_ref[:, h * HEAD_DIM:(h + 1) * HEAD_DIM] = _dot(p.astype(BF16), vg)
        imp_ref[:, g * 128:(g + 1) * 128] = jnp.dot(psum, m_ref[0:nk, :], precision=lax.Precision.HIGHEST,
                                                    preferred_element_type=F32)


def _cmp_select_kernel(q_ref, kc_ref, vc_ref, m_ref, oc_ref, imp_ref, *, tq, nc):
    i = pl.program_id(0)
    reach = (i * tq + tq - 1 - (CMP_BLOCK - 1)) // CMP_STRIDE + 1
    for nk in range(CMP_KEY_STEP, nc + 1, CMP_KEY_STEP):
        covers = reach <= nk if nk == CMP_KEY_STEP else (reach > nk - CMP_KEY_STEP) & (reach <= nk)
        if nk == nc:
            covers = reach > nk - CMP_KEY_STEP

        @pl.when(covers)
        def _(nk=nk):
            _cmp_select_tile(q_ref, kc_ref, vc_ref, m_ref, oc_ref, imp_ref, i, tq, nk)


def _cmp_select(z, kcmp, vcmp, imp_m, tq):
    t = z.shape[0]
    nc = kcmp.shape[0]
    n_sel = imp_m.shape[1]
    assert n_sel == 128
    full = lambda shape: pl.BlockSpec(shape, lambda i: (0, 0))
    return pl.pallas_call(
        functools.partial(_cmp_select_kernel, tq=tq, nc=nc),
        out_shape=(jax.ShapeDtypeStruct((t, N_HEADS * HEAD_DIM), F32), jax.ShapeDtypeStruct((t, N_KV * 128), F32)),
        grid=(t // tq,),
        in_specs=[pl.BlockSpec((tq, N_HEADS * HEAD_DIM), lambda i: (i, COL_Q // (N_HEADS * HEAD_DIM))),
                  full(kcmp.shape), full(vcmp.shape), full(imp_m.shape)],
        out_specs=(pl.BlockSpec((tq, N_HEADS * HEAD_DIM), lambda i: (i, 0)),
                   pl.BlockSpec((tq, N_KV * 128), lambda i: (i, 0))),
        compiler_params=pltpu.CompilerParams(dimension_semantics=("arbitrary",), vmem_limit_bytes=VMEM_LIMIT),
        name="cmp_select",
    )(z, kcmp, vcmp, imp_m)


def _flash_update(s, v_bf, m_ref, l_ref, acc_ref, rows):
    m_old = m_ref[rows, :]
    m_new = jnp.maximum(m_old, jnp.max(s, axis=-1, keepdims=True))
    alpha = jnp.exp(m_old - m_new)
    p = jnp.exp(s - m_new)
    l_ref[rows, :] = alpha * l_ref[rows, :] + jnp.sum(p, axis=-1, keepdims=True)
    acc_ref[rows, :] = alpha * acc_ref[rows, :] + _dot(p.astype(BF16), v_bf)
    m_ref[rows, :] = m_new


V_ROWS = 2 * HEAD_DIM


def _selected_prompt_kernel(flag_ref, q_ref, k_ref, v_ref, sel_ref, et_ref, o_ref,
                            kbf, vaug, qs, m_sc, acc_sc, *, tq, tk):
    g = pl.program_id(0)
    qi = pl.program_id(1)
    nkt = kbf.shape[0]
    step = g * pl.num_programs(1) + qi

    @pl.when(qi == 0)
    def _():
        ones_row = jnp.where(_iota((V_ROWS - HEAD_DIM, tk), 0) == 0, 1.0, 0.0).astype(BF16)
        for ki in range(nkt):
            rows = slice(ki * tk, (ki + 1) * tk)
            kbf[ki] = k_ref[rows, :].astype(BF16)
            vaug[ki, 0:HEAD_DIM, :] = v_ref[rows, :].T.astype(BF16)
            vaug[ki, HEAD_DIM:V_ROWS, :] = ones_row

    for r in range(GROUP):
        qs[r] = (q_ref[:, r * HEAD_DIM:(r + 1) * HEAD_DIM] * (SCALE * LOG2E)).astype(BF16)
    m_sc[...] = jnp.full(m_sc.shape, M_INIT, F32)
    acc_sc[...] = jnp.zeros(acc_sc.shape, F32)
    sel_bf = sel_ref[...].astype(BF16)
    qpos = qi * tq + _iota((1, tq), 1)

    def body(ki, carry):
        @pl.when(flag_ref[step * nkt + ki] > 0)
        def _():
            k = kbf[ki]
            va = vaug[ki]
            sel_t = _dot_nt(et_ref[pl.ds(pl.multiple_of(ki * tk, tk), tk), :], sel_bf)
            kpos = ki * tk + _iota((tk, tq), 0)
            mb = jnp.where((sel_t > 0.5) & (kpos <= qpos), 0.0, NEG)
            krel = (kpos - qi * tq).astype(F32)
            for r in range(GROUP):
                slope = jnp.where(g == 0, SLOPES[r], SLOPES[GROUP + r]) * LOG2E
                s = _dot_nt(k, qs[r]) + (mb + slope * krel)
                m_old = m_sc[r]
                m_new = jnp.maximum(m_old, jnp.max(s, axis=0, keepdims=True))
                p = jnp.exp2(s - m_new).astype(BF16)
                acc_sc[r] = acc_sc[r] * jnp.exp2(m_old - m_new) + _dot(va, p)
                m_sc[r] = m_new
        return carry

    lax.fori_loop(0, (qi * tq + tq - 1) // tk + 1, body, 0)
    for r in range(GROUP):
        o_t = acc_sc[r, 0:HEAD_DIM, :] * (1.0 / acc_sc[r, HEAD_DIM:HEAD_DIM + 1, :])
        o_ref[:, r * HEAD_DIM:(r + 1) * HEAD_DIM] = o_t.T


def _tile_flags(anyb, t, tq, tk):
    nq, nkt = t // tq, t // tk
    a = anyb.reshape(nq, 8, N_KV, nkt, tk // SEL_BLOCK)[:, 0]
    return jnp.transpose(jnp.max(a, axis=-1) > 0.5, (1, 0, 2)).astype(jnp.int32).reshape(-1)


def _selected_prompt(z, sel, anyb, tq, tk=512):
    t = z.shape[0]
    gw = GROUP * HEAD_DIM
    flags = _tile_flags(anyb, t, tq, tk)
    et = jnp.asarray(np.arange(t)[:, None] // SEL_BLOCK == np.arange(128)[None, :], dtype=BF16)
    return pl.pallas_call(
        functools.partial(_selected_prompt_kernel, tq=tq, tk=tk),
        out_shape=jax.ShapeDtypeStruct((t, N_HEADS * HEAD_DIM), F32),
        grid_spec=pltpu.PrefetchScalarGridSpec(
            num_scalar_prefetch=1,
            grid=(N_KV, t // tq),
            in_specs=[
                pl.BlockSpec((tq, gw), lambda g, qi, f: (qi, COL_Q // gw + g)),
                pl.BlockSpec((t, HEAD_DIM), lambda g, qi, f: (0, COL_KS // HEAD_DIM + g)),
                pl.BlockSpec((t, HEAD_DIM), lambda g, qi, f: (0, COL_VS // HEAD_DIM + g)),
                pl.BlockSpec((tq, 128), lambda g, qi, f: (qi, g)),
                pl.BlockSpec((t, 128), lambda g, qi, f: (0, 0)),
            ],
            out_specs=pl.BlockSpec((tq, gw), lambda g, qi, f: (qi, g)),
            scratch_shapes=[pltpu.VMEM((t // tk, tk, HEAD_DIM), BF16), pltpu.VMEM((t // tk, V_ROWS, tk), BF16),
                            pltpu.VMEM((GROUP, tq, HEAD_DIM), BF16), pltpu.VMEM((GROUP, 1, tq), F32),
                            pltpu.VMEM((GROUP, V_ROWS, tq), F32)],
        ),
        compiler_params=pltpu.CompilerParams(
            dimension_semantics=("arbitrary", "arbitrary"), vmem_limit_bytes=VMEM_LIMIT),
        name="selected_prompt",
    )(flags, z, z, z, sel, et)


def _window_prompt_kernel(q_ref, kp_ref, kc_ref, vp_ref, vc_ref, o_ref, *, tq):
    g = pl.program_id(0)
    qi = pl.program_id(1)
    k = jnp.concatenate([kp_ref[...], kc_ref[...]], axis=0).astype(BF16)
    v = jnp.concatenate([vp_ref[...], vc_ref[...]], axis=0).astype(BF16)
    krel = _iota((1, 2 * tq), 1) - tq
    dist = _iota((tq, 1), 0) - krel
    mask = (dist >= 0) & (dist < WINDOW) & ((krel >= 0) | (qi > 0))
    distf = dist.astype(F32)
    for r in range(GROUP):
        slope = jnp.where(g == 0, SLOPES[r], SLOPES[GROUP + r])
        qh = q_ref[:, r * HEAD_DIM:(r + 1) * HEAD_DIM].astype(BF16)
        s = jnp.where(mask, _dot_nt(qh, k) * SCALE - slope * distf, NEG)
        m = jnp.max(s, axis=-1, keepdims=True)
        e = jnp.where(mask, jnp.exp(s - m), 0.0)
        p = e * (1.0 / jnp.maximum(jnp.sum(e, axis=-1, keepdims=True), 1e-30))
        o_ref[:, r * HEAD_DIM:(r + 1) * HEAD_DIM] = _dot(p.astype(BF16), v)


def _window_prompt(z, tq=WINDOW):
    t = z.shape[0]
    gw = GROUP * HEAD_DIM
    prev = lambda c: pl.BlockSpec((tq, HEAD_DIM), lambda g, qi: (jnp.maximum(qi - 1, 0), c // HEAD_DIM + g))
    cur = lambda c: pl.BlockSpec((tq, HEAD_DIM), lambda g, qi: (qi, c // HEAD_DIM + g))
    return pl.pallas_call(
        functools.partial(_window_prompt_kernel, tq=tq),
        out_shape=jax.ShapeDtypeStruct((t, N_HEADS * HEAD_DIM), F32),
        grid=(N_KV, t // tq),
        in_specs=[pl.BlockSpec((tq, gw), lambda g, qi: (qi, COL_Q // gw + g)),
                  prev(COL_KW), cur(COL_KW), prev(COL_VW), cur(COL_VW)],
        out_specs=pl.BlockSpec((tq, gw), lambda g, qi: (qi, g)),
        compiler_params=pltpu.CompilerParams(
            dimension_semantics=("arbitrary", "arbitrary"), vmem_limit_bytes=VMEM_LIMIT),
        name="window_prompt",
    )(z, z, z, z, z)


SEL_PAD = 384


def _slope_col(g, ds):
    return jnp.concatenate([jnp.full((ds, 1), SLOPES[g * GROUP + r], F32) for r in range(GROUP)], axis=0)


def _stack_heads(q_ref, g):
    return jnp.concatenate([q_ref[:, (g * GROUP + r) * HEAD_DIM:(g * GROUP + r + 1) * HEAD_DIM]
                            for r in range(GROUP)], axis=0).astype(BF16)


def _pad_rows(a, n):
    return jnp.concatenate([a, jnp.zeros((n - a.shape[0], a.shape[1]), a.dtype)], axis=0)


def _sample_small_kernel(q_ref, kc_ref, vc_ref, m_ref, sk_ref, sv_ref, nk_ref, nv_ref,
                         oc_ref, imp_ref, ow_ref, ko_ref, vo_ref, *, ds, nc, past, wb):
    rows = GROUP * ds
    qidx = _iota((rows, 1), 0) & (ds - 1)
    spos = past + qidx
    end = _iota((1, nc), 1) * CMP_STRIDE + (CMP_BLOCK - 1)
    dist_c = spos - end
    mask_c = dist_c >= 0
    ist = _iota((1, wb), 1)
    dist_s = wb + qidx - ist
    mask_s = dist_s < WINDOW
    jn = _iota((1, 128), 1)
    dist_n = qidx - jn
    mask_n = (dist_n >= 0) & (jn < ds)
    for g in range(N_KV):
        lanes = slice(g * HEAD_DIM, (g + 1) * HEAD_DIM)
        qs = _stack_heads(q_ref, g)
        slope = _slope_col(g, ds)
        s = _dot_nt(qs, kc_ref[:, lanes].astype(BF16)) * SCALE - slope * dist_c.astype(F32)
        s = jnp.where(mask_c, s, NEG)
        m = jnp.max(s, axis=-1, keepdims=True)
        e = jnp.where(mask_c, jnp.exp(s - m), 0.0)
        p = e * (1.0 / jnp.maximum(jnp.sum(e, axis=-1, keepdims=True), 1e-30))
        o = _dot(p.astype(BF16), vc_ref[:, lanes].astype(BF16))
        psum = p[0:ds]
        for r in range(1, GROUP):
            psum = psum + p[r * ds:(r + 1) * ds]
        imp_ref[:, g * SEL_PAD:(g + 1) * SEL_PAD] = jnp.dot(psum, m_ref[...], precision=lax.Precision.HIGHEST,
                                                            preferred_element_type=F32)
        kst = sk_ref[pl.ds(g, wb, stride=N_KV), :].astype(BF16)
        vst = sv_ref[pl.ds(g, wb, stride=N_KV), :].astype(BF16)
        kn = _pad_rows(nk_ref[pl.ds(g, ds, stride=N_KV), :], 128).astype(BF16)
        vn = _pad_rows(nv_ref[pl.ds(g, ds, stride=N_KV), :], 128).astype(BF16)
        s1 = jnp.where(mask_s, _dot_nt(qs, kst) * SCALE - slope * dist_s.astype(F32), NEG)
        s2 = jnp.where(mask_n, _dot_nt(qs, kn) * SCALE - slope * dist_n.astype(F32), NEG)
        mw = jnp.maximum(jnp.max(s1, axis=-1, keepdims=True), jnp.max(s2, axis=-1, keepdims=True))
        e1 = jnp.where(mask_s, jnp.exp(s1 - mw), 0.0)
        e2 = jnp.where(mask_n, jnp.exp(s2 - mw), 0.0)
        inv = 1.0 / jnp.maximum(jnp.sum(e1, axis=-1, keepdims=True) + jnp.sum(e2, axis=-1, keepdims=True), 1e-30)
        w = _dot((e1 * inv).astype(BF16), vst) + _dot((e2 * inv).astype(BF16), vn)
        for r in range(GROUP):
            h = g * GROUP + r
            oc_ref[:, h * HEAD_DIM:(h + 1) * HEAD_DIM] = o[r * ds:(r + 1) * ds]
            ow_ref[:, h * HEAD_DIM:(h + 1) * HEAD_DIM] = w[r * ds:(r + 1) * ds]
    keep = (wb - ds) * N_KV
    for s_ref, n_ref, o_ref in ((sk_ref, nk_ref, ko_ref), (sv_ref, nv_ref, vo_ref)):
        o_ref[0:keep, :] = s_ref[ds * N_KV:wb * N_KV, :]
        o_ref[keep:wb * N_KV, :] = n_ref[...]


def _sample_small(z2, kw2, vw2, kcmp2, vcmp2, imp_m2, state_k, state_v, nb, past):
    t = z2.shape[0]
    ds = t // nb
    nc = kcmp2.shape[0] // nb
    wb = state_k.shape[0] // (nb * N_KV)
    qw = N_HEADS * HEAD_DIM
    kvw = N_KV * HEAD_DIM
    rows = lambda n: pl.BlockSpec((n * N_KV, HEAD_DIM), lambda b: (b, 0))
    return pl.pallas_call(
        functools.partial(_sample_small_kernel, ds=ds, nc=nc, past=past, wb=wb),
        out_shape=(jax.ShapeDtypeStruct((t, qw), F32), jax.ShapeDtypeStruct((t, N_KV * SEL_PAD), F32),
                   jax.ShapeDtypeStruct((t, qw), F32),
                   jax.ShapeDtypeStruct(state_k.shape, F32), jax.ShapeDtypeStruct(state_v.shape, F32)),
        grid=(nb,),
        in_specs=[pl.BlockSpec((ds, qw), lambda b: (b, COL_Q // qw)),
                  pl.BlockSpec((nc, kvw), lambda b: (b, 0)), pl.BlockSpec((nc, kvw), lambda b: (b, 0)),
                  pl.BlockSpec(imp_m2.shape, lambda b: (0, 0)),
                  rows(wb), rows(wb), rows(ds), rows(ds)],
        out_specs=(pl.BlockSpec((ds, qw), lambda b: (b, 0)), pl.BlockSpec((ds, N_KV * SEL_PAD), lambda b: (b, 0)),
                   pl.BlockSpec((ds, qw), lambda b: (b, 0)), rows(wb), rows(wb)),
        compiler_params=pltpu.CompilerParams(dimension_semantics=("arbitrary",), vmem_limit_bytes=VMEM_LIMIT),
        name="sample_small",
    )(z2, kcmp2, vcmp2, imp_m2, state_k, state_v, kw2, vw2)


SLC_PAGES = 32
SLC_KEYS = SLC_PAGES * PAGE_SIZE
SLC_BLOCKS = SLC_KEYS // SEL_BLOCK
WIN_STEPS = 128 // SLC_BLOCKS


def _expand_matrix():
    j = np.arange(128)[:, None]
    c = np.arange(SLC_KEYS)[None, :]
    e = np.concatenate([(j == w * SLC_BLOCKS + c // SEL_BLOCK) for w in range(WIN_STEPS)], axis=0)
    return jnp.asarray(e, dtype=BF16)


def _selected_sample_kernel(pt_ref, ck_hbm, cv_hbm, q_ref, sel0_ref, sel1_ref, e_ref, nk_ref, nv_ref, o_ref,
                            kbuf, vbuf, sem, m_sc, l_sc, acc_sc, *, ds, steps, past):
    slot = _gather_pipeline(pt_ref, (ck_hbm, cv_hbm), (kbuf, vbuf), sem, SLC_PAGES, PAGE_ROWS)
    kt = pl.program_id(0) % steps
    rows = GROUP * ds
    nk = SLC_KEYS
    qidx = _iota((rows, 1), 0) & (ds - 1)

    @pl.when(kt == 0)
    def _():
        m_sc[...] = jnp.full(m_sc.shape, M_INIT, F32)
        l_sc[...] = jnp.zeros(l_sc.shape, F32)
        acc_sc[...] = jnp.zeros(acc_sc.shape, F32)

    krel = (kt * nk - past + _iota((1, nk), 1)).astype(F32)
    e = e_ref[pl.ds(pl.multiple_of((kt % WIN_STEPS) * 128, 128), 128), :]
    for g, sel_ref in enumerate((sel0_ref, sel1_ref)):
        grows = slice(g * rows, (g + 1) * rows)
        qs = _stack_heads(q_ref, g)
        slope = _slope_col(g, ds)
        selexp = _dot(sel_ref[...].astype(BF16), e)
        mb = jnp.where(jnp.concatenate([selexp] * GROUP, axis=0) > 0.5, 0.0, NEG)
        kg = kbuf[slot, pl.ds(g, nk, stride=N_KV), :].astype(BF16)
        vg = vbuf[slot, pl.ds(g, nk, stride=N_KV), :].astype(BF16)
        s = _dot_nt(qs, kg) * SCALE + (mb + slope * krel)
        _flash_update(s, vg, m_sc, l_sc, acc_sc, grows)

    @pl.when(kt == steps - 1)
    def _():
        jn = _iota((1, 128), 1)
        mb_n = jnp.where((jn <= qidx) & (jn < ds), 0.0, NEG)
        for g in range(N_KV):
            grows = slice(g * rows, (g + 1) * rows)
            qs = _stack_heads(q_ref, g)
            kn = _pad_rows(nk_ref[pl.ds(g, ds, stride=N_KV), :], 128).astype(BF16)
            vn = _pad_rows(nv_ref[pl.ds(g, ds, stride=N_KV), :], 128).astype(BF16)
            s = _dot_nt(qs, kn) * SCALE + (mb_n + _slope_col(g, ds) * jn.astype(F32))
            _flash_update(s, vn, m_sc, l_sc, acc_sc, grows)
            o = acc_sc[grows, :] * (1.0 / l_sc[grows, :])
            for r in range(GROUP):
                h = g * GROUP + r
                o_ref[:, h * HEAD_DIM:(h + 1) * HEAD_DIM] = o[r * ds:(r + 1) * ds]


def _selected_sample(page_table, cache_k, cache_v, z2, ks2, vs2, sel2, nb, past):
    t = z2.shape[0]
    ds = t // nb
    n_phys = cache_k.shape[0]
    ck = cache_k.reshape(n_phys, PAGE_ROWS, HEAD_DIM)
    cv = cache_v.reshape(n_phys, PAGE_ROWS, HEAD_DIM)
    steps = page_table.shape[1] // SLC_PAGES
    qw = N_HEADS * HEAD_DIM
    rows = N_KV * GROUP * ds
    e = _expand_matrix()
    win = lambda g: pl.BlockSpec(
        (ds, 128), lambda s, pt: (s // steps, g * (SEL_PAD // 128) + (s % steps) // WIN_STEPS))
    new = pl.BlockSpec((ds * N_KV, HEAD_DIM), lambda s, pt: (s // steps, 0))
    return pl.pallas_call(
        functools.partial(_selected_sample_kernel, ds=ds, steps=steps, past=past),
        out_shape=jax.ShapeDtypeStruct((t, qw), F32),
        grid_spec=pltpu.PrefetchScalarGridSpec(
            num_scalar_prefetch=1,
            grid=(nb * steps,),
            in_specs=[pl.BlockSpec(memory_space=pl.ANY), pl.BlockSpec(memory_space=pl.ANY),
                      pl.BlockSpec((ds, qw), lambda s, pt: (s // steps, COL_Q // qw)),
                      win(0), win(1), pl.BlockSpec(e.shape, lambda s, pt: (0, 0)), new, new],
            out_specs=pl.BlockSpec((ds, qw), lambda s, pt: (s // steps, 0)),
            scratch_shapes=[pltpu.VMEM((2, SLC_PAGES * PAGE_ROWS, HEAD_DIM), F32),
                            pltpu.VMEM((2, SLC_PAGES * PAGE_ROWS, HEAD_DIM), F32),
                            pltpu.SemaphoreType.DMA((2, 2)),
                            pltpu.VMEM((rows, 1), F32), pltpu.VMEM((rows, 1), F32),
                            pltpu.VMEM((rows, HEAD_DIM), F32)],
        ),
        compiler_params=pltpu.CompilerParams(dimension_semantics=("arbitrary",), vmem_limit_bytes=VMEM_LIMIT),
        name="selected_sample",
    )(page_table.reshape(-1), ck, cv, z2, sel2, sel2, e, ks2, vs2)


def _merge_kernel(x_ref, conv_ref, oc_ref, os_ref, ow_ref, gt_ref, zn_ref, w_ref, y_ref):
    gt = gt_ref[...]
    parts = [conv_ref[...].astype(BF16)]
    for h in range(N_HEADS):
        lanes = slice(h * HEAD_DIM, (h + 1) * HEAD_DIM)
        o = (gt[:, 3 * h:3 * h + 1] * oc_ref[:, lanes] + gt[:, 3 * h + 1:3 * h + 2] * os_ref[:, lanes]
             + gt[:, 3 * h + 2:3 * h + 3] * ow_ref[:, lanes])
        parts.append((o * _silu(zn_ref[:, lanes])).astype(BF16))
    y_ref[...] = x_ref[...] + _dot(jnp.concatenate(parts, axis=1), w_ref[...])


def _merge(x, conv_o, o_c, o_s, o_w, z, w_out_bf, tm=256):
    t, d = x.shape
    qw = N_HEADS * HEAD_DIM
    rowblk = lambda w: pl.BlockSpec((tm, w), lambda i: (i, 0))
    return pl.pallas_call(
        _merge_kernel,
        out_shape=jax.ShapeDtypeStruct((t, d), F32),
        grid=(t // tm,),
        in_specs=[rowblk(d), rowblk(C_CONV), rowblk(qw), rowblk(qw), rowblk(qw),
                  pl.BlockSpec((tm, 128), lambda i: (i, COL_GT // 128)),
                  pl.BlockSpec((tm, qw), lambda i: (i, COL_ZN // qw)),
                  pl.BlockSpec(w_out_bf.shape, lambda i: (0, 0))],
        out_specs=rowblk(d),
        compiler_params=pltpu.CompilerParams(dimension_semantics=("arbitrary",), vmem_limit_bytes=VMEM_LIMIT),
        name="merge",
    )(x, conv_o, o_c, o_s, o_w, z, z, w_out_bf)


N_GT = 3 * N_HEADS


def _pad_w_kernel(wt_hbm, o_ref, buf, sem):
    j = pl.program_id(0)
    slot = j % 2

    def tile_copy(jj, sl):
        src = jnp.where(jj <= TILE_GT, jj * TN, jj * TN - COL_ZN + COL_GT + N_GT)
        return pltpu.make_async_copy(wt_hbm.at[pl.ds(pl.multiple_of(src, 8), TN)], buf.at[sl], sem.at[sl])

    @pl.when(j == 0)
    def _():
        tile_copy(0, 0).start()

    @pl.when(j + 1 < pl.num_programs(0))
    def _():
        tile_copy(j + 1, 1 - slot).start()

    tile_copy(j, slot).wait()
    w = buf[slot]
    keep = (j != TILE_GT) | (_iota((TN, 1), 0) < N_GT)
    o_ref[...] = jnp.where(keep, w, 0.0).T.astype(BF16)


def _pad_w_in(w_in):
    d, n = w_in.shape
    assert n == COL_GT + N_GT + ZW - COL_ZN and COL_ZN == (TILE_GT + 1) * TN
    return pl.pallas_call(
        _pad_w_kernel,
        out_shape=jax.ShapeDtypeStruct((d, ZW), BF16),
        grid=(N_TILES,),
        in_specs=[pl.BlockSpec(memory_space=pl.ANY)],
        out_specs=pl.BlockSpec((d, TN), lambda j: (0, j)),
        scratch_shapes=[pltpu.VMEM((2, TN, d), F32), pltpu.SemaphoreType.DMA((2,))],
        compiler_params=pltpu.CompilerParams(dimension_semantics=("arbitrary",), vmem_limit_bytes=VMEM_LIMIT),
        name="pad_w_in",
    )(w_in.T)


def _cmp_weights(pe, w1, w2):
    half = CMP_STRIDE
    k = half * HEAD_DIM
    w1cat = jnp.concatenate([w1[:half].reshape(k, -1), w1[half:].reshape(k, -1)], axis=1).astype(BF16)
    pe2 = jnp.concatenate([pe[:half].reshape(1, k), pe[half:].reshape(1, k), jnp.zeros((6, k), pe.dtype)], axis=0)
    return w1cat, pe2, w2.astype(BF16)


def kernel(x_prompt, x_sample, cache_k_cmp, cache_v_cmp, cache_k_slc, cache_v_slc, state_k_win, state_v_win,
           state_conv, page_table, g_norm, w_in, pe_cmp_k, w_cmp_k1, w_cmp_k2, pe_cmp_v, w_cmp_v1, w_cmp_v2,
           g_q, g_k_cmp, g_k_slc, g_k_win, w_dw, b_dw, ln_g, ln_b, w_pw2, b_pw2, w_out):
    _, t, d = x_prompt.shape
    db, ds, _ = x_sample.shape
    past = page_table.shape[1] * PAGE_SIZE
    wb = state_k_win.shape[1]
    kvw = N_KV * HEAD_DIM
    assert ds < CMP_STRIDE and wb == WINDOW and t % SEL_BLOCK == 0 and t // SEL_BLOCK == 128

    w_p = _pad_w_in(w_in)
    w_pw_bf = w_pw2.astype(BF16)
    w_out_bf = w_out.astype(BF16)
    wk1, pek, wk2 = _cmp_weights(pe_cmp_k, w_cmp_k1, w_cmp_k2)
    wv1, pev, wv2 = _cmp_weights(pe_cmp_v, w_cmp_v1, w_cmp_v2)

    xp = x_prompt.reshape(t, d)
    z, kc, vc, ks, vs, kw, vw = _in_proj(xp, g_norm, w_p, g_q, g_k_slc, g_k_win, tm=1024)
    conv_o, conv_st = _conv_prompt(z, w_dw, b_dw, ln_g, ln_b, w_pw_bf, b_pw2)
    nc = t // CMP_STRIDE
    fk, fv = _feats_prompt(kc, vc, wk1, wv1)
    kcmp, vcmp = _cmp_finish(fk, fv, (wk1, pek, wk2), g_k_cmp, (wv1, pev, wv2), nc)
    imp_m = _imp_matrix(nc, t // SEL_BLOCK, nc - 1, t // SEL_BLOCK)
    o_c, imp = _cmp_select(z, kcmp, vcmp, imp_m, tq=256)
    cur = (jnp.arange(t, dtype=jnp.int32) // SEL_BLOCK).reshape(1, t)
    sel, anyb = _select(imp, cur, 128, tr=512, any_rows=256)
    o_s = _selected_prompt(z, sel, anyb, tq=256)
    o_w = _window_prompt(z)
    y_prompt = _merge(xp, conv_o, o_c, o_s, o_w, z, w_out_bf)

    xs = x_sample.reshape(db * ds, d)
    z2, kc2, vc2, ks2, vs2, kw2, vw2 = _in_proj(xs, g_norm, w_p, g_q, g_k_slc, g_k_win, tm=db * ds)
    conv_o2, u2 = _conv_sample(z2, state_conv, w_dw, b_dw, ln_g, ln_b, w_pw_bf, b_pw2)
    nc2 = past // CMP_STRIDE
    kcmp2, vcmp2 = _cmp_sample(page_table, cache_k_cmp, cache_v_cmp, (wk1, pek, wk2), g_k_cmp, (wv1, pev, wv2))
    n_sel2 = -(-(past + ds) // SEL_BLOCK)
    assert n_sel2 <= SEL_PAD
    imp_m2 = _imp_matrix(nc2, SEL_PAD, nc2 - 1, n_sel2)
    rows = lambda a: a.reshape(-1, HEAD_DIM)
    o_c2, imp2, o_w2, k_win, v_win = _sample_small(z2, kw2, vw2, kcmp2, vcmp2, imp_m2, rows(state_k_win),
                                                   rows(state_v_win), db, past)
    cur2 = ((past + jnp.arange(db * ds, dtype=jnp.int32) % ds) // SEL_BLOCK).reshape(1, db * ds)
    sel2, = _select(imp2, cur2, SEL_PAD, tr=db * ds)
    o_s2 = _selected_sample(page_table, cache_k_slc, cache_v_slc, z2, ks2, vs2, sel2, db, past)
    y_sample = _merge(xs, conv_o2, o_c2, o_s2, o_w2, z2, w_out_bf)

    kv4 = lambda a, b: a.reshape(b, -1, N_KV, HEAD_DIM)
    return (y_prompt.reshape(1, t, d), y_sample.reshape(db, ds, d),
            kv4(kc, 1), kv4(vc, 1), kv4(ks, 1), kv4(vs, 1), kv4(kw, 1)[:, t - wb:], kv4(vw, 1)[:, t - wb:],
            conv_st[HALO - (CONV_WIDTH - 1):][None],
            kv4(kc2, db), kv4(vc2, db), kv4(ks2, db), kv4(vs2, db), kv4(k_win, db), kv4(v_win, db),
            jnp.concatenate([state_conv[:, ds:], u2.reshape(db, ds, C_CONV)], axis=1))
```

```python
import functools

import numpy as np
import jax
import jax.numpy as jnp
from jax import lax
from jax.experimental import pallas as pl
from jax.experimental.pallas import tpu as pltpu

F32 = jnp.float32
BF16 = jnp.bfloat16

HEAD_DIM = 128
N_HEADS = 8
N_KV = 2
GROUP = 4
C_CONV = 1024
CONV_WIDTH = 31
CMP_STRIDE = 16
CMP_BLOCK = 32
SEL_BLOCK = 64
N_SELECT = 16
WINDOW = 512
PAGE_SIZE = 128
EPS = 1e-6
SCALE = HEAD_DIM ** -0.5
LOG2E = 1.4426950408889634
SLOPES = tuple(2.0 ** -(h + 1) for h in range(N_HEADS))

TN = 512
N_TILES = 14
ZW = TN * N_TILES
COL_UA, COL_UB, COL_ZC, COL_Q = 0, 1024, 2048, 3072
COL_KC, COL_VC, COL_KS, COL_VS, COL_KW, COL_VW = 4096, 4352, 4608, 4864, 5120, 5376
COL_GT, COL_ZN = 5632, 6144
TILE_Q0, TILE_Q1, TILE_CMP, TILE_SLC, TILE_WIN, TILE_GT = 6, 7, 8, 9, 10, 11

NEG = -1e30
M_INIT = -1e29
VMEM_LIMIT = 48 * 1024 * 1024
INPROJ_VMEM_LIMIT = 56 * 1024 * 1024


def _sigmoid(x):
    return 1.0 / (1.0 + jnp.exp(-x))


def _silu(x):
    return x * _sigmoid(x)


def _dot(a, b):
    return jnp.dot(a, b, preferred_element_type=F32)


def _dot_nt(a, b):
    return lax.dot_general(a, b, (((1,), (1,)), ((), ())), preferred_element_type=F32)


def _rms(a, g):
    return a * lax.rsqrt(jnp.mean(a * a, axis=-1, keepdims=True) + EPS) * g


def _iota(shape, dim):
    return lax.broadcasted_iota(jnp.int32, shape, dim)


def _inproj_kernel(x_ref, gn_ref, w_ref, gq_ref, gks_ref, gkw_ref,
                   z_ref, kc_o, vc_o, ks_o, vs_o, kw_o, vw_o, xn_ref, *, tm):
    j = pl.program_id(1)

    @pl.when(j == 0)
    def _():
        x = x_ref[...]
        ms = jnp.mean(x * x, axis=-1, keepdims=True)
        xn_ref[...] = (x * lax.rsqrt(ms + EPS) * gn_ref[...]).astype(BF16)

    z_ref[...] = _dot(xn_ref[...], w_ref[...])

    def kv_tile(g_ref, k_o, v_o):
        for c in range(TN // HEAD_DIM):
            lanes = slice(c * HEAD_DIM, (c + 1) * HEAD_DIM)
            a = z_ref[:, lanes]
            if c < N_KV and g_ref is not None:
                a = _rms(a, g_ref[...])
                z_ref[:, lanes] = a
            (k_o if c < N_KV else v_o)[pl.ds(c % N_KV, tm, stride=N_KV), :] = a

    @pl.when((j == TILE_Q0) | (j == TILE_Q1))
    def _():
        for c in range(TN // HEAD_DIM):
            lanes = slice(c * HEAD_DIM, (c + 1) * HEAD_DIM)
            z_ref[:, lanes] = _rms(z_ref[:, lanes], gq_ref[...])

    @pl.when(j == TILE_CMP)
    def _():
        kv_tile(None, kc_o, vc_o)

    @pl.when(j == TILE_SLC)
    def _():
        kv_tile(gks_ref, ks_o, vs_o)

    @pl.when(j == TILE_WIN)
    def _():
        kv_tile(gkw_ref, kw_o, vw_o)

    @pl.when(j == TILE_GT)
    def _():
        z_ref[...] = _sigmoid(z_ref[...])


def _in_proj(x, g_norm, w_p, g_q, g_ks, g_kw, tm):
    t, d = x.shape
    row = lambda a: a.reshape(1, -1)
    kv_shape = jax.ShapeDtypeStruct((t * N_KV, HEAD_DIM), F32)
    kv_spec = pl.BlockSpec((tm * N_KV, HEAD_DIM), lambda i, j: (i, 0))
    return pl.pallas_call(
        functools.partial(_inproj_kernel, tm=tm),
        out_shape=(jax.ShapeDtypeStruct((t, ZW), F32),) + (kv_shape,) * 6,
        grid=(t // tm, N_TILES),
        in_specs=[
            pl.BlockSpec((tm, d), lambda i, j: (i, 0)),
            pl.BlockSpec((1, d), lambda i, j: (0, 0)),
            pl.BlockSpec((d, TN), lambda i, j: (0, j)),
            pl.BlockSpec((1, HEAD_DIM), lambda i, j: (0, 0)),
            pl.BlockSpec((1, HEAD_DIM), lambda i, j: (0, 0)),
            pl.BlockSpec((1, HEAD_DIM), lambda i, j: (0, 0)),
        ],
        out_specs=(pl.BlockSpec((tm, TN), lambda i, j: (i, j)),) + (kv_spec,) * 6,
        scratch_shapes=[pltpu.VMEM((tm, d), BF16)],
        compiler_params=pltpu.CompilerParams(
            dimension_semantics=("arbitrary", "arbitrary"), vmem_limit_bytes=INPROJ_VMEM_LIMIT),
        name="in_proj",
    )(x, row(g_norm), w_p, row(g_q), row(g_ks), row(g_kw))


HALO = 32
CONV_RB = 64
CONV_CB = 128


def _conv_tail(y, zc, lng_ref, lnb_ref, wpw_ref, bpw_ref):
    mu = jnp.mean(y, axis=-1, keepdims=True)
    yc = y - mu
    var = jnp.mean(yc * yc, axis=-1, keepdims=True)
    yn = yc * lax.rsqrt(var + EPS) * lng_ref[...] + lnb_ref[...]
    act = _silu(yn).astype(BF16)
    return (_dot(act, wpw_ref[...]) + bpw_ref[...]) * _silu(zc)


def _conv_prompt_kernel(ua_ref, ub_ref, zc_ref, uah_ref, ubh_ref, wdw_ref, bdw_ref, lng_ref, lnb_ref,
                        wpw_ref, bpw_ref, o_ref, st_ref, buf, ybuf, *, tt):
    i = pl.program_id(0)
    uh = uah_ref[...] * _sigmoid(ubh_ref[...])
    buf[0:HALO, :] = jnp.where(i > 0, uh, 0.0)
    buf[HALO:HALO + tt, :] = ua_ref[...] * _sigmoid(ub_ref[...])
    off = HALO - (CONV_WIDTH - 1)
    for c0 in range(0, C_CONV, CONV_CB):
        lanes = slice(c0, c0 + CONV_CB)
        for r0 in range(0, tt, CONV_RB):
            acc = jnp.broadcast_to(bdw_ref[:, lanes], (CONV_RB, CONV_CB))
            for b in range(8):
                n = CONV_RB if b == 0 else CONV_RB + 8
                zb = None
                for a in range(-(-(off - b) // 8), (off + CONV_WIDTH - 1 - b) // 8 + 1):
                    k = 8 * a + b - off
                    term = wdw_ref[k:k + 1, lanes] * buf[r0 + 8 * a:r0 + 8 * a + n, lanes]
                    zb = term if zb is None else zb + term
                acc = acc + zb[b:b + CONV_RB]
            ybuf[r0:r0 + CONV_RB, lanes] = acc
    o_ref[...] = _conv_tail(ybuf[...], zc_ref[...], lng_ref, lnb_ref, wpw_ref, bpw_ref)

    @pl.when(i == pl.num_programs(0) - 1)
    def _():
        st_ref[...] = buf[tt:tt + HALO, :]


def _conv_prompt(z, w_dw, b_dw, ln_g, ln_b, w_pw_bf, b_pw, tt=256):
    t = z.shape[0]
    row = lambda a: a.reshape(1, -1)
    hb = tt // HALO
    cur = lambda c: pl.BlockSpec((tt, C_CONV), lambda i: (i, c))
    halo = lambda c: pl.BlockSpec((HALO, C_CONV), lambda i: (jnp.maximum(i * hb - 1, 0), c))
    full = lambda shape: pl.BlockSpec(shape, lambda i: (0, 0))
    return pl.pallas_call(
        functools.partial(_conv_prompt_kernel, tt=tt),
        out_shape=(jax.ShapeDtypeStruct((t, C_CONV), F32), jax.ShapeDtypeStruct((HALO, C_CONV), F32)),
        grid=(t // tt,),
        in_specs=[cur(0), cur(1), cur(2), halo(0), halo(1),
                  full((CONV_WIDTH, C_CONV)), full((1, C_CONV)), full((1, C_CONV)), full((1, C_CONV)),
                  full((C_CONV, C_CONV)), full((1, C_CONV))],
        out_specs=(pl.BlockSpec((tt, C_CONV), lambda i: (i, 0)), pl.BlockSpec((HALO, C_CONV), lambda i: (0, 0))),
        scratch_shapes=[pltpu.VMEM((HALO + tt, C_CONV), F32), pltpu.VMEM((tt, C_CONV), F32)],
        compiler_params=pltpu.CompilerParams(dimension_semantics=("arbitrary",), vmem_limit_bytes=VMEM_LIMIT),
        name="conv_prompt",
    )(z, z, z, z, z, w_dw, row(b_dw), row(ln_g), row(ln_b), w_pw_bf, row(b_pw))


ST_ROWS = 40


def _conv_sample_kernel(ua_ref, ub_ref, zc_ref, st_ref, wdw_ref, bdw_ref, lng_ref, lnb_ref, wpw_ref, bpw_ref,
                        o_ref, u_ref, fbuf, ybuf, *, nb, ds):
    u = ua_ref[...] * _sigmoid(ub_ref[...])
    u_ref[...] = u
    rows = _iota((nb * ds, 1), 0) & (ds - 1)
    acc_u = jnp.broadcast_to(bdw_ref[...], (nb * ds, C_CONV))
    for d in range(ds):
        sh = u if d == 0 else pltpu.roll(u, d, 0)
        acc_u = acc_u + jnp.where(rows >= d, sh, 0.0) * wdw_ref[CONV_WIDTH - 1 - d:CONV_WIDTH - d, :]
    ybuf[...] = acc_u
    fbuf[:, 24:ST_ROWS, :] = jnp.zeros((nb, ST_ROWS - 24, C_CONV), F32)
    fbuf[:, 0:CONV_WIDTH - 1, :] = st_ref[...]

    def body(b, carry):
        acc = jnp.zeros((ds, C_CONV), F32)
        for k in range(CONV_WIDTH - 1):
            acc = acc + wdw_ref[k:k + 1, :] * fbuf[b, k:k + ds, :]
        r = pl.multiple_of(b * ds, ds)
        ybuf[pl.ds(r, ds), :] = ybuf[pl.ds(r, ds), :] + acc
        return carry

    lax.fori_loop(0, nb, body, 0)
    o_ref[...] = _conv_tail(ybuf[...], zc_ref[...], lng_ref, lnb_ref, wpw_ref, bpw_ref)


def _conv_sample(z2, state_conv, w_dw, b_dw, ln_g, ln_b, w_pw_bf, b_pw):
    nb, sw, _ = state_conv.shape
    t = z2.shape[0]
    ds = t // nb
    assert sw == CONV_WIDTH - 1 and ds == 8
    row = lambda a: a.reshape(1, -1)
    col = lambda c: pl.BlockSpec((t, C_CONV), lambda i: (0, c))
    full = lambda shape: pl.BlockSpec(shape, lambda i: (0,) * len(shape))
    return pl.pallas_call(
        functools.partial(_conv_sample_kernel, nb=nb, ds=ds),
        out_shape=(jax.ShapeDtypeStruct((t, C_CONV), F32), jax.ShapeDtypeStruct((t, C_CONV), F32)),
        grid=(1,),
        in_specs=[col(0), col(1), col(2), full((nb, sw, C_CONV)),
                  full((CONV_WIDTH, C_CONV)), full((1, C_CONV)), full((1, C_CONV)), full((1, C_CONV)),
                  full((C_CONV, C_CONV)), full((1, C_CONV))],
        out_specs=(full((t, C_CONV)), full((t, C_CONV))),
        scratch_shapes=[pltpu.VMEM((nb, ST_ROWS, C_CONV), F32), pltpu.VMEM((t, C_CONV), F32)],
        compiler_params=pltpu.CompilerParams(dimension_semantics=("arbitrary",), vmem_limit_bytes=VMEM_LIMIT),
        name="conv_sample",
    )(z2, z2, z2, state_conv, w_dw, row(b_dw), row(ln_g), row(ln_b), w_pw_bf, row(b_pw))


FEAT_W = 2 * N_KV * HEAD_DIM


N_FEAT = FEAT_W // HEAD_DIM


def _chunk_feats(piece, w):
    outs = []
    for g in range(N_KV):
        xg = jnp.concatenate([piece(c, g).astype(BF16) for c in range(CMP_STRIDE)], axis=1)
        outs.append(_dot(xg, w))
    return jnp.concatenate(outs, axis=1)


CHUNK_ROWS = CMP_STRIDE * N_KV


def _feats_kernel(xk_ref, xv_ref, wk_ref, wv_ref, fk_ref, fv_ref, *, tm):
    for x_ref, w_ref, f_ref in ((xk_ref, wk_ref, fk_ref), (xv_ref, wv_ref, fv_ref)):
        f = _chunk_feats(lambda c, g: x_ref[pl.ds(c * N_KV + g, tm, stride=CHUNK_ROWS), :], w_ref[...])
        for cb in range(N_FEAT):
            f_ref[cb] = f[:, cb * HEAD_DIM:(cb + 1) * HEAD_DIM]


def _feats_prompt(xk, xv, wk, wv, tm=128):
    nc = xk.shape[0] // CHUNK_ROWS
    full = lambda shape: pl.BlockSpec(shape, lambda i: (0, 0))
    rows = pl.BlockSpec((tm * CHUNK_ROWS, HEAD_DIM), lambda i: (i, 0))
    ospec = pl.BlockSpec((N_FEAT, tm, HEAD_DIM), lambda i: (0, i, 0))
    return pl.pallas_call(
        functools.partial(_feats_kernel, tm=tm),
        out_shape=(jax.ShapeDtypeStruct((N_FEAT, nc, HEAD_DIM), F32),) * 2,
        grid=(nc // tm,),
        in_specs=[rows, rows, full(wk.shape), full(wv.shape)],
        out_specs=(ospec, ospec),
        compiler_params=pltpu.CompilerParams(dimension_semantics=("arbitrary",), vmem_limit_bytes=VMEM_LIMIT),
        name="feats_prompt",
    )(xk, xv, wk, wv)


FEAT_PAGES = 32
CHUNKS_PER_PAGE = PAGE_SIZE // CMP_STRIDE
PAGE_ROWS = PAGE_SIZE * N_KV
FEAT_PITCH = PAGE_ROWS + 8


GATHER_SLOTS = 3


def _page_copies(pt_ref, srcs, bufs, sem, step, slot, pages, pitch, look):
    out = []
    last = pt_ref.shape[0] - 1
    for p in range(pages + look):
        page = pt_ref[jnp.minimum(step * pages + p, last)] if look else pt_ref[step * pages + p]
        for n, (src, buf) in enumerate(zip(srcs, bufs)):
            out.append(pltpu.make_async_copy(src.at[page], buf.at[slot, pl.ds(p * pitch, PAGE_ROWS)], sem.at[n, slot]))
    return out


def _gather_pipeline(pt_ref, srcs, bufs, sem, pages, pitch, look=0):
    s = pl.program_id(0)
    slot = s % GATHER_SLOTS

    @pl.when(s == 0)
    def _():
        for first in range(GATHER_SLOTS - 1):
            for c in _page_copies(pt_ref, srcs, bufs, sem, first, first, pages, pitch, look):
                c.start()

    @pl.when(s + GATHER_SLOTS - 1 < pl.num_programs(0))
    def _():
        ahead = s + GATHER_SLOTS - 1
        for c in _page_copies(pt_ref, srcs, bufs, sem, ahead, ahead % GATHER_SLOTS, pages, pitch, look):
            c.start()

    for c in _page_copies(pt_ref, srcs, bufs, sem, s, slot, pages, pitch, look):
        c.wait()
    return slot


FEAT_CHUNKS = FEAT_PAGES * CHUNKS_PER_PAGE


def _cmp_sample_kernel(pt_ref, ck_hbm, cv_hbm, wk1_ref, pek_ref, wk2_ref, gk_ref, wv1_ref, pev_ref, wv2_ref,
                       ok_ref, ov_ref, kbuf, vbuf, sem, fsc, bias_sc, *, steps_per_seq):
    slot = _gather_pipeline(pt_ref, (ck_hbm, cv_hbm), (kbuf, vbuf), sem, FEAT_PAGES, FEAT_PITCH, look=1)
    last_of_seq = (pl.program_id(0) % steps_per_seq) == steps_per_seq - 1
    row = _iota((FEAT_CHUNKS, 1), 0)
    keep = jnp.logical_not(last_of_seq & (row == FEAT_CHUNKS - 1))

    def piece(buf):
        def get(c, g):
            parts = [buf[slot, pl.ds(n * CHUNK_ROWS + c * N_KV + g, FEAT_PAGES, stride=FEAT_PITCH), :]
                     for n in range(CHUNKS_PER_PAGE)]
            parts.append(buf[slot, pl.ds(FEAT_PAGES * FEAT_PITCH + c * N_KV + g, 8, stride=8), :])
            return jnp.concatenate(parts, axis=0)
        return get

    @pl.when(pl.program_id(0) == 0)
    def _():
        for n, (pe_ref, w1_ref) in enumerate(((pek_ref, wk1_ref), (pev_ref, wv1_ref))):
            bias_sc[n] = _dot(pe_ref[...].astype(BF16), w1_ref[...])

    for kv, (buf, w1_ref, w2_ref, g_ref, o_ref) in enumerate(((kbuf, wk1_ref, wk2_ref, gk_ref, ok_ref),
                                                              (vbuf, wv1_ref, wv2_ref, None, ov_ref))):
        f = _chunk_feats(piece(buf), w1_ref[...])
        for cb in range(N_FEAT):
            lanes = slice(cb * HEAD_DIM, (cb + 1) * HEAD_DIM)
            for n in range(CHUNKS_PER_PAGE):
                fsc[cb, pl.ds(n, FEAT_PAGES, stride=CHUNKS_PER_PAGE), :] = f[n * FEAT_PAGES:(n + 1) * FEAT_PAGES, lanes]
            fsc[cb, FEAT_CHUNKS:FEAT_CHUNKS + 8, :] = f[FEAT_CHUNKS:FEAT_CHUNKS + 8, lanes]
        ba = bias_sc[kv, 0:1, 0:HEAD_DIM]
        bb = bias_sc[kv, 1:2, HEAD_DIM:2 * HEAD_DIM]
        for g in range(N_KV):
            fa = fsc[2 * g, 0:FEAT_CHUNKS, :] + ba
            fb_next = fsc[2 * g + 1, 1:FEAT_CHUNKS + 1, :] + bb
            o = _dot(_silu(fa + fb_next).astype(BF16), w2_ref[...])
            if g_ref is not None:
                o = _rms(o, g_ref[...])
            o_ref[:, g * HEAD_DIM:(g + 1) * HEAD_DIM] = jnp.where(keep, o, 0.0)


def _cmp_sample(page_table, cache_k, cache_v, k_weights, gain_k, v_weights):
    n_seq, pages_per_seq = page_table.shape
    n_pages = page_table.size
    n_phys = cache_k.shape[0]
    ck = cache_k.reshape(n_phys, PAGE_ROWS, HEAD_DIM)
    cv = cache_v.reshape(n_phys, PAGE_ROWS, HEAD_DIM)
    brows = (FEAT_PAGES + 1) * FEAT_PITCH
    full = lambda a: pl.BlockSpec(a.shape, lambda s, pt: (0, 0))
    ospec = pl.BlockSpec((FEAT_CHUNKS, N_KV * HEAD_DIM), lambda s, pt: (s, 0))
    gk = gain_k.reshape(1, -1)
    return pl.pallas_call(
        functools.partial(_cmp_sample_kernel, steps_per_seq=pages_per_seq // FEAT_PAGES),
        out_shape=(jax.ShapeDtypeStruct((n_pages * CHUNKS_PER_PAGE, N_KV * HEAD_DIM), F32),) * 2,
        grid_spec=pltpu.PrefetchScalarGridSpec(
            num_scalar_prefetch=1,
            grid=(n_pages // FEAT_PAGES,),
            in_specs=[pl.BlockSpec(memory_space=pl.ANY), pl.BlockSpec(memory_space=pl.ANY)]
            + [full(a) for a in k_weights] + [full(gk)] + [full(a) for a in v_weights],
            out_specs=(ospec, ospec),
            scratch_shapes=[pltpu.VMEM((GATHER_SLOTS, brows, HEAD_DIM), F32),
                            pltpu.VMEM((GATHER_SLOTS, brows, HEAD_DIM), F32),
                            pltpu.SemaphoreType.DMA((2, GATHER_SLOTS)),
                            pltpu.VMEM((N_FEAT, FEAT_CHUNKS + 8, HEAD_DIM), F32),
                            pltpu.VMEM((2, 8, 2 * HEAD_DIM), F32)],
        ),
        compiler_params=pltpu.CompilerParams(dimension_semantics=("arbitrary",), vmem_limit_bytes=VMEM_LIMIT),
        name="cmp_sample",
    )(page_table.reshape(-1), ck, cv, *k_weights, gk, *v_weights)


def _cmp_finish_one(f_ref, w1_ref, pe_ref, w2_ref, g_ref, o_ref, nc):
    bias = _dot(pe_ref[...].astype(BF16), w1_ref[...])
    ba = bias[0:1, 0:HEAD_DIM]
    bb = bias[1:2, HEAD_DIM:2 * HEAD_DIM]
    row = _iota((nc, 1), 0)
    for g in range(N_KV):
        fa = f_ref[2 * g] + ba
        fb = f_ref[2 * g + 1] + bb
        hid = _silu(fa + pltpu.roll(fb, nc - 1, 0))
        o = _dot(hid.astype(BF16), w2_ref[...])
        if g_ref is not None:
            o = _rms(o, g_ref[...])
        o_ref[:, g * HEAD_DIM:(g + 1) * HEAD_DIM] = jnp.where(row < nc - 1, o, 0.0)


def _cmp_finish_kernel(fk_ref, fv_ref, wk1_ref, pek_ref, wk2_ref, gk_ref, wv1_ref, pev_ref, wv2_ref,
                       ok_ref, ov_ref, *, nc):
    _cmp_finish_one(fk_ref, wk1_ref, pek_ref, wk2_ref, gk_ref, ok_ref, nc)
    _cmp_finish_one(fv_ref, wv1_ref, pev_ref, wv2_ref, None, ov_ref, nc)


def _cmp_finish(fk, fv, k_weights, gain_k, v_weights, nc):
    nb = fk.shape[1] // nc
    full = lambda a: pl.BlockSpec(a.shape, lambda b: (0, 0))
    fspec = pl.BlockSpec((N_FEAT, nc, HEAD_DIM), lambda b: (0, b, 0))
    ospec = pl.BlockSpec((nc, N_KV * HEAD_DIM), lambda b: (b, 0))
    gk = gain_k.reshape(1, -1)
    return pl.pallas_call(
        functools.partial(_cmp_finish_kernel, nc=nc),
        out_shape=(jax.ShapeDtypeStruct((nb * nc, N_KV * HEAD_DIM), F32),) * 2,
        grid=(nb,),
        in_specs=[fspec, fspec] + [full(a) for a in k_weights] + [full(gk)] + [full(a) for a in v_weights],
        out_specs=(ospec, ospec),
        compiler_params=pltpu.CompilerParams(dimension_semantics=("arbitrary",), vmem_limit_bytes=VMEM_LIMIT),
        name="cmp_finish",
    )(fk, fv, *k_weights, gk, *v_weights)


SELECT_LANES = 128


def _select_kernel(imp_ref, cur_ref, sel_ref, *any_refs, any_rows):
    n_rows, n_blk = imp_ref.shape
    shape = (n_blk, SELECT_LANES)
    j = _iota(shape, 0)
    jf = j.astype(F32)

    def pick(score, sel_t):
        m = jnp.max(score, axis=0, keepdims=True)
        first = jnp.min(jnp.where(score == m, jf, 1e9), axis=0, keepdims=True)
        hit = jf == first
        return jnp.where(hit, -2.0, score), jnp.where(hit, 1.0, sel_t)

    def body(_, carry):
        return pick(*carry[0:2]) + pick(*carry[2:4])

    def start(c):
        rows = slice(c * SELECT_LANES, (c + 1) * SELECT_LANES)
        imp_t = imp_ref[rows, :].T
        cur = cur_ref[:, rows]
        forced = (j == 0) | (j == cur) | (j == cur - 1)
        return jnp.where(forced, 1e30, jnp.where(j <= cur, imp_t, -1.0)), jnp.zeros(shape, F32)

    for c in range(0, n_rows // SELECT_LANES, 2):
        out = lax.fori_loop(0, N_SELECT, body, start(c) + start(c + 1))
        sel_ref[c * SELECT_LANES:(c + 1) * SELECT_LANES, :] = out[1].T
        sel_ref[(c + 1) * SELECT_LANES:(c + 2) * SELECT_LANES, :] = out[3].T
    sel = sel_ref[...]
    if any_refs:
        for n in range(sel.shape[0] // any_rows):
            blk = jnp.max(sel[n * any_rows:(n + 1) * any_rows], axis=0, keepdims=True)
            any_refs[0][n * 8:(n + 1) * 8, :] = jnp.broadcast_to(blk, (8, sel.shape[1]))


def _select(imp, cur, n_lanes, tr, any_rows=None):
    r, w = imp.shape
    groups = w // n_lanes
    out_shape = [jax.ShapeDtypeStruct((r, w), F32)]
    out_specs = [pl.BlockSpec((tr, n_lanes), lambda i, g: (i, g))]
    if any_rows is not None:
        out_shape.append(jax.ShapeDtypeStruct((r // any_rows * 8, w), F32))
        out_specs.append(pl.BlockSpec((tr // any_rows * 8, n_lanes), lambda i, g: (i, g)))
    return pl.pallas_call(
        functools.partial(_select_kernel, any_rows=any_rows),
        out_shape=tuple(out_shape),
        grid=(r // tr, groups),
        in_specs=[pl.BlockSpec((tr, n_lanes), lambda i, g: (i, g)), pl.BlockSpec((1, tr), lambda i, g: (0, i))],
        out_specs=tuple(out_specs),
        compiler_params=pltpu.CompilerParams(
            dimension_semantics=("arbitrary", "arbitrary"), vmem_limit_bytes=VMEM_LIMIT),
        name="select",
    )(imp, cur)


def _imp_matrix(n_cmp_pad, n_sel_pad, n_cmp, n_sel):
    r = SEL_BLOCK // CMP_STRIDE
    lead = CMP_BLOCK // CMP_STRIDE - 1
    m = np.zeros((n_cmp_pad, n_sel_pad), np.float32)
    for jb in range(n_sel):
        for o in range(-lead, r):
            start = o * CMP_STRIDE
            w = (min(start + CMP_BLOCK, SEL_BLOCK) - max(start, 0)) / CMP_BLOCK
            i = r * jb + o
            if 0 <= i < n_cmp:
                m[i, jb] += w
    return jnp.asarray(m)


CMP_KEY_STEP = 128


def _cmp_select_tile(q_ref, kc_ref, vc_ref, m_ref, oc_ref, imp_ref, i, tq, nk):
    qpos = i * tq + _iota((tq, 1), 0)
    end = _iota((1, nk), 1) * CMP_STRIDE + (CMP_BLOCK - 1)
    dist = qpos - end
    mask = dist >= 0
    distf = dist.astype(F32)
    for g in range(N_KV):
        kg = kc_ref[0:nk, g * HEAD_DIM:(g + 1) * HEAD_DIM].astype(BF16)
        vg = vc_ref[0:nk, g * HEAD_DIM:(g + 1) * HEAD_DIM].astype(BF16)
        psum = jnp.zeros((tq, nk), F32)
        for r in range(GROUP):
            h = g * GROUP + r
            qh = q_ref[:, h * HEAD_DIM:(h + 1) * HEAD_DIM].astype(BF16)
            s = _dot_nt(qh, kg) * SCALE - SLOPES[h] * distf
            s = jnp.where(mask, s, NEG)
            m = jnp.max(s, axis=-1, keepdims=True)
            e = jnp.where(mask, jnp.exp(s - m), 0.0)
            p = e * (1.0 / jnp.maximum(jnp.sum(e, axis=-1, keepdims=True), 1e-30))
            psum = psum + p
            oc_ref[:, h * HEAD_DIM:(h + 1) * HEAD_DIM] = _dot(p.astype(BF16), vg)
        imp_ref[:, g * 128:(g + 1) * 128] = jnp.dot(psum, m_ref[0:nk, :], precision=lax.Precision.HIGHEST,
                                                    preferred_element_type=F32)


def _cmp_select_kernel(q_ref, kc_ref, vc_ref, m_ref, oc_ref, imp_ref, *, tq, nc):
    i = pl.program_id(0)
    reach = (i * tq + tq - 1 - (CMP_BLOCK - 1)) // CMP_STRIDE + 1
    for nk in range(CMP_KEY_STEP, nc + 1, CMP_KEY_STEP):
        covers = reach <= nk if nk == CMP_KEY_STEP else (reach > nk - CMP_KEY_STEP) & (reach <= nk)
        if nk == nc:
            covers = reach > nk - CMP_KEY_STEP

        @pl.when(covers)
        def _(nk=nk):
            _cmp_select_tile(q_ref, kc_ref, vc_ref, m_ref, oc_ref, imp_ref, i, tq, nk)


def _cmp_select(z, kcmp, vcmp, imp_m, tq):
    t = z.shape[0]
    nc = kcmp.shape[0]
    n_sel = imp_m.shape[1]
    assert n_sel == 128
    full = lambda shape: pl.BlockSpec(shape, lambda i: (0, 0))
    return pl.pallas_call(
        functools.partial(_cmp_select_kernel, tq=tq, nc=nc),
        out_shape=(jax.ShapeDtypeStruct((t, N_HEADS * HEAD_DIM), F32), jax.ShapeDtypeStruct((t, N_KV * 128), F32)),
        grid=(t // tq,),
        in_specs=[pl.BlockSpec((tq, N_HEADS * HEAD_DIM), lambda i: (i, COL_Q // (N_HEADS * HEAD_DIM))),
                  full(kcmp.shape), full(vcmp.shape), full(imp_m.shape)],
        out_specs=(pl.BlockSpec((tq, N_HEADS * HEAD_DIM), lambda i: (i, 0)),
                   pl.BlockSpec((tq, N_KV * 128), lambda i: (i, 0))),
        compiler_params=pltpu.CompilerParams(dimension_semantics=("arbitrary",), vmem_limit_bytes=VMEM_LIMIT),
        name="cmp_select",
    )(z, kcmp, vcmp, imp_m)


def _flash_update(s, v_bf, m_ref, l_ref, acc_ref, rows):
    m_old = m_ref[rows, :]
    m_new = jnp.maximum(m_old, jnp.max(s, axis=-1, keepdims=True))
    alpha = jnp.exp(m_old - m_new)
    p = jnp.exp(s - m_new)
    l_ref[rows, :] = alpha * l_ref[rows, :] + jnp.sum(p, axis=-1, keepdims=True)
    acc_ref[rows, :] = alpha * acc_ref[rows, :] + _dot(p.astype(BF16), v_bf)
    m_ref[rows, :] = m_new


V_ROWS = 2 * HEAD_DIM


def _selected_prompt_kernel(flag_ref, q_ref, k_ref, v_ref, sel_ref, et_ref, o_ref,
                            kbf, vaug, qs, m_sc, acc_sc, *, tq, tk):
    g = pl.program_id(0)
    qi = pl.program_id(1)
    nkt = kbf.shape[0]
    step = g * pl.num_programs(1) + qi

    @pl.when(qi == 0)
    def _():
        ones_row = jnp.where(_iota((V_ROWS - HEAD_DIM, tk), 0) == 0, 1.0, 0.0).astype(BF16)
        for ki in range(nkt):
            rows = slice(ki * tk, (ki + 1) * tk)
            kbf[ki] = k_ref[rows, :].astype(BF16)
            vaug[ki, 0:HEAD_DIM, :] = v_ref[rows, :].T.astype(BF16)
            vaug[ki, HEAD_DIM:V_ROWS, :] = ones_row

    for r in range(GROUP):
        qs[r] = (q_ref[:, r * HEAD_DIM:(r + 1) * HEAD_DIM] * (SCALE * LOG2E)).astype(BF16)
    m_sc[...] = jnp.full(m_sc.shape, M_INIT, F32)
    acc_sc[...] = jnp.zeros(acc_sc.shape, F32)
    sel_bf = sel_ref[...].astype(BF16)
    qpos = qi * tq + _iota((1, tq), 1)

    def body(ki, carry):
        @pl.when(flag_ref[step * nkt + ki] > 0)
        def _():
            k = kbf[ki]
            va = vaug[ki]
            sel_t = _dot_nt(et_ref[pl.ds(pl.multiple_of(ki * tk, tk), tk), :], sel_bf)
            kpos = ki * tk + _iota((tk, tq), 0)
            mb = jnp.where((sel_t > 0.5) & (kpos <= qpos), 0.0, NEG)
            krel = (kpos - qi * tq).astype(F32)
            for r in range(GROUP):
                slope = jnp.where(g == 0, SLOPES[r], SLOPES[GROUP + r]) * LOG2E
                s = _dot_nt(k, qs[r]) + (mb + slope * krel)
                m_old = m_sc[r]
                m_new = jnp.maximum(m_old, jnp.max(s, axis=0, keepdims=True))
                p = jnp.exp2(s - m_new).astype(BF16)
                acc_sc[r] = acc_sc[r] * jnp.exp2(m_old - m_new) + _dot(va, p)
                m_sc[r] = m_new
        return carry

    lax.fori_loop(0, (qi * tq + tq - 1) // tk + 1, body, 0)
    for r in range(GROUP):
        o_t = acc_sc[r, 0:HEAD_DIM, :] * (1.0 / acc_sc[r, HEAD_DIM:HEAD_DIM + 1, :])
        o_ref[:, r * HEAD_DIM:(r + 1) * HEAD_DIM] = o_t.T


def _tile_flags(anyb, t, tq, tk):
    nq, nkt = t // tq, t // tk
    a = anyb.reshape(nq, 8, N_KV, nkt, tk // SEL_BLOCK)[:, 0]
    return jnp.transpose(jnp.max(a, axis=-1) > 0.5, (1, 0, 2)).astype(jnp.int32).reshape(-1)


def _selected_prompt(z, sel, anyb, tq, tk=512):
    t = z.shape[0]
    gw = GROUP * HEAD_DIM
    flags = _tile_flags(anyb, t, tq, tk)
    et = jnp.asarray(np.arange(t)[:, None] // SEL_BLOCK == np.arange(128)[None, :], dtype=BF16)
    return pl.pallas_call(
        functools.partial(_selected_prompt_kernel, tq=tq, tk=tk),
        out_shape=jax.ShapeDtypeStruct((t, N_HEADS * HEAD_DIM), F32),
        grid_spec=pltpu.PrefetchScalarGridSpec(
            num_scalar_prefetch=1,
            grid=(N_KV, t // tq),
            in_specs=[
                pl.BlockSpec((tq, gw), lambda g, qi, f: (qi, COL_Q // gw + g)),
                pl.BlockSpec((t, HEAD_DIM), lambda g, qi, f: (0, COL_KS // HEAD_DIM + g)),
                pl.BlockSpec((t, HEAD_DIM), lambda g, qi, f: (0, COL_VS // HEAD_DIM + g)),
                pl.BlockSpec((tq, 128), lambda g, qi, f: (qi, g)),
                pl.BlockSpec((t, 128), lambda g, qi, f: (0, 0)),
            ],
            out_specs=pl.BlockSpec((tq, gw), lambda g, qi, f: (qi, g)),
            scratch_shapes=[pltpu.VMEM((t // tk, tk, HEAD_DIM), BF16), pltpu.VMEM((t // tk, V_ROWS, tk), BF16),
                            pltpu.VMEM((GROUP, tq, HEAD_DIM), BF16), pltpu.VMEM((GROUP, 1, tq), F32),
                            pltpu.VMEM((GROUP, V_ROWS, tq), F32)],
        ),
        compiler_params=pltpu.CompilerParams(
            dimension_semantics=("arbitrary", "arbitrary"), vmem_limit_bytes=VMEM_LIMIT),
        name="selected_prompt",
    )(flags, z, z, z, sel, et)


def _window_prompt_kernel(q_ref, kp_ref, kc_ref, vp_ref, vc_ref, o_ref, *, tq):
    g = pl.program_id(0)
    qi = pl.program_id(1)
    k = jnp.concatenate([kp_ref[...], kc_ref[...]], axis=0).astype(BF16)
    v = jnp.concatenate([vp_ref[...], vc_ref[...]], axis=0).astype(BF16)
    krel = _iota((1, 2 * tq), 1) - tq
    dist = _iota((tq, 1), 0) - krel
    mask = (dist >= 0) & (dist < WINDOW) & ((krel >= 0) | (qi > 0))
    distf = dist.astype(F32)
    for r in range(GROUP):
        slope = jnp.where(g == 0, SLOPES[r], SLOPES[GROUP + r])
        qh = q_ref[:, r * HEAD_DIM:(r + 1) * HEAD_DIM].astype(BF16)
        s = jnp.where(mask, _dot_nt(qh, k) * SCALE - slope * distf, NEG)
        m = jnp.max(s, axis=-1, keepdims=True)
        e = jnp.where(mask, jnp.exp(s - m), 0.0)
        p = e * (1.0 / jnp.maximum(jnp.sum(e, axis=-1, keepdims=True), 1e-30))
        o_ref[:, r * HEAD_DIM:(r + 1) * HEAD_DIM] = _dot(p.astype(BF16), v)


def _window_prompt(z, tq=WINDOW):
    t = z.shape[0]
    gw = GROUP * HEAD_DIM
    prev = lambda c: pl.BlockSpec((tq, HEAD_DIM), lambda g, qi: (jnp.maximum(qi - 1, 0), c // HEAD_DIM + g))
    cur = lambda c: pl.BlockSpec((tq, HEAD_DIM), lambda g, qi: (qi, c // HEAD_DIM + g))
    return pl.pallas_call(
        functools.partial(_window_prompt_kernel, tq=tq),
        out_shape=jax.ShapeDtypeStruct((t, N_HEADS * HEAD_DIM), F32),
        grid=(N_KV, t // tq),
        in_specs=[pl.BlockSpec((tq, gw), lambda g, qi: (qi, COL_Q // gw + g)),
                  prev(COL_KW), cur(COL_KW), prev(COL_VW), cur(COL_VW)],
        out_specs=pl.BlockSpec((tq, gw), lambda g, qi: (qi, g)),
        compiler_params=pltpu.CompilerParams(
            dimension_semantics=("arbitrary", "arbitrary"), vmem_limit_bytes=VMEM_LIMIT),
        name="window_prompt",
    )(z, z, z, z, z)


SEL_PAD = 384


def _slope_col(g, ds):
    return jnp.concatenate([jnp.full((ds, 1), SLOPES[g * GROUP + r], F32) for r in range(GROUP)], axis=0)


def _stack_heads(q_ref, g):
    return jnp.concatenate([q_ref[:, (g * GROUP + r) * HEAD_DIM:(g * GROUP + r + 1) * HEAD_DIM]
                            for r in range(GROUP)], axis=0).astype(BF16)


def _pad_rows(a, n):
    return jnp.concatenate([a, jnp.zeros((n - a.shape[0], a.shape[1]), a.dtype)], axis=0)


def _sample_small_kernel(q_ref, kc_ref, vc_ref, m_ref, sk_ref, sv_ref, nk_ref, nv_ref,
                         oc_ref, imp_ref, ow_ref, ko_ref, vo_ref, *, ds, nc, past, wb):
    rows = GROUP * ds
    qidx = _iota((rows, 1), 0) & (ds - 1)
    spos = past + qidx
    end = _iota((1, nc), 1) * CMP_STRIDE + (CMP_BLOCK - 1)
    dist_c = spos - end
    mask_c = dist_c >= 0
    ist = _iota((1, wb), 1)
    dist_s = wb + qidx - ist
    mask_s = dist_s < WINDOW
    jn = _iota((1, 128), 1)
    dist_n = qidx - jn
    mask_n = (dist_n >= 0) & (jn < ds)
    for g in range(N_KV):
        lanes = slice(g * HEAD_DIM, (g + 1) * HEAD_DIM)
        qs = _stack_heads(q_ref, g)
        slope = _slope_col(g, ds)
        s = _dot_nt(qs, kc_ref[:, lanes].astype(BF16)) * SCALE - slope * dist_c.astype(F32)
        s = jnp.where(mask_c, s, NEG)
        m = jnp.max(s, axis=-1, keepdims=True)
        e = jnp.where(mask_c, jnp.exp(s - m), 0.0)
        p = e * (1.0 / jnp.maximum(jnp.sum(e, axis=-1, keepdims=True), 1e-30))
        o = _dot(p.astype(BF16), vc_ref[:, lanes].astype(BF16))
        psum = p[0:ds]
        for r in range(1, GROUP):
            psum = psum + p[r * ds:(r + 1) * ds]
        imp_ref[:, g * SEL_PAD:(g + 1) * SEL_PAD] = jnp.dot(psum, m_ref[...], precision=lax.Precision.HIGHEST,
                                                            preferred_element_type=F32)
        kst = sk_ref[pl.ds(g, wb, stride=N_KV), :].astype(BF16)
        vst = sv_ref[pl.ds(g, wb, stride=N_KV), :].astype(BF16)
        kn = _pad_rows(nk_ref[pl.ds(g, ds, stride=N_KV), :], 128).astype(BF16)
        vn = _pad_rows(nv_ref[pl.ds(g, ds, stride=N_KV), :], 128).astype(BF16)
        s1 = jnp.where(mask_s, _dot_nt(qs, kst) * SCALE - slope * dist_s.astype(F32), NEG)
        s2 = jnp.where(mask_n, _dot_nt(qs, kn) * SCALE - slope * dist_n.astype(F32), NEG)
        mw = jnp.maximum(jnp.max(s1, axis=-1, keepdims=True), jnp.max(s2, axis=-1, keepdims=True))
        e1 = jnp.where(mask_s, jnp.exp(s1 - mw), 0.0)
        e2 = jnp.where(mask_n, jnp.exp(s2 - mw), 0.0)
        inv = 1.0 / jnp.maximum(jnp.sum(e1, axis=-1, keepdims=True) + jnp.sum(e2, axis=-1, keepdims=True), 1e-30)
        w = _dot((e1 * inv).astype(BF16), vst) + _dot((e2 * inv).astype(BF16), vn)
        for r in range(GROUP):
            h = g * GROUP + r
            oc_ref[:, h * HEAD_DIM:(h + 1) * HEAD_DIM] = o[r * ds:(r + 1) * ds]
            ow_ref[:, h * HEAD_DIM:(h + 1) * HEAD_DIM] = w[r * ds:(r + 1) * ds]
    keep = (wb - ds) * N_KV
    for s_ref, n_ref, o_ref in ((sk_ref, nk_ref, ko_ref), (sv_ref, nv_ref, vo_ref)):
        o_ref[0:keep, :] = s_ref[ds * N_KV:wb * N_KV, :]
        o_ref[keep:wb * N_KV, :] = n_ref[...]


def _sample_small(z2, kw2, vw2, kcmp2, vcmp2, imp_m2, state_k, state_v, nb, past):
    t = z2.shape[0]
    ds = t // nb
    nc = kcmp2.shape[0] // nb
    wb = state_k.shape[0] // (nb * N_KV)
    qw = N_HEADS * HEAD_DIM
    kvw = N_KV * HEAD_DIM
    rows = lambda n: pl.BlockSpec((n * N_KV, HEAD_DIM), lambda b: (b, 0))
    return pl.pallas_call(
        functools.partial(_sample_small_kernel, ds=ds, nc=nc, past=past, wb=wb),
        out_shape=(jax.ShapeDtypeStruct((t, qw), F32), jax.ShapeDtypeStruct((t, N_KV * SEL_PAD), F32),
                   jax.ShapeDtypeStruct((t, qw), F32),
                   jax.ShapeDtypeStruct(state_k.shape, F32), jax.ShapeDtypeStruct(state_v.shape, F32)),
        grid=(nb,),
        in_specs=[pl.BlockSpec((ds, qw), lambda b: (b, COL_Q // qw)),
                  pl.BlockSpec((nc, kvw), lambda b: (b, 0)), pl.BlockSpec((nc, kvw), lambda b: (b, 0)),
                  pl.BlockSpec(imp_m2.shape, lambda b: (0, 0)),
                  rows(wb), rows(wb), rows(ds), rows(ds)],
        out_specs=(pl.BlockSpec((ds, qw), lambda b: (b, 0)), pl.BlockSpec((ds, N_KV * SEL_PAD), lambda b: (b, 0)),
                   pl.BlockSpec((ds, qw), lambda b: (b, 0)), rows(wb), rows(wb)),
        compiler_params=pltpu.CompilerParams(dimension_semantics=("arbitrary",), vmem_limit_bytes=VMEM_LIMIT),
        name="sample_small",
    )(z2, kcmp2, vcmp2, imp_m2, state_k, state_v, kw2, vw2)


SLC_PAGES = 32
SLC_KEYS = SLC_PAGES * PAGE_SIZE
SLC_BLOCKS = SLC_KEYS // SEL_BLOCK
WIN_STEPS = 128 // SLC_BLOCKS


def _expand_matrix():
    j = np.arange(128)[:, None]
    c = np.arange(SLC_KEYS)[None, :]
    e = np.concatenate([(j == w * SLC_BLOCKS + c // SEL_BLOCK) for w in range(WIN_STEPS)], axis=0)
    return jnp.asarray(e, dtype=BF16)


def _selected_sample_kernel(pt_ref, ck_hbm, cv_hbm, q_ref, sel0_ref, sel1_ref, e_ref, nk_ref, nv_ref, o_ref,
                            kbuf, vbuf, sem, m_sc, l_sc, acc_sc, *, ds, steps, past):
    slot = _gather_pipeline(pt_ref, (ck_hbm, cv_hbm), (kbuf, vbuf), sem, SLC_PAGES, PAGE_ROWS)
    kt = pl.program_id(0) % steps
    rows = GROUP * ds
    nk = SLC_KEYS
    qidx = _iota((rows, 1), 0) & (ds - 1)

    @pl.when(kt == 0)
    def _():
        m_sc[...] = jnp.full(m_sc.shape, M_INIT, F32)
        l_sc[...] = jnp.zeros(l_sc.shape, F32)
        acc_sc[...] = jnp.zeros(acc_sc.shape, F32)

    krel = (kt * nk - past + _iota((1, nk), 1)).astype(F32)
    e = e_ref[pl.ds(pl.multiple_of((kt % WIN_STEPS) * 128, 128), 128), :]
    for g, sel_ref in enumerate((sel0_ref, sel1_ref)):
        grows = slice(g * rows, (g + 1) * rows)
        qs = _stack_heads(q_ref, g)
        slope = _slope_col(g, ds)
        selexp = _dot(sel_ref[...].astype(BF16), e)
        mb = jnp.where(jnp.concatenate([selexp] * GROUP, axis=0) > 0.5, 0.0, NEG)
        kg = kbuf[slot, pl.ds(g, nk, stride=N_KV), :].astype(BF16)
        vg = vbuf[slot, pl.ds(g, nk, stride=N_KV), :].astype(BF16)
        s = _dot_nt(qs, kg) * SCALE + (mb + slope * krel)
        _flash_update(s, vg, m_sc, l_sc, acc_sc, grows)

    @pl.when(kt == steps - 1)
    def _():
        jn = _iota((1, 128), 1)
        mb_n = jnp.where((jn <= qidx) & (jn < ds), 0.0, NEG)
        for g in range(N_KV):
            grows = slice(g * rows, (g + 1) * rows)
            qs = _stack_heads(q_ref, g)
            kn = _pad_rows(nk_ref[pl.ds(g, ds, stride=N_KV), :], 128).astype(BF16)
            vn = _pad_rows(nv_ref[pl.ds(g, ds, stride=N_KV), :], 128).astype(BF16)
            s = _dot_nt(qs, kn) * SCALE + (mb_n + _slope_col(g, ds) * jn.astype(F32))
            _flash_update(s, vn, m_sc, l_sc, acc_sc, grows)
            o = acc_sc[grows, :] * (1.0 / l_sc[grows, :])
            for r in range(GROUP):
                h = g * GROUP + r
                o_ref[:, h * HEAD_DIM:(h + 1) * HEAD_DIM] = o[r * ds:(r + 1) * ds]


def _selected_sample(page_table, cache_k, cache_v, z2, ks2, vs2, sel2, nb, past):
    t = z2.shape[0]
    ds = t // nb
    n_phys = cache_k.shape[0]
    ck = cache_k.reshape(n_phys, PAGE_ROWS, HEAD_DIM)
    cv = cache_v.reshape(n_phys, PAGE_ROWS, HEAD_DIM)
    steps = page_table.shape[1] // SLC_PAGES
    qw = N_HEADS * HEAD_DIM
    rows = N_KV * GROUP * ds
    e = _expand_matrix()
    win = lambda g: pl.BlockSpec(
        (ds, 128), lambda s, pt: (s // steps, g * (SEL_PAD // 128) + (s % steps) // WIN_STEPS))
    new = pl.BlockSpec((ds * N_KV, HEAD_DIM), lambda s, pt: (s // steps, 0))
    return pl.pallas_call(
        functools.partial(_selected_sample_kernel, ds=ds, steps=steps, past=past),
        out_shape=jax.ShapeDtypeStruct((t, qw), F32),
        grid_spec=pltpu.PrefetchScalarGridSpec(
            num_scalar_prefetch=1,
            grid=(nb * steps,),
            in_specs=[pl.BlockSpec(memory_space=pl.ANY), pl.BlockSpec(memory_space=pl.ANY),
                      pl.BlockSpec((ds, qw), lambda s, pt: (s // steps, COL_Q // qw)),
                      win(0), win(1), pl.BlockSpec(e.shape, lambda s, pt: (0, 0)), new, new],
            out_specs=pl.BlockSpec((ds, qw), lambda s, pt: (s // steps, 0)),
            scratch_shapes=[pltpu.VMEM((GATHER_SLOTS, SLC_PAGES * PAGE_ROWS, HEAD_DIM), F32),
                            pltpu.VMEM((GATHER_SLOTS, SLC_PAGES * PAGE_ROWS, HEAD_DIM), F32),
                            pltpu.SemaphoreType.DMA((2, GATHER_SLOTS)),
                            pltpu.VMEM((rows, 1), F32), pltpu.VMEM((rows, 1), F32),
                            pltpu.VMEM((rows, HEAD_DIM), F32)],
        ),
        compiler_params=pltpu.CompilerParams(dimension_semantics=("arbitrary",), vmem_limit_bytes=VMEM_LIMIT),
        name="selected_sample",
    )(page_table.reshape(-1), ck, cv, z2, sel2, sel2, e, ks2, vs2)


def _merge_kernel(x_ref, conv_ref, oc_ref, os_ref, ow_ref, gt_ref, zn_ref, w_ref, y_ref):
    gt = gt_ref[...]
    parts = [conv_ref[...].astype(BF16)]
    for h in range(N_HEADS):
        lanes = slice(h * HEAD_DIM, (h + 1) * HEAD_DIM)
        o = (gt[:, 3 * h:3 * h + 1] * oc_ref[:, lanes] + gt[:, 3 * h + 1:3 * h + 2] * os_ref[:, lanes]
             + gt[:, 3 * h + 2:3 * h + 3] * ow_ref[:, lanes])
        parts.append((o * _silu(zn_ref[:, lanes])).astype(BF16))
    y_ref[...] = x_ref[...] + _dot(jnp.concatenate(parts, axis=1), w_ref[...])


def _merge(x, conv_o, o_c, o_s, o_w, z, w_out_bf, tm=256):
    t, d = x.shape
    qw = N_HEADS * HEAD_DIM
    rowblk = lambda w: pl.BlockSpec((tm, w), lambda i: (i, 0))
    return pl.pallas_call(
        _merge_kernel,
        out_shape=jax.ShapeDtypeStruct((t, d), F32),
        grid=(t // tm,),
        in_specs=[rowblk(d), rowblk(C_CONV), rowblk(qw), rowblk(qw), rowblk(qw),
                  pl.BlockSpec((tm, 128), lambda i: (i, COL_GT // 128)),
                  pl.BlockSpec((tm, qw), lambda i: (i, COL_ZN // qw)),
                  pl.BlockSpec(w_out_bf.shape, lambda i: (0, 0))],
        out_specs=rowblk(d),
        compiler_params=pltpu.CompilerParams(dimension_semantics=("arbitrary",), vmem_limit_bytes=VMEM_LIMIT),
        name="merge",
    )(x, conv_o, o_c, o_s, o_w, z, z, w_out_bf)


N_GT = 3 * N_HEADS


def _pad_w_kernel(wt_hbm, o_ref, buf, sem):
    j = pl.program_id(0)
    slot = j % 2

    def tile_copy(jj, sl):
        src = jnp.where(jj <= TILE_GT, jj * TN, jj * TN - COL_ZN + COL_GT + N_GT)
        return pltpu.make_async_copy(wt_hbm.at[pl.ds(pl.multiple_of(src, 8), TN)], buf.at[sl], sem.at[sl])

    @pl.when(j == 0)
    def _():
        tile_copy(0, 0).start()

    @pl.when(j + 1 < pl.num_programs(0))
    def _():
        tile_copy(j + 1, 1 - slot).start()

    tile_copy(j, slot).wait()
    w = buf[slot]
    keep = (j != TILE_GT) | (_iota((TN, 1), 0) < N_GT)
    o_ref[...] = jnp.where(keep, w, 0.0).T.astype(BF16)


def _pad_w_in(w_in):
    d, n = w_in.shape
    assert n == COL_GT + N_GT + ZW - COL_ZN and COL_ZN == (TILE_GT + 1) * TN
    return pl.pallas_call(
        _pad_w_kernel,
        out_shape=jax.ShapeDtypeStruct((d, ZW), BF16),
        grid=(N_TILES,),
        in_specs=[pl.BlockSpec(memory_space=pl.ANY)],
        out_specs=pl.BlockSpec((d, TN), lambda j: (0, j)),
        scratch_shapes=[pltpu.VMEM((2, TN, d), F32), pltpu.SemaphoreType.DMA((2,))],
        compiler_params=pltpu.CompilerParams(dimension_semantics=("arbitrary",), vmem_limit_bytes=VMEM_LIMIT),
        name="pad_w_in",
    )(w_in.T)


def _cmp_weights(pe, w1, w2):
    half = CMP_STRIDE
    k = half * HEAD_DIM
    w1cat = jnp.concatenate([w1[:half].reshape(k, -1), w1[half:].reshape(k, -1)], axis=1).astype(BF16)
    pe2 = jnp.concatenate([pe[:half].reshape(1, k), pe[half:].reshape(1, k), jnp.zeros((6, k), pe.dtype)], axis=0)
    return w1cat, pe2, w2.astype(BF16)


def kernel(x_prompt, x_sample, cache_k_cmp, cache_v_cmp, cache_k_slc, cache_v_slc, state_k_win, state_v_win,
           state_conv, page_table, g_norm, w_in, pe_cmp_k, w_cmp_k1, w_cmp_k2, pe_cmp_v, w_cmp_v1, w_cmp_v2,
           g_q, g_k_cmp, g_k_slc, g_k_win, w_dw, b_dw, ln_g, ln_b, w_pw2, b_pw2, w_out):
    _, t, d = x_prompt.shape
    db, ds, _ = x_sample.shape
    past = page_table.shape[1] * PAGE_SIZE
    wb = state_k_win.shape[1]
    kvw = N_KV * HEAD_DIM
    assert ds < CMP_STRIDE and wb == WINDOW and t % SEL_BLOCK == 0 and t // SEL_BLOCK == 128

    w_p = _pad_w_in(w_in)
    w_pw_bf = w_pw2.astype(BF16)
    w_out_bf = w_out.astype(BF16)
    wk1, pek, wk2 = _cmp_weights(pe_cmp_k, w_cmp_k1, w_cmp_k2)
    wv1, pev, wv2 = _cmp_weights(pe_cmp_v, w_cmp_v1, w_cmp_v2)

    xp = x_prompt.reshape(t, d)
    z, kc, vc, ks, vs, kw, vw = _in_proj(xp, g_norm, w_p, g_q, g_k_slc, g_k_win, tm=1024)
    conv_o, conv_st = _conv_prompt(z, w_dw, b_dw, ln_g, ln_b, w_pw_bf, b_pw2)
    nc = t // CMP_STRIDE
    fk, fv = _feats_prompt(kc, vc, wk1, wv1)
    kcmp, vcmp = _cmp_finish(fk, fv, (wk1, pek, wk2), g_k_cmp, (wv1, pev, wv2), nc)
    imp_m = _imp_matrix(nc, t // SEL_BLOCK, nc - 1, t // SEL_BLOCK)
    o_c, imp = _cmp_select(z, kcmp, vcmp, imp_m, tq=256)
    cur = (jnp.arange(t, dtype=jnp.int32) // SEL_BLOCK).reshape(1, t)
    sel, anyb = _select(imp, cur, 128, tr=512, any_rows=256)
    o_s = _selected_prompt(z, sel, anyb, tq=256)
    o_w = _window_prompt(z)
    y_prompt = _merge(xp, conv_o, o_c, o_s, o_w, z, w_out_bf)

    xs = x_sample.reshape(db * ds, d)
    z2, kc2, vc2, ks2, vs2, kw2, vw2 = _in_proj(xs, g_norm, w_p, g_q, g_k_slc, g_k_win, tm=db * ds)
    conv_o2, u2 = _conv_sample(z2, state_conv, w_dw, b_dw, ln_g, ln_b, w_pw_bf, b_pw2)
    nc2 = past // CMP_STRIDE
    kcmp2, vcmp2 = _cmp_sample(page_table, cache_k_cmp, cache_v_cmp, (wk1, pek, wk2), g_k_cmp, (wv1, pev, wv2))
    n_sel2 = -(-(past + ds) // SEL_BLOCK)
    assert n_sel2 <= SEL_PAD
    imp_m2 = _imp_matrix(nc2, SEL_PAD, nc2 - 1, n_sel2)
    rows = lambda a: a.reshape(-1, HEAD_DIM)
    o_c2, imp2, o_w2, k_win, v_win = _sample_small(z2, kw2, vw2, kcmp2, vcmp2, imp_m2, rows(state_k_win),
                                                   rows(state_v_win), db, past)
    cur2 = ((past + jnp.arange(db * ds, dtype=jnp.int32) % ds) // SEL_BLOCK).reshape(1, db * ds)
    sel2, = _select(imp2, cur2, SEL_PAD, tr=db * ds)
    o_s2 = _selected_sample(page_table, cache_k_slc, cache_v_slc, z2, ks2, vs2, sel2, db, past)
    y_sample = _merge(xs, conv_o2, o_c2, o_s2, o_w2, z2, w_out_bf)

    kv4 = lambda a, b: a.reshape(b, -1, N_KV, HEAD_DIM)
    return (y_prompt.reshape(1, t, d), y_sample.reshape(db, ds, d),
            kv4(kc, 1), kv4(vc, 1), kv4(ks, 1), kv4(vs, 1), kv4(kw, 1)[:, t - wb:], kv4(vw, 1)[:, t - wb:],
            conv_st[HALO - (CONV_WIDTH - 1):][None],
            kv4(kc2, db), kv4(vc2, db), kv4(ks2, db), kv4(vs2, db), kv4(k_win, db), kv4(v_win, db),
            jnp.concatenate([state_conv[:, ds:], u2.reshape(db, ds, C_CONV)], axis=1))
```

```python
import functools

import numpy as np
import jax
import jax.numpy as jnp
from jax import lax
from jax.experimental import pallas as pl
from jax.experimental.pallas import tpu as pltpu

F32 = jnp.float32
BF16 = jnp.bfloat16

HEAD_DIM = 128
N_HEADS = 8
N_KV = 2
GROUP = 4
C_CONV = 1024
CONV_WIDTH = 31
CMP_STRIDE = 16
CMP_BLOCK = 32
SEL_BLOCK = 64
N_SELECT = 16
WINDOW = 512
PAGE_SIZE = 128
EPS = 1e-6
SCALE = HEAD_DIM ** -0.5
LOG2E = 1.4426950408889634
SLOPES = tuple(2.0 ** -(h + 1) for h in range(N_HEADS))

TN = 512
N_TILES = 14
ZW = TN * N_TILES
COL_UA, COL_UB, COL_ZC, COL_Q = 0, 1024, 2048, 3072
COL_KC, COL_VC, COL_KS, COL_VS, COL_KW, COL_VW = 4096, 4352, 4608, 4864, 5120, 5376
COL_GT, COL_ZN = 5632, 6144
TILE_Q0, TILE_Q1, TILE_CMP, TILE_SLC, TILE_WIN, TILE_GT = 6, 7, 8, 9, 10, 11

NEG = -1e30
M_INIT = -1e29
VMEM_LIMIT = 48 * 1024 * 1024
INPROJ_VMEM_LIMIT = 56 * 1024 * 1024


def _sigmoid(x):
    return 1.0 / (1.0 + jnp.exp(-x))


def _silu(x):
    return x * _sigmoid(x)


def _dot(a, b):
    return jnp.dot(a, b, preferred_element_type=F32)


def _dot_nt(a, b):
    return lax.dot_general(a, b, (((1,), (1,)), ((), ())), preferred_element_type=F32)


def _rms(a, g):
    return a * lax.rsqrt(jnp.mean(a * a, axis=-1, keepdims=True) + EPS) * g


def _iota(shape, dim):
    return lax.broadcasted_iota(jnp.int32, shape, dim)


def _inproj_kernel(x_ref, gn_ref, w_ref, gq_ref, gks_ref, gkw_ref,
                   z_ref, kc_o, vc_o, ks_o, vs_o, kw_o, vw_o, xn_ref, *, tm):
    j = pl.program_id(1)

    @pl.when(j == 0)
    def _():
        x = x_ref[...]
        ms = jnp.mean(x * x, axis=-1, keepdims=True)
        xn_ref[...] = (x * lax.rsqrt(ms + EPS) * gn_ref[...]).astype(BF16)

    z_ref[...] = _dot(xn_ref[...], w_ref[...])

    def kv_tile(g_ref, k_o, v_o):
        for c in range(TN // HEAD_DIM):
            lanes = slice(c * HEAD_DIM, (c + 1) * HEAD_DIM)
            a = z_ref[:, lanes]
            if c < N_KV and g_ref is not None:
                a = _rms(a, g_ref[...])
                z_ref[:, lanes] = a
            (k_o if c < N_KV else v_o)[pl.ds(c % N_KV, tm, stride=N_KV), :] = a

    @pl.when((j == TILE_Q0) | (j == TILE_Q1))
    def _():
        for c in range(TN // HEAD_DIM):
            lanes = slice(c * HEAD_DIM, (c + 1) * HEAD_DIM)
            z_ref[:, lanes] = _rms(z_ref[:, lanes], gq_ref[...])

    @pl.when(j == TILE_CMP)
    def _():
        kv_tile(None, kc_o, vc_o)

    @pl.when(j == TILE_SLC)
    def _():
        kv_tile(gks_ref, ks_o, vs_o)

    @pl.when(j == TILE_WIN)
    def _():
        kv_tile(gkw_ref, kw_o, vw_o)

    @pl.when(j == TILE_GT)
    def _():
        z_ref[...] = _sigmoid(z_ref[...])


def _in_proj(x, g_norm, w_p, g_q, g_ks, g_kw, tm):
    t, d = x.shape
    row = lambda a: a.reshape(1, -1)
    kv_shape = jax.ShapeDtypeStruct((t * N_KV, HEAD_DIM), F32)
    kv_spec = pl.BlockSpec((tm * N_KV, HEAD_DIM), lambda i, j: (i, 0))
    return pl.pallas_call(
        functools.partial(_inproj_kernel, tm=tm),
        out_shape=(jax.ShapeDtypeStruct((t, ZW), F32),) + (kv_shape,) * 6,
        grid=(t // tm, N_TILES),
        in_specs=[
            pl.BlockSpec((tm, d), lambda i, j: (i, 0)),
            pl.BlockSpec((1, d), lambda i, j: (0, 0)),
            pl.BlockSpec((d, TN), lambda i, j: (0, j)),
            pl.BlockSpec((1, HEAD_DIM), lambda i, j: (0, 0)),
            pl.BlockSpec((1, HEAD_DIM), lambda i, j: (0, 0)),
            pl.BlockSpec((1, HEAD_DIM), lambda i, j: (0, 0)),
        ],
        out_specs=(pl.BlockSpec((tm, TN), lambda i, j: (i, j)),) + (kv_spec,) * 6,
        scratch_shapes=[pltpu.VMEM((tm, d), BF16)],
        compiler_params=pltpu.CompilerParams(
            dimension_semantics=("arbitrary", "arbitrary"), vmem_limit_bytes=INPROJ_VMEM_LIMIT),
        name="in_proj",
    )(x, row(g_norm), w_p, row(g_q), row(g_ks), row(g_kw))


HALO = 32
CONV_RB = 64
CONV_CB = 128


def _conv_tail(y, zc, lng_ref, lnb_ref, wpw_ref, bpw_ref):
    mu = jnp.mean(y, axis=-1, keepdims=True)
    yc = y - mu
    var = jnp.mean(yc * yc, axis=-1, keepdims=True)
    yn = yc * lax.rsqrt(var + EPS) * lng_ref[...] + lnb_ref[...]
    act = _silu(yn).astype(BF16)
    return (_dot(act, wpw_ref[...]) + bpw_ref[...]) * _silu(zc)


def _conv_prompt_kernel(ua_ref, ub_ref, zc_ref, uah_ref, ubh_ref, wdw_ref, bdw_ref, lng_ref, lnb_ref,
                        wpw_ref, bpw_ref, o_ref, st_ref, buf, ybuf, *, tt):
    i = pl.program_id(0)
    uh = uah_ref[...] * _sigmoid(ubh_ref[...])
    buf[0:HALO, :] = jnp.where(i > 0, uh, 0.0)
    buf[HALO:HALO + tt, :] = ua_ref[...] * _sigmoid(ub_ref[...])
    off = HALO - (CONV_WIDTH - 1)
    for c0 in range(0, C_CONV, CONV_CB):
        lanes = slice(c0, c0 + CONV_CB)
        for r0 in range(0, tt, CONV_RB):
            acc = jnp.broadcast_to(bdw_ref[:, lanes], (CONV_RB, CONV_CB))
            for b in range(8):
                n = CONV_RB if b == 0 else CONV_RB + 8
                zb = None
                for a in range(-(-(off - b) // 8), (off + CONV_WIDTH - 1 - b) // 8 + 1):
                    k = 8 * a + b - off
                    term = wdw_ref[k:k + 1, lanes] * buf[r0 + 8 * a:r0 + 8 * a + n, lanes]
                    zb = term if zb is None else zb + term
                acc = acc + zb[b:b + CONV_RB]
            ybuf[r0:r0 + CONV_RB, lanes] = acc
    o_ref[...] = _conv_tail(ybuf[...], zc_ref[...], lng_ref, lnb_ref, wpw_ref, bpw_ref)

    @pl.when(i == pl.num_programs(0) - 1)
    def _():
        st_ref[...] = buf[tt:tt + HALO, :]


def _conv_prompt(z, w_dw, b_dw, ln_g, ln_b, w_pw_bf, b_pw, tt=256):
    t = z.shape[0]
    row = lambda a: a.reshape(1, -1)
    hb = tt // HALO
    cur = lambda c: pl.BlockSpec((tt, C_CONV), lambda i: (i, c))
    halo = lambda c: pl.BlockSpec((HALO, C_CONV), lambda i: (jnp.maximum(i * hb - 1, 0), c))
    full = lambda shape: pl.BlockSpec(shape, lambda i: (0, 0))
    return pl.pallas_call(
        functools.partial(_conv_prompt_kernel, tt=tt),
        out_shape=(jax.ShapeDtypeStruct((t, C_CONV), F32), jax.ShapeDtypeStruct((HALO, C_CONV), F32)),
        grid=(t // tt,),
        in_specs=[cur(0), cur(1), cur(2), halo(0), halo(1),
                  full((CONV_WIDTH, C_CONV)), full((1, C_CONV)), full((1, C_CONV)), full((1, C_CONV)),
                  full((C_CONV, C_CONV)), full((1, C_CONV))],
        out_specs=(pl.BlockSpec((tt, C_CONV), lambda i: (i, 0)), pl.BlockSpec((HALO, C_CONV), lambda i: (0, 0))),
        scratch_shapes=[pltpu.VMEM((HALO + tt, C_CONV), F32), pltpu.VMEM((tt, C_CONV), F32)],
        compiler_params=pltpu.CompilerParams(dimension_semantics=("arbitrary",), vmem_limit_bytes=VMEM_LIMIT),
        name="conv_prompt",
    )(z, z, z, z, z, w_dw, row(b_dw), row(ln_g), row(ln_b), w_pw_bf, row(b_pw))


ST_ROWS = 40


def _conv_sample_kernel(ua_ref, ub_ref, zc_ref, st_ref, wdw_ref, bdw_ref, lng_ref, lnb_ref, wpw_ref, bpw_ref,
                        o_ref, u_ref, fbuf, ybuf, *, nb, ds):
    u = ua_ref[...] * _sigmoid(ub_ref[...])
    u_ref[...] = u
    rows = _iota((nb * ds, 1), 0) & (ds - 1)
    acc_u = jnp.broadcast_to(bdw_ref[...], (nb * ds, C_CONV))
    for d in range(ds):
        sh = u if d == 0 else pltpu.roll(u, d, 0)
        acc_u = acc_u + jnp.where(rows >= d, sh, 0.0) * wdw_ref[CONV_WIDTH - 1 - d:CONV_WIDTH - d, :]
    ybuf[...] = acc_u
    fbuf[:, 24:ST_ROWS, :] = jnp.zeros((nb, ST_ROWS - 24, C_CONV), F32)
    fbuf[:, 0:CONV_WIDTH - 1, :] = st_ref[...]

    def body(b, carry):
        acc = jnp.zeros((ds, C_CONV), F32)
        for k in range(CONV_WIDTH - 1):
            acc = acc + wdw_ref[k:k + 1, :] * fbuf[b, k:k + ds, :]
        r = pl.multiple_of(b * ds, ds)
        ybuf[pl.ds(r, ds), :] = ybuf[pl.ds(r, ds), :] + acc
        return carry

    lax.fori_loop(0, nb, body, 0)
    o_ref[...] = _conv_tail(ybuf[...], zc_ref[...], lng_ref, lnb_ref, wpw_ref, bpw_ref)


def _conv_sample(z2, state_conv, w_dw, b_dw, ln_g, ln_b, w_pw_bf, b_pw):
    nb, sw, _ = state_conv.shape
    t = z2.shape[0]
    ds = t // nb
    assert sw == CONV_WIDTH - 1 and ds == 8
    row = lambda a: a.reshape(1, -1)
    col = lambda c: pl.BlockSpec((t, C_CONV), lambda i: (0, c))
    full = lambda shape: pl.BlockSpec(shape, lambda i: (0,) * len(shape))
    return pl.pallas_call(
        functools.partial(_conv_sample_kernel, nb=nb, ds=ds),
        out_shape=(jax.ShapeDtypeStruct((t, C_CONV), F32), jax.ShapeDtypeStruct((t, C_CONV), F32)),
        grid=(1,),
        in_specs=[col(0), col(1), col(2), full((nb, sw, C_CONV)),
                  full((CONV_WIDTH, C_CONV)), full((1, C_CONV)), full((1, C_CONV)), full((1, C_CONV)),
                  full((C_CONV, C_CONV)), full((1, C_CONV))],
        out_specs=(full((t, C_CONV)), full((t, C_CONV))),
        scratch_shapes=[pltpu.VMEM((nb, ST_ROWS, C_CONV), F32), pltpu.VMEM((t, C_CONV), F32)],
        compiler_params=pltpu.CompilerParams(dimension_semantics=("arbitrary",), vmem_limit_bytes=VMEM_LIMIT),
        name="conv_sample",
    )(z2, z2, z2, state_conv, w_dw, row(b_dw), row(ln_g), row(ln_b), w_pw_bf, row(b_pw))


FEAT_W = 2 * N_KV * HEAD_DIM


N_FEAT = FEAT_W // HEAD_DIM


def _chunk_feats(piece, w):
    outs = []
    for g in range(N_KV):
        xg = jnp.concatenate([piece(c, g).astype(BF16) for c in range(CMP_STRIDE)], axis=1)
        outs.append(_dot(xg, w))
    return jnp.concatenate(outs, axis=1)


CHUNK_ROWS = CMP_STRIDE * N_KV


def _feats_kernel(xk_ref, xv_ref, wk_ref, wv_ref, fk_ref, fv_ref, *, tm):
    for x_ref, w_ref, f_ref in ((xk_ref, wk_ref, fk_ref), (xv_ref, wv_ref, fv_ref)):
        f = _chunk_feats(lambda c, g: x_ref[pl.ds(c * N_KV + g, tm, stride=CHUNK_ROWS), :], w_ref[...])
        for cb in range(N_FEAT):
            f_ref[cb] = f[:, cb * HEAD_DIM:(cb + 1) * HEAD_DIM]


def _feats_prompt(xk, xv, wk, wv, tm=128):
    nc = xk.shape[0] // CHUNK_ROWS
    full = lambda shape: pl.BlockSpec(shape, lambda i: (0, 0))
    rows = pl.BlockSpec((tm * CHUNK_ROWS, HEAD_DIM), lambda i: (i, 0))
    ospec = pl.BlockSpec((N_FEAT, tm, HEAD_DIM), lambda i: (0, i, 0))
    return pl.pallas_call(
        functools.partial(_feats_kernel, tm=tm),
        out_shape=(jax.ShapeDtypeStruct((N_FEAT, nc, HEAD_DIM), F32),) * 2,
        grid=(nc // tm,),
        in_specs=[rows, rows, full(wk.shape), full(wv.shape)],
        out_specs=(ospec, ospec),
        compiler_params=pltpu.CompilerParams(dimension_semantics=("arbitrary",), vmem_limit_bytes=VMEM_LIMIT),
        name="feats_prompt",
    )(xk, xv, wk, wv)


FEAT_PAGES = 32
CHUNKS_PER_PAGE = PAGE_SIZE // CMP_STRIDE
PAGE_ROWS = PAGE_SIZE * N_KV
FEAT_PITCH = PAGE_ROWS + 8


GATHER_SLOTS = 4


def _page_copies(pt_ref, srcs, bufs, sem, step, slot, pages, pitch, look):
    out = []
    last = pt_ref.shape[0] - 1
    for p in range(pages + look):
        page = pt_ref[jnp.minimum(step * pages + p, last)] if look else pt_ref[step * pages + p]
        for n, (src, buf) in enumerate(zip(srcs, bufs)):
            out.append(pltpu.make_async_copy(src.at[page], buf.at[slot, pl.ds(p * pitch, PAGE_ROWS)], sem.at[n, slot]))
    return out


def _gather_pipeline(pt_ref, srcs, bufs, sem, pages, pitch, look=0):
    s = pl.program_id(0)
    slot = s % GATHER_SLOTS

    @pl.when(s == 0)
    def _():
        for first in range(GATHER_SLOTS - 1):
            for c in _page_copies(pt_ref, srcs, bufs, sem, first, first, pages, pitch, look):
                c.start()

    @pl.when(s + GATHER_SLOTS - 1 < pl.num_programs(0))
    def _():
        ahead = s + GATHER_SLOTS - 1
        for c in _page_copies(pt_ref, srcs, bufs, sem, ahead, ahead % GATHER_SLOTS, pages, pitch, look):
            c.start()

    for c in _page_copies(pt_ref, srcs, bufs, sem, s, slot, pages, pitch, look):
        c.wait()
    return slot


FEAT_CHUNKS = FEAT_PAGES * CHUNKS_PER_PAGE


def _cmp_sample_kernel(pt_ref, ck_hbm, cv_hbm, wk1_ref, pek_ref, wk2_ref, gk_ref, wv1_ref, pev_ref, wv2_ref,
                       ok_ref, ov_ref, kbuf, vbuf, sem, fsc, bias_sc, *, steps_per_seq):
    slot = _gather_pipeline(pt_ref, (ck_hbm, cv_hbm), (kbuf, vbuf), sem, FEAT_PAGES, FEAT_PITCH, look=1)
    last_of_seq = (pl.program_id(0) % steps_per_seq) == steps_per_seq - 1
    row = _iota((FEAT_CHUNKS, 1), 0)
    keep = jnp.logical_not(last_of_seq & (row == FEAT_CHUNKS - 1))

    def piece(buf):
        def get(c, g):
            parts = [buf[slot, pl.ds(n * CHUNK_ROWS + c * N_KV + g, FEAT_PAGES, stride=FEAT_PITCH), :]
                     for n in range(CHUNKS_PER_PAGE)]
            parts.append(buf[slot, pl.ds(FEAT_PAGES * FEAT_PITCH + c * N_KV + g, 8, stride=8), :])
            return jnp.concatenate(parts, axis=0)
        return get

    @pl.when(pl.program_id(0) == 0)
    def _():
        for n, (pe_ref, w1_ref) in enumerate(((pek_ref, wk1_ref), (pev_ref, wv1_ref))):
            bias_sc[n] = _dot(pe_ref[...].astype(BF16), w1_ref[...])

    for kv, (buf, w1_ref, w2_ref, g_ref, o_ref) in enumerate(((kbuf, wk1_ref, wk2_ref, gk_ref, ok_ref),
                                                              (vbuf, wv1_ref, wv2_ref, None, ov_ref))):
        f = _chunk_feats(piece(buf), w1_ref[...])
        for cb in range(N_FEAT):
            lanes = slice(cb * HEAD_DIM, (cb + 1) * HEAD_DIM)
            for n in range(CHUNKS_PER_PAGE):
                fsc[cb, pl.ds(n, FEAT_PAGES, stride=CHUNKS_PER_PAGE), :] = f[n * FEAT_PAGES:(n + 1) * FEAT_PAGES, lanes]
            fsc[cb, FEAT_CHUNKS:FEAT_CHUNKS + 8, :] = f[FEAT_CHUNKS:FEAT_CHUNKS + 8, lanes]
        ba = bias_sc[kv, 0:1, 0:HEAD_DIM]
        bb = bias_sc[kv, 1:2, HEAD_DIM:2 * HEAD_DIM]
        for g in range(N_KV):
            fa = fsc[2 * g, 0:FEAT_CHUNKS, :] + ba
            fb_next = fsc[2 * g + 1, 1:FEAT_CHUNKS + 1, :] + bb
            o = _dot(_silu(fa + fb_next).astype(BF16), w2_ref[...])
            if g_ref is not None:
                o = _rms(o, g_ref[...])
            o_ref[:, g * HEAD_DIM:(g + 1) * HEAD_DIM] = jnp.where(keep, o, 0.0)


def _cmp_sample(page_table, cache_k, cache_v, k_weights, gain_k, v_weights):
    n_seq, pages_per_seq = page_table.shape
    n_pages = page_table.size
    n_phys = cache_k.shape[0]
    ck = cache_k.reshape(n_phys, PAGE_ROWS, HEAD_DIM)
    cv = cache_v.reshape(n_phys, PAGE_ROWS, HEAD_DIM)
    brows = (FEAT_PAGES + 1) * FEAT_PITCH
    full = lambda a: pl.BlockSpec(a.shape, lambda s, pt: (0, 0))
    ospec = pl.BlockSpec((FEAT_CHUNKS, N_KV * HEAD_DIM), lambda s, pt: (s, 0))
    gk = gain_k.reshape(1, -1)
    return pl.pallas_call(
        functools.partial(_cmp_sample_kernel, steps_per_seq=pages_per_seq // FEAT_PAGES),
        out_shape=(jax.ShapeDtypeStruct((n_pages * CHUNKS_PER_PAGE, N_KV * HEAD_DIM), F32),) * 2,
        grid_spec=pltpu.PrefetchScalarGridSpec(
            num_scalar_prefetch=1,
            grid=(n_pages // FEAT_PAGES,),
            in_specs=[pl.BlockSpec(memory_space=pl.ANY), pl.BlockSpec(memory_space=pl.ANY)]
            + [full(a) for a in k_weights] + [full(gk)] + [full(a) for a in v_weights],
            out_specs=(ospec, ospec),
            scratch_shapes=[pltpu.VMEM((GATHER_SLOTS, brows, HEAD_DIM), F32),
                            pltpu.VMEM((GATHER_SLOTS, brows, HEAD_DIM), F32),
                            pltpu.SemaphoreType.DMA((2, GATHER_SLOTS)),
                            pltpu.VMEM((N_FEAT, FEAT_CHUNKS + 8, HEAD_DIM), F32),
                            pltpu.VMEM((2, 8, 2 * HEAD_DIM), F32)],
        ),
        compiler_params=pltpu.CompilerParams(dimension_semantics=("arbitrary",), vmem_limit_bytes=VMEM_LIMIT),
        name="cmp_sample",
    )(page_table.reshape(-1), ck, cv, *k_weights, gk, *v_weights)


def _cmp_finish_one(f_ref, w1_ref, pe_ref, w2_ref, g_ref, o_ref, nc):
    bias = _dot(pe_ref[...].astype(BF16), w1_ref[...])
    ba = bias[0:1, 0:HEAD_DIM]
    bb = bias[1:2, HEAD_DIM:2 * HEAD_DIM]
    row = _iota((nc, 1), 0)
    for g in range(N_KV):
        fa = f_ref[2 * g] + ba
        fb = f_ref[2 * g + 1] + bb
        hid = _silu(fa + pltpu.roll(fb, nc - 1, 0))
        o = _dot(hid.astype(BF16), w2_ref[...])
        if g_ref is not None:
            o = _rms(o, g_ref[...])
        o_ref[:, g * HEAD_DIM:(g + 1) * HEAD_DIM] = jnp.where(row < nc - 1, o, 0.0)


def _cmp_finish_kernel(fk_ref, fv_ref, wk1_ref, pek_ref, wk2_ref, gk_ref, wv1_ref, pev_ref, wv2_ref,
                       ok_ref, ov_ref, *, nc):
    _cmp_finish_one(fk_ref, wk1_ref, pek_ref, wk2_ref, gk_ref, ok_ref, nc)
    _cmp_finish_one(fv_ref, wv1_ref, pev_ref, wv2_ref, None, ov_ref, nc)


def _cmp_finish(fk, fv, k_weights, gain_k, v_weights, nc):
    nb = fk.shape[1] // nc
    full = lambda a: pl.BlockSpec(a.shape, lambda b: (0, 0))
    fspec = pl.BlockSpec((N_FEAT, nc, HEAD_DIM), lambda b: (0, b, 0))
    ospec = pl.BlockSpec((nc, N_KV * HEAD_DIM), lambda b: (b, 0))
    gk = gain_k.reshape(1, -1)
    return pl.pallas_call(
        functools.partial(_cmp_finish_kernel, nc=nc),
        out_shape=(jax.ShapeDtypeStruct((nb * nc, N_KV * HEAD_DIM), F32),) * 2,
        grid=(nb,),
        in_specs=[fspec, fspec] + [full(a) for a in k_weights] + [full(gk)] + [full(a) for a in v_weights],
        out_specs=(ospec, ospec),
        compiler_params=pltpu.CompilerParams(dimension_semantics=("arbitrary",), vmem_limit_bytes=VMEM_LIMIT),
        name="cmp_finish",
    )(fk, fv, *k_weights, gk, *v_weights)


SELECT_LANES = 128


def _select_kernel(imp_ref, cur_ref, sel_ref, *any_refs, any_rows):
    n_rows, n_blk = imp_ref.shape
    shape = (n_blk, SELECT_LANES)
    j = _iota(shape, 0)
    jf = j.astype(F32)

    def pick(score, sel_t):
        m = jnp.max(score, axis=0, keepdims=True)
        first = jnp.min(jnp.where(score == m, jf, 1e9), axis=0, keepdims=True)
        hit = jf == first
        return jnp.where(hit, -2.0, score), jnp.where(hit, 1.0, sel_t)

    def body(_, carry):
        return pick(*carry[0:2]) + pick(*carry[2:4])

    def start(c):
        rows = slice(c * SELECT_LANES, (c + 1) * SELECT_LANES)
        imp_t = imp_ref[rows, :].T
        cur = cur_ref[:, rows]
        forced = (j == 0) | (j == cur) | (j == cur - 1)
        return jnp.where(forced, 1e30, jnp.where(j <= cur, imp_t, -1.0)), jnp.zeros(shape, F32)

    for c in range(0, n_rows // SELECT_LANES, 2):
        out = lax.fori_loop(0, N_SELECT, body, start(c) + start(c + 1))
        sel_ref[c * SELECT_LANES:(c + 1) * SELECT_LANES, :] = out[1].T
        sel_ref[(c + 1) * SELECT_LANES:(c + 2) * SELECT_LANES, :] = out[3].T
    sel = sel_ref[...]
    if any_refs:
        for n in range(sel.shape[0] // any_rows):
            blk = jnp.max(sel[n * any_rows:(n + 1) * any_rows], axis=0, keepdims=True)
            any_refs[0][n * 8:(n + 1) * 8, :] = jnp.broadcast_to(blk, (8, sel.shape[1]))


def _select(imp, cur, n_lanes, tr, any_rows=None):
    r, w = imp.shape
    groups = w // n_lanes
    out_shape = [jax.ShapeDtypeStruct((r, w), F32)]
    out_specs = [pl.BlockSpec((tr, n_lanes), lambda i, g: (i, g))]
    if any_rows is not None:
        out_shape.append(jax.ShapeDtypeStruct((r // any_rows * 8, w), F32))
        out_specs.append(pl.BlockSpec((tr // any_rows * 8, n_lanes), lambda i, g: (i, g)))
    return pl.pallas_call(
        functools.partial(_select_kernel, any_rows=any_rows),
        out_shape=tuple(out_shape),
        grid=(r // tr, groups),
        in_specs=[pl.BlockSpec((tr, n_lanes), lambda i, g: (i, g)), pl.BlockSpec((1, tr), lambda i, g: (0, i))],
        out_specs=tuple(out_specs),
        compiler_params=pltpu.CompilerParams(
            dimension_semantics=("arbitrary", "arbitrary"), vmem_limit_bytes=VMEM_LIMIT),
        name="select",
    )(imp, cur)


def _imp_matrix(n_cmp_pad, n_sel_pad, n_cmp, n_sel):
    r = SEL_BLOCK // CMP_STRIDE
    lead = CMP_BLOCK // CMP_STRIDE - 1
    m = np.zeros((n_cmp_pad, n_sel_pad), np.float32)
    for jb in range(n_sel):
        for o in range(-lead, r):
            start = o * CMP_STRIDE
            w = (min(start + CMP_BLOCK, SEL_BLOCK) - max(start, 0)) / CMP_BLOCK
            i = r * jb + o
            if 0 <= i < n_cmp:
                m[i, jb] += w
    return jnp.asarray(m)


CMP_KEY_STEP = 128


def _cmp_select_tile(q_ref, kc_ref, vc_ref, m_ref, oc_ref, imp_ref, i, tq, nk):
    qpos = i * tq + _iota((tq, 1), 0)
    end = _iota((1, nk), 1) * CMP_STRIDE + (CMP_BLOCK - 1)
    dist = qpos - end
    mask = dist >= 0
    distf = dist.astype(F32)
    for g in range(N_KV):
        kg = kc_ref[0:nk, g * HEAD_DIM:(g + 1) * HEAD_DIM].astype(BF16)
        vg = vc_ref[0:nk, g * HEAD_DIM:(g + 1) * HEAD_DIM].astype(BF16)
        psum = jnp.zeros((tq, nk), F32)
        for r in range(GROUP):
            h = g * GROUP + r
            qh = q_ref[:, h * HEAD_DIM:(h + 1) * HEAD_DIM].astype(BF16)
            s = _dot_nt(qh, kg) * SCALE - SLOPES[h] * distf
            s = jnp.where(mask, s, NEG)
            m = jnp.max(s, axis=-1, keepdims=True)
            e = jnp.where(mask, jnp.exp(s - m), 0.0)
            p = e * (1.0 / jnp.maximum(jnp.sum(e, axis=-1, keepdims=True), 1e-30))
            psum = psum + p
            oc_ref[:, h * HEAD_DIM:(h + 1) * HEAD_DIM] = _dot(p.astype(BF16), vg)
        imp_ref[:, g * 128:(g + 1) * 128] = jnp.dot(psum, m_ref[0:nk, :], precision=lax.Precision.HIGHEST,
                                                    preferred_element_type=F32)


def _cmp_select_kernel(q_ref, kc_ref, vc_ref, m_ref, oc_ref, imp_ref, *, tq, nc):
    i = pl.program_id(0)
    reach = (i * tq + tq - 1 - (CMP_BLOCK - 1)) // CMP_STRIDE + 1
    for nk in range(CMP_KEY_STEP, nc + 1, CMP_KEY_STEP):
        covers = reach <= nk if nk == CMP_KEY_STEP else (reach > nk - CMP_KEY_STEP) & (reach <= nk)
        if nk == nc:
            covers = reach > nk - CMP_KEY_STEP

        @pl.when(covers)
        def _(nk=nk):
            _cmp_select_tile(q_ref, kc_ref, vc_ref, m_ref, oc_ref, imp_ref, i, tq, nk)


def _cmp_select(z, kcmp, vcmp, imp_m, tq):
    t = z.shape[0]
    nc = kcmp.shape[0]
    n_sel = imp_m.shape[1]
    assert n_sel == 128
    full = lambda shape: pl.BlockSpec(shape, lambda i: (0, 0))
    return pl.pallas_call(
        functools.partial(_cmp_select_kernel, tq=tq, nc=nc),
        out_shape=(jax.ShapeDtypeStruct((t, N_HEADS * HEAD_DIM), F32), jax.ShapeDtypeStruct((t, N_KV * 128), F32)),
        grid=(t // tq,),
        in_specs=[pl.BlockSpec((tq, N_HEADS * HEAD_DIM), lambda i: (i, COL_Q // (N_HEADS * HEAD_DIM))),
                  full(kcmp.shape), full(vcmp.shape), full(imp_m.shape)],
        out_specs=(pl.BlockSpec((tq, N_HEADS * HEAD_DIM), lambda i: (i, 0)),
                   pl.BlockSpec((tq, N_KV * 128), lambda i: (i, 0))),
        compiler_params=pltpu.CompilerParams(dimension_semantics=("arbitrary",), vmem_limit_bytes=VMEM_LIMIT),
        name="cmp_select",
    )(z, kcmp, vcmp, imp_m)


def _flash_update(s, v_bf, m_ref, l_ref, acc_ref, rows):
    m_old = m_ref[rows, :]
    m_new = jnp.maximum(m_old, jnp.max(s, axis=-1, keepdims=True))
    alpha = jnp.exp(m_old - m_new)
    p = jnp.exp(s - m_new)
    l_ref[rows, :] = alpha * l_ref[rows, :] + jnp.sum(p, axis=-1, keepdims=True)
    acc_ref[rows, :] = alpha * acc_ref[rows, :] + _dot(p.astype(BF16), v_bf)
    m_ref[rows, :] = m_new


V_ROWS = 2 * HEAD_DIM


def _selected_prompt_kernel(flag_ref, q_ref, k_ref, v_ref, sel_ref, et_ref, o_ref,
                            kbf, vaug, qs, m_sc, acc_sc, *, tq, tk):
    g = pl.program_id(0)
    qi = pl.program_id(1)
    nkt = kbf.shape[0]
    step = g * pl.num_programs(1) + qi

    @pl.when(qi == 0)
    def _():
        ones_row = jnp.where(_iota((V_ROWS - HEAD_DIM, tk), 0) == 0, 1.0, 0.0).astype(BF16)
        for ki in range(nkt):
            rows = slice(ki * tk, (ki + 1) * tk)
            kbf[ki] = k_ref[rows, :].astype(BF16)
            vaug[ki, 0:HEAD_DIM, :] = v_ref[rows, :].T.astype(BF16)
            vaug[ki, HEAD_DIM:V_ROWS, :] = ones_row

    for r in range(GROUP):
        qs[r] = (q_ref[:, r * HEAD_DIM:(r + 1) * HEAD_DIM] * (SCALE * LOG2E)).astype(BF16)
    m_sc[...] = jnp.full(m_sc.shape, M_INIT, F32)
    acc_sc[...] = jnp.zeros(acc_sc.shape, F32)
    sel_bf = sel_ref[...].astype(BF16)
    qpos = qi * tq + _iota((1, tq), 1)

    def body(ki, carry):
        @pl.when(flag_ref[step * nkt + ki] > 0)
        def _():
            k = kbf[ki]
            va = vaug[ki]
            sel_t = _dot_nt(et_ref[pl.ds(pl.multiple_of(ki * tk, tk), tk), :], sel_bf)
            kpos = ki * tk + _iota((tk, tq), 0)
            mb = jnp.where((sel_t > 0.5) & (kpos <= qpos), 0.0, NEG)
            krel = (kpos - qi * tq).astype(F32)
            for r in range(GROUP):
                slope = jnp.where(g == 0, SLOPES[r], SLOPES[GROUP + r]) * LOG2E
                s = _dot_nt(k, qs[r]) + (mb + slope * krel)
                m_old = m_sc[r]
                m_new = jnp.maximum(m_old, jnp.max(s, axis=0, keepdims=True))
                p = jnp.exp2(s - m_new).astype(BF16)
                acc_sc[r] = acc_sc[r] * jnp.exp2(m_old - m_new) + _dot(va, p)
                m_sc[r] = m_new
        return carry

    lax.fori_loop(0, (qi * tq + tq - 1) // tk + 1, body, 0)
    for r in range(GROUP):
        o_t = acc_sc[r, 0:HEAD_DIM, :] * (1.0 / acc_sc[r, HEAD_DIM:HEAD_DIM + 1, :])
        o_ref[:, r * HEAD_DIM:(r + 1) * HEAD_DIM] = o_t.T


def _tile_flags(anyb, t, tq, tk):
    nq, nkt = t // tq, t // tk
    a = anyb.reshape(nq, 8, N_KV, nkt, tk // SEL_BLOCK)[:, 0]
    return jnp.transpose(jnp.max(a, axis=-1) > 0.5, (1, 0, 2)).astype(jnp.int32).reshape(-1)


def _selected_prompt(z, sel, anyb, tq, tk=512):
    t = z.shape[0]
    gw = GROUP * HEAD_DIM
    flags = _tile_flags(anyb, t, tq, tk)
    et = jnp.asarray(np.arange(t)[:, None] // SEL_BLOCK == np.arange(128)[None, :], dtype=BF16)
    return pl.pallas_call(
        functools.partial(_selected_prompt_kernel, tq=tq, tk=tk),
        out_shape=jax.ShapeDtypeStruct((t, N_HEADS * HEAD_DIM), F32),
        grid_spec=pltpu.PrefetchScalarGridSpec(
            num_scalar_prefetch=1,
            grid=(N_KV, t // tq),
            in_specs=[
                pl.BlockSpec((tq, gw), lambda g, qi, f: (qi, COL_Q // gw + g)),
                pl.BlockSpec((t, HEAD_DIM), lambda g, qi, f: (0, COL_KS // HEAD_DIM + g)),
                pl.BlockSpec((t, HEAD_DIM), lambda g, qi, f: (0, COL_VS // HEAD_DIM + g)),
                pl.BlockSpec((tq, 128), lambda g, qi, f: (qi, g)),
                pl.BlockSpec((t, 128), lambda g, qi, f: (0, 0)),
            ],
            out_specs=pl.BlockSpec((tq, gw), lambda g, qi, f: (qi, g)),
            scratch_shapes=[pltpu.VMEM((t // tk, tk, HEAD_DIM), BF16), pltpu.VMEM((t // tk, V_ROWS, tk), BF16),
                            pltpu.VMEM((GROUP, tq, HEAD_DIM), BF16), pltpu.VMEM((GROUP, 1, tq), F32),
                            pltpu.VMEM((GROUP, V_ROWS, tq), F32)],
        ),
        compiler_params=pltpu.CompilerParams(
            dimension_semantics=("arbitrary", "arbitrary"), vmem_limit_bytes=VMEM_LIMIT),
        name="selected_prompt",
    )(flags, z, z, z, sel, et)


def _window_prompt_kernel(q_ref, kp_ref, kc_ref, vp_ref, vc_ref, o_ref, *, tq):
    g = pl.program_id(0)
    qi = pl.program_id(1)
    k = jnp.concatenate([kp_ref[...], kc_ref[...]], axis=0).astype(BF16)
    v = jnp.concatenate([vp_ref[...], vc_ref[...]], axis=0).astype(BF16)
    krel = _iota((1, 2 * tq), 1) - tq
    dist = _iota((tq, 1), 0) - krel
    mask = (dist >= 0) & (dist < WINDOW) & ((krel >= 0) | (qi > 0))
    distf = dist.astype(F32)
    for r in range(GROUP):
        slope = jnp.where(g == 0, SLOPES[r], SLOPES[GROUP + r])
        qh = q_ref[:, r * HEAD_DIM:(r + 1) * HEAD_DIM].astype(BF16)
        s = jnp.where(mask, _dot_nt(qh, k) * SCALE - slope * distf, NEG)
        m = jnp.max(s, axis=-1, keepdims=True)
        e = jnp.where(mask, jnp.exp(s - m), 0.0)
        p = e * (1.0 / jnp.maximum(jnp.sum(e, axis=-1, keepdims=True), 1e-30))
        o_ref[:, r * HEAD_DIM:(r + 1) * HEAD_DIM] = _dot(p.astype(BF16), v)


def _window_prompt(z, tq=WINDOW):
    t = z.shape[0]
    gw = GROUP * HEAD_DIM
    prev = lambda c: pl.BlockSpec((tq, HEAD_DIM), lambda g, qi: (jnp.maximum(qi - 1, 0), c // HEAD_DIM + g))
    cur = lambda c: pl.BlockSpec((tq, HEAD_DIM), lambda g, qi: (qi, c // HEAD_DIM + g))
    return pl.pallas_call(
        functools.partial(_window_prompt_kernel, tq=tq),
        out_shape=jax.ShapeDtypeStruct((t, N_HEADS * HEAD_DIM), F32),
        grid=(N_KV, t // tq),
        in_specs=[pl.BlockSpec((tq, gw), lambda g, qi: (qi, COL_Q // gw + g)),
                  prev(COL_KW), cur(COL_KW), prev(COL_VW), cur(COL_VW)],
        out_specs=pl.BlockSpec((tq, gw), lambda g, qi: (qi, g)),
        compiler_params=pltpu.CompilerParams(
            dimension_semantics=("arbitrary", "arbitrary"), vmem_limit_bytes=VMEM_LIMIT),
        name="window_prompt",
    )(z, z, z, z, z)


SEL_PAD = 384


def _slope_col(g, ds):
    return jnp.concatenate([jnp.full((ds, 1), SLOPES[g * GROUP + r], F32) for r in range(GROUP)], axis=0)


def _stack_heads(q_ref, g):
    return jnp.concatenate([q_ref[:, (g * GROUP + r) * HEAD_DIM:(g * GROUP + r + 1) * HEAD_DIM]
                            for r in range(GROUP)], axis=0).astype(BF16)


def _pad_rows(a, n):
    return jnp.concatenate([a, jnp.zeros((n - a.shape[0], a.shape[1]), a.dtype)], axis=0)


def _sample_small_kernel(q_ref, kc_ref, vc_ref, m_ref, sk_ref, sv_ref, nk_ref, nv_ref,
                         oc_ref, imp_ref, ow_ref, ko_ref, vo_ref, *, ds, nc, past, wb):
    rows = GROUP * ds
    qidx = _iota((rows, 1), 0) & (ds - 1)
    spos = past + qidx
    end = _iota((1, nc), 1) * CMP_STRIDE + (CMP_BLOCK - 1)
    dist_c = spos - end
    mask_c = dist_c >= 0
    ist = _iota((1, wb), 1)
    dist_s = wb + qidx - ist
    mask_s = dist_s < WINDOW
    jn = _iota((1, 128), 1)
    dist_n = qidx - jn
    mask_n = (dist_n >= 0) & (jn < ds)
    for g in range(N_KV):
        lanes = slice(g * HEAD_DIM, (g + 1) * HEAD_DIM)
        qs = _stack_heads(q_ref, g)
        slope = _slope_col(g, ds)
        s = _dot_nt(qs, kc_ref[:, lanes].astype(BF16)) * SCALE - slope * dist_c.astype(F32)
        s = jnp.where(mask_c, s, NEG)
        m = jnp.max(s, axis=-1, keepdims=True)
        e = jnp.where(mask_c, jnp.exp(s - m), 0.0)
        p = e * (1.0 / jnp.maximum(jnp.sum(e, axis=-1, keepdims=True), 1e-30))
        o = _dot(p.astype(BF16), vc_ref[:, lanes].astype(BF16))
        psum = p[0:ds]
        for r in range(1, GROUP):
            psum = psum + p[r * ds:(r + 1) * ds]
        imp_ref[:, g * SEL_PAD:(g + 1) * SEL_PAD] = jnp.dot(psum, m_ref[...], precision=lax.Precision.HIGHEST,
                                                            preferred_element_type=F32)
        kst = sk_ref[pl.ds(g, wb, stride=N_KV), :].astype(BF16)
        vst = sv_ref[pl.ds(g, wb, stride=N_KV), :].astype(BF16)
        kn = _pad_rows(nk_ref[pl.ds(g, ds, stride=N_KV), :], 128).astype(BF16)
        vn = _pad_rows(nv_ref[pl.ds(g, ds, stride=N_KV), :], 128).astype(BF16)
        s1 = jnp.where(mask_s, _dot_nt(qs, kst) * SCALE - slope * dist_s.astype(F32), NEG)
        s2 = jnp.where(mask_n, _dot_nt(qs, kn) * SCALE - slope * dist_n.astype(F32), NEG)
        mw = jnp.maximum(jnp.max(s1, axis=-1, keepdims=True), jnp.max(s2, axis=-1, keepdims=True))
        e1 = jnp.where(mask_s, jnp.exp(s1 - mw), 0.0)
        e2 = jnp.where(mask_n, jnp.exp(s2 - mw), 0.0)
        inv = 1.0 / jnp.maximum(jnp.sum(e1, axis=-1, keepdims=True) + jnp.sum(e2, axis=-1, keepdims=True), 1e-30)
        w = _dot((e1 * inv).astype(BF16), vst) + _dot((e2 * inv).astype(BF16), vn)
        for r in range(GROUP):
            h = g * GROUP + r
            oc_ref[:, h * HEAD_DIM:(h + 1) * HEAD_DIM] = o[r * ds:(r + 1) * ds]
            ow_ref[:, h * HEAD_DIM:(h + 1) * HEAD_DIM] = w[r * ds:(r + 1) * ds]
    keep = (wb - ds) * N_KV
    for s_ref, n_ref, o_ref in ((sk_ref, nk_ref, ko_ref), (sv_ref, nv_ref, vo_ref)):
        o_ref[0:keep, :] = s_ref[ds * N_KV:wb * N_KV, :]
        o_ref[keep:wb * N_KV, :] = n_ref[...]


def _sample_small(z2, kw2, vw2, kcmp2, vcmp2, imp_m2, state_k, state_v, nb, past):
    t = z2.shape[0]
    ds = t // nb
    nc = kcmp2.shape[0] // nb
    wb = state_k.shape[0] // (nb * N_KV)
    qw = N_HEADS * HEAD_DIM
    kvw = N_KV * HEAD_DIM
    rows = lambda n: pl.BlockSpec((n * N_KV, HEAD_DIM), lambda b: (b, 0))
    return pl.pallas_call(
        functools.partial(_sample_small_kernel, ds=ds, nc=nc, past=past, wb=wb),
        out_shape=(jax.ShapeDtypeStruct((t, qw), F32), jax.ShapeDtypeStruct((t, N_KV * SEL_PAD), F32),
                   jax.ShapeDtypeStruct((t, qw), F32),
                   jax.ShapeDtypeStruct(state_k.shape, F32), jax.ShapeDtypeStruct(state_v.shape, F32)),
        grid=(nb,),
        in_specs=[pl.BlockSpec((ds, qw), lambda b: (b, COL_Q // qw)),
                  pl.BlockSpec((nc, kvw), lambda b: (b, 0)), pl.BlockSpec((nc, kvw), lambda b: (b, 0)),
                  pl.BlockSpec(imp_m2.shape, lambda b: (0, 0)),
                  rows(wb), rows(wb), rows(ds), rows(ds)],
        out_specs=(pl.BlockSpec((ds, qw), lambda b: (b, 0)), pl.BlockSpec((ds, N_KV * SEL_PAD), lambda b: (b, 0)),
                   pl.BlockSpec((ds, qw), lambda b: (b, 0)), rows(wb), rows(wb)),
        compiler_params=pltpu.CompilerParams(dimension_semantics=("arbitrary",), vmem_limit_bytes=VMEM_LIMIT),
        name="sample_small",
    )(z2, kcmp2, vcmp2, imp_m2, state_k, state_v, kw2, vw2)


SLC_PAGES = 32
SLC_KEYS = SLC_PAGES * PAGE_SIZE
SLC_BLOCKS = SLC_KEYS // SEL_BLOCK
WIN_STEPS = 128 // SLC_BLOCKS


def _expand_matrix():
    j = np.arange(128)[:, None]
    c = np.arange(SLC_KEYS)[None, :]
    e = np.concatenate([(j == w * SLC_BLOCKS + c // SEL_BLOCK) for w in range(WIN_STEPS)], axis=0)
    return jnp.asarray(e, dtype=BF16)


def _selected_sample_kernel(pt_ref, ck_hbm, cv_hbm, q_ref, sel0_ref, sel1_ref, e_ref, nk_ref, nv_ref, o_ref,
                            kbuf, vbuf, sem, m_sc, l_sc, acc_sc, *, ds, steps, past):
    slot = _gather_pipeline(pt_ref, (ck_hbm, cv_hbm), (kbuf, vbuf), sem, SLC_PAGES, PAGE_ROWS)
    kt = pl.program_id(0) % steps
    rows = GROUP * ds
    nk = SLC_KEYS
    qidx = _iota((rows, 1), 0) & (ds - 1)

    @pl.when(kt == 0)
    def _():
        m_sc[...] = jnp.full(m_sc.shape, M_INIT, F32)
        l_sc[...] = jnp.zeros(l_sc.shape, F32)
        acc_sc[...] = jnp.zeros(acc_sc.shape, F32)

    krel = (kt * nk - past + _iota((1, nk), 1)).astype(F32)
    e = e_ref[pl.ds(pl.multiple_of((kt % WIN_STEPS) * 128, 128), 128), :]
    for g, sel_ref in enumerate((sel0_ref, sel1_ref)):
        grows = slice(g * rows, (g + 1) * rows)
        qs = _stack_heads(q_ref, g)
        slope = _slope_col(g, ds)
        selexp = _dot(sel_ref[...].astype(BF16), e)
        mb = jnp.where(jnp.concatenate([selexp] * GROUP, axis=0) > 0.5, 0.0, NEG)
        kg = kbuf[slot, pl.ds(g, nk, stride=N_KV), :].astype(BF16)
        vg = vbuf[slot, pl.ds(g, nk, stride=N_KV), :].astype(BF16)
        s = _dot_nt(qs, kg) * SCALE + (mb + slope * krel)
        _flash_update(s, vg, m_sc, l_sc, acc_sc, grows)

    @pl.when(kt == steps - 1)
    def _():
        jn = _iota((1, 128), 1)
        mb_n = jnp.where((jn <= qidx) & (jn < ds), 0.0, NEG)
        for g in range(N_KV):
            grows = slice(g * rows, (g + 1) * rows)
            qs = _stack_heads(q_ref, g)
            kn = _pad_rows(nk_ref[pl.ds(g, ds, stride=N_KV), :], 128).astype(BF16)
            vn = _pad_rows(nv_ref[pl.ds(g, ds, stride=N_KV), :], 128).astype(BF16)
            s = _dot_nt(qs, kn) * SCALE + (mb_n + _slope_col(g, ds) * jn.astype(F32))
            _flash_update(s, vn, m_sc, l_sc, acc_sc, grows)
            o = acc_sc[grows, :] * (1.0 / l_sc[grows, :])
            for r in range(GROUP):
                h = g * GROUP + r
                o_ref[:, h * HEAD_DIM:(h + 1) * HEAD_DIM] = o[r * ds:(r + 1) * ds]


def _selected_sample(page_table, cache_k, cache_v, z2, ks2, vs2, sel2, nb, past):
    t = z2.shape[0]
    ds = t // nb
    n_phys = cache_k.shape[0]
    ck = cache_k.reshape(n_phys, PAGE_ROWS, HEAD_DIM)
    cv = cache_v.reshape(n_phys, PAGE_ROWS, HEAD_DIM)
    steps = page_table.shape[1] // SLC_PAGES
    qw = N_HEADS * HEAD_DIM
    rows = N_KV * GROUP * ds
    e = _expand_matrix()
    win = lambda g: pl.BlockSpec(
        (ds, 128), lambda s, pt: (s // steps, g * (SEL_PAD // 128) + (s % steps) // WIN_STEPS))
    new = pl.BlockSpec((ds * N_KV, HEAD_DIM), lambda s, pt: (s // steps, 0))
    return pl.pallas_call(
        functools.partial(_selected_sample_kernel, ds=ds, steps=steps, past=past),
        out_shape=jax.ShapeDtypeStruct((t, qw), F32),
        grid_spec=pltpu.PrefetchScalarGridSpec(
            num_scalar_prefetch=1,
            grid=(nb * steps,),
            in_specs=[pl.BlockSpec(memory_space=pl.ANY), pl.BlockSpec(memory_space=pl.ANY),
                      pl.BlockSpec((ds, qw), lambda s, pt: (s // steps, COL_Q // qw)),
                      win(0), win(1), pl.BlockSpec(e.shape, lambda s, pt: (0, 0)), new, new],
            out_specs=pl.BlockSpec((ds, qw), lambda s, pt: (s // steps, 0)),
            scratch_shapes=[pltpu.VMEM((GATHER_SLOTS, SLC_PAGES * PAGE_ROWS, HEAD_DIM), F32),
                            pltpu.VMEM((GATHER_SLOTS, SLC_PAGES * PAGE_ROWS, HEAD_DIM), F32),
                            pltpu.SemaphoreType.DMA((2, GATHER_SLOTS)),
                            pltpu.VMEM((rows, 1), F32), pltpu.VMEM((rows, 1), F32),
                            pltpu.VMEM((rows, HEAD_DIM), F32)],
        ),
        compiler_params=pltpu.CompilerParams(dimension_semantics=("arbitrary",), vmem_limit_bytes=VMEM_LIMIT),
        name="selected_sample",
    )(page_table.reshape(-1), ck, cv, z2, sel2, sel2, e, ks2, vs2)


def _merge_kernel(x_ref, conv_ref, oc_ref, os_ref, ow_ref, gt_ref, zn_ref, w_ref, y_ref):
    gt = gt_ref[...]
    parts = [conv_ref[...].astype(BF16)]
    for h in range(N_HEADS):
        lanes = slice(h * HEAD_DIM, (h + 1) * HEAD_DIM)
        o = (gt[:, 3 * h:3 * h + 1] * oc_ref[:, lanes] + gt[:, 3 * h + 1:3 * h + 2] * os_ref[:, lanes]
             + gt[:, 3 * h + 2:3 * h + 3] * ow_ref[:, lanes])
        parts.append((o * _silu(zn_ref[:, lanes])).astype(BF16))
    y_ref[...] = x_ref[...] + _dot(jnp.concatenate(parts, axis=1), w_ref[...])


def _merge(x, conv_o, o_c, o_s, o_w, z, w_out_bf, tm=256):
    t, d = x.shape
    qw = N_HEADS * HEAD_DIM
    rowblk = lambda w: pl.BlockSpec((tm, w), lambda i: (i, 0))
    return pl.pallas_call(
        _merge_kernel,
        out_shape=jax.ShapeDtypeStruct((t, d), F32),
        grid=(t // tm,),
        in_specs=[rowblk(d), rowblk(C_CONV), rowblk(qw), rowblk(qw), rowblk(qw),
                  pl.BlockSpec((tm, 128), lambda i: (i, COL_GT // 128)),
                  pl.BlockSpec((tm, qw), lambda i: (i, COL_ZN // qw)),
                  pl.BlockSpec(w_out_bf.shape, lambda i: (0, 0))],
        out_specs=rowblk(d),
        compiler_params=pltpu.CompilerParams(dimension_semantics=("arbitrary",), vmem_limit_bytes=VMEM_LIMIT),
        name="merge",
    )(x, conv_o, o_c, o_s, o_w, z, z, w_out_bf)


N_GT = 3 * N_HEADS


def _pad_w_kernel(wt_hbm, o_ref, buf, sem):
    j = pl.program_id(0)
    slot = j % 2

    def tile_copy(jj, sl):
        src = jnp.where(jj <= TILE_GT, jj * TN, jj * TN - COL_ZN + COL_GT + N_GT)
        return pltpu.make_async_copy(wt_hbm.at[pl.ds(pl.multiple_of(src, 8), TN)], buf.at[sl], sem.at[sl])

    @pl.when(j == 0)
    def _():
        tile_copy(0, 0).start()

    @pl.when(j + 1 < pl.num_programs(0))
    def _():
        tile_copy(j + 1, 1 - slot).start()

    tile_copy(j, slot).wait()
    w = buf[slot]
    keep = (j != TILE_GT) | (_iota((TN, 1), 0) < N_GT)
    o_ref[...] = jnp.where(keep, w, 0.0).T.astype(BF16)


def _pad_w_in(w_in):
    d, n = w_in.shape
    assert n == COL_GT + N_GT + ZW - COL_ZN and COL_ZN == (TILE_GT + 1) * TN
    return pl.pallas_call(
        _pad_w_kernel,
        out_shape=jax.ShapeDtypeStruct((d, ZW), BF16),
        grid=(N_TILES,),
        in_specs=[pl.BlockSpec(memory_space=pl.ANY)],
        out_specs=pl.BlockSpec((d, TN), lambda j: (0, j)),
        scratch_shapes=[pltpu.VMEM((2, TN, d), F32), pltpu.SemaphoreType.DMA((2,))],
        compiler_params=pltpu.CompilerParams(dimension_semantics=("arbitrary",), vmem_limit_bytes=VMEM_LIMIT),
        name="pad_w_in",
    )(w_in.T)


def _cmp_weights(pe, w1, w2):
    half = CMP_STRIDE
    k = half * HEAD_DIM
    w1cat = jnp.concatenate([w1[:half].reshape(k, -1), w1[half:].reshape(k, -1)], axis=1).astype(BF16)
    pe2 = jnp.concatenate([pe[:half].reshape(1, k), pe[half:].reshape(1, k), jnp.zeros((6, k), pe.dtype)], axis=0)
    return w1cat, pe2, w2.astype(BF16)


def kernel(x_prompt, x_sample, cache_k_cmp, cache_v_cmp, cache_k_slc, cache_v_slc, state_k_win, state_v_win,
           state_conv, page_table, g_norm, w_in, pe_cmp_k, w_cmp_k1, w_cmp_k2, pe_cmp_v, w_cmp_v1, w_cmp_v2,
           g_q, g_k_cmp, g_k_slc, g_k_win, w_dw, b_dw, ln_g, ln_b, w_pw2, b_pw2, w_out):
    _, t, d = x_prompt.shape
    db, ds, _ = x_sample.shape
    past = page_table.shape[1] * PAGE_SIZE
    wb = state_k_win.shape[1]
    kvw = N_KV * HEAD_DIM
    assert ds < CMP_STRIDE and wb == WINDOW and t % SEL_BLOCK == 0 and t // SEL_BLOCK == 128

    w_p = _pad_w_in(w_in)
    w_pw_bf = w_pw2.astype(BF16)
    w_out_bf = w_out.astype(BF16)
    wk1, pek, wk2 = _cmp_weights(pe_cmp_k, w_cmp_k1, w_cmp_k2)
    wv1, pev, wv2 = _cmp_weights(pe_cmp_v, w_cmp_v1, w_cmp_v2)

    xp = x_prompt.reshape(t, d)
    z, kc, vc, ks, vs, kw, vw = _in_proj(xp, g_norm, w_p, g_q, g_k_slc, g_k_win, tm=1024)
    conv_o, conv_st = _conv_prompt(z, w_dw, b_dw, ln_g, ln_b, w_pw_bf, b_pw2)
    nc = t // CMP_STRIDE
    fk, fv = _feats_prompt(kc, vc, wk1, wv1)
    kcmp, vcmp = _cmp_finish(fk, fv, (wk1, pek, wk2), g_k_cmp, (wv1, pev, wv2), nc)
    imp_m = _imp_matrix(nc, t // SEL_BLOCK, nc - 1, t // SEL_BLOCK)
    o_c, imp = _cmp_select(z, kcmp, vcmp, imp_m, tq=256)
    cur = (jnp.arange(t, dtype=jnp.int32) // SEL_BLOCK).reshape(1, t)
    sel, anyb = _select(imp, cur, 128, tr=512, any_rows=256)
    o_s = _selected_prompt(z, sel, anyb, tq=256)
    o_w = _window_prompt(z)
    y_prompt = _merge(xp, conv_o, o_c, o_s, o_w, z, w_out_bf)

    xs = x_sample.reshape(db * ds, d)
    z2, kc2, vc2, ks2, vs2, kw2, vw2 = _in_proj(xs, g_norm, w_p, g_q, g_k_slc, g_k_win, tm=db * ds)
    conv_o2, u2 = _conv_sample(z2, state_conv, w_dw, b_dw, ln_g, ln_b, w_pw_bf, b_pw2)
    nc2 = past // CMP_STRIDE
    kcmp2, vcmp2 = _cmp_sample(page_table, cache_k_cmp, cache_v_cmp, (wk1, pek, wk2), g_k_cmp, (wv1, pev, wv2))
    n_sel2 = -(-(past + ds) // SEL_BLOCK)
    assert n_sel2 <= SEL_PAD
    imp_m2 = _imp_matrix(nc2, SEL_PAD, nc2 - 1, n_sel2)
    rows = lambda a: a.reshape(-1, HEAD_DIM)
    o_c2, imp2, o_w2, k_win, v_win = _sample_small(z2, kw2, vw2, kcmp2, vcmp2, imp_m2, rows(state_k_win),
                                                   rows(state_v_win), db, past)
    cur2 = ((past + jnp.arange(db * ds, dtype=jnp.int32) % ds) // SEL_BLOCK).reshape(1, db * ds)
    sel2, = _select(imp2, cur2, SEL_PAD, tr=db * ds)
    o_s2 = _selected_sample(page_table, cache_k_slc, cache_v_slc, z2, ks2, vs2, sel2, db, past)
    y_sample = _merge(xs, conv_o2, o_c2, o_s2, o_w2, z2, w_out_bf)

    kv4 = lambda a, b: a.reshape(b, -1, N_KV, HEAD_DIM)
    return (y_prompt.reshape(1, t, d), y_sample.reshape(db, ds, d),
            kv4(kc, 1), kv4(vc, 1), kv4(ks, 1), kv4(vs, 1), kv4(kw, 1)[:, t - wb:], kv4(vw, 1)[:, t - wb:],
            conv_st[HALO - (CONV_WIDTH - 1):][None],
            kv4(kc2, db), kv4(vc2, db), kv4(ks2, db), kv4(vs2, db), kv4(k_win, db), kv4(v_win, db),
            jnp.concatenate([state_conv[:, ds:], u2.reshape(db, ds, C_CONV)], axis=1))
```
